```python
import jax, jax.numpy as jnp
from jax import lax
import numpy as np

D_MODEL = 1024
BATCH = 8
SEQ = 2048
DEPTH = 2

N_META = 16
BLOCK = 128
SSD_HEADS = 8
SSD_HEAD_DIM = 64
SSD_D = SSD_HEADS * SSD_HEAD_DIM
SSD_GROUPS = 2
SSD_STATE = 64
SSD_CONV = 4
SSD_CONV_DIM = SSD_D + 2 * SSD_GROUPS * SSD_STATE
FOX_HEADS = 4
FOX_HEAD_DIM = 64
FOX_D = FOX_HEADS * FOX_HEAD_DIM
MLA_HEADS = 4
MLA_Q_LORA = 256
MLA_KV_LORA = 128
MLA_NOPE = 64
MLA_ROPE = 32
MLA_V = 64
MLA_D = MLA_HEADS * MLA_V
ROPE_THETA = 10000.0
D_MIX = SSD_D + FOX_D + MLA_D
IN_SIZES = [SSD_D, SSD_CONV_DIM, SSD_HEADS,
            FOX_D, FOX_D, FOX_D, FOX_HEADS,
            MLA_Q_LORA, MLA_KV_LORA, MLA_ROPE]
N_IN = sum(IN_SIZES)
IN_SPLITS = [int(s) for s in np.cumsum(IN_SIZES)[:-1]]
D_FF = 2816
ALPHA = (2 * DEPTH) ** 0.25
BETA = (8 * DEPTH) ** -0.25
EPS = 1e-5

kernel_name = "hybrid_ssd_fox_mla_macaron_deepnorm"


def layer_norm(x, g, b):
    xf = x.astype(jnp.float32)
    mu = jnp.mean(xf, -1, keepdims=True)
    var = jnp.mean(jnp.square(xf - mu), -1, keepdims=True)
    return ((xf - mu) * lax.rsqrt(var + EPS) * g + b).astype(x.dtype)


def rms_norm(x, g):
    xf = x.astype(jnp.float32)
    y = xf * lax.rsqrt(jnp.mean(jnp.square(xf), -1, keepdims=True) + EPS)
    return (y * g).astype(x.dtype)


def swiglu(x, w_gate, w_up, w_down):
    return (jax.nn.silu(x @ w_gate) * (x @ w_up)) @ w_down


def rope(x, cos, sin):
    x1, x2 = jnp.split(x.astype(jnp.float32), 2, axis=-1)
    return jnp.concatenate([x1 * cos - x2 * sin, x2 * cos + x1 * sin], -1).astype(x.dtype)


def block_edges(total):
    return sorted(set([0] + list(range(N_META, total, BLOCK)) + [total]))


def blocked_causal_attention(logits_fn, v):
    total = v.shape[1]
    outs = []
    edges = block_edges(total)
    for q0, q1 in zip(edges[:-1], edges[1:]):
        s = logits_fn(q0, q1).astype(jnp.float32)
        causal = jnp.arange(q1)[None, :] <= jnp.arange(q0, q1)[:, None]
        s = jnp.where(causal, s, -jnp.inf)
        p = jax.nn.softmax(s, axis=-1).astype(v.dtype)
        outs.append(jnp.einsum('bhqk,bkhd->bqhd', p, v[:, :q1]))
    return jnp.concatenate(outs, axis=1)


def causal_depthwise_conv(x, w, bias):
    out = lax.conv_general_dilated(
        x, w[:, None, :], window_strides=(1,), padding=[(SSD_CONV - 1, 0)],
        dimension_numbers=('NWC', 'WIO', 'NWC'), feature_group_count=x.shape[-1])
    return out + bias


def ssd_chunked(x, dt, A, Bm, Cm):
    b, l, h, p = x.shape
    n = Bm.shape[-1]
    nc = l // BLOCK
    x = x.reshape(b, nc, BLOCK, h, p)
    dt = dt.reshape(b, nc, BLOCK, h)
    Bm = Bm.reshape(b, nc, BLOCK, h, n)
    Cm = Cm.reshape(b, nc, BLOCK, h, n)
    a = jnp.moveaxis(dt * A, -1, 1)
    a_cum = jnp.cumsum(a, axis=-1)
    xdt = x * dt[..., None]
    idx = jnp.arange(BLOCK)
    causal = idx[:, None] >= idx[None, :]
    seg = jnp.exp(jnp.where(causal, a_cum[..., :, None] - a_cum[..., None, :], -jnp.inf))
    cb = jnp.einsum('bclhn,bcshn->bhcls', Cm, Bm)
    y_diag = jnp.einsum('bhcls,bcshp->bclhp', cb * seg, xdt)
    decay_states = jnp.exp(a_cum[..., -1:] - a_cum)
    states = jnp.einsum('bclhn,bhcl,bclhp->bchpn', Bm, decay_states, xdt)
    chunk_decay = jnp.exp(a_cum[..., -1])

    def step(s, inp):
        st, dec = inp
        return s * dec[..., None, None] + st, s

    init = jnp.zeros((b, h, p, n), x.dtype)
    _, prev = lax.scan(step, init, (jnp.moveaxis(states, 1, 0), jnp.moveaxis(chunk_decay, 2, 0)))
    prev = jnp.moveaxis(prev, 0, 1)
    y_off = jnp.einsum('bclhn,bchpn,bhcl->bclhp', Cm, prev, jnp.exp(a_cum))
    return (y_diag + y_off).reshape(b, l, h, p)


def ssd_mixer(z, xbc, dt_raw, conv_w, conv_b, dt_bias, a_log, d_skip, norm_g):
    b, L, _ = xbc.shape
    f32 = jnp.float32
    xbc = jax.nn.silu(causal_depthwise_conv(xbc, conv_w, conv_b)).astype(f32)
    xs, Bm, Cm = jnp.split(xbc, [SSD_D, SSD_D + SSD_GROUPS * SSD_STATE], axis=-1)
    xs = xs.reshape(b, L, SSD_HEADS, SSD_HEAD_DIM)
    rep = SSD_HEADS // SSD_GROUPS
    Bm = jnp.repeat(Bm.reshape(b, L, SSD_GROUPS, SSD_STATE), rep, axis=2)
    Cm = jnp.repeat(Cm.reshape(b, L, SSD_GROUPS, SSD_STATE), rep, axis=2)
    dt = jax.nn.softplus(dt_raw.astype(f32) + dt_bias.astype(f32))
    A = -jnp.exp(a_log.astype(f32))
    pad = (-L) % BLOCK
    padf = lambda t: jnp.pad(t, ((0, 0), (pad, 0)) + ((0, 0),) * (t.ndim - 2))
    y = ssd_chunked(padf(xs), padf(dt), A, padf(Bm), padf(Cm))[:, pad:]
    y = y + d_skip.astype(f32)[:, None] * xs
    y = y.reshape(b, L, SSD_D) * jax.nn.silu(z.astype(f32))
    y = rms_norm(y.reshape(b, L, SSD_GROUPS, SSD_D // SSD_GROUPS), 1.0).reshape(b, L, SSD_D) * norm_g
    return y.astype(z.dtype)


def fox_mixer(q, k, v, f_raw, f_b):
    b, L, _ = q.shape
    q = q.reshape(b, L, FOX_HEADS, FOX_HEAD_DIM)
    k = k.reshape(b, L, FOX_HEADS, FOX_HEAD_DIM)
    v = v.reshape(b, L, FOX_HEADS, FOX_HEAD_DIM)
    log_f = jax.nn.log_sigmoid(f_raw.astype(jnp.float32) + f_b.astype(jnp.float32))
    c = jnp.cumsum(log_f, axis=1).transpose(0, 2, 1)
    scale = FOX_HEAD_DIM ** -0.5

    def logits(q0, q1):
        s = jnp.einsum('bqhd,bkhd->bhqk', q[:, q0:q1], k[:, :q1]).astype(jnp.float32) * scale
        return s + (c[:, :, q0:q1, None] - c[:, :, None, :q1])

    return blocked_causal_attention(logits, v).reshape(b, L, FOX_D)


def mla_mixer(cq, ckv, k_rope, q_norm_g, w_uq, kv_norm_g, w_ukv, cos, sin):
    b, L, _ = cq.shape
    qh = (rms_norm(cq, q_norm_g) @ w_uq).reshape(b, L, MLA_HEADS, MLA_NOPE + MLA_ROPE)
    q_nope, q_rope = qh[..., :MLA_NOPE], qh[..., MLA_NOPE:]
    q_rope = rope(q_rope, cos[None, :, None, :], sin[None, :, None, :])
    kv = (rms_norm(ckv, kv_norm_g) @ w_ukv).reshape(b, L, MLA_HEADS, MLA_NOPE + MLA_V)
    k_nope, v = kv[..., :MLA_NOPE], kv[..., MLA_NOPE:]
    k_rope = rope(k_rope, cos[None], sin[None])
    scale = (MLA_NOPE + MLA_ROPE) ** -0.5

    def logits(q0, q1):
        s = jnp.einsum('bqhd,bkhd->bhqk', q_nope[:, q0:q1], k_nope[:, :q1])
        s = s + jnp.einsum('bqhr,bkr->bhqk', q_rope[:, q0:q1], k_rope[:, :q1])
        return s * scale

    return blocked_causal_attention(logits, v).reshape(b, L, MLA_D)


def setup_inputs(seed: int = 0) -> dict:
    key = jax.random.key(seed)
    ks = iter(jax.random.split(key, 48))
    f32 = jnp.float32
    Dm, F, NL = D_MODEL, D_FF, DEPTH

    def nrm(shape, scale):
        return jax.random.normal(next(ks), shape, f32) * scale

    def gain(shape):
        return 1.0 + nrm(shape, 0.02)

    u = jax.random.uniform(next(ks), (NL, SSD_HEADS), f32)
    dt0 = jnp.exp(u * (np.log(0.1) - np.log(0.001)) + np.log(0.001))
    dt_bias = dt0 + jnp.log(-jnp.expm1(-dt0))
    a_log = jnp.log(jax.random.uniform(next(ks), (NL, SSD_HEADS), f32, 1.0, 16.0))
    return {
        "x": nrm((BATCH, SEQ, Dm), 1.0),
        "meta": nrm((N_META, Dm), 1.0),
        "ffn1_w_gate": nrm((NL, Dm, F), Dm ** -0.5),
        "ffn1_w_up": nrm((NL, Dm, F), Dm ** -0.5),
        "ffn1_w_down": nrm((NL, F, Dm), F ** -0.5 * BETA),
        "ln1_g": gain((NL, Dm)),
        "ln1_b": nrm((NL, Dm), 0.02),
        "w_in": nrm((NL, Dm, N_IN), Dm ** -0.5),
        "conv_w": nrm((NL, SSD_CONV, SSD_CONV_DIM), SSD_CONV ** -0.5),
        "conv_b": nrm((NL, SSD_CONV_DIM), 0.02),
        "dt_bias": dt_bias,
        "a_log": a_log,
        "d_skip": gain((NL, SSD_HEADS)),
        "ssd_norm_g": gain((NL, SSD_D)),
        "fox_f_b": 3.0 + nrm((NL, FOX_HEADS), 0.5),
        "mla_q_norm_g": gain((NL, MLA_Q_LORA)),
        "mla_w_uq": nrm((NL, MLA_Q_LORA, MLA_HEADS * (MLA_NOPE + MLA_ROPE)), MLA_Q_LORA ** -0.5),
        "mla_kv_norm_g": gain((NL, MLA_KV_LORA)),
        "mla_w_ukv": nrm((NL, MLA_KV_LORA, MLA_HEADS * (MLA_NOPE + MLA_V)), MLA_KV_LORA ** -0.5),
        "w_out": nrm((NL, D_MIX, Dm), D_MIX ** -0.5 * BETA),
        "ln2_g": gain((NL, Dm)),
        "ln2_b": nrm((NL, Dm), 0.02),
        "ffn2_w_gate": nrm((NL, Dm, F), Dm ** -0.5),
        "ffn2_w_up": nrm((NL, Dm, F), Dm ** -0.5),
        "ffn2_w_down": nrm((NL, F, Dm), F ** -0.5 * BETA),
        "ln3_g": gain((NL, Dm)),
        "ln3_b": nrm((NL, Dm), 0.02),
    }


def reference(x, meta, ffn1_w_gate, ffn1_w_up, ffn1_w_down, ln1_g, ln1_b, w_in,
              conv_w, conv_b, dt_bias, a_log, d_skip, ssd_norm_g, fox_f_b,
              mla_q_norm_g, mla_w_uq, mla_kv_norm_g, mla_w_ukv, w_out, ln2_g, ln2_b,
              ffn2_w_gate, ffn2_w_up, ffn2_w_down, ln3_g, ln3_b):
    b = x.shape[0]
    h = jnp.concatenate([jnp.broadcast_to(meta[None].astype(x.dtype), (b, N_META, D_MODEL)), x], axis=1)
    total = h.shape[1]
    pos = jnp.arange(total, dtype=jnp.float32)
    inv_freq = 1.0 / (ROPE_THETA ** (jnp.arange(0, MLA_ROPE, 2, dtype=jnp.float32) / MLA_ROPE))
    ang = pos[:, None] * inv_freq[None, :]
    cos, sin = jnp.cos(ang), jnp.sin(ang)

    for l in range(DEPTH):
        h = layer_norm(ALPHA * h + 0.5 * swiglu(h, ffn1_w_gate[l], ffn1_w_up[l], ffn1_w_down[l]),
                       ln1_g[l], ln1_b[l])
        proj = h @ w_in[l]
        (z, xbc, dt_raw, fq, fk, fv, f_raw, cq, ckv, k_rope) = jnp.split(proj, IN_SPLITS, axis=-1)
        y_ssd = ssd_mixer(z, xbc, dt_raw, conv_w[l], conv_b[l], dt_bias[l], a_log[l],
                          d_skip[l], ssd_norm_g[l])
        y_fox = fox_mixer(fq, fk, fv, f_raw, fox_f_b[l])
        y_mla = mla_mixer(cq, ckv, k_rope, mla_q_norm_g[l], mla_w_uq[l], mla_kv_norm_g[l],
                          mla_w_ukv[l], cos, sin)
        mix = jnp.concatenate([y_ssd, y_fox.astype(h.dtype), y_mla.astype(h.dtype)], axis=-1) @ w_out[l]
        h = layer_norm(ALPHA * h + mix, ln2_g[l], ln2_b[l])
        h = layer_norm(ALPHA * h + 0.5 * swiglu(h, ffn2_w_gate[l], ffn2_w_up[l], ffn2_w_down[l]),
                       ln3_g[l], ln3_b[l])
    return h[:, N_META:]
```

```python
import functools

import numpy as np
import jax
import jax.numpy as jnp
from jax import lax
from jax.experimental import pallas as pl
from jax.experimental.pallas import tpu as pltpu

F32 = jnp.float32
BF16 = jnp.bfloat16

D_MODEL = 1024
DEPTH = 2
N_META = 16
BLOCK = 128
SSD_HEADS = 8
SSD_HEAD_DIM = 64
SSD_D = SSD_HEADS * SSD_HEAD_DIM
SSD_GROUPS = 2
SSD_STATE = 64
SSD_CONV = 4
SSD_CONV_DIM = SSD_D + 2 * SSD_GROUPS * SSD_STATE
FOX_HEADS = 4
FOX_HEAD_DIM = 64
FOX_D = FOX_HEADS * FOX_HEAD_DIM
MLA_HEADS = 4
MLA_Q_LORA = 256
MLA_KV_LORA = 128
MLA_NOPE = 64
MLA_ROPE = 32
MLA_V = 64
MLA_D = MLA_HEADS * MLA_V
ROPE_THETA = 10000.0
D_MIX = SSD_D + FOX_D + MLA_D
D_FF = 2816
ALPHA = (2 * DEPTH) ** 0.25
EPS = 1e-5

LANES = 128
FF_CHUNK = 256
N_FF_CHUNKS = D_FF // FF_CHUNK
N_OUT_CHUNKS = D_MODEL // FF_CHUNK
NEG_BIG = -1e30
VMEM_LIMIT = 56 * 1024 * 1024

C_Z = 0
C_XBC = C_Z + SSD_D
C_FQ = C_XBC + SSD_CONV_DIM
C_FK = C_FQ + FOX_D
C_FV = C_FK + FOX_D
C_CQ = C_FV + FOX_D
C_CKV = C_CQ + MLA_Q_LORA
C_KR = C_CKV + MLA_KV_LORA
C_KRR = C_KR + LANES
C_SMALL = C_KRR + LANES
N_IN_ARR = C_SMALL + LANES
SMALL_DT = 0
SMALL_F = 8


def _sigmoid(x):
    return 1.0 / (1.0 + jnp.exp(-x))


def _softplus(x):
    return jnp.maximum(x, 0.0) + jnp.log(1.0 + jnp.exp(-jnp.abs(x)))


def _layer_norm_rows(y, g, b):
    mu = jnp.mean(y, axis=-1, keepdims=True)
    yc = y - mu
    var = jnp.mean(yc * yc, axis=-1, keepdims=True)
    return yc * lax.rsqrt(var + EPS) * g + b


def _split3(x):
    x1 = x.astype(BF16)
    r1 = x - x1.astype(F32)
    x2 = r1.astype(BF16)
    r2 = r1 - x2.astype(F32)
    return x1, x2, r2.astype(BF16)


def _dot(a, b):
    return jnp.dot(a, b, preferred_element_type=F32)


def _dot_nt(a, b):
    return lax.dot_general(a, b, (((1,), (1,)), ((), ())), preferred_element_type=F32)


def _ffn_ln_kernel(x_ref, wgu_ref, wd_ref, g_ref, b_ref, o_ref, a_scr, y_scr):
    xb = x_ref[...].astype(BF16)

    def gate_up(c, carry):
        hg = _dot(xb, wgu_ref[c])
        gate = hg[:, :FF_CHUNK]
        a_scr[c] = (gate * _sigmoid(gate) * hg[:, FF_CHUNK:]).astype(BF16)
        return carry

    lax.fori_loop(0, N_FF_CHUNKS, gate_up, 0)

    def down(n, carry):
        acc = _dot(a_scr[0], wd_ref[n, 0])
        for c in range(1, N_FF_CHUNKS):
            acc = acc + _dot(a_scr[c], wd_ref[n, c])
        y_scr[n] = acc
        return carry

    lax.fori_loop(0, N_OUT_CHUNKS, down, 0)

    y = jnp.concatenate([y_scr[n] for n in range(N_OUT_CHUNKS)], axis=-1)
    y = ALPHA * x_ref[...] + 0.5 * y
    o_ref[...] = _layer_norm_rows(y, g_ref[...], b_ref[...])


def _ffn_ln(h, wgu, wd4, g, b, tm):
    n_rows = h.shape[0]
    const = dict(pipeline_mode=pl.Buffered(1))
    return pl.pallas_call(
        _ffn_ln_kernel,
        grid=(n_rows // tm,),
        in_specs=[
            pl.BlockSpec((tm, D_MODEL), lambda i: (i, 0)),
            pl.BlockSpec(wgu.shape, lambda i: (0, 0, 0), **const),
            pl.BlockSpec(wd4.shape, lambda i: (0, 0, 0, 0), **const),
            pl.BlockSpec((1, D_MODEL), lambda i: (0, 0)),
            pl.BlockSpec((1, D_MODEL), lambda i: (0, 0)),
        ],
        out_specs=pl.BlockSpec((tm, D_MODEL), lambda i: (i, 0)),
        out_shape=jax.ShapeDtypeStruct((n_rows, D_MODEL), F32),
        scratch_shapes=[
            pltpu.VMEM((N_FF_CHUNKS, tm, FF_CHUNK), BF16),
            pltpu.VMEM((N_OUT_CHUNKS, tm, FF_CHUNK), F32),
        ],
        compiler_params=pltpu.CompilerParams(
            dimension_semantics=("arbitrary",), vmem_limit_bytes=VMEM_LIMIT),
        name="ffn_ln",
    )(h, wgu, wd4, g, b)


def _rms_rows(x, g):
    return x * lax.rsqrt(jnp.mean(x * x, axis=-1, keepdims=True) + EPS) * g


def _proj_kernel(h_ref, win_ref, wuq_ref, wuqr_ref, wkk_ref, wkv_ref, qg_ref, kvg_ref,
                 cos_ref, sin_ref,
                 zx_ref, small_ref, fq_ref, fk_ref, fv_ref, mq_ref, mk_ref, mv_ref):
    hb = h_ref[...].astype(BF16)
    zx_ref[...] = _dot(hb, win_ref[:, C_Z:C_FQ])
    small_ref[...] = _dot(hb, win_ref[:, C_SMALL:N_IN_ARR])
    fq_ref[...] = (_dot(hb, win_ref[:, C_FQ:C_FK]) * (FOX_HEAD_DIM ** -0.5)).astype(BF16)
    fk_ref[...] = _dot(hb, win_ref[:, C_FK:C_FV]).astype(BF16)
    fv_ref[...] = _dot(hb, win_ref[:, C_FV:C_CQ]).astype(BF16)

    cos = cos_ref[...]
    sin = sin_ref[...]
    cqn = _rms_rows(_dot(hb, win_ref[:, C_CQ:C_CKV]), qg_ref[...]).astype(BF16)
    q = _dot(cqn, wuq_ref[...])
    qr = _dot(cqn, wuqr_ref[...])
    kvn = _rms_rows(_dot(hb, win_ref[:, C_CKV:C_KR]), kvg_ref[...]).astype(BF16)
    kn = _dot(kvn, wkk_ref[...])
    mv_ref[...] = _dot(kvn, wkv_ref[...]).astype(BF16)
    krope = (_dot(hb, win_ref[:, C_KR:C_KRR]) * cos
             + _dot(hb, win_ref[:, C_KRR:C_SMALL]) * sin)
    scale = (MLA_NOPE + MLA_ROPE) ** -0.5
    for hd in range(MLA_HEADS):
        sl = slice(hd * LANES, (hd + 1) * LANES)
        mq_ref[:, sl] = ((q[:, sl] * cos + qr[:, sl] * sin) * scale).astype(BF16)
        mk_ref[:, sl] = (kn[:, sl] + krope).astype(BF16)


def _proj(h, win, wuq, wuqr, wkk, wkv, qg, kvg, cos, sin, tm):
    n_rows = h.shape[0]
    const = dict(pipeline_mode=pl.Buffered(1))

    def rows(width):
        return pl.BlockSpec((tm, width), lambda i: (i, 0))

    def whole(a):
        return pl.BlockSpec(a.shape, lambda i: (0, 0), **const)

    out_widths = [(C_FQ, F32), (LANES, F32), (FOX_D, BF16), (FOX_D, BF16), (FOX_D, BF16),
                  (MLA_HEADS * LANES, BF16), (MLA_HEADS * LANES, BF16), (MLA_D, BF16)]
    return pl.pallas_call(
        _proj_kernel,
        grid=(n_rows // tm,),
        in_specs=[rows(D_MODEL), whole(win), whole(wuq), whole(wuqr), whole(wkk), whole(wkv),
                  whole(qg), whole(kvg), rows(LANES), rows(LANES)],
        out_specs=[rows(w) for w, _ in out_widths],
        out_shape=[jax.ShapeDtypeStruct((n_rows, w), dt) for w, dt in out_widths],
        compiler_params=pltpu.CompilerParams(
            dimension_semantics=("arbitrary",), vmem_limit_bytes=VMEM_LIMIT),
        name="in_proj",
    )(h, win, wuq, wuqr, wkk, wkv, qg, kvg, cos, sin)


def _ssd_kernel(pad, zx_ref, small_ref, cw_ref, cb_ref, dtb_ref, a_ref, dsk_ref, ng_ref,
                o_ref, conv_scr, s_scr):
    Q = BLOCK
    c = pl.program_id(1)
    row = lax.broadcasted_iota(jnp.int32, (Q, 1), 0)
    valid = jnp.logical_or(c > 0, row >= pad)
    lane = lax.broadcasted_iota(jnp.int32, (1, LANES), 1)
    lane_lo = lane < SSD_HEAD_DIM
    sub = lax.broadcasted_iota(jnp.int32, (LANES, 1), 0)

    @pl.when(c == 0)
    def _():
        conv_scr[0:8, :] = jnp.zeros((8, SSD_CONV_DIM), F32)
        s_scr[...] = jnp.zeros(s_scr.shape, F32)

    conv_scr[8:8 + Q, :] = jnp.where(valid, zx_ref[0, :, C_XBC:C_FQ], 0.0)
    acc = cb_ref[...]
    for k in range(SSD_CONV):
        off = 8 - (SSD_CONV - 1) + k
        acc = acc + cw_ref[k:k + 1, :] * conv_scr[off:off + Q, :]
    conv_scr[0:8, :] = conv_scr[Q:Q + 8, :]
    xbc = acc * _sigmoid(acc)
    bm = xbc[:, SSD_D:SSD_D + LANES]
    cm = xbc[:, SSD_D + LANES:SSD_D + 2 * LANES]

    dt = jnp.where(valid, _softplus(small_ref[0] + dtb_ref[...]), 0.0)
    a = dt * a_ref[...]
    tri = (lax.broadcasted_iota(jnp.int32, (Q, Q), 0)
           >= lax.broadcasted_iota(jnp.int32, (Q, Q), 1))
    tri_b = jnp.where(tri, 1.0, 0.0).astype(BF16)
    a1, a2, a3 = _split3(a)
    a_cum = _dot(tri_b, a1) + _dot(tri_b, a2) + _dot(tri_b, a3)
    a_cum_t = a_cum.T
    bm_t = bm.T

    cm_b = cm.astype(BF16)
    bm_b = bm.astype(BF16)
    cb_g = [_dot_nt(jnp.where(lane_lo, cm, 0.0).astype(BF16), bm_b),
            _dot_nt(jnp.where(lane_lo, 0.0, cm).astype(BF16), bm_b)]
    rows_g = [sub < SSD_STATE, sub >= SSD_STATE]

    pairs_per_group = SSD_HEADS // 2 // SSD_GROUPS
    y_pairs = []
    for p in range(SSD_HEADS // 2):
        g = p // pairs_per_group
        psl = slice(p * LANES, (p + 1) * LANES)
        xs_p = xbc[:, psl]
        dt_pair = jnp.where(lane_lo, dt[:, 2 * p:2 * p + 1], dt[:, 2 * p + 1:2 * p + 2])
        xdt = (xs_p * dt_pair).astype(BF16)
        s_old = s_scr[p]
        yd, upd, e_col, e_last = [], [], [], []
        for par in range(2):
            hd = 2 * p + par
            col = a_cum[:, hd:hd + 1]
            rowv = a_cum_t[hd:hd + 1, :]
            last = a_cum_t[hd:hd + 1, Q - 1:Q]
            seg = jnp.exp(jnp.where(tri, col - rowv, NEG_BIG))
            yd.append(_dot((cb_g[g] * seg).astype(BF16), xdt))
            upd.append(_dot((bm_t * jnp.exp(last - rowv)).astype(BF16), xdt))
            e_col.append(jnp.exp(col))
            e_last.append(jnp.exp(last))
        y_off = _dot(cm_b, s_old.astype(BF16)) * jnp.where(lane_lo, e_col[0], e_col[1])
        s_new = (jnp.where(lane_lo, e_last[0], e_last[1]) * s_old
                 + jnp.where(rows_g[g], jnp.where(lane_lo, upd[0], upd[1]), 0.0))
        s_scr[p] = s_new
        y_p = jnp.where(lane_lo, yd[0], yd[1]) + y_off + dsk_ref[:, psl] * xs_p
        z_p = zx_ref[0, :, psl]
        y_pairs.append(y_p * (z_p * _sigmoid(z_p)))

    for g in range(SSD_GROUPS):
        ps = range(g * pairs_per_group, (g + 1) * pairs_per_group)
        ss = sum(jnp.sum(y_pairs[p] * y_pairs[p], axis=-1, keepdims=True) for p in ps)
        inv = lax.rsqrt(ss * (1.0 / (pairs_per_group * LANES)) + EPS)
        for p in ps:
            psl = slice(p * LANES, (p + 1) * LANES)
            o_ref[0, :, psl] = (y_pairs[p] * inv * ng_ref[:, psl]).astype(BF16)


def _ssd(zx, small, cw, cb, dtb, a_row, dsk, ng, pad):
    bsz, lp, _ = zx.shape

    def whole(a):
        return pl.BlockSpec(a.shape, lambda b, c: (0, 0))

    return pl.pallas_call(
        functools.partial(_ssd_kernel, pad),
        grid=(bsz, lp // BLOCK),
        in_specs=[pl.BlockSpec((1, BLOCK, C_FQ), lambda b, c: (b, c, 0)),
                  pl.BlockSpec((1, BLOCK, LANES), lambda b, c: (b, c, 0)),
                  whole(cw), whole(cb), whole(dtb), whole(a_row), whole(dsk), whole(ng)],
        out_specs=pl.BlockSpec((1, BLOCK, SSD_D), lambda b, c: (b, c, 0)),
        out_shape=jax.ShapeDtypeStruct((bsz, lp, SSD_D), BF16),
        scratch_shapes=[pltpu.VMEM((BLOCK + 8, SSD_CONV_DIM), F32),
                        pltpu.VMEM((SSD_HEADS // 2, LANES, LANES), F32)],
        compiler_params=pltpu.CompilerParams(
            dimension_semantics=("arbitrary", "arbitrary"), vmem_limit_bytes=VMEM_LIMIT),
        name="ssd_mixer",
    )(zx, small, cw, cb, dtb, a_row, dsk, ng)


def _attn_kernel(pad, n_heads, q_per_pair, use_forget, q_ref, k_ref, v_ref, small_ref, fb_ref,
                 o_ref, bias_scr):
    T = BLOCK
    qi = pl.program_id(1)
    n_blocks = bias_scr.shape[0]
    lane = lax.broadcasted_iota(jnp.int32, (1, LANES), 1)
    lane_lo = lane < 64

    @pl.when(qi == 0)
    def _():
        pad_key = lane < pad
        if use_forget:
            upper = (lax.broadcasted_iota(jnp.int32, (T, T), 0)
                     <= lax.broadcasted_iota(jnp.int32, (T, T), 1))
            upper_b = jnp.where(upper, 1.0, 0.0).astype(BF16)
            carry = jnp.zeros((8, 1), F32)
            for j in range(n_blocks):
                logit_t = small_ref[0, j * T:(j + 1) * T, :].T[SMALL_F:SMALL_F + 8, :]
                log_f = -_softplus(-(logit_t + fb_ref[...]))
                if j == 0:
                    log_f = jnp.where(pad_key, 0.0, log_f)
                f1, f2, f3 = _split3(log_f)
                c_blk = carry + _dot(f1, upper_b) + _dot(f2, upper_b) + _dot(f3, upper_b)
                carry = c_blk[:, T - 1:T]
                bias = -c_blk
                if j == 0:
                    bias = jnp.where(pad_key, NEG_BIG, bias)
                bias_scr[j] = bias
        else:
            bias_scr[...] = jnp.zeros(bias_scr.shape, F32)
            bias_scr[0] = jnp.where(pad_key, NEG_BIG, jnp.zeros((8, LANES), F32))

    causal = (lax.broadcasted_iota(jnp.int32, (T, T), 1)
              <= lax.broadcasted_iota(jnp.int32, (T, T), 0))

    for p in range(n_heads // 2):
        psl = slice(p * LANES, (p + 1) * LANES)
        accs, ls = [], []
        for par in range(2):
            hd = 2 * p + par
            if q_per_pair:
                qp = q_ref[0, :, psl]
                keep = lane_lo if par == 0 else jnp.logical_not(lane_lo)
                qh = jnp.where(keep, qp, jnp.zeros_like(qp))
                ksl = psl
            else:
                ksl = slice(hd * LANES, (hd + 1) * LANES)
                qh = q_ref[0, :, ksl]

            def tile(j, carry, masked, qh=qh, ksl=ksl, hd=hd):
                m, l, acc = carry
                r0 = pl.multiple_of(j * T, T)
                s = _dot_nt(qh, k_ref[0, pl.ds(r0, T), ksl]) + bias_scr[j][hd:hd + 1, :]
                if masked:
                    s = jnp.where(causal, s, NEG_BIG)
                m_new = jnp.maximum(m, jnp.max(s, axis=-1, keepdims=True))
                corr = jnp.exp(m - m_new)
                pr = jnp.exp(s - m_new)
                l = corr * l + jnp.sum(pr, axis=-1, keepdims=True)
                acc = corr * acc + _dot(pr.astype(BF16), v_ref[0, pl.ds(r0, T), psl])
                return m_new, l, acc

            init = (jnp.full((T, 1), NEG_BIG, F32), jnp.zeros((T, 1), F32),
                    jnp.zeros((T, LANES), F32))
            carry = lax.fori_loop(0, qi, functools.partial(tile, masked=False), init)
            _, l, acc = tile(qi, carry, True)
            accs.append(acc)
            ls.append(l)
        inv = jnp.where(lane_lo, 1.0 / ls[0], 1.0 / ls[1])
        o_ref[0, :, psl] = (jnp.where(lane_lo, accs[0], accs[1]) * inv).astype(BF16)


def _attention(q, k, v, small, fb, pad, n_heads, q_per_pair, use_forget, name):
    bsz, lp, qw = q.shape
    kw = k.shape[-1]
    vw = v.shape[-1]
    return pl.pallas_call(
        functools.partial(_attn_kernel, pad, n_heads, q_per_pair, use_forget),
        grid=(bsz, lp // BLOCK),
        in_specs=[pl.BlockSpec((1, BLOCK, qw), lambda b, i: (b, i, 0)),
                  pl.BlockSpec((1, lp, kw), lambda b, i: (b, 0, 0)),
                  pl.BlockSpec((1, lp, vw), lambda b, i: (b, 0, 0)),
                  pl.BlockSpec((1, lp, LANES), lambda b, i: (b, 0, 0)),
                  pl.BlockSpec(fb.shape, lambda b, i: (0, 0))],
        out_specs=pl.BlockSpec((1, BLOCK, vw), lambda b, i: (b, i, 0)),
        out_shape=jax.ShapeDtypeStruct((bsz, lp, vw), BF16),
        scratch_shapes=[pltpu.VMEM((lp // BLOCK, 8, LANES), F32)],
        compiler_params=pltpu.CompilerParams(
            dimension_semantics=("arbitrary", "arbitrary"), vmem_limit_bytes=VMEM_LIMIT),
        name=name,
    )(q, k, v, small, fb)


def _out_ln_kernel(h_ref, ys_ref, yf_ref, ym_ref, w_ref, g_ref, b_ref, o_ref):
    mix = (_dot(ys_ref[...], w_ref[0:SSD_D, :])
           + _dot(yf_ref[...], w_ref[SSD_D:SSD_D + FOX_D, :])
           + _dot(ym_ref[...], w_ref[SSD_D + FOX_D:D_MIX, :]))
    o_ref[...] = _layer_norm_rows(ALPHA * h_ref[...] + mix, g_ref[...], b_ref[...])


def _out_ln(h, ys, yf, ym, w, g, b, tm):
    n_rows = h.shape[0]

    def rows(width):
        return pl.BlockSpec((tm, width), lambda i: (i, 0))

    return pl.pallas_call(
        _out_ln_kernel,
        grid=(n_rows // tm,),
        in_specs=[rows(D_MODEL), rows(SSD_D), rows(FOX_D), rows(MLA_D),
                  pl.BlockSpec(w.shape, lambda i: (0, 0), pipeline_mode=pl.Buffered(1)),
                  pl.BlockSpec((1, D_MODEL), lambda i: (0, 0)),
                  pl.BlockSpec((1, D_MODEL), lambda i: (0, 0))],
        out_specs=rows(D_MODEL),
        out_shape=jax.ShapeDtypeStruct((n_rows, D_MODEL), F32),
        compiler_params=pltpu.CompilerParams(
            dimension_semantics=("arbitrary",), vmem_limit_bytes=VMEM_LIMIT),
        name="out_proj_ln",
    )(h, ys, yf, ym, w, g, b)


def _ffn_weights(w_gate, w_up, w_down):
    wg = w_gate.reshape(D_MODEL, N_FF_CHUNKS, FF_CHUNK)
    wu = w_up.reshape(D_MODEL, N_FF_CHUNKS, FF_CHUNK)
    wgu = jnp.concatenate([wg, wu], axis=-1).transpose(1, 0, 2).astype(BF16)
    wd4 = (w_down.reshape(N_FF_CHUNKS, FF_CHUNK, N_OUT_CHUNKS, FF_CHUNK)
           .transpose(2, 0, 1, 3).astype(BF16))
    return wgu, wd4


def _rot_half_cols(w):
    half = MLA_ROPE // 2
    return jnp.concatenate([-w[..., half:], w[..., :half]], axis=-1)


def _in_proj_weights(w_in):
    sizes = [SSD_D, SSD_CONV_DIM, SSD_HEADS, FOX_D, FOX_D, FOX_D, FOX_HEADS,
             MLA_Q_LORA, MLA_KV_LORA, MLA_ROPE]
    splits = [int(s) for s in np.cumsum(sizes)[:-1]]
    z, xbc, dt, fq, fk, fv, fr, cq, ckv, kr = jnp.split(w_in, splits, axis=-1)
    zeros = lambda n: jnp.zeros((D_MODEL, n), w_in.dtype)
    rope_pad = LANES - MLA_NOPE - MLA_ROPE
    kr128 = jnp.concatenate([zeros(MLA_NOPE), kr, zeros(rope_pad)], axis=-1)
    krr128 = jnp.concatenate([zeros(MLA_NOPE), _rot_half_cols(kr), zeros(rope_pad)], axis=-1)
    small = jnp.concatenate([dt, fr, zeros(LANES - SMALL_F - FOX_HEADS)], axis=-1)
    return jnp.concatenate([z, xbc, fq, fk, fv, cq, ckv, kr128, krr128, small],
                           axis=-1).astype(BF16)


def _mla_weights(w_uq, w_ukv):
    rope_pad = LANES - MLA_NOPE - MLA_ROPE
    wq = w_uq.reshape(MLA_Q_LORA, MLA_HEADS, MLA_NOPE + MLA_ROPE)
    zq = jnp.zeros((MLA_Q_LORA, MLA_HEADS, rope_pad), w_uq.dtype)
    wuq = jnp.concatenate([wq, zq], axis=-1).reshape(MLA_Q_LORA, MLA_HEADS * LANES)
    wuqr = jnp.concatenate([jnp.zeros_like(wq[..., :MLA_NOPE]),
                            _rot_half_cols(wq[..., MLA_NOPE:]), zq],
                           axis=-1).reshape(MLA_Q_LORA, MLA_HEADS * LANES)
    wkv = w_ukv.reshape(MLA_KV_LORA, MLA_HEADS, MLA_NOPE + MLA_V)
    wkk = jnp.concatenate([wkv[..., :MLA_NOPE], jnp.zeros_like(wkv[..., MLA_NOPE:])],
                          axis=-1).reshape(MLA_KV_LORA, MLA_HEADS * LANES)
    wv = wkv[..., MLA_NOPE:].reshape(MLA_KV_LORA, MLA_D)
    return wuq.astype(BF16), wuqr.astype(BF16), wkk.astype(BF16), wv.astype(BF16)


def _lane_row(v, width=LANES):
    v = v.astype(F32)
    return jnp.pad(v, (0, width - v.shape[0]))[None, :]


def _rope_tables(lp, pad, bsz):
    pos = jnp.arange(lp, dtype=F32) - pad
    inv_freq = 1.0 / (ROPE_THETA ** (jnp.arange(0, MLA_ROPE, 2, dtype=F32) / MLA_ROPE))
    ang = pos[:, None] * inv_freq[None, :]
    cos, sin = jnp.cos(ang), jnp.sin(ang)
    rope_pad = LANES - MLA_NOPE - MLA_ROPE
    cos128 = jnp.concatenate([jnp.ones((lp, MLA_NOPE), F32), cos, cos,
                              jnp.zeros((lp, rope_pad), F32)], axis=-1)
    sin128 = jnp.concatenate([jnp.zeros((lp, MLA_NOPE), F32), sin, sin,
                              jnp.zeros((lp, rope_pad), F32)], axis=-1)
    return jnp.tile(cos128, (bsz, 1)), jnp.tile(sin128, (bsz, 1))


def _row_tile(n_rows, target):
    tm = target
    while n_rows % tm:
        tm //= 2
    return tm


def kernel(x, meta, ffn1_w_gate, ffn1_w_up, ffn1_w_down, ln1_g, ln1_b, w_in, conv_w, conv_b, dt_bias, a_log, d_skip, ssd_norm_g, fox_f_b, mla_q_norm_g, mla_w_uq, mla_kv_norm_g, mla_w_ukv, w_out, ln2_g, ln2_b, ffn2_w_gate, ffn2_w_up, ffn2_w_down, ln3_g, ln3_b):
    bsz, seq, _ = x.shape
    assert seq % BLOCK == 0
    pad = (-N_META) % BLOCK
    lp = pad + N_META + seq
    n_rows = bsz * lp
    tm = _row_tile(n_rows, 512)

    head = jnp.concatenate([jnp.zeros((pad, D_MODEL), x.dtype), meta.astype(x.dtype)], axis=0)
    h = jnp.concatenate([jnp.broadcast_to(head[None], (bsz, pad + N_META, D_MODEL)), x], axis=1)
    h = h.reshape(n_rows, D_MODEL)
    cos, sin = _rope_tables(lp, pad, bsz)
    row = lambda v: v.astype(F32)[None, :]

    for l in range(DEPTH):
        wgu, wd4 = _ffn_weights(ffn1_w_gate[l], ffn1_w_up[l], ffn1_w_down[l])
        h = _ffn_ln(h, wgu, wd4, row(ln1_g[l]), row(ln1_b[l]), tm)

        win = _in_proj_weights(w_in[l])
        wuq, wuqr, wkk, wkv = _mla_weights(mla_w_uq[l], mla_w_ukv[l])
        zx, small, fq, fk, fv, mq, mk, mv = _proj(
            h, win, wuq, wuqr, wkk, wkv, row(mla_q_norm_g[l]), row(mla_kv_norm_g[l]), cos, sin, tm)
        b3 = lambda a: a.reshape(bsz, lp, a.shape[-1])

        a_row = _lane_row(-jnp.exp(a_log[l].astype(F32)))
        dsk = jnp.repeat(d_skip[l].astype(F32), SSD_HEAD_DIM)[None, :]
        y_ssd = _ssd(b3(zx), b3(small), conv_w[l].astype(F32), row(conv_b[l]),
                     _lane_row(dt_bias[l]), a_row, dsk, row(ssd_norm_g[l]), pad)

        fb = jnp.broadcast_to(_lane_row(fox_f_b[l], 8).T, (8, LANES))
        y_fox = _attention(b3(fq), b3(fk), b3(fv), b3(small), fb, pad, FOX_HEADS, True, True,
                           "fox_attention")
        y_mla = _attention(b3(mq), b3(mk), b3(mv), b3(small), fb, pad, MLA_HEADS, False, False,
                           "mla_attention")

        flat = lambda a: a.reshape(n_rows, a.shape[-1])
        h = _out_ln(h, flat(y_ssd), flat(y_fox), flat(y_mla), w_out[l].astype(BF16),
                    row(ln2_g[l]), row(ln2_b[l]), tm)

        wgu, wd4 = _ffn_weights(ffn2_w_gate[l], ffn2_w_up[l], ffn2_w_down[l])
        h = _ffn_ln(h, wgu, wd4, row(ln3_g[l]), row(ln3_b[l]), tm)

    return h.reshape(bsz, lp, D_MODEL)[:, pad + N_META:]
```

```python
import functools

import numpy as np
import jax
import jax.numpy as jnp
from jax import lax
from jax.experimental import pallas as pl
from jax.experimental.pallas import tpu as pltpu

F32 = jnp.float32
BF16 = jnp.bfloat16

D_MODEL = 1024
DEPTH = 2
N_META = 16
BLOCK = 128
SSD_HEADS = 8
SSD_HEAD_DIM = 64
SSD_D = SSD_HEADS * SSD_HEAD_DIM
SSD_GROUPS = 2
SSD_STATE = 64
SSD_CONV = 4
SSD_CONV_DIM = SSD_D + 2 * SSD_GROUPS * SSD_STATE
FOX_HEADS = 4
FOX_HEAD_DIM = 64
FOX_D = FOX_HEADS * FOX_HEAD_DIM
MLA_HEADS = 4
MLA_Q_LORA = 256
MLA_KV_LORA = 128
MLA_NOPE = 64
MLA_ROPE = 32
MLA_V = 64
MLA_D = MLA_HEADS * MLA_V
ROPE_THETA = 10000.0
D_MIX = SSD_D + FOX_D + MLA_D
D_FF = 2816
ALPHA = (2 * DEPTH) ** 0.25
EPS = 1e-5

LANES = 128
FF_CHUNK = 256
N_FF_CHUNKS = D_FF // FF_CHUNK
N_OUT_CHUNKS = D_MODEL // FF_CHUNK
NEG_BIG = -1e30
VMEM_LIMIT = 56 * 1024 * 1024

C_Z = 0
C_XBC = C_Z + SSD_D
C_FQ = C_XBC + SSD_CONV_DIM
C_FK = C_FQ + FOX_HEADS * LANES
C_CQ = C_FK + FOX_HEADS * LANES
C_CKV = C_CQ + MLA_Q_LORA
C_KR = C_CKV + MLA_KV_LORA
C_KRR = C_KR + LANES
C_SMALL = C_KRR + LANES
N_IN_ARR = C_SMALL + LANES
SMALL_DT = 0
SMALL_F = 8
HEAD_W = LANES
FOX_BIAS_LANE = FOX_HEAD_DIM
MLA_MASK_LANE = MLA_NOPE + MLA_ROPE
M_INIT = 2 * NEG_BIG


def _sigmoid(x):
    return 1.0 / (1.0 + jnp.exp(-x))


def _softplus(x):
    return jnp.maximum(x, 0.0) + jnp.log(1.0 + jnp.exp(-jnp.abs(x)))


def _layer_norm_rows(y, g, b):
    mu = jnp.mean(y, axis=-1, keepdims=True)
    yc = y - mu
    var = jnp.mean(yc * yc, axis=-1, keepdims=True)
    return yc * lax.rsqrt(var + EPS) * g + b


def _split3(x):
    x1 = x.astype(BF16)
    r1 = x - x1.astype(F32)
    x2 = r1.astype(BF16)
    r2 = r1 - x2.astype(F32)
    return x1, x2, r2.astype(BF16)


def _dot(a, b):
    return jnp.dot(a, b, preferred_element_type=F32)


def _dot_nt(a, b):
    return lax.dot_general(a, b, (((1,), (1,)), ((), ())), preferred_element_type=F32)


def _ffn_ln_kernel(x_ref, wgu_ref, wd_ref, g_ref, b_ref, o_ref, a_scr, y_scr):
    xb = x_ref[...].astype(BF16)

    def gate_up(c, carry):
        hg = _dot(xb, wgu_ref[c])
        gate = hg[:, :FF_CHUNK]
        a_scr[c] = (gate * _sigmoid(gate) * hg[:, FF_CHUNK:]).astype(BF16)
        return carry

    lax.fori_loop(0, N_FF_CHUNKS, gate_up, 0)

    def down(n, carry):
        acc = _dot(a_scr[0], wd_ref[n, 0])
        for c in range(1, N_FF_CHUNKS):
            acc = acc + _dot(a_scr[c], wd_ref[n, c])
        y_scr[n] = acc
        return carry

    lax.fori_loop(0, N_OUT_CHUNKS, down, 0)

    y = jnp.concatenate([y_scr[n] for n in range(N_OUT_CHUNKS)], axis=-1)
    y = ALPHA * x_ref[...] + 0.5 * y
    o_ref[...] = _layer_norm_rows(y, g_ref[...], b_ref[...])


def _ffn_ln(h, wgu, wd4, g, b, tm):
    n_rows = h.shape[0]
    const = dict(pipeline_mode=pl.Buffered(1))
    return pl.pallas_call(
        _ffn_ln_kernel,
        grid=(n_rows // tm,),
        in_specs=[
            pl.BlockSpec((tm, D_MODEL), lambda i: (i, 0)),
            pl.BlockSpec(wgu.shape, lambda i: (0, 0, 0), **const),
            pl.BlockSpec(wd4.shape, lambda i: (0, 0, 0, 0), **const),
            pl.BlockSpec((1, D_MODEL), lambda i: (0, 0)),
            pl.BlockSpec((1, D_MODEL), lambda i: (0, 0)),
        ],
        out_specs=pl.BlockSpec((tm, D_MODEL), lambda i: (i, 0)),
        out_shape=jax.ShapeDtypeStruct((n_rows, D_MODEL), F32),
        scratch_shapes=[
            pltpu.VMEM((N_FF_CHUNKS, tm, FF_CHUNK), BF16),
            pltpu.VMEM((N_OUT_CHUNKS, tm, FF_CHUNK), F32),
        ],
        compiler_params=pltpu.CompilerParams(
            dimension_semantics=("arbitrary",), vmem_limit_bytes=VMEM_LIMIT),
        name="ffn_ln",
    )(h, wgu, wd4, g, b)


def _rms_rows(x, g):
    return x * lax.rsqrt(jnp.mean(x * x, axis=-1, keepdims=True) + EPS) * g


def _store_t_blocks(out_ref, val_t):
    for r in range(out_ref.shape[0]):
        out_ref[r] = val_t[:, r * BLOCK:(r + 1) * BLOCK].astype(out_ref.dtype)


def _proj_kernel(h_ref, win_ref, wfvt_ref, wuq_ref, wuqr_ref, wkk_ref, wkvt_ref, qg_ref, kvg_ref,
                 cos_ref, sin_ref, kadd_ref,
                 zx_ref, small_ref, fq_ref, fk_ref, fvt_ref, mq_ref, mk_ref, mvt_ref):
    hb = h_ref[...].astype(BF16)
    zx_ref[...] = _dot(hb, win_ref[:, C_Z:C_FQ])
    small_ref[...] = _dot(hb, win_ref[:, C_SMALL:N_IN_ARR])

    lane = lax.broadcasted_iota(jnp.int32, (1, LANES), 1)
    fox_one = jnp.where((lane >= FOX_BIAS_LANE) & (lane < FOX_BIAS_LANE + 3), 1.0, 0.0)
    mla_one = jnp.where(lane == MLA_MASK_LANE, 1.0, 0.0)

    fq = _dot(hb, win_ref[:, C_FQ:C_FK]) * (FOX_HEAD_DIM ** -0.5)
    for hd in range(FOX_HEADS):
        sl = slice(hd * HEAD_W, (hd + 1) * HEAD_W)
        fq_ref[:, sl] = (fq[:, sl] + fox_one).astype(BF16)
    fk_ref[...] = _dot(hb, win_ref[:, C_FK:C_CQ]).astype(BF16)
    _store_t_blocks(fvt_ref, _dot_nt(wfvt_ref[...], hb))

    cos = cos_ref[...]
    sin = sin_ref[...]
    cqn = _rms_rows(_dot(hb, win_ref[:, C_CQ:C_CKV]), qg_ref[...]).astype(BF16)
    q = _dot(cqn, wuq_ref[...])
    qr = _dot(cqn, wuqr_ref[...])
    kvn = _rms_rows(_dot(hb, win_ref[:, C_CKV:C_KR]), kvg_ref[...]).astype(BF16)
    kn = _dot(kvn, wkk_ref[...])
    _store_t_blocks(mvt_ref, _dot_nt(wkvt_ref[...], kvn))
    krope = (_dot(hb, win_ref[:, C_KR:C_KRR]) * cos
             + _dot(hb, win_ref[:, C_KRR:C_SMALL]) * sin + kadd_ref[...])
    scale = (MLA_NOPE + MLA_ROPE) ** -0.5
    for hd in range(MLA_HEADS):
        sl = slice(hd * HEAD_W, (hd + 1) * HEAD_W)
        mq_ref[:, sl] = ((q[:, sl] * cos + qr[:, sl] * sin) * scale + mla_one).astype(BF16)
        mk_ref[:, sl] = (kn[:, sl] + krope).astype(BF16)


def _proj(h, win, wfvt, wuq, wuqr, wkk, wkvt, qg, kvg, cos, sin, kadd, tm):
    n_rows = h.shape[0]
    const = dict(pipeline_mode=pl.Buffered(1))

    def rows(width):
        return pl.BlockSpec((tm, width), lambda i: (i, 0))

    def whole(a):
        return pl.BlockSpec(a.shape, lambda i: (0, 0), **const)

    t_spec = pl.BlockSpec((tm // BLOCK, FOX_D, BLOCK), lambda i: (i, 0, 0))
    t_shape = jax.ShapeDtypeStruct((n_rows // BLOCK, FOX_D, BLOCK), BF16)
    qk_w = FOX_HEADS * HEAD_W
    row_outs = [(C_FQ, F32), (LANES, F32), (qk_w, BF16), (qk_w, BF16)]
    return pl.pallas_call(
        _proj_kernel,
        grid=(n_rows // tm,),
        in_specs=[rows(D_MODEL), whole(win), whole(wfvt), whole(wuq), whole(wuqr), whole(wkk),
                  whole(wkvt), whole(qg), whole(kvg), rows(LANES), rows(LANES), rows(LANES)],
        out_specs=([rows(w) for w, _ in row_outs] + [t_spec, rows(qk_w), rows(qk_w), t_spec]),
        out_shape=([jax.ShapeDtypeStruct((n_rows, w), dt) for w, dt in row_outs]
                   + [t_shape, jax.ShapeDtypeStruct((n_rows, qk_w), BF16),
                      jax.ShapeDtypeStruct((n_rows, qk_w), BF16), t_shape]),
        compiler_params=pltpu.CompilerParams(
            dimension_semantics=("arbitrary",), vmem_limit_bytes=VMEM_LIMIT),
        name="in_proj",
    )(h, win, wfvt, wuq, wuqr, wkk, wkvt, qg, kvg, cos, sin, kadd)


def _ssd_kernel(pad, zx_ref, small_ref, cw_ref, cb_ref, dtb_ref, a_ref, dsk_ref, ng_ref,
                o_ref, conv_scr, s_scr):
    Q = BLOCK
    c = pl.program_id(1)
    row = lax.broadcasted_iota(jnp.int32, (Q, 1), 0)
    valid = jnp.logical_or(c > 0, row >= pad)
    lane = lax.broadcasted_iota(jnp.int32, (1, LANES), 1)
    lane_lo = lane < SSD_HEAD_DIM
    sub = lax.broadcasted_iota(jnp.int32, (LANES, 1), 0)

    @pl.when(c == 0)
    def _():
        conv_scr[0:8, :] = jnp.zeros((8, SSD_CONV_DIM), F32)
        s_scr[...] = jnp.zeros(s_scr.shape, F32)

    conv_scr[8:8 + Q, :] = jnp.where(valid, zx_ref[0, :, C_XBC:C_FQ], 0.0)
    acc = cb_ref[...]
    for k in range(SSD_CONV):
        off = 8 - (SSD_CONV - 1) + k
        acc = acc + cw_ref[k:k + 1, :] * conv_scr[off:off + Q, :]
    conv_scr[0:8, :] = conv_scr[Q:Q + 8, :]
    xbc = acc * _sigmoid(acc)
    bm = xbc[:, SSD_D:SSD_D + LANES]
    cm = xbc[:, SSD_D + LANES:SSD_D + 2 * LANES]

    dt = jnp.where(valid, _softplus(small_ref[0] + dtb_ref[...]), 0.0)
    a = dt * a_ref[...]
    tri = (lax.broadcasted_iota(jnp.int32, (Q, Q), 0)
           >= lax.broadcasted_iota(jnp.int32, (Q, Q), 1))
    tri_b = jnp.where(tri, 1.0, 0.0).astype(BF16)
    a1, a2, a3 = _split3(a)
    a_cum = _dot(tri_b, a1) + _dot(tri_b, a2) + _dot(tri_b, a3)
    a_cum_t = a_cum.T
    bm_t = bm.T

    cm_b = cm.astype(BF16)
    bm_b = bm.astype(BF16)
    cb_g = [_dot_nt(jnp.where(lane_lo, cm, 0.0).astype(BF16), bm_b),
            _dot_nt(jnp.where(lane_lo, 0.0, cm).astype(BF16), bm_b)]
    rows_g = [sub < SSD_STATE, sub >= SSD_STATE]

    pairs_per_group = SSD_HEADS // 2 // SSD_GROUPS
    y_pairs = []
    for p in range(SSD_HEADS // 2):
        g = p // pairs_per_group
        psl = slice(p * LANES, (p + 1) * LANES)
        xs_p = xbc[:, psl]
        dt_pair = jnp.where(lane_lo, dt[:, 2 * p:2 * p + 1], dt[:, 2 * p + 1:2 * p + 2])
        xdt = (xs_p * dt_pair).astype(BF16)
        s_old = s_scr[p]
        yd, upd, e_col, e_last = [], [], [], []
        for par in range(2):
            hd = 2 * p + par
            col = a_cum[:, hd:hd + 1]
            rowv = a_cum_t[hd:hd + 1, :]
            last = a_cum_t[hd:hd + 1, Q - 1:Q]
            seg = jnp.exp(jnp.where(tri, col - rowv, NEG_BIG))
            yd.append(_dot((cb_g[g] * seg).astype(BF16), xdt))
            upd.append(_dot((bm_t * jnp.exp(last - rowv)).astype(BF16), xdt))
            e_col.append(jnp.exp(col))
            e_last.append(jnp.exp(last))
        y_off = _dot(cm_b, s_old.astype(BF16)) * jnp.where(lane_lo, e_col[0], e_col[1])
        s_new = (jnp.where(lane_lo, e_last[0], e_last[1]) * s_old
                 + jnp.where(rows_g[g], jnp.where(lane_lo, upd[0], upd[1]), 0.0))
        s_scr[p] = s_new
        y_p = jnp.where(lane_lo, yd[0], yd[1]) + y_off + dsk_ref[:, psl] * xs_p
        z_p = zx_ref[0, :, psl]
        y_pairs.append(y_p * (z_p * _sigmoid(z_p)))

    for g in range(SSD_GROUPS):
        ps = range(g * pairs_per_group, (g + 1) * pairs_per_group)
        ss = sum(jnp.sum(y_pairs[p] * y_pairs[p], axis=-1, keepdims=True) for p in ps)
        inv = lax.rsqrt(ss * (1.0 / (pairs_per_group * LANES)) + EPS)
        for p in ps:
            psl = slice(p * LANES, (p + 1) * LANES)
            o_ref[0, :, psl] = (y_pairs[p] * inv * ng_ref[:, psl]).astype(BF16)


def _ssd(zx, small, cw, cb, dtb, a_row, dsk, ng, pad):
    bsz, lp, _ = zx.shape

    def whole(a):
        return pl.BlockSpec(a.shape, lambda b, c: (0, 0))

    return pl.pallas_call(
        functools.partial(_ssd_kernel, pad),
        grid=(bsz, lp // BLOCK),
        in_specs=[pl.BlockSpec((1, BLOCK, C_FQ), lambda b, c: (b, c, 0)),
                  pl.BlockSpec((1, BLOCK, LANES), lambda b, c: (b, c, 0)),
                  whole(cw), whole(cb), whole(dtb), whole(a_row), whole(dsk), whole(ng)],
        out_specs=pl.BlockSpec((1, BLOCK, SSD_D), lambda b, c: (b, c, 0)),
        out_shape=jax.ShapeDtypeStruct((bsz, lp, SSD_D), BF16),
        scratch_shapes=[pltpu.VMEM((BLOCK + 8, SSD_CONV_DIM), F32),
                        pltpu.VMEM((SSD_HEADS // 2, LANES, LANES), F32)],
        compiler_params=pltpu.CompilerParams(
            dimension_semantics=("arbitrary", "arbitrary"), vmem_limit_bytes=VMEM_LIMIT),
        name="ssd_mixer",
    )(zx, small, cw, cb, dtb, a_row, dsk, ng)


def _fox_keys_kernel(pad, k_ref, small_ref, fb_ref, o_ref):
    T = BLOCK
    n_blocks = k_ref.shape[1] // T
    width = k_ref.shape[2]
    row = lax.broadcasted_iota(jnp.int32, (T, 1), 0)
    tri = (lax.broadcasted_iota(jnp.int32, (T, T), 0)
           >= lax.broadcasted_iota(jnp.int32, (T, T), 1))
    tri_b = jnp.where(tri, 1.0, 0.0).astype(BF16)
    src = lax.broadcasted_iota(jnp.int32, (LANES, width), 0)
    dst = lax.broadcasted_iota(jnp.int32, (LANES, width), 1)
    dst_head = jnp.right_shift(dst, HEAD_W.bit_length() - 1)
    dst_lane = jnp.bitwise_and(dst, HEAD_W - 1)
    sel = [jnp.where((src == SMALL_F + dst_head) & (dst_lane == FOX_BIAS_LANE + i),
                     1.0, 0.0).astype(BF16) for i in range(3)]
    carry = jnp.zeros((1, LANES), F32)
    for j in range(n_blocks):
        rows = slice(j * T, (j + 1) * T)
        log_f = -_softplus(-(small_ref[0, rows, :] + fb_ref[...]))
        if j == 0:
            log_f = jnp.where(row < pad, 0.0, log_f)
        f1, f2, f3 = _split3(log_f)
        c_blk = carry + _dot(tri_b, f1) + _dot(tri_b, f2) + _dot(tri_b, f3)
        carry = c_blk[T - 1:T, :]
        bias = -c_blk
        if j == 0:
            bias = jnp.where(row < pad, NEG_BIG, bias)
        b1, b2, b3 = _split3(bias)
        placed = _dot(b1, sel[0]) + _dot(b2, sel[1]) + _dot(b3, sel[2])
        o_ref[0, rows, :] = (k_ref[0, rows, :].astype(F32) + placed).astype(BF16)


def _fox_keys(k, small, fb, pad):
    bsz, lp, width = k.shape
    return pl.pallas_call(
        functools.partial(_fox_keys_kernel, pad),
        grid=(bsz,),
        in_specs=[pl.BlockSpec((1, lp, width), lambda b: (b, 0, 0)),
                  pl.BlockSpec((1, lp, LANES), lambda b: (b, 0, 0)),
                  pl.BlockSpec(fb.shape, lambda b: (0, 0))],
        out_specs=pl.BlockSpec((1, lp, width), lambda b: (b, 0, 0)),
        out_shape=jax.ShapeDtypeStruct(k.shape, BF16),
        compiler_params=pltpu.CompilerParams(
            dimension_semantics=("arbitrary",), vmem_limit_bytes=VMEM_LIMIT),
        name="fox_keys",
    )(k, small, fb)


def _attn_kernel(n_heads, q_ref, k_ref, vt_ref, o_ref, acc_scr):
    T = BLOCK
    qi = pl.program_id(1)
    dv = acc_scr.shape[1]
    heads = range(n_heads)
    qs = [q_ref[0, :, hd * HEAD_W:(hd + 1) * HEAD_W] for hd in heads]
    acc_scr[...] = jnp.zeros(acc_scr.shape, F32)

    def chunk(j0, nblk, carry, masked):
        ms, ls = carry
        r0 = pl.multiple_of(j0 * T, T)
        ss = [_dot_nt(k_ref[0, pl.ds(r0, nblk * T), hd * HEAD_W:(hd + 1) * HEAD_W], qs[hd])
              for hd in heads]
        if masked:
            ahead = (lax.broadcasted_iota(jnp.int32, (nblk * T, T), 0)
                     - lax.broadcasted_iota(jnp.int32, (nblk * T, T), 1))
            causal = ahead <= (qi - j0) * T
            ss = [jnp.where(causal, s, NEG_BIG) for s in ss]
        new_ms = [jnp.maximum(ms[hd], jnp.max(ss[hd], axis=0, keepdims=True)) for hd in heads]
        corrs = [jnp.exp(ms[hd] - new_ms[hd]) for hd in heads]
        prs = [jnp.exp(ss[hd] - new_ms[hd]) for hd in heads]
        new_ls = [corrs[hd] * ls[hd] + jnp.sum(prs[hd], axis=0, keepdims=True) for hd in heads]
        for hd in heads:
            vt = jnp.concatenate([vt_ref[j0 + b, hd * dv:(hd + 1) * dv, :] for b in range(nblk)],
                                 axis=1)
            acc_scr[hd] = corrs[hd] * acc_scr[hd] + _dot(vt, prs[hd].astype(BF16))
        return tuple(new_ms), tuple(new_ls)

    init = (tuple(jnp.full((1, T), M_INIT, F32) for _ in heads),
            tuple(jnp.zeros((1, T), F32) for _ in heads))
    carry = lax.fori_loop(0, qi // 2, lambda p, c: chunk(2 * p, 2, c, False), init)
    _, ls = lax.cond(qi % 2 == 1,
                     lambda c: chunk(qi - 1, 2, c, True),
                     lambda c: chunk(qi, 1, c, True), carry)
    for p in range(n_heads // 2):
        y_t = jnp.concatenate([acc_scr[2 * p] * (1.0 / ls[2 * p]),
                               acc_scr[2 * p + 1] * (1.0 / ls[2 * p + 1])], axis=0)
        o_ref[0, :, p * LANES:(p + 1) * LANES] = y_t.T.astype(BF16)


def _attention(q, k, vt, n_heads, name):
    bsz, lp, qw = q.shape
    n_blk = lp // BLOCK
    dv = vt.shape[1] // n_heads
    return pl.pallas_call(
        functools.partial(_attn_kernel, n_heads),
        grid=(bsz, n_blk),
        in_specs=[pl.BlockSpec((1, BLOCK, qw), lambda b, i: (b, i, 0)),
                  pl.BlockSpec((1, lp, qw), lambda b, i: (b, 0, 0)),
                  pl.BlockSpec((n_blk,) + vt.shape[1:], lambda b, i: (b, 0, 0))],
        out_specs=pl.BlockSpec((1, BLOCK, n_heads * dv), lambda b, i: (b, i, 0)),
        out_shape=jax.ShapeDtypeStruct((bsz, lp, n_heads * dv), BF16),
        scratch_shapes=[pltpu.VMEM((n_heads, dv, BLOCK), F32)],
        compiler_params=pltpu.CompilerParams(
            dimension_semantics=("arbitrary", "arbitrary"), vmem_limit_bytes=VMEM_LIMIT),
        name=name,
    )(q, k, vt)


def _out_ln_kernel(h_ref, ys_ref, yf_ref, ym_ref, w_ref, g_ref, b_ref, o_ref):
    mix = (_dot(ys_ref[...], w_ref[0:SSD_D, :])
           + _dot(yf_ref[...], w_ref[SSD_D:SSD_D + FOX_D, :])
           + _dot(ym_ref[...], w_ref[SSD_D + FOX_D:D_MIX, :]))
    o_ref[...] = _layer_norm_rows(ALPHA * h_ref[...] + mix, g_ref[...], b_ref[...])


def _out_ln(h, ys, yf, ym, w, g, b, tm):
    n_rows = h.shape[0]

    def rows(width):
        return pl.BlockSpec((tm, width), lambda i: (i, 0))

    return pl.pallas_call(
        _out_ln_kernel,
        grid=(n_rows // tm,),
        in_specs=[rows(D_MODEL), rows(SSD_D), rows(FOX_D), rows(MLA_D),
                  pl.BlockSpec(w.shape, lambda i: (0, 0), pipeline_mode=pl.Buffered(1)),
                  pl.BlockSpec((1, D_MODEL), lambda i: (0, 0)),
                  pl.BlockSpec((1, D_MODEL), lambda i: (0, 0))],
        out_specs=rows(D_MODEL),
        out_shape=jax.ShapeDtypeStruct((n_rows, D_MODEL), F32),
        compiler_params=pltpu.CompilerParams(
            dimension_semantics=("arbitrary",), vmem_limit_bytes=VMEM_LIMIT),
        name="out_proj_ln",
    )(h, ys, yf, ym, w, g, b)


def _ffn_weights(w_gate, w_up, w_down):
    wg = w_gate.reshape(D_MODEL, N_FF_CHUNKS, FF_CHUNK)
    wu = w_up.reshape(D_MODEL, N_FF_CHUNKS, FF_CHUNK)
    wgu = jnp.concatenate([wg, wu], axis=-1).transpose(1, 0, 2).astype(BF16)
    wd4 = (w_down.reshape(N_FF_CHUNKS, FF_CHUNK, N_OUT_CHUNKS, FF_CHUNK)
           .transpose(2, 0, 1, 3).astype(BF16))
    return wgu, wd4


def _rot_half_cols(w):
    half = MLA_ROPE // 2
    return jnp.concatenate([-w[..., half:], w[..., :half]], axis=-1)


def _in_proj_weights(w_in):
    sizes = [SSD_D, SSD_CONV_DIM, SSD_HEADS, FOX_D, FOX_D, FOX_D, FOX_HEADS,
             MLA_Q_LORA, MLA_KV_LORA, MLA_ROPE]
    splits = [int(s) for s in np.cumsum(sizes)[:-1]]
    z, xbc, dt, fq, fk, fv, fr, cq, ckv, kr = jnp.split(w_in, splits, axis=-1)
    zeros = lambda n: jnp.zeros((D_MODEL, n), w_in.dtype)
    rope_pad = LANES - MLA_NOPE - MLA_ROPE
    kr128 = jnp.concatenate([zeros(MLA_NOPE), kr, zeros(rope_pad)], axis=-1)
    krr128 = jnp.concatenate([zeros(MLA_NOPE), _rot_half_cols(kr), zeros(rope_pad)], axis=-1)
    small = jnp.concatenate([dt, fr, zeros(LANES - SMALL_F - FOX_HEADS)], axis=-1)

    def per_head(w):
        w = w.reshape(D_MODEL, FOX_HEADS, FOX_HEAD_DIM)
        return jnp.pad(w, ((0, 0), (0, 0), (0, HEAD_W - FOX_HEAD_DIM))).reshape(D_MODEL, -1)

    win = jnp.concatenate([z, xbc, per_head(fq), per_head(fk), cq, ckv, kr128, krr128, small],
                          axis=-1).astype(BF16)
    return win, fv.T.astype(BF16)


def _mla_weights(w_uq, w_ukv):
    rope_pad = LANES - MLA_NOPE - MLA_ROPE
    wq = w_uq.reshape(MLA_Q_LORA, MLA_HEADS, MLA_NOPE + MLA_ROPE)
    zq = jnp.zeros((MLA_Q_LORA, MLA_HEADS, rope_pad), w_uq.dtype)
    wuq = jnp.concatenate([wq, zq], axis=-1).reshape(MLA_Q_LORA, MLA_HEADS * LANES)
    wuqr = jnp.concatenate([jnp.zeros_like(wq[..., :MLA_NOPE]),
                            _rot_half_cols(wq[..., MLA_NOPE:]), zq],
                           axis=-1).reshape(MLA_Q_LORA, MLA_HEADS * LANES)
    wkv = w_ukv.reshape(MLA_KV_LORA, MLA_HEADS, MLA_NOPE + MLA_V)
    wkk = jnp.concatenate([wkv[..., :MLA_NOPE], jnp.zeros_like(wkv[..., MLA_NOPE:])],
                          axis=-1).reshape(MLA_KV_LORA, MLA_HEADS * LANES)
    wvt = wkv[..., MLA_NOPE:].reshape(MLA_KV_LORA, MLA_D).T
    return wuq.astype(BF16), wuqr.astype(BF16), wkk.astype(BF16), wvt.astype(BF16)


def _lane_row(v, width=LANES):
    v = v.astype(F32)
    return jnp.pad(v, (0, width - v.shape[0]))[None, :]


def _rope_tables(lp, pad, bsz):
    pos = jnp.arange(lp, dtype=F32) - pad
    inv_freq = 1.0 / (ROPE_THETA ** (jnp.arange(0, MLA_ROPE, 2, dtype=F32) / MLA_ROPE))
    ang = pos[:, None] * inv_freq[None, :]
    cos, sin = jnp.cos(ang), jnp.sin(ang)
    rope_pad = LANES - MLA_NOPE - MLA_ROPE
    cos128 = jnp.concatenate([jnp.ones((lp, MLA_NOPE), F32), cos, cos,
                              jnp.zeros((lp, rope_pad), F32)], axis=-1)
    sin128 = jnp.concatenate([jnp.zeros((lp, MLA_NOPE), F32), sin, sin,
                              jnp.zeros((lp, rope_pad), F32)], axis=-1)
    kadd = jnp.where((jnp.arange(lp)[:, None] < pad) & (jnp.arange(LANES)[None, :] == MLA_MASK_LANE),
                     NEG_BIG, 0.0).astype(F32)
    return tuple(jnp.tile(t, (bsz, 1)) for t in (cos128, sin128, kadd))


def _row_tile(n_rows, target):
    tm = target
    while n_rows % tm:
        tm //= 2
    return tm


def kernel(x, meta, ffn1_w_gate, ffn1_w_up, ffn1_w_down, ln1_g, ln1_b, w_in, conv_w, conv_b, dt_bias, a_log, d_skip, ssd_norm_g, fox_f_b, mla_q_norm_g, mla_w_uq, mla_kv_norm_g, mla_w_ukv, w_out, ln2_g, ln2_b, ffn2_w_gate, ffn2_w_up, ffn2_w_down, ln3_g, ln3_b):
    bsz, seq, _ = x.shape
    assert seq % BLOCK == 0
    pad = (-N_META) % BLOCK
    lp = pad + N_META + seq
    n_rows = bsz * lp
    tm = _row_tile(n_rows, 512)

    head = jnp.concatenate([jnp.zeros((pad, D_MODEL), x.dtype), meta.astype(x.dtype)], axis=0)
    h = jnp.concatenate([jnp.broadcast_to(head[None], (bsz, pad + N_META, D_MODEL)), x], axis=1)
    h = h.reshape(n_rows, D_MODEL)
    cos, sin, kadd = _rope_tables(lp, pad, bsz)
    row = lambda v: v.astype(F32)[None, :]

    for l in range(DEPTH):
        wgu, wd4 = _ffn_weights(ffn1_w_gate[l], ffn1_w_up[l], ffn1_w_down[l])
        h = _ffn_ln(h, wgu, wd4, row(ln1_g[l]), row(ln1_b[l]), tm)

        win, wfvt = _in_proj_weights(w_in[l])
        wuq, wuqr, wkk, wkvt = _mla_weights(mla_w_uq[l], mla_w_ukv[l])
        zx, small, fq, fk, fvt, mq, mk, mvt = _proj(
            h, win, wfvt, wuq, wuqr, wkk, wkvt, row(mla_q_norm_g[l]), row(mla_kv_norm_g[l]),
            cos, sin, kadd, tm)
        b3 = lambda a: a.reshape(bsz, lp, a.shape[-1])

        a_row = _lane_row(-jnp.exp(a_log[l].astype(F32)))
        dsk = jnp.repeat(d_skip[l].astype(F32), SSD_HEAD_DIM)[None, :]
        y_ssd = _ssd(b3(zx), b3(small), conv_w[l].astype(F32), row(conv_b[l]),
                     _lane_row(dt_bias[l]), a_row, dsk, row(ssd_norm_g[l]), pad)

        fb = jnp.pad(fox_f_b[l].astype(F32), (SMALL_F, LANES - SMALL_F - FOX_HEADS))[None, :]
        fk_aug = _fox_keys(b3(fk), b3(small), fb, pad)
        y_fox = _attention(b3(fq), fk_aug, fvt, FOX_HEADS, "fox_attention")
        y_mla = _attention(b3(mq), b3(mk), mvt, MLA_HEADS, "mla_attention")

        flat = lambda a: a.reshape(n_rows, a.shape[-1])
        h = _out_ln(h, flat(y_ssd), flat(y_fox), flat(y_mla), w_out[l].astype(BF16),
                    row(ln2_g[l]), row(ln2_b[l]), tm)

        wgu, wd4 = _ffn_weights(ffn2_w_gate[l], ffn2_w_up[l], ffn2_w_down[l])
        h = _ffn_ln(h, wgu, wd4, row(ln3_g[l]), row(ln3_b[l]), tm)

    return h.reshape(bsz, lp, D_MODEL)[:, pad + N_META:]
```

```python
import functools

import numpy as np
import jax
import jax.numpy as jnp
from jax import lax
from jax.experimental import pallas as pl
from jax.experimental.pallas import tpu as pltpu

F32 = jnp.float32
BF16 = jnp.bfloat16

D_MODEL = 1024
DEPTH = 2
N_META = 16
BLOCK = 128
SSD_HEADS = 8
SSD_HEAD_DIM = 64
SSD_D = SSD_HEADS * SSD_HEAD_DIM
SSD_GROUPS = 2
SSD_STATE = 64
SSD_CONV = 4
SSD_CONV_DIM = SSD_D + 2 * SSD_GROUPS * SSD_STATE
FOX_HEADS = 4
FOX_HEAD_DIM = 64
FOX_D = FOX_HEADS * FOX_HEAD_DIM
MLA_HEADS = 4
MLA_Q_LORA = 256
MLA_KV_LORA = 128
MLA_NOPE = 64
MLA_ROPE = 32
MLA_V = 64
MLA_D = MLA_HEADS * MLA_V
ROPE_THETA = 10000.0
D_MIX = SSD_D + FOX_D + MLA_D
D_FF = 2816
ALPHA = (2 * DEPTH) ** 0.25
EPS = 1e-5

LANES = 128
MXU_W = 256
FF_CHUNK = MXU_W
N_FF_CHUNKS = D_FF // FF_CHUNK
N_OUT_CHUNKS = D_MODEL // FF_CHUNK
TQ = 2 * BLOCK
PAD = BLOCK - N_META
NEG_BIG = -1e30
M_INIT = 2 * NEG_BIG
VMEM_LIMIT = 56 * 1024 * 1024

C_Z = 0
C_XBC = C_Z + SSD_D
C_FQ = C_XBC + SSD_CONV_DIM
C_FK = C_FQ + FOX_HEADS * LANES
C_CQ = C_FK + FOX_HEADS * LANES
C_CKV = C_CQ + MLA_Q_LORA
C_KR = C_CKV + MLA_KV_LORA
C_KRR = C_KR + LANES
C_SMALL = C_KRR + LANES
N_IN_ARR = C_SMALL + LANES
SMALL_DT = 0
SMALL_F = 8
HEAD_W = LANES
FOX_BIAS_LANE = FOX_HEAD_DIM
MLA_MASK_LANE = MLA_NOPE + MLA_ROPE


def _sigmoid(x):
    return 1.0 / (1.0 + jnp.exp(-x))


def _softplus(x):
    return jnp.maximum(x, 0.0) + jnp.log(1.0 + jnp.exp(-jnp.abs(x)))


def _layer_norm_rows(y, g, b):
    mu = jnp.mean(y, axis=-1, keepdims=True)
    yc = y - mu
    var = jnp.mean(yc * yc, axis=-1, keepdims=True)
    return yc * lax.rsqrt(var + EPS) * g + b


def _split3(x):
    x1 = x.astype(BF16)
    r1 = x - x1.astype(F32)
    x2 = r1.astype(BF16)
    r2 = r1 - x2.astype(F32)
    return x1, x2, r2.astype(BF16)


def _dot(a, b):
    return jnp.dot(a, b, preferred_element_type=F32)


def _dot_nt(a, b):
    return lax.dot_general(a, b, (((1,), (1,)), ((), ())), preferred_element_type=F32)


def _resident(shape):
    return pl.BlockSpec(shape, lambda *_: (0,) * len(shape), pipeline_mode=pl.Buffered(1))


def _ffn_ln_kernel(n_main_tiles, *refs):
    if n_main_tiles is None:
        x_ref, wg_ref, wu_ref, wd_ref, g_ref, b_ref, o_ref, a_scr, y_scr = refs
        x = x_ref[...]
    else:
        x_ref, m_ref, wg_ref, wu_ref, wd_ref, g_ref, b_ref, o_ref, a_scr, y_scr = refs
        x = jnp.where(pl.program_id(0) < n_main_tiles, x_ref[...], m_ref[...])
    xb = x.astype(BF16)

    def gate_up(c, carry):
        cols = pl.ds(pl.multiple_of(c * FF_CHUNK, FF_CHUNK), FF_CHUNK)
        gate = _dot(xb, wg_ref[:, cols])
        up = _dot(xb, wu_ref[:, cols])
        a_scr[c] = (gate * _sigmoid(gate) * up).astype(BF16)
        return carry

    lax.fori_loop(0, N_FF_CHUNKS, gate_up, 0)

    for n in range(N_OUT_CHUNKS):
        cols = slice(n * FF_CHUNK, (n + 1) * FF_CHUNK)
        acc = _dot(a_scr[0], wd_ref[0:FF_CHUNK, cols])
        for c in range(1, N_FF_CHUNKS):
            acc = acc + _dot(a_scr[c], wd_ref[c * FF_CHUNK:(c + 1) * FF_CHUNK, cols])
        y_scr[:, cols] = ALPHA * x[:, cols] + 0.5 * acc
    o_ref[...] = _layer_norm_rows(y_scr[...], g_ref[...], b_ref[...])


def _ffn_ln(h, meta_tile, wg, wu, wd, g, b, tm, n_out_rows):
    n_main_tiles = h.shape[0] // tm
    rows = pl.BlockSpec((tm, D_MODEL), lambda i: (i, 0))
    if meta_tile is None:
        x_specs, x_args, n_main = [rows], [h], None
    else:
        x_specs = [pl.BlockSpec((tm, D_MODEL), lambda i: (jnp.minimum(i, n_main_tiles - 1), 0)),
                   _resident(meta_tile.shape)]
        x_args, n_main = [h, meta_tile], n_main_tiles
    return pl.pallas_call(
        functools.partial(_ffn_ln_kernel, n_main),
        grid=(n_out_rows // tm,),
        in_specs=x_specs + [_resident(wg.shape), _resident(wu.shape), _resident(wd.shape),
                            _resident(g.shape), _resident(b.shape)],
        out_specs=rows,
        out_shape=jax.ShapeDtypeStruct((n_out_rows, D_MODEL), F32),
        scratch_shapes=[pltpu.VMEM((N_FF_CHUNKS, tm, FF_CHUNK), BF16),
                        pltpu.VMEM((tm, D_MODEL), F32)],
        compiler_params=pltpu.CompilerParams(
            dimension_semantics=("arbitrary",), vmem_limit_bytes=VMEM_LIMIT),
        name="ffn_ln",
    )(*x_args, wg, wu, wd, g, b)


def _rms_rows(x, g):
    return x * lax.rsqrt(jnp.mean(x * x, axis=-1, keepdims=True) + EPS) * g


def _store_t_blocks(out_ref, val_t):
    for r in range(out_ref.shape[0]):
        out_ref[r] = val_t[:, r * BLOCK:(r + 1) * BLOCK].astype(out_ref.dtype)


def _proj_kernel(h_ref, win_ref, wfvt_ref, wuq_ref, wuqr_ref, wkk_ref, wkvt_ref, qg_ref, kvg_ref,
                 cos_ref, sin_ref, kadd_ref,
                 zx_ref, small_ref, fq_ref, fk_ref, fvt_ref, mq_ref, mk_ref, mvt_ref):
    hb = h_ref[...].astype(BF16)
    zx_ref[...] = _dot(hb, win_ref[:, C_Z:C_FQ])
    small_ref[...] = _dot(hb, win_ref[:, C_SMALL:N_IN_ARR])

    lane = lax.broadcasted_iota(jnp.int32, (1, LANES), 1)
    fox_one = jnp.where((lane >= FOX_BIAS_LANE) & (lane < FOX_BIAS_LANE + 3), 1.0, 0.0)
    mla_one = jnp.where(lane == MLA_MASK_LANE, 1.0, 0.0)

    fq = _dot(hb, win_ref[:, C_FQ:C_FK]) * (FOX_HEAD_DIM ** -0.5)
    for hd in range(FOX_HEADS):
        sl = slice(hd * HEAD_W, (hd + 1) * HEAD_W)
        fq_ref[:, sl] = (fq[:, sl] + fox_one).astype(BF16)
    fk_ref[...] = _dot(hb, win_ref[:, C_FK:C_CQ]).astype(BF16)
    _store_t_blocks(fvt_ref, _dot_nt(wfvt_ref[...], hb))

    cos = cos_ref[...]
    sin = sin_ref[...]
    cqn = _rms_rows(_dot(hb, win_ref[:, C_CQ:C_CKV]), qg_ref[...]).astype(BF16)
    q = _dot(cqn, wuq_ref[...])
    qr = _dot(cqn, wuqr_ref[...])
    kvn = _rms_rows(_dot(hb, win_ref[:, C_CKV:C_KR]), kvg_ref[...]).astype(BF16)
    kn = _dot(kvn, wkk_ref[...])
    _store_t_blocks(mvt_ref, _dot_nt(wkvt_ref[...], kvn))
    krope = (_dot(hb, win_ref[:, C_KR:C_KRR]) * cos
             + _dot(hb, win_ref[:, C_KRR:C_SMALL]) * sin + kadd_ref[...])
    scale = (MLA_NOPE + MLA_ROPE) ** -0.5
    for hd in range(MLA_HEADS):
        sl = slice(hd * HEAD_W, (hd + 1) * HEAD_W)
        mq_ref[:, sl] = ((q[:, sl] * cos + qr[:, sl] * sin) * scale + mla_one).astype(BF16)
        mk_ref[:, sl] = (kn[:, sl] + krope).astype(BF16)


def _proj(h, win, wfvt, wuq, wuqr, wkk, wkvt, qg, kvg, tables, tm, seq):
    n_rows = h.shape[0]
    n_main_tiles = n_rows // tm - 1
    tiles_per_seq = seq // tm

    def rows(width):
        return pl.BlockSpec((tm, width), lambda i: (i, 0))

    tab = pl.BlockSpec((tm, LANES), lambda i: (
        jnp.where(i < n_main_tiles, i % tiles_per_seq, tiles_per_seq), 0))
    t_spec = pl.BlockSpec((tm // BLOCK, FOX_D, BLOCK), lambda i: (i, 0, 0))
    t_shape = jax.ShapeDtypeStruct((n_rows // BLOCK, FOX_D, BLOCK), BF16)
    qk_w = FOX_HEADS * HEAD_W
    row_outs = [(C_FQ, F32), (LANES, F32), (qk_w, BF16), (qk_w, BF16)]
    weights = [win, wfvt, wuq, wuqr, wkk, wkvt, qg, kvg]
    return pl.pallas_call(
        _proj_kernel,
        grid=(n_rows // tm,),
        in_specs=[rows(D_MODEL)] + [_resident(w.shape) for w in weights] + [tab, tab, tab],
        out_specs=([rows(w) for w, _ in row_outs] + [t_spec, rows(qk_w), rows(qk_w), t_spec]),
        out_shape=([jax.ShapeDtypeStruct((n_rows, w), dt) for w, dt in row_outs]
                   + [t_shape, jax.ShapeDtypeStruct((n_rows, qk_w), BF16),
                      jax.ShapeDtypeStruct((n_rows, qk_w), BF16), t_shape]),
        compiler_params=pltpu.CompilerParams(
            dimension_semantics=("arbitrary",), vmem_limit_bytes=VMEM_LIMIT),
        name="in_proj",
    )(h, *weights, *tables)


def _ssd_kernel(zx_ref, small_ref, cw_ref, cb_ref, dtb_ref, a_ref, dsk_ref, ng_ref,
                o_ref, conv_scr, s_scr, meta_conv_scr, meta_s_scr):
    b = pl.program_id(0)
    c = pl.program_id(1)

    @pl.when((c == 0) & (b == 0))
    def _():
        conv_scr[0:8, :] = jnp.zeros((8, SSD_CONV_DIM), F32)
        s_scr[...] = jnp.zeros(s_scr.shape, F32)

    @pl.when((c == 0) & (b > 0))
    def _():
        conv_scr[0:8, :] = meta_conv_scr[...]
        s_scr[...] = meta_s_scr[...]

    @pl.when((c > 0) | (b == 0))
    def _():
        _ssd_chunk(c, zx_ref, small_ref, cw_ref, cb_ref, dtb_ref, a_ref, dsk_ref, ng_ref,
                   o_ref, conv_scr, s_scr)

    @pl.when((c == 0) & (b == 0))
    def _():
        meta_conv_scr[...] = conv_scr[0:8, :]
        meta_s_scr[...] = s_scr[...]


def _ssd_chunk(c, zx_ref, small_ref, cw_ref, cb_ref, dtb_ref, a_ref, dsk_ref, ng_ref,
               o_ref, conv_scr, s_scr):
    Q = BLOCK
    row = lax.broadcasted_iota(jnp.int32, (Q, 1), 0)
    valid = jnp.logical_or(c > 0, row >= PAD)
    lane = lax.broadcasted_iota(jnp.int32, (1, LANES), 1)
    lane_lo = lane < SSD_HEAD_DIM
    sub = lax.broadcasted_iota(jnp.int32, (LANES, 1), 0)

    conv_scr[8:8 + Q, :] = jnp.where(valid, zx_ref[:, C_XBC:C_FQ], 0.0)
    acc = cb_ref[...]
    for k in range(SSD_CONV):
        off = 8 - (SSD_CONV - 1) + k
        acc = acc + cw_ref[k:k + 1, :] * conv_scr[off:off + Q, :]
    conv_scr[0:8, :] = conv_scr[Q:Q + 8, :]
    xbc = acc * _sigmoid(acc)
    bm = xbc[:, SSD_D:SSD_D + LANES]
    cm = xbc[:, SSD_D + LANES:SSD_D + 2 * LANES]

    dt = jnp.where(valid, _softplus(small_ref[...] + dtb_ref[...]), 0.0)
    a = dt * a_ref[...]
    tri = (lax.broadcasted_iota(jnp.int32, (Q, Q), 0)
           >= lax.broadcasted_iota(jnp.int32, (Q, Q), 1))
    tri_b = jnp.where(tri, 1.0, 0.0).astype(BF16)
    a1, a2, a3 = _split3(a)
    a_cum = _dot(tri_b, a1) + _dot(tri_b, a2) + _dot(tri_b, a3)
    a_cum_t = a_cum.T
    bm_t = bm.T

    cm_b = cm.astype(BF16)
    bm_b = bm.astype(BF16)
    cb_g = [_dot_nt(jnp.where(lane_lo, cm, 0.0).astype(BF16), bm_b),
            _dot_nt(jnp.where(lane_lo, 0.0, cm).astype(BF16), bm_b)]
    rows_g = [sub < SSD_STATE, sub >= SSD_STATE]

    pairs_per_group = SSD_HEADS // 2 // SSD_GROUPS
    y_pairs = []
    for p in range(SSD_HEADS // 2):
        g = p // pairs_per_group
        psl = slice(p * LANES, (p + 1) * LANES)
        xs_p = xbc[:, psl]
        dt_pair = jnp.where(lane_lo, dt[:, 2 * p:2 * p + 1], dt[:, 2 * p + 1:2 * p + 2])
        xdt = (xs_p * dt_pair).astype(BF16)
        s_old = s_scr[p]
        yd, upd, e_col, e_last = [], [], [], []
        for par in range(2):
            hd = 2 * p + par
            col = a_cum[:, hd:hd + 1]
            rowv = a_cum_t[hd:hd + 1, :]
            last = a_cum_t[hd:hd + 1, Q - 1:Q]
            seg = jnp.exp(jnp.where(tri, col - rowv, NEG_BIG))
            yd.append(_dot((cb_g[g] * seg).astype(BF16), xdt))
            upd.append(_dot((bm_t * jnp.exp(last - rowv)).astype(BF16), xdt))
            e_col.append(jnp.exp(col))
            e_last.append(jnp.exp(last))
        y_off = _dot(cm_b, s_old.astype(BF16)) * jnp.where(lane_lo, e_col[0], e_col[1])
        s_new = (jnp.where(lane_lo, e_last[0], e_last[1]) * s_old
                 + jnp.where(rows_g[g], jnp.where(lane_lo, upd[0], upd[1]), 0.0))
        s_scr[p] = s_new
        y_p = jnp.where(lane_lo, yd[0], yd[1]) + y_off + dsk_ref[:, psl] * xs_p
        z_p = zx_ref[:, psl]
        y_pairs.append(y_p * (z_p * _sigmoid(z_p)))

    for g in range(SSD_GROUPS):
        ps = range(g * pairs_per_group, (g + 1) * pairs_per_group)
        ss = sum(jnp.sum(y_pairs[p] * y_pairs[p], axis=-1, keepdims=True) for p in ps)
        inv = lax.rsqrt(ss * (1.0 / (pairs_per_group * LANES)) + EPS)
        for p in ps:
            psl = slice(p * LANES, (p + 1) * LANES)
            o_ref[:, psl] = (y_pairs[p] * inv * ng_ref[:, psl]).astype(BF16)


def _ssd(zx, small, cw, cb, dtb, a_row, dsk, ng, bsz, seq):
    n_rows = zx.shape[0]
    blocks_per_seq = seq // BLOCK
    meta_block = n_rows // BLOCK - 1

    def block(b, c):
        first = jnp.where(b == 0, meta_block, b * blocks_per_seq)
        return (jnp.where(c == 0, first, b * blocks_per_seq + c - 1), 0)

    params = [cw, cb, dtb, a_row, dsk, ng]
    return pl.pallas_call(
        _ssd_kernel,
        grid=(bsz, blocks_per_seq + 1),
        in_specs=[pl.BlockSpec((BLOCK, C_FQ), block), pl.BlockSpec((BLOCK, LANES), block)]
                 + [_resident(p.shape) for p in params],
        out_specs=pl.BlockSpec((BLOCK, SSD_D), block),
        out_shape=jax.ShapeDtypeStruct((n_rows, SSD_D), BF16),
        scratch_shapes=[pltpu.VMEM((BLOCK + 8, SSD_CONV_DIM), F32),
                        pltpu.VMEM((SSD_HEADS // 2, LANES, LANES), F32),
                        pltpu.VMEM((8, SSD_CONV_DIM), F32),
                        pltpu.VMEM((SSD_HEADS // 2, LANES, LANES), F32)],
        compiler_params=pltpu.CompilerParams(
            dimension_semantics=("arbitrary", "arbitrary"), vmem_limit_bytes=VMEM_LIMIT),
        name="ssd_mixer",
    )(zx, small, *params)


def _fox_keys_kernel(km_ref, k_ref, sm_ref, s_ref, fb_ref, om_ref, o_ref):
    T = BLOCK
    width = k_ref.shape[1]
    row = lax.broadcasted_iota(jnp.int32, (T, 1), 0)
    tri = (lax.broadcasted_iota(jnp.int32, (T, T), 0)
           >= lax.broadcasted_iota(jnp.int32, (T, T), 1))
    tri_b = jnp.where(tri, 1.0, 0.0).astype(BF16)
    src = lax.broadcasted_iota(jnp.int32, (LANES, width), 0)
    dst = lax.broadcasted_iota(jnp.int32, (LANES, width), 1)
    dst_head = jnp.right_shift(dst, HEAD_W.bit_length() - 1)
    dst_lane = jnp.bitwise_and(dst, HEAD_W - 1)
    sel = [jnp.where((src == SMALL_F + dst_head) & (dst_lane == FOX_BIAS_LANE + i),
                     1.0, 0.0).astype(BF16) for i in range(3)]

    def block(carry, k_blk, small_blk, is_meta):
        log_f = -_softplus(-(small_blk + fb_ref[...]))
        if is_meta:
            log_f = jnp.where(row < PAD, 0.0, log_f)
        f1, f2, f3 = _split3(log_f)
        c_blk = carry + _dot(tri_b, f1) + _dot(tri_b, f2) + _dot(tri_b, f3)
        bias = -c_blk
        if is_meta:
            bias = jnp.where(row < PAD, NEG_BIG, bias)
        b1, b2, b3 = _split3(bias)
        placed = _dot(b1, sel[0]) + _dot(b2, sel[1]) + _dot(b3, sel[2])
        return c_blk[T - 1:T, :], (k_blk.astype(F32) + placed).astype(BF16)

    carry, om_ref[...] = block(jnp.zeros((1, LANES), F32), km_ref[...], sm_ref[...], True)
    for j in range(k_ref.shape[0] // T):
        rows = slice(j * T, (j + 1) * T)
        carry, o_ref[rows, :] = block(carry, k_ref[rows, :], s_ref[rows, :], False)


def _fox_keys(k, small, fb, bsz, seq):
    n_rows, width = k.shape
    meta_block = n_rows // BLOCK - 1
    main = lambda w: pl.BlockSpec((seq, w), lambda b: (b, 0))
    meta = lambda w: pl.BlockSpec((BLOCK, w), lambda b: (meta_block, 0))
    return pl.pallas_call(
        _fox_keys_kernel,
        grid=(bsz,),
        in_specs=[meta(width), main(width), meta(LANES), main(LANES), _resident(fb.shape)],
        out_specs=[pl.BlockSpec((BLOCK, width), lambda b: (0, 0)), main(width)],
        out_shape=[jax.ShapeDtypeStruct((BLOCK, width), BF16),
                   jax.ShapeDtypeStruct((bsz * seq, width), BF16)],
        compiler_params=pltpu.CompilerParams(
            dimension_semantics=("arbitrary",), vmem_limit_bytes=VMEM_LIMIT),
        name="fox_keys",
    )(k, k, small, small, fb)


def _attn_kernel(n_heads, q_ref, km_ref, k_ref, vtm_ref, vt_ref, o_ref, acc_scr):
    t = pl.program_id(1)
    dv = acc_scr.shape[1]
    heads = range(n_heads)
    hsl = [slice(hd * HEAD_W, (hd + 1) * HEAD_W) for hd in heads]
    vsl = [slice(hd * dv, (hd + 1) * dv) for hd in heads]
    qs = [q_ref[:, hsl[hd]] for hd in heads]
    ahead = (lax.broadcasted_iota(jnp.int32, (TQ, TQ), 0)
             - lax.broadcasted_iota(jnp.int32, (TQ, TQ), 1))
    acc_scr[...] = jnp.zeros(acc_scr.shape, F32)

    def chunk(ks, vts, carry, mask):
        ms, ls = carry
        ss = [_dot_nt(ks[hd], qs[hd]) for hd in heads]
        if mask is not None:
            ss = [jnp.where(mask, s, NEG_BIG) for s in ss]
        new_ms = [jnp.maximum(ms[hd], jnp.max(ss[hd], axis=0, keepdims=True)) for hd in heads]
        corrs = [jnp.exp(ms[hd] - new_ms[hd]) for hd in heads]
        prs = [jnp.exp(ss[hd] - new_ms[hd]) for hd in heads]
        new_ls = [corrs[hd] * ls[hd] + jnp.sum(prs[hd], axis=0, keepdims=True) for hd in heads]
        for hd in heads:
            acc_scr[hd] = corrs[hd] * acc_scr[hd] + _dot(vts[hd], prs[hd].astype(BF16))
        return tuple(new_ms), tuple(new_ls)

    def meta_chunk(carry, mask):
        return chunk([km_ref[:, hsl[hd]] for hd in heads],
                     [vtm_ref[0, vsl[hd], :] for hd in heads], carry, mask)

    def main_chunk(c, carry, mask):
        r0 = pl.multiple_of(c * TQ, TQ)
        n_sub = TQ // BLOCK
        return chunk([k_ref[pl.ds(r0, TQ), hsl[hd]] for hd in heads],
                     [jnp.concatenate([vt_ref[n_sub * c + i, vsl[hd], :] for i in range(n_sub)],
                                      axis=1) for hd in heads], carry, mask)

    def finish(ls):
        for p in range(n_heads // 2):
            y_t = jnp.concatenate([acc_scr[2 * p] * (1.0 / ls[2 * p]),
                                   acc_scr[2 * p + 1] * (1.0 / ls[2 * p + 1])], axis=0)
            o_ref[:, p * LANES:(p + 1) * LANES] = y_t.T.astype(BF16)

    init = (tuple(jnp.full((1, TQ), M_INIT, F32) for _ in heads),
            tuple(jnp.zeros((1, TQ), F32) for _ in heads))

    @pl.when((t == 0) & (pl.program_id(0) == 0))
    def _():
        _, ls = meta_chunk(init, ahead[:BLOCK, :] <= -(TQ - BLOCK))
        finish(ls)

    @pl.when(t > 0)
    def _():
        carry = meta_chunk(init, None)
        carry = lax.fori_loop(0, t - 1, lambda c, cr: main_chunk(c, cr, None), carry)
        _, ls = main_chunk(t - 1, carry, ahead <= 0)
        finish(ls)


def _attention(q, k_meta, k_main, vt, n_heads, bsz, seq, meta_k_block, name):
    n_rows, qw = q.shape
    dv = vt.shape[1] // n_heads
    tiles_per_seq = seq // TQ
    meta_q_tile = n_rows // TQ - 1
    meta_block = n_rows // BLOCK - 1

    def q_tile(b, t):
        first = jnp.where(b == 0, meta_q_tile, b * tiles_per_seq)
        return (jnp.where(t == 0, first, b * tiles_per_seq + t - 1), 0)

    return pl.pallas_call(
        functools.partial(_attn_kernel, n_heads),
        grid=(bsz, tiles_per_seq + 1),
        in_specs=[pl.BlockSpec((TQ, qw), q_tile),
                  pl.BlockSpec((BLOCK, qw), lambda b, t: (meta_k_block, 0)),
                  pl.BlockSpec((seq, qw), lambda b, t: (b, 0)),
                  pl.BlockSpec((1,) + vt.shape[1:], lambda b, t: (meta_block, 0, 0)),
                  pl.BlockSpec((seq // BLOCK,) + vt.shape[1:], lambda b, t: (b, 0, 0))],
        out_specs=pl.BlockSpec((TQ, n_heads * dv), q_tile),
        out_shape=jax.ShapeDtypeStruct((n_rows, n_heads * dv), BF16),
        scratch_shapes=[pltpu.VMEM((n_heads, dv, TQ), F32)],
        compiler_params=pltpu.CompilerParams(
            dimension_semantics=("arbitrary", "arbitrary"), vmem_limit_bytes=VMEM_LIMIT),
        name=name,
    )(q, k_meta, k_main, vt, vt)


def _out_ln_kernel(h_ref, ys_ref, yf_ref, ym_ref, w_ref, g_ref, b_ref, o_ref):
    mix = (_dot(ys_ref[...], w_ref[0:SSD_D, :])
           + _dot(yf_ref[...], w_ref[SSD_D:SSD_D + FOX_D, :])
           + _dot(ym_ref[...], w_ref[SSD_D + FOX_D:D_MIX, :]))
    o_ref[...] = _layer_norm_rows(ALPHA * h_ref[...] + mix, g_ref[...], b_ref[...])


def _out_ln(h, ys, yf, ym, w, g, b, tm):
    n_rows = h.shape[0]

    def rows(width):
        return pl.BlockSpec((tm, width), lambda i: (i, 0))

    return pl.pallas_call(
        _out_ln_kernel,
        grid=(n_rows // tm,),
        in_specs=[rows(D_MODEL), rows(SSD_D), rows(FOX_D), rows(MLA_D),
                  _resident(w.shape), _resident(g.shape), _resident(b.shape)],
        out_specs=rows(D_MODEL),
        out_shape=jax.ShapeDtypeStruct((n_rows, D_MODEL), F32),
        compiler_params=pltpu.CompilerParams(
            dimension_semantics=("arbitrary",), vmem_limit_bytes=VMEM_LIMIT),
        name="out_proj_ln",
    )(h, ys, yf, ym, w, g, b)


def _rot_half_cols(w):
    half = MLA_ROPE // 2
    return jnp.concatenate([-w[..., half:], w[..., :half]], axis=-1)


def _in_proj_weights(w_in):
    sizes = [SSD_D, SSD_CONV_DIM, SSD_HEADS, FOX_D, FOX_D, FOX_D, FOX_HEADS,
             MLA_Q_LORA, MLA_KV_LORA, MLA_ROPE]
    splits = [int(s) for s in np.cumsum(sizes)[:-1]]
    z, xbc, dt, fq, fk, fv, fr, cq, ckv, kr = jnp.split(w_in, splits, axis=-1)
    zeros = lambda n: jnp.zeros((D_MODEL, n), w_in.dtype)
    rope_pad = LANES - MLA_NOPE - MLA_ROPE
    kr128 = jnp.concatenate([zeros(MLA_NOPE), kr, zeros(rope_pad)], axis=-1)
    krr128 = jnp.concatenate([zeros(MLA_NOPE), _rot_half_cols(kr), zeros(rope_pad)], axis=-1)
    small = jnp.concatenate([dt, fr, zeros(LANES - SMALL_F - FOX_HEADS)], axis=-1)

    def per_head(w):
        w = w.reshape(D_MODEL, FOX_HEADS, FOX_HEAD_DIM)
        return jnp.pad(w, ((0, 0), (0, 0), (0, HEAD_W - FOX_HEAD_DIM))).reshape(D_MODEL, -1)

    win = jnp.concatenate([z, xbc, per_head(fq), per_head(fk), cq, ckv, kr128, krr128, small],
                          axis=-1).astype(BF16)
    return win, fv.T.astype(BF16)


def _mla_weights(w_uq, w_ukv):
    rope_pad = LANES - MLA_NOPE - MLA_ROPE
    wq = w_uq.reshape(MLA_Q_LORA, MLA_HEADS, MLA_NOPE + MLA_ROPE)
    zq = jnp.zeros((MLA_Q_LORA, MLA_HEADS, rope_pad), w_uq.dtype)
    wuq = jnp.concatenate([wq, zq], axis=-1).reshape(MLA_Q_LORA, MLA_HEADS * LANES)
    wuqr = jnp.concatenate([jnp.zeros_like(wq[..., :MLA_NOPE]),
                            _rot_half_cols(wq[..., MLA_NOPE:]), zq],
                           axis=-1).reshape(MLA_Q_LORA, MLA_HEADS * LANES)
    wkv = w_ukv.reshape(MLA_KV_LORA, MLA_HEADS, MLA_NOPE + MLA_V)
    wkk = jnp.concatenate([wkv[..., :MLA_NOPE], jnp.zeros_like(wkv[..., MLA_NOPE:])],
                          axis=-1).reshape(MLA_KV_LORA, MLA_HEADS * LANES)
    wvt = wkv[..., MLA_NOPE:].reshape(MLA_KV_LORA, MLA_D).T
    return wuq.astype(BF16), wuqr.astype(BF16), wkk.astype(BF16), wvt.astype(BF16)


def _lane_row(v, width=LANES):
    v = v.astype(F32)
    return jnp.pad(v, (0, width - v.shape[0]))[None, :]


def _position_tables(seq, tm):
    pos = jnp.concatenate([N_META + jnp.arange(seq, dtype=F32),
                           jnp.arange(tm, dtype=F32) - (tm - N_META)])
    is_pad = jnp.concatenate([jnp.zeros((seq,), bool), jnp.arange(tm) < tm - N_META])
    inv_freq = 1.0 / (ROPE_THETA ** (jnp.arange(0, MLA_ROPE, 2, dtype=F32) / MLA_ROPE))
    ang = pos[:, None] * inv_freq[None, :]
    cos, sin = jnp.cos(ang), jnp.sin(ang)
    n = seq + tm
    rope_pad = LANES - MLA_NOPE - MLA_ROPE
    cos128 = jnp.concatenate([jnp.ones((n, MLA_NOPE), F32), cos, cos,
                              jnp.zeros((n, rope_pad), F32)], axis=-1)
    sin128 = jnp.concatenate([jnp.zeros((n, MLA_NOPE), F32), sin, sin,
                              jnp.zeros((n, rope_pad), F32)], axis=-1)
    kadd = jnp.where(is_pad[:, None] & (jnp.arange(LANES)[None, :] == MLA_MASK_LANE),
                     NEG_BIG, 0.0).astype(F32)
    return cos128, sin128, kadd


def kernel(x, meta, ffn1_w_gate, ffn1_w_up, ffn1_w_down, ln1_g, ln1_b, w_in, conv_w, conv_b, dt_bias, a_log, d_skip, ssd_norm_g, fox_f_b, mla_q_norm_g, mla_w_uq, mla_kv_norm_g, mla_w_ukv, w_out, ln2_g, ln2_b, ffn2_w_gate, ffn2_w_up, ffn2_w_down, ln3_g, ln3_b):
    bsz, seq, _ = x.shape
    assert seq % TQ == 0
    tm = 2 * TQ if seq % (2 * TQ) == 0 else TQ
    n_main = bsz * seq
    n_rows = n_main + tm
    meta_block = n_rows // BLOCK - 1

    h = x.reshape(n_main, D_MODEL)
    meta_tile = jnp.concatenate([jnp.zeros((tm - N_META, D_MODEL), x.dtype),
                                 meta.astype(x.dtype)], axis=0)
    tables = _position_tables(seq, tm)
    row = lambda v: v.astype(F32)[None, :]
    bf = lambda w: w.astype(BF16)

    for l in range(DEPTH):
        h = _ffn_ln(h, meta_tile if l == 0 else None, bf(ffn1_w_gate[l]), bf(ffn1_w_up[l]),
                    bf(ffn1_w_down[l]), row(ln1_g[l]), row(ln1_b[l]), tm, n_rows)

        win, wfvt = _in_proj_weights(w_in[l])
        wuq, wuqr, wkk, wkvt = _mla_weights(mla_w_uq[l], mla_w_ukv[l])
        zx, small, fq, fk, fvt, mq, mk, mvt = _proj(
            h, win, wfvt, wuq, wuqr, wkk, wkvt, row(mla_q_norm_g[l]), row(mla_kv_norm_g[l]),
            tables, tm, seq)

        a_row = _lane_row(-jnp.exp(a_log[l].astype(F32)))
        dsk = jnp.repeat(d_skip[l].astype(F32), SSD_HEAD_DIM)[None, :]
        y_ssd = _ssd(zx, small, conv_w[l].astype(F32), row(conv_b[l]), _lane_row(dt_bias[l]),
                     a_row, dsk, row(ssd_norm_g[l]), bsz, seq)

        fb = jnp.pad(fox_f_b[l].astype(F32), (SMALL_F, LANES - SMALL_F - FOX_HEADS))[None, :]
        fk_meta, fk_main = _fox_keys(fk, small, fb, bsz, seq)
        y_fox = _attention(fq, fk_meta, fk_main, fvt, FOX_HEADS, bsz, seq, 0, "fox_attention")
        y_mla = _attention(mq, mk, mk, mvt, MLA_HEADS, bsz, seq, meta_block, "mla_attention")

        h = _out_ln(h, y_ssd, y_fox, y_mla, bf(w_out[l]), row(ln2_g[l]), row(ln2_b[l]), tm)

        h = _ffn_ln(h, None, bf(ffn2_w_gate[l]), bf(ffn2_w_up[l]), bf(ffn2_w_down[l]),
                    row(ln3_g[l]), row(ln3_b[l]), tm, n_rows if l < DEPTH - 1 else n_main)

    return h.reshape(bsz, seq, D_MODEL)
```

```python
import functools

import numpy as np
import jax
import jax.numpy as jnp
from jax import lax
from jax.experimental import pallas as pl
from jax.experimental.pallas import tpu as pltpu

F32 = jnp.float32
BF16 = jnp.bfloat16

D_MODEL = 1024
DEPTH = 2
N_META = 16
BLOCK = 128
SSD_HEADS = 8
SSD_HEAD_DIM = 64
SSD_D = SSD_HEADS * SSD_HEAD_DIM
SSD_GROUPS = 2
SSD_STATE = 64
SSD_CONV = 4
SSD_CONV_DIM = SSD_D + 2 * SSD_GROUPS * SSD_STATE
FOX_HEADS = 4
FOX_HEAD_DIM = 64
FOX_D = FOX_HEADS * FOX_HEAD_DIM
MLA_HEADS = 4
MLA_Q_LORA = 256
MLA_KV_LORA = 128
MLA_NOPE = 64
MLA_ROPE = 32
MLA_V = 64
MLA_D = MLA_HEADS * MLA_V
ROPE_THETA = 10000.0
D_MIX = SSD_D + FOX_D + MLA_D
D_FF = 2816
ALPHA = (2 * DEPTH) ** 0.25
EPS = 1e-5

LANES = 128
MXU_W = 256
FF_CHUNK = MXU_W
N_FF_CHUNKS = D_FF // FF_CHUNK
N_OUT_CHUNKS = D_MODEL // FF_CHUNK
TQ = 2 * BLOCK
PAD = BLOCK - N_META
NEG_BIG = -1e30
M_INIT = 2 * NEG_BIG
VMEM_LIMIT = 56 * 1024 * 1024

C_Z = 0
C_XBC = C_Z + SSD_D
C_FQ = C_XBC + SSD_CONV_DIM
C_FK = C_FQ + FOX_HEADS * LANES
C_CQ = C_FK + FOX_HEADS * LANES
C_CKV = C_CQ + MLA_Q_LORA
C_KR = C_CKV + MLA_KV_LORA
C_KRR = C_KR + LANES
C_SMALL = C_KRR + LANES
N_IN_ARR = C_SMALL + LANES
SMALL_DT = 0
SMALL_F = 8
HEAD_W = LANES
FOX_BIAS_LANE = FOX_HEAD_DIM
MLA_MASK_LANE = MLA_NOPE + MLA_ROPE


def _sigmoid(x):
    return 1.0 / (1.0 + jnp.exp(-x))


def _softplus(x):
    return jnp.maximum(x, 0.0) + jnp.log(1.0 + jnp.exp(-jnp.abs(x)))


def _layer_norm_rows(y, g, b):
    mu = jnp.mean(y, axis=-1, keepdims=True)
    yc = y - mu
    var = jnp.mean(yc * yc, axis=-1, keepdims=True)
    return yc * lax.rsqrt(var + EPS) * g + b


def _split3(x):
    x1 = x.astype(BF16)
    r1 = x - x1.astype(F32)
    x2 = r1.astype(BF16)
    r2 = r1 - x2.astype(F32)
    return x1, x2, r2.astype(BF16)


def _dot(a, b):
    return jnp.dot(a, b, preferred_element_type=F32)


def _dot_nt(a, b):
    return lax.dot_general(a, b, (((1,), (1,)), ((), ())), preferred_element_type=F32)


def _resident(shape):
    return pl.BlockSpec(shape, lambda *_: (0,) * len(shape), pipeline_mode=pl.Buffered(1))


def _ffn_ln_kernel(n_main_tiles, *refs):
    if n_main_tiles is None:
        x_ref, wg_ref, wu_ref, wd_ref, g_ref, b_ref, o_ref, a_scr, y_scr = refs
        x = x_ref[...]
    else:
        x_ref, m_ref, wg_ref, wu_ref, wd_ref, g_ref, b_ref, o_ref, a_scr, y_scr = refs
        x = jnp.where(pl.program_id(0) < n_main_tiles, x_ref[...], m_ref[...])
    xb = x.astype(BF16)

    for c in range(N_FF_CHUNKS):
        cols = slice(c * FF_CHUNK, (c + 1) * FF_CHUNK)
        gate = _dot(xb, wg_ref[:, cols])
        up = _dot(xb, wu_ref[:, cols])
        a_scr[c] = (gate * _sigmoid(gate) * up).astype(BF16)

    for n in range(N_OUT_CHUNKS):
        cols = slice(n * FF_CHUNK, (n + 1) * FF_CHUNK)
        acc = _dot(a_scr[0], wd_ref[0:FF_CHUNK, cols])
        for c in range(1, N_FF_CHUNKS):
            acc = acc + _dot(a_scr[c], wd_ref[c * FF_CHUNK:(c + 1) * FF_CHUNK, cols])
        y_scr[:, cols] = ALPHA * x[:, cols] + 0.5 * acc
    o_ref[...] = _layer_norm_rows(y_scr[...], g_ref[...], b_ref[...])


def _ffn_ln(h, meta_tile, wg, wu, wd, g, b, tm, n_out_rows):
    n_main_tiles = h.shape[0] // tm
    rows = pl.BlockSpec((tm, D_MODEL), lambda i: (i, 0))
    if meta_tile is None:
        x_specs, x_args, n_main = [rows], [h], None
    else:
        x_specs = [pl.BlockSpec((tm, D_MODEL), lambda i: (jnp.minimum(i, n_main_tiles - 1), 0)),
                   _resident(meta_tile.shape)]
        x_args, n_main = [h, meta_tile], n_main_tiles
    return pl.pallas_call(
        functools.partial(_ffn_ln_kernel, n_main),
        grid=(n_out_rows // tm,),
        in_specs=x_specs + [_resident(wg.shape), _resident(wu.shape), _resident(wd.shape),
                            _resident(g.shape), _resident(b.shape)],
        out_specs=rows,
        out_shape=jax.ShapeDtypeStruct((n_out_rows, D_MODEL), F32),
        scratch_shapes=[pltpu.VMEM((N_FF_CHUNKS, tm, FF_CHUNK), BF16),
                        pltpu.VMEM((tm, D_MODEL), F32)],
        compiler_params=pltpu.CompilerParams(
            dimension_semantics=("arbitrary",), vmem_limit_bytes=VMEM_LIMIT),
        name="ffn_ln",
    )(*x_args, wg, wu, wd, g, b)


def _rms_rows(x, g):
    return x * lax.rsqrt(jnp.mean(x * x, axis=-1, keepdims=True) + EPS) * g


def _store_t_blocks(out_ref, val_t):
    for r in range(out_ref.shape[0]):
        out_ref[r] = val_t[:, r * BLOCK:(r + 1) * BLOCK].astype(out_ref.dtype)


def _proj_kernel(h_ref, win_ref, wfvt_ref, wuq_ref, wuqr_ref, wkk_ref, wkvt_ref, qg_ref, kvg_ref,
                 cos_ref, sin_ref, kadd_ref,
                 zx_ref, small_ref, fq_ref, fk_ref, fvt_ref, mq_ref, mk_ref, mvt_ref):
    hb = h_ref[...].astype(BF16)
    zx_ref[...] = _dot(hb, win_ref[:, C_Z:C_FQ])
    small_ref[...] = _dot(hb, win_ref[:, C_SMALL:N_IN_ARR])

    lane = lax.broadcasted_iota(jnp.int32, (1, LANES), 1)
    fox_one = jnp.where((lane >= FOX_BIAS_LANE) & (lane < FOX_BIAS_LANE + 3), 1.0, 0.0)
    mla_one = jnp.where(lane == MLA_MASK_LANE, 1.0, 0.0)

    fq = _dot(hb, win_ref[:, C_FQ:C_FK]) * (FOX_HEAD_DIM ** -0.5)
    for hd in range(FOX_HEADS):
        sl = slice(hd * HEAD_W, (hd + 1) * HEAD_W)
        fq_ref[:, sl] = (fq[:, sl] + fox_one).astype(BF16)
    fk_ref[...] = _dot(hb, win_ref[:, C_FK:C_CQ]).astype(BF16)
    _store_t_blocks(fvt_ref, _dot_nt(wfvt_ref[...], hb))

    cos = cos_ref[...]
    sin = sin_ref[...]
    cqn = _rms_rows(_dot(hb, win_ref[:, C_CQ:C_CKV]), qg_ref[...]).astype(BF16)
    q = _dot(cqn, wuq_ref[...])
    qr = _dot(cqn, wuqr_ref[...])
    kvn = _rms_rows(_dot(hb, win_ref[:, C_CKV:C_KR]), kvg_ref[...]).astype(BF16)
    kn = _dot(kvn, wkk_ref[...])
    _store_t_blocks(mvt_ref, _dot_nt(wkvt_ref[...], kvn))
    krope = (_dot(hb, win_ref[:, C_KR:C_KRR]) * cos
             + _dot(hb, win_ref[:, C_KRR:C_SMALL]) * sin + kadd_ref[...])
    scale = (MLA_NOPE + MLA_ROPE) ** -0.5
    for hd in range(MLA_HEADS):
        sl = slice(hd * HEAD_W, (hd + 1) * HEAD_W)
        mq_ref[:, sl] = ((q[:, sl] * cos + qr[:, sl] * sin) * scale + mla_one).astype(BF16)
        mk_ref[:, sl] = (kn[:, sl] + krope).astype(BF16)


def _proj(h, win, wfvt, wuq, wuqr, wkk, wkvt, qg, kvg, tables, tm, seq):
    n_rows = h.shape[0]
    n_main_tiles = n_rows // tm - 1
    tiles_per_seq = seq // tm

    def rows(width):
        return pl.BlockSpec((tm, width), lambda i: (i, 0))

    tab = pl.BlockSpec((tm, LANES), lambda i: (
        jnp.where(i < n_main_tiles, i % tiles_per_seq, tiles_per_seq), 0))
    t_spec = pl.BlockSpec((tm // BLOCK, FOX_D, BLOCK), lambda i: (i, 0, 0))
    t_shape = jax.ShapeDtypeStruct((n_rows // BLOCK, FOX_D, BLOCK), BF16)
    qk_w = FOX_HEADS * HEAD_W
    row_outs = [(C_FQ, F32), (LANES, F32), (qk_w, BF16), (qk_w, BF16)]
    weights = [win, wfvt, wuq, wuqr, wkk, wkvt, qg, kvg]
    return pl.pallas_call(
        _proj_kernel,
        grid=(n_rows // tm,),
        in_specs=[rows(D_MODEL)] + [_resident(w.shape) for w in weights] + [tab, tab, tab],
        out_specs=([rows(w) for w, _ in row_outs] + [t_spec, rows(qk_w), rows(qk_w), t_spec]),
        out_shape=([jax.ShapeDtypeStruct((n_rows, w), dt) for w, dt in row_outs]
                   + [t_shape, jax.ShapeDtypeStruct((n_rows, qk_w), BF16),
                      jax.ShapeDtypeStruct((n_rows, qk_w), BF16), t_shape]),
        compiler_params=pltpu.CompilerParams(
            dimension_semantics=("arbitrary",), vmem_limit_bytes=VMEM_LIMIT),
        name="in_proj",
    )(h, *weights, *tables)


def _ssd_kernel(zx_ref, small_ref, cw_ref, cb_ref, dtb_ref, a_ref, dsk_ref, ng_ref,
                o_ref, conv_scr, s_scr, meta_conv_scr, meta_s_scr):
    b = pl.program_id(0)
    c = pl.program_id(1)

    @pl.when((c == 0) & (b == 0))
    def _():
        conv_scr[0:8, :] = jnp.zeros((8, SSD_CONV_DIM), F32)
        s_scr[...] = jnp.zeros(s_scr.shape, F32)

    @pl.when((c == 0) & (b > 0))
    def _():
        conv_scr[0:8, :] = meta_conv_scr[...]
        s_scr[...] = meta_s_scr[...]

    @pl.when((c > 0) | (b == 0))
    def _():
        _ssd_chunk(c, zx_ref, small_ref, cw_ref, cb_ref, dtb_ref, a_ref, dsk_ref, ng_ref,
                   o_ref, conv_scr, s_scr)

    @pl.when((c == 0) & (b == 0))
    def _():
        meta_conv_scr[...] = conv_scr[0:8, :]
        meta_s_scr[...] = s_scr[...]


def _ssd_chunk(c, zx_ref, small_ref, cw_ref, cb_ref, dtb_ref, a_ref, dsk_ref, ng_ref,
               o_ref, conv_scr, s_scr):
    Q = BLOCK
    row = lax.broadcasted_iota(jnp.int32, (Q, 1), 0)
    valid = jnp.logical_or(c > 0, row >= PAD)
    lane = lax.broadcasted_iota(jnp.int32, (1, LANES), 1)
    lane_lo = lane < SSD_HEAD_DIM
    sub = lax.broadcasted_iota(jnp.int32, (LANES, 1), 0)

    conv_scr[8:8 + Q, :] = jnp.where(valid, zx_ref[:, C_XBC:C_FQ], 0.0)
    acc = cb_ref[...]
    for k in range(SSD_CONV):
        off = 8 - (SSD_CONV - 1) + k
        acc = acc + cw_ref[k:k + 1, :] * conv_scr[off:off + Q, :]
    conv_scr[0:8, :] = conv_scr[Q:Q + 8, :]
    xbc = acc * _sigmoid(acc)
    bm = xbc[:, SSD_D:SSD_D + LANES]
    cm = xbc[:, SSD_D + LANES:SSD_D + 2 * LANES]

    dt = jnp.where(valid, _softplus(small_ref[...] + dtb_ref[...]), 0.0)
    a = dt * a_ref[...]
    tri = (lax.broadcasted_iota(jnp.int32, (Q, Q), 0)
           >= lax.broadcasted_iota(jnp.int32, (Q, Q), 1))
    tri_b = jnp.where(tri, 1.0, 0.0).astype(BF16)
    a1, a2, a3 = _split3(a)
    a_cum = _dot(tri_b, a1) + _dot(tri_b, a2) + _dot(tri_b, a3)
    a_cum_t = a_cum.T
    bm_t = bm.T

    cm_b = cm.astype(BF16)
    bm_b = bm.astype(BF16)
    cb_g = [_dot_nt(jnp.where(lane_lo, cm, 0.0).astype(BF16), bm_b),
            _dot_nt(jnp.where(lane_lo, 0.0, cm).astype(BF16), bm_b)]
    rows_g = [sub < SSD_STATE, sub >= SSD_STATE]

    pairs_per_group = SSD_HEADS // 2 // SSD_GROUPS
    y_pairs = []
    for p in range(SSD_HEADS // 2):
        g = p // pairs_per_group
        psl = slice(p * LANES, (p + 1) * LANES)
        xs_p = xbc[:, psl]
        dt_pair = jnp.where(lane_lo, dt[:, 2 * p:2 * p + 1], dt[:, 2 * p + 1:2 * p + 2])
        xdt = (xs_p * dt_pair).astype(BF16)
        s_old = s_scr[p]
        yd, upd, e_col, e_last = [], [], [], []
        for par in range(2):
            hd = 2 * p + par
            col = a_cum[:, hd:hd + 1]
            rowv = a_cum_t[hd:hd + 1, :]
            last = a_cum_t[hd:hd + 1, Q - 1:Q]
            seg = jnp.exp(jnp.where(tri, col - rowv, NEG_BIG))
            yd.append(_dot((cb_g[g] * seg).astype(BF16), xdt))
            upd.append(_dot((bm_t * jnp.exp(last - rowv)).astype(BF16), xdt))
            e_col.append(jnp.exp(col))
            e_last.append(jnp.exp(last))
        y_off = _dot(cm_b, s_old.astype(BF16)) * jnp.where(lane_lo, e_col[0], e_col[1])
        s_new = (jnp.where(lane_lo, e_last[0], e_last[1]) * s_old
                 + jnp.where(rows_g[g], jnp.where(lane_lo, upd[0], upd[1]), 0.0))
        s_scr[p] = s_new
        y_p = jnp.where(lane_lo, yd[0], yd[1]) + y_off + dsk_ref[:, psl] * xs_p
        z_p = zx_ref[:, psl]
        y_pairs.append(y_p * (z_p * _sigmoid(z_p)))

    for g in range(SSD_GROUPS):
        ps = range(g * pairs_per_group, (g + 1) * pairs_per_group)
        ss = sum(jnp.sum(y_pairs[p] * y_pairs[p], axis=-1, keepdims=True) for p in ps)
        inv = lax.rsqrt(ss * (1.0 / (pairs_per_group * LANES)) + EPS)
        for p in ps:
            psl = slice(p * LANES, (p + 1) * LANES)
            o_ref[:, psl] = (y_pairs[p] * inv * ng_ref[:, psl]).astype(BF16)


def _ssd(zx, small, cw, cb, dtb, a_row, dsk, ng, bsz, seq):
    n_rows = zx.shape[0]
    blocks_per_seq = seq // BLOCK
    meta_block = n_rows // BLOCK - 1

    def block(b, c):
        first = jnp.where(b == 0, meta_block, b * blocks_per_seq)
        return (jnp.where(c == 0, first, b * blocks_per_seq + c - 1), 0)

    params = [cw, cb, dtb, a_row, dsk, ng]
    return pl.pallas_call(
        _ssd_kernel,
        grid=(bsz, blocks_per_seq + 1),
        in_specs=[pl.BlockSpec((BLOCK, C_FQ), block), pl.BlockSpec((BLOCK, LANES), block)]
                 + [_resident(p.shape) for p in params],
        out_specs=pl.BlockSpec((BLOCK, SSD_D), block),
        out_shape=jax.ShapeDtypeStruct((n_rows, SSD_D), BF16),
        scratch_shapes=[pltpu.VMEM((BLOCK + 8, SSD_CONV_DIM), F32),
                        pltpu.VMEM((SSD_HEADS // 2, LANES, LANES), F32),
                        pltpu.VMEM((8, SSD_CONV_DIM), F32),
                        pltpu.VMEM((SSD_HEADS // 2, LANES, LANES), F32)],
        compiler_params=pltpu.CompilerParams(
            dimension_semantics=("arbitrary", "arbitrary"), vmem_limit_bytes=VMEM_LIMIT),
        name="ssd_mixer",
    )(zx, small, *params)


def _fox_keys_kernel(km_ref, k_ref, sm_ref, s_ref, fb_ref, om_ref, o_ref):
    T = BLOCK
    width = k_ref.shape[1]
    row = lax.broadcasted_iota(jnp.int32, (T, 1), 0)
    tri = (lax.broadcasted_iota(jnp.int32, (T, T), 0)
           >= lax.broadcasted_iota(jnp.int32, (T, T), 1))
    tri_b = jnp.where(tri, 1.0, 0.0).astype(BF16)
    src = lax.broadcasted_iota(jnp.int32, (LANES, width), 0)
    dst = lax.broadcasted_iota(jnp.int32, (LANES, width), 1)
    dst_head = jnp.right_shift(dst, HEAD_W.bit_length() - 1)
    dst_lane = jnp.bitwise_and(dst, HEAD_W - 1)
    sel = [jnp.where((src == SMALL_F + dst_head) & (dst_lane == FOX_BIAS_LANE + i),
                     1.0, 0.0).astype(BF16) for i in range(3)]

    def block(carry, k_blk, small_blk, is_meta):
        log_f = -_softplus(-(small_blk + fb_ref[...]))
        if is_meta:
            log_f = jnp.where(row < PAD, 0.0, log_f)
        f1, f2, f3 = _split3(log_f)
        c_blk = carry + _dot(tri_b, f1) + _dot(tri_b, f2) + _dot(tri_b, f3)
        bias = -c_blk
        if is_meta:
            bias = jnp.where(row < PAD, NEG_BIG, bias)
        b1, b2, b3 = _split3(bias)
        placed = _dot(b1, sel[0]) + _dot(b2, sel[1]) + _dot(b3, sel[2])
        return c_blk[T - 1:T, :], (k_blk.astype(F32) + placed).astype(BF16)

    carry, om_ref[...] = block(jnp.zeros((1, LANES), F32), km_ref[...], sm_ref[...], True)
    for j in range(k_ref.shape[0] // T):
        rows = slice(j * T, (j + 1) * T)
        carry, o_ref[rows, :] = block(carry, k_ref[rows, :], s_ref[rows, :], False)


def _fox_keys(k, small, fb, bsz, seq):
    n_rows, width = k.shape
    meta_block = n_rows // BLOCK - 1
    main = lambda w: pl.BlockSpec((seq, w), lambda b: (b, 0))
    meta = lambda w: pl.BlockSpec((BLOCK, w), lambda b: (meta_block, 0))
    return pl.pallas_call(
        _fox_keys_kernel,
        grid=(bsz,),
        in_specs=[meta(width), main(width), meta(LANES), main(LANES), _resident(fb.shape)],
        out_specs=[pl.BlockSpec((BLOCK, width), lambda b: (0, 0)), main(width)],
        out_shape=[jax.ShapeDtypeStruct((BLOCK, width), BF16),
                   jax.ShapeDtypeStruct((bsz * seq, width), BF16)],
        compiler_params=pltpu.CompilerParams(
            dimension_semantics=("arbitrary",), vmem_limit_bytes=VMEM_LIMIT),
        name="fox_keys",
    )(k, k, small, small, fb)


def _attn_kernel(n_heads, q_ref, km_ref, k_ref, vtm_ref, vt_ref, o_ref, acc_scr):
    t = pl.program_id(1)
    dv = acc_scr.shape[1]
    heads = range(n_heads)
    hsl = [slice(hd * HEAD_W, (hd + 1) * HEAD_W) for hd in heads]
    vsl = [slice(hd * dv, (hd + 1) * dv) for hd in heads]
    qs = [q_ref[:, hsl[hd]] for hd in heads]
    ahead = (lax.broadcasted_iota(jnp.int32, (TQ, TQ), 0)
             - lax.broadcasted_iota(jnp.int32, (TQ, TQ), 1))
    acc_scr[...] = jnp.zeros(acc_scr.shape, F32)

    def chunk(ks, vts, carry, mask):
        ms, ls = carry
        ss = [_dot_nt(ks[hd], qs[hd]) for hd in heads]
        if mask is not None:
            ss = [jnp.where(mask, s, NEG_BIG) for s in ss]
        new_ms = [jnp.maximum(ms[hd], jnp.max(ss[hd], axis=0, keepdims=True)) for hd in heads]
        corrs = [jnp.exp(ms[hd] - new_ms[hd]) for hd in heads]
        prs = [jnp.exp(ss[hd] - new_ms[hd]) for hd in heads]
        new_ls = [corrs[hd] * ls[hd] + jnp.sum(prs[hd], axis=0, keepdims=True) for hd in heads]
        for hd in heads:
            acc_scr[hd] = corrs[hd] * acc_scr[hd] + _dot(vts[hd], prs[hd].astype(BF16))
        return tuple(new_ms), tuple(new_ls)

    def meta_chunk(carry, mask):
        return chunk([km_ref[:, hsl[hd]] for hd in heads],
                     [vtm_ref[0, vsl[hd], :] for hd in heads], carry, mask)

    def main_chunk(c, carry, mask):
        r0 = pl.multiple_of(c * TQ, TQ)
        n_sub = TQ // BLOCK
        return chunk([k_ref[pl.ds(r0, TQ), hsl[hd]] for hd in heads],
                     [jnp.concatenate([vt_ref[n_sub * c + i, vsl[hd], :] for i in range(n_sub)],
                                      axis=1) for hd in heads], carry, mask)

    def finish(ls):
        for p in range(n_heads // 2):
            y_t = jnp.concatenate([acc_scr[2 * p] * (1.0 / ls[2 * p]),
                                   acc_scr[2 * p + 1] * (1.0 / ls[2 * p + 1])], axis=0)
            o_ref[:, p * LANES:(p + 1) * LANES] = y_t.T.astype(BF16)

    init = (tuple(jnp.full((1, TQ), M_INIT, F32) for _ in heads),
            tuple(jnp.zeros((1, TQ), F32) for _ in heads))

    @pl.when((t == 0) & (pl.program_id(0) == 0))
    def _():
        _, ls = meta_chunk(init, ahead[:BLOCK, :] <= -(TQ - BLOCK))
        finish(ls)

    @pl.when(t > 0)
    def _():
        carry = meta_chunk(init, None)
        carry = lax.fori_loop(0, t - 1, lambda c, cr: main_chunk(c, cr, None), carry)
        _, ls = main_chunk(t - 1, carry, ahead <= 0)
        finish(ls)


def _attention(q, k_meta, k_main, vt, n_heads, bsz, seq, meta_k_block, name):
    n_rows, qw = q.shape
    dv = vt.shape[1] // n_heads
    tiles_per_seq = seq // TQ
    meta_q_tile = n_rows // TQ - 1
    meta_block = n_rows // BLOCK - 1

    def q_tile(b, t):
        first = jnp.where(b == 0, meta_q_tile, b * tiles_per_seq)
        return (jnp.where(t == 0, first, b * tiles_per_seq + t - 1), 0)

    return pl.pallas_call(
        functools.partial(_attn_kernel, n_heads),
        grid=(bsz, tiles_per_seq + 1),
        in_specs=[pl.BlockSpec((TQ, qw), q_tile),
                  pl.BlockSpec((BLOCK, qw), lambda b, t: (meta_k_block, 0)),
                  pl.BlockSpec((seq, qw), lambda b, t: (b, 0)),
                  pl.BlockSpec((1,) + vt.shape[1:], lambda b, t: (meta_block, 0, 0)),
                  pl.BlockSpec((seq // BLOCK,) + vt.shape[1:], lambda b, t: (b, 0, 0))],
        out_specs=pl.BlockSpec((TQ, n_heads * dv), q_tile),
        out_shape=jax.ShapeDtypeStruct((n_rows, n_heads * dv), BF16),
        scratch_shapes=[pltpu.VMEM((n_heads, dv, TQ), F32)],
        compiler_params=pltpu.CompilerParams(
            dimension_semantics=("arbitrary", "arbitrary"), vmem_limit_bytes=VMEM_LIMIT),
        name=name,
    )(q, k_meta, k_main, vt, vt)


def _out_ln_kernel(h_ref, ys_ref, yf_ref, ym_ref, w_ref, g_ref, b_ref, o_ref):
    mix = (_dot(ys_ref[...], w_ref[0:SSD_D, :])
           + _dot(yf_ref[...], w_ref[SSD_D:SSD_D + FOX_D, :])
           + _dot(ym_ref[...], w_ref[SSD_D + FOX_D:D_MIX, :]))
    o_ref[...] = _layer_norm_rows(ALPHA * h_ref[...] + mix, g_ref[...], b_ref[...])


def _out_ln(h, ys, yf, ym, w, g, b, tm):
    n_rows = h.shape[0]

    def rows(width):
        return pl.BlockSpec((tm, width), lambda i: (i, 0))

    return pl.pallas_call(
        _out_ln_kernel,
        grid=(n_rows // tm,),
        in_specs=[rows(D_MODEL), rows(SSD_D), rows(FOX_D), rows(MLA_D),
                  _resident(w.shape), _resident(g.shape), _resident(b.shape)],
        out_specs=rows(D_MODEL),
        out_shape=jax.ShapeDtypeStruct((n_rows, D_MODEL), F32),
        compiler_params=pltpu.CompilerParams(
            dimension_semantics=("arbitrary",), vmem_limit_bytes=VMEM_LIMIT),
        name="out_proj_ln",
    )(h, ys, yf, ym, w, g, b)


def _rot_half_cols(w):
    half = MLA_ROPE // 2
    return jnp.concatenate([-w[..., half:], w[..., :half]], axis=-1)


def _in_proj_weights(w_in):
    sizes = [SSD_D, SSD_CONV_DIM, SSD_HEADS, FOX_D, FOX_D, FOX_D, FOX_HEADS,
             MLA_Q_LORA, MLA_KV_LORA, MLA_ROPE]
    splits = [int(s) for s in np.cumsum(sizes)[:-1]]
    z, xbc, dt, fq, fk, fv, fr, cq, ckv, kr = jnp.split(w_in, splits, axis=-1)
    zeros = lambda n: jnp.zeros((D_MODEL, n), w_in.dtype)
    rope_pad = LANES - MLA_NOPE - MLA_ROPE
    kr128 = jnp.concatenate([zeros(MLA_NOPE), kr, zeros(rope_pad)], axis=-1)
    krr128 = jnp.concatenate([zeros(MLA_NOPE), _rot_half_cols(kr), zeros(rope_pad)], axis=-1)
    small = jnp.concatenate([dt, fr, zeros(LANES - SMALL_F - FOX_HEADS)], axis=-1)

    def per_head(w):
        w = w.reshape(D_MODEL, FOX_HEADS, FOX_HEAD_DIM)
        return jnp.pad(w, ((0, 0), (0, 0), (0, HEAD_W - FOX_HEAD_DIM))).reshape(D_MODEL, -1)

    win = jnp.concatenate([z, xbc, per_head(fq), per_head(fk), cq, ckv, kr128, krr128, small],
                          axis=-1).astype(BF16)
    return win, fv.T.astype(BF16)


def _mla_weights(w_uq, w_ukv):
    rope_pad = LANES - MLA_NOPE - MLA_ROPE
    wq = w_uq.reshape(MLA_Q_LORA, MLA_HEADS, MLA_NOPE + MLA_ROPE)
    zq = jnp.zeros((MLA_Q_LORA, MLA_HEADS, rope_pad), w_uq.dtype)
    wuq = jnp.concatenate([wq, zq], axis=-1).reshape(MLA_Q_LORA, MLA_HEADS * LANES)
    wuqr = jnp.concatenate([jnp.zeros_like(wq[..., :MLA_NOPE]),
                            _rot_half_cols(wq[..., MLA_NOPE:]), zq],
                           axis=-1).reshape(MLA_Q_LORA, MLA_HEADS * LANES)
    wkv = w_ukv.reshape(MLA_KV_LORA, MLA_HEADS, MLA_NOPE + MLA_V)
    wkk = jnp.concatenate([wkv[..., :MLA_NOPE], jnp.zeros_like(wkv[..., MLA_NOPE:])],
                          axis=-1).reshape(MLA_KV_LORA, MLA_HEADS * LANES)
    wvt = wkv[..., MLA_NOPE:].reshape(MLA_KV_LORA, MLA_D).T
    return wuq.astype(BF16), wuqr.astype(BF16), wkk.astype(BF16), wvt.astype(BF16)


def _lane_row(v, width=LANES):
    v = v.astype(F32)
    return jnp.pad(v, (0, width - v.shape[0]))[None, :]


def _position_tables(seq, tm):
    pos = jnp.concatenate([N_META + jnp.arange(seq, dtype=F32),
                           jnp.arange(tm, dtype=F32) - (tm - N_META)])
    is_pad = jnp.concatenate([jnp.zeros((seq,), bool), jnp.arange(tm) < tm - N_META])
    inv_freq = 1.0 / (ROPE_THETA ** (jnp.arange(0, MLA_ROPE, 2, dtype=F32) / MLA_ROPE))
    ang = pos[:, None] * inv_freq[None, :]
    cos, sin = jnp.cos(ang), jnp.sin(ang)
    n = seq + tm
    rope_pad = LANES - MLA_NOPE - MLA_ROPE
    cos128 = jnp.concatenate([jnp.ones((n, MLA_NOPE), F32), cos, cos,
                              jnp.zeros((n, rope_pad), F32)], axis=-1)
    sin128 = jnp.concatenate([jnp.zeros((n, MLA_NOPE), F32), sin, sin,
                              jnp.zeros((n, rope_pad), F32)], axis=-1)
    kadd = jnp.where(is_pad[:, None] & (jnp.arange(LANES)[None, :] == MLA_MASK_LANE),
                     NEG_BIG, 0.0).astype(F32)
    return cos128, sin128, kadd


def kernel(x, meta, ffn1_w_gate, ffn1_w_up, ffn1_w_down, ln1_g, ln1_b, w_in, conv_w, conv_b, dt_bias, a_log, d_skip, ssd_norm_g, fox_f_b, mla_q_norm_g, mla_w_uq, mla_kv_norm_g, mla_w_ukv, w_out, ln2_g, ln2_b, ffn2_w_gate, ffn2_w_up, ffn2_w_down, ln3_g, ln3_b):
    bsz, seq, _ = x.shape
    assert seq % TQ == 0
    tm = 2 * TQ if seq % (2 * TQ) == 0 else TQ
    n_main = bsz * seq
    n_rows = n_main + tm
    meta_block = n_rows // BLOCK - 1

    h = x.reshape(n_main, D_MODEL)
    meta_tile = jnp.concatenate([jnp.zeros((tm - N_META, D_MODEL), x.dtype),
                                 meta.astype(x.dtype)], axis=0)
    tables = _position_tables(seq, tm)
    row = lambda v: v.astype(F32)[None, :]
    bf = lambda w: w.astype(BF16)

    for l in range(DEPTH):
        h = _ffn_ln(h, meta_tile if l == 0 else None, bf(ffn1_w_gate[l]), bf(ffn1_w_up[l]),
                    bf(ffn1_w_down[l]), row(ln1_g[l]), row(ln1_b[l]), tm, n_rows)

        win, wfvt = _in_proj_weights(w_in[l])
        wuq, wuqr, wkk, wkvt = _mla_weights(mla_w_uq[l], mla_w_ukv[l])
        zx, small, fq, fk, fvt, mq, mk, mvt = _proj(
            h, win, wfvt, wuq, wuqr, wkk, wkvt, row(mla_q_norm_g[l]), row(mla_kv_norm_g[l]),
            tables, tm, seq)

        a_row = _lane_row(-jnp.exp(a_log[l].astype(F32)))
        dsk = jnp.repeat(d_skip[l].astype(F32), SSD_HEAD_DIM)[None, :]
        y_ssd = _ssd(zx, small, conv_w[l].astype(F32), row(conv_b[l]), _lane_row(dt_bias[l]),
                     a_row, dsk, row(ssd_norm_g[l]), bsz, seq)

        fb = jnp.pad(fox_f_b[l].astype(F32), (SMALL_F, LANES - SMALL_F - FOX_HEADS))[None, :]
        fk_meta, fk_main = _fox_keys(fk, small, fb, bsz, seq)
        y_fox = _attention(fq, fk_meta, fk_main, fvt, FOX_HEADS, bsz, seq, 0, "fox_attention")
        y_mla = _attention(mq, mk, mk, mvt, MLA_HEADS, bsz, seq, meta_block, "mla_attention")

        h = _out_ln(h, y_ssd, y_fox, y_mla, bf(w_out[l]), row(ln2_g[l]), row(ln2_b[l]), tm)

        h = _ffn_ln(h, None, bf(ffn2_w_gate[l]), bf(ffn2_w_up[l]), bf(ffn2_w_down[l]),
                    row(ln3_g[l]), row(ln3_b[l]), tm, n_rows if l < DEPTH - 1 else n_main)

    return h.reshape(bsz, seq, D_MODEL)
```

```python
import functools

import numpy as np
import jax
import jax.numpy as jnp
from jax import lax
from jax.experimental import pallas as pl
from jax.experimental.pallas import tpu as pltpu

F32 = jnp.float32
BF16 = jnp.bfloat16

D_MODEL = 1024
DEPTH = 2
N_META = 16
BLOCK = 128
SSD_HEADS = 8
SSD_HEAD_DIM = 64
SSD_D = SSD_HEADS * SSD_HEAD_DIM
SSD_GROUPS = 2
SSD_STATE = 64
SSD_CONV = 4
SSD_CONV_DIM = SSD_D + 2 * SSD_GROUPS * SSD_STATE
FOX_HEADS = 4
FOX_HEAD_DIM = 64
FOX_D = FOX_HEADS * FOX_HEAD_DIM
MLA_HEADS = 4
MLA_Q_LORA = 256
MLA_KV_LORA = 128
MLA_NOPE = 64
MLA_ROPE = 32
MLA_V = 64
MLA_D = MLA_HEADS * MLA_V
ROPE_THETA = 10000.0
D_MIX = SSD_D + FOX_D + MLA_D
D_FF = 2816
ALPHA = (2 * DEPTH) ** 0.25
EPS = 1e-5

LANES = 128
MXU_W = 256
FF_CHUNK = MXU_W
N_FF_CHUNKS = D_FF // FF_CHUNK
N_OUT_CHUNKS = D_MODEL // FF_CHUNK
TQ = 2 * BLOCK
PAD = BLOCK - N_META
NEG_BIG = -1e30
M_INIT = 2 * NEG_BIG
VMEM_LIMIT = 56 * 1024 * 1024

C_Z = 0
C_XBC = C_Z + SSD_D
C_FQ = C_XBC + SSD_CONV_DIM
C_FK = C_FQ + FOX_HEADS * LANES
C_CQ = C_FK + FOX_HEADS * LANES
C_CKV = C_CQ + MLA_Q_LORA
C_KR = C_CKV + MLA_KV_LORA
C_KRR = C_KR + LANES
C_SMALL = C_KRR + LANES
N_IN_ARR = C_SMALL + LANES
SMALL_DT = 0
SMALL_F = 8
HEAD_W = LANES
FOX_BIAS_LANE = FOX_HEAD_DIM
MLA_MASK_LANE = MLA_NOPE + MLA_ROPE


def _sigmoid(x):
    return 1.0 / (1.0 + jnp.exp(-x))


def _softplus(x):
    return jnp.maximum(x, 0.0) + jnp.log(1.0 + jnp.exp(-jnp.abs(x)))


def _layer_norm_rows(y, g, b):
    mu = jnp.mean(y, axis=-1, keepdims=True)
    yc = y - mu
    var = jnp.mean(yc * yc, axis=-1, keepdims=True)
    return yc * lax.rsqrt(var + EPS) * g + b


def _split3(x):
    x1 = x.astype(BF16)
    r1 = x - x1.astype(F32)
    x2 = r1.astype(BF16)
    r2 = r1 - x2.astype(F32)
    return x1, x2, r2.astype(BF16)


def _dot(a, b):
    return jnp.dot(a, b, preferred_element_type=F32)


def _dot_nt(a, b):
    return lax.dot_general(a, b, (((1,), (1,)), ((), ())), preferred_element_type=F32)


def _resident(shape):
    return pl.BlockSpec(shape, lambda *_: (0,) * len(shape), pipeline_mode=pl.Buffered(1))


def _ffn_ln_kernel(n_main_tiles, *refs):
    if n_main_tiles is None:
        x_ref, wg_ref, wu_ref, wd_ref, g_ref, b_ref, o_ref, a_scr, y_scr = refs
        x = x_ref[...]
    else:
        x_ref, m_ref, wg_ref, wu_ref, wd_ref, g_ref, b_ref, o_ref, a_scr, y_scr = refs
        x = jnp.where(pl.program_id(0) < n_main_tiles, x_ref[...], m_ref[...])
    xb = x.astype(BF16)

    for c in range(N_FF_CHUNKS):
        cols = slice(c * FF_CHUNK, (c + 1) * FF_CHUNK)
        gate = _dot(xb, wg_ref[:, cols])
        up = _dot(xb, wu_ref[:, cols])
        a_scr[c] = (gate * _sigmoid(gate) * up).astype(BF16)

    for n in range(N_OUT_CHUNKS):
        cols = slice(n * FF_CHUNK, (n + 1) * FF_CHUNK)
        acc = _dot(a_scr[0], wd_ref[0:FF_CHUNK, cols])
        for c in range(1, N_FF_CHUNKS):
            acc = acc + _dot(a_scr[c], wd_ref[c * FF_CHUNK:(c + 1) * FF_CHUNK, cols])
        y_scr[:, cols] = ALPHA * x[:, cols] + 0.5 * acc
    o_ref[...] = _layer_norm_rows(y_scr[...], g_ref[...], b_ref[...])


def _ffn_ln(h, meta_tile, wg, wu, wd, g, b, tm, n_out_rows):
    n_main_tiles = h.shape[0] // tm
    rows = pl.BlockSpec((tm, D_MODEL), lambda i: (i, 0))
    if meta_tile is None:
        x_specs, x_args, n_main = [rows], [h], None
    else:
        x_specs = [pl.BlockSpec((tm, D_MODEL), lambda i: (jnp.minimum(i, n_main_tiles - 1), 0)),
                   _resident(meta_tile.shape)]
        x_args, n_main = [h, meta_tile], n_main_tiles
    return pl.pallas_call(
        functools.partial(_ffn_ln_kernel, n_main),
        grid=(n_out_rows // tm,),
        in_specs=x_specs + [_resident(wg.shape), _resident(wu.shape), _resident(wd.shape),
                            _resident(g.shape), _resident(b.shape)],
        out_specs=rows,
        out_shape=jax.ShapeDtypeStruct((n_out_rows, D_MODEL), F32),
        scratch_shapes=[pltpu.VMEM((N_FF_CHUNKS, tm, FF_CHUNK), BF16),
                        pltpu.VMEM((tm, D_MODEL), F32)],
        compiler_params=pltpu.CompilerParams(
            dimension_semantics=("arbitrary",), vmem_limit_bytes=VMEM_LIMIT),
        name="ffn_ln",
    )(*x_args, wg, wu, wd, g, b)


def _rms_rows(x, g):
    return x * lax.rsqrt(jnp.mean(x * x, axis=-1, keepdims=True) + EPS) * g


def _store_t_blocks(out_ref, val_t):
    for r in range(out_ref.shape[0]):
        out_ref[r] = val_t[:, r * BLOCK:(r + 1) * BLOCK].astype(out_ref.dtype)


def _proj_kernel(h_ref, win_ref, wfvt_ref, wuq_ref, wuqr_ref, wkk_ref, wkvt_ref, qg_ref, kvg_ref,
                 cos_ref, sin_ref, kadd_ref,
                 zx_ref, small_ref, fq_ref, fk_ref, fvt_ref, mq_ref, mk_ref, mvt_ref):
    hb = h_ref[...].astype(BF16)
    zx_ref[...] = _dot(hb, win_ref[:, C_Z:C_FQ])
    small_ref[...] = _dot(hb, win_ref[:, C_SMALL:N_IN_ARR])

    lane = lax.broadcasted_iota(jnp.int32, (1, LANES), 1)
    fox_one = jnp.where((lane >= FOX_BIAS_LANE) & (lane < FOX_BIAS_LANE + 3), 1.0, 0.0)
    mla_one = jnp.where(lane == MLA_MASK_LANE, 1.0, 0.0)

    fq = _dot(hb, win_ref[:, C_FQ:C_FK]) * (FOX_HEAD_DIM ** -0.5)
    for hd in range(FOX_HEADS):
        sl = slice(hd * HEAD_W, (hd + 1) * HEAD_W)
        fq_ref[:, sl] = (fq[:, sl] + fox_one).astype(BF16)
    fk_ref[...] = _dot(hb, win_ref[:, C_FK:C_CQ]).astype(BF16)
    _store_t_blocks(fvt_ref, _dot_nt(wfvt_ref[...], hb))

    cos = cos_ref[...]
    sin = sin_ref[...]
    cqn = _rms_rows(_dot(hb, win_ref[:, C_CQ:C_CKV]), qg_ref[...]).astype(BF16)
    q = _dot(cqn, wuq_ref[...])
    qr = _dot(cqn, wuqr_ref[...])
    kvn = _rms_rows(_dot(hb, win_ref[:, C_CKV:C_KR]), kvg_ref[...]).astype(BF16)
    kn = _dot(kvn, wkk_ref[...])
    _store_t_blocks(mvt_ref, _dot_nt(wkvt_ref[...], kvn))
    krope = (_dot(hb, win_ref[:, C_KR:C_KRR]) * cos
             + _dot(hb, win_ref[:, C_KRR:C_SMALL]) * sin + kadd_ref[...])
    scale = (MLA_NOPE + MLA_ROPE) ** -0.5
    for hd in range(MLA_HEADS):
        sl = slice(hd * HEAD_W, (hd + 1) * HEAD_W)
        mq_ref[:, sl] = ((q[:, sl] * cos + qr[:, sl] * sin) * scale + mla_one).astype(BF16)
        mk_ref[:, sl] = (kn[:, sl] + krope).astype(BF16)


def _proj(h, win, wfvt, wuq, wuqr, wkk, wkvt, qg, kvg, tables, tm, seq):
    n_rows = h.shape[0]
    n_main_tiles = n_rows // tm - 1
    tiles_per_seq = seq // tm

    def rows(width):
        return pl.BlockSpec((tm, width), lambda i: (i, 0))

    tab = pl.BlockSpec((tm, LANES), lambda i: (
        jnp.where(i < n_main_tiles, i % tiles_per_seq, tiles_per_seq), 0))
    t_spec = pl.BlockSpec((tm // BLOCK, FOX_D, BLOCK), lambda i: (i, 0, 0))
    t_shape = jax.ShapeDtypeStruct((n_rows // BLOCK, FOX_D, BLOCK), BF16)
    qk_w = FOX_HEADS * HEAD_W
    row_outs = [(C_FQ, F32), (LANES, F32), (qk_w, BF16), (qk_w, BF16)]
    weights = [win, wfvt, wuq, wuqr, wkk, wkvt, qg, kvg]
    return pl.pallas_call(
        _proj_kernel,
        grid=(n_rows // tm,),
        in_specs=[rows(D_MODEL)] + [_resident(w.shape) for w in weights] + [tab, tab, tab],
        out_specs=([rows(w) for w, _ in row_outs] + [t_spec, rows(qk_w), rows(qk_w), t_spec]),
        out_shape=([jax.ShapeDtypeStruct((n_rows, w), dt) for w, dt in row_outs]
                   + [t_shape, jax.ShapeDtypeStruct((n_rows, qk_w), BF16),
                      jax.ShapeDtypeStruct((n_rows, qk_w), BF16), t_shape]),
        compiler_params=pltpu.CompilerParams(
            dimension_semantics=("arbitrary",), vmem_limit_bytes=VMEM_LIMIT),
        name="in_proj",
    )(h, *weights, *tables)


def _ssd_kernel(zx_ref, small_ref, cw_ref, cb_ref, dtb_ref, a_ref, dsk_ref, ng_ref,
                zero_fill_ref, o_ref, conv_scr, s_scr, meta_conv_scr, meta_s_scr):
    del zero_fill_ref
    b = pl.program_id(0)
    c = pl.program_id(1)

    @pl.when((c == 0) & (b == 0))
    def _():
        conv_scr[0:8, :] = jnp.zeros((8, SSD_CONV_DIM), F32)
        s_scr[...] = jnp.zeros(s_scr.shape, F32)

    @pl.when((c == 0) & (b > 0))
    def _():
        conv_scr[0:8, :] = meta_conv_scr[...]
        s_scr[...] = meta_s_scr[...]

    @pl.when((c > 0) | (b == 0))
    def _():
        _ssd_chunk(c, zx_ref, small_ref, cw_ref, cb_ref, dtb_ref, a_ref, dsk_ref, ng_ref,
                   o_ref, conv_scr, s_scr)

    @pl.when((c == 0) & (b == 0))
    def _():
        meta_conv_scr[...] = conv_scr[0:8, :]
        meta_s_scr[...] = s_scr[...]


def _ssd_chunk(c, zx_ref, small_ref, cw_ref, cb_ref, dtb_ref, a_ref, dsk_ref, ng_ref,
               o_ref, conv_scr, s_scr):
    Q = BLOCK
    row = lax.broadcasted_iota(jnp.int32, (Q, 1), 0)
    valid = jnp.logical_or(c > 0, row >= PAD)
    lane = lax.broadcasted_iota(jnp.int32, (1, LANES), 1)
    lane_lo = lane < SSD_HEAD_DIM
    sub = lax.broadcasted_iota(jnp.int32, (LANES, 1), 0)

    conv_scr[8:8 + Q, :] = jnp.where(valid, zx_ref[:, C_XBC:C_FQ], 0.0)
    acc = cb_ref[...]
    for k in range(SSD_CONV):
        off = 8 - (SSD_CONV - 1) + k
        acc = acc + cw_ref[k:k + 1, :] * conv_scr[off:off + Q, :]
    conv_scr[0:8, :] = conv_scr[Q:Q + 8, :]
    xbc = acc * _sigmoid(acc)
    bm = xbc[:, SSD_D:SSD_D + LANES]
    cm = xbc[:, SSD_D + LANES:SSD_D + 2 * LANES]

    dt = jnp.where(valid, _softplus(small_ref[...] + dtb_ref[...]), 0.0)
    a = dt * a_ref[...]
    tri = (lax.broadcasted_iota(jnp.int32, (Q, Q), 0)
           >= lax.broadcasted_iota(jnp.int32, (Q, Q), 1))
    tri_b = jnp.where(tri, 1.0, 0.0).astype(BF16)
    a1, a2, a3 = _split3(a)
    a_cum = _dot(tri_b, a1) + _dot(tri_b, a2) + _dot(tri_b, a3)
    a_cum_t = a_cum.T
    bm_t = bm.T

    cm_b = cm.astype(BF16)
    bm_b = bm.astype(BF16)
    cb_g = [_dot_nt(jnp.where(lane_lo, cm, 0.0).astype(BF16), bm_b),
            _dot_nt(jnp.where(lane_lo, 0.0, cm).astype(BF16), bm_b)]
    rows_g = [sub < SSD_STATE, sub >= SSD_STATE]

    pairs_per_group = SSD_HEADS // 2 // SSD_GROUPS
    y_pairs = []
    for p in range(SSD_HEADS // 2):
        g = p // pairs_per_group
        psl = slice(p * LANES, (p + 1) * LANES)
        xs_p = xbc[:, psl]
        dt_pair = jnp.where(lane_lo, dt[:, 2 * p:2 * p + 1], dt[:, 2 * p + 1:2 * p + 2])
        xdt = (xs_p * dt_pair).astype(BF16)
        s_old = s_scr[p]
        yd, upd, e_col, e_last = [], [], [], []
        for par in range(2):
            hd = 2 * p + par
            col = a_cum[:, hd:hd + 1]
            rowv = a_cum_t[hd:hd + 1, :]
            last = a_cum_t[hd:hd + 1, Q - 1:Q]
            seg = jnp.exp(jnp.where(tri, col - rowv, NEG_BIG))
            yd.append(_dot((cb_g[g] * seg).astype(BF16), xdt))
            upd.append(_dot((bm_t * jnp.exp(last - rowv)).astype(BF16), xdt))
            e_col.append(jnp.exp(col))
            e_last.append(jnp.exp(last))
        y_off = _dot(cm_b, s_old.astype(BF16)) * jnp.where(lane_lo, e_col[0], e_col[1])
        s_new = (jnp.where(lane_lo, e_last[0], e_last[1]) * s_old
                 + jnp.where(rows_g[g], jnp.where(lane_lo, upd[0], upd[1]), 0.0))
        s_scr[p] = s_new
        y_p = jnp.where(lane_lo, yd[0], yd[1]) + y_off + dsk_ref[:, psl] * xs_p
        z_p = zx_ref[:, psl]
        y_pairs.append(y_p * (z_p * _sigmoid(z_p)))

    for g in range(SSD_GROUPS):
        ps = range(g * pairs_per_group, (g + 1) * pairs_per_group)
        ss = sum(jnp.sum(y_pairs[p] * y_pairs[p], axis=-1, keepdims=True) for p in ps)
        inv = lax.rsqrt(ss * (1.0 / (pairs_per_group * LANES)) + EPS)
        for p in ps:
            psl = slice(p * LANES, (p + 1) * LANES)
            o_ref[:, psl] = (y_pairs[p] * inv * ng_ref[:, psl]).astype(BF16)


def _ssd(zx, small, cw, cb, dtb, a_row, dsk, ng, bsz, seq):
    n_rows = zx.shape[0]
    blocks_per_seq = seq // BLOCK
    meta_block = n_rows // BLOCK - 1

    def block(b, c):
        first = jnp.where(b == 0, meta_block, b * blocks_per_seq)
        return (jnp.where(c == 0, first, b * blocks_per_seq + c - 1), 0)

    params = [cw, cb, dtb, a_row, dsk, ng]
    return pl.pallas_call(
        _ssd_kernel,
        grid=(bsz, blocks_per_seq + 1),
        in_specs=[pl.BlockSpec((BLOCK, C_FQ), block), pl.BlockSpec((BLOCK, LANES), block)]
                 + [_resident(p.shape) for p in params] + [pl.BlockSpec(memory_space=pl.ANY)],
        out_specs=pl.BlockSpec((BLOCK, SSD_D), block),
        out_shape=jax.ShapeDtypeStruct((n_rows, SSD_D), BF16),
        input_output_aliases={2 + len(params): 0},
        scratch_shapes=[pltpu.VMEM((BLOCK + 8, SSD_CONV_DIM), F32),
                        pltpu.VMEM((SSD_HEADS // 2, LANES, LANES), F32),
                        pltpu.VMEM((8, SSD_CONV_DIM), F32),
                        pltpu.VMEM((SSD_HEADS // 2, LANES, LANES), F32)],
        compiler_params=pltpu.CompilerParams(
            dimension_semantics=("arbitrary", "arbitrary"), vmem_limit_bytes=VMEM_LIMIT),
        name="ssd_mixer",
    )(zx, small, *params, jnp.zeros((n_rows, SSD_D), BF16))


def _fox_keys_kernel(km_ref, k_ref, sm_ref, s_ref, fb_ref, om_ref, o_ref):
    T = BLOCK
    width = k_ref.shape[1]
    row = lax.broadcasted_iota(jnp.int32, (T, 1), 0)
    tri = (lax.broadcasted_iota(jnp.int32, (T, T), 0)
           >= lax.broadcasted_iota(jnp.int32, (T, T), 1))
    tri_b = jnp.where(tri, 1.0, 0.0).astype(BF16)
    src = lax.broadcasted_iota(jnp.int32, (LANES, width), 0)
    dst = lax.broadcasted_iota(jnp.int32, (LANES, width), 1)
    dst_head = jnp.right_shift(dst, HEAD_W.bit_length() - 1)
    dst_lane = jnp.bitwise_and(dst, HEAD_W - 1)
    sel = [jnp.where((src == SMALL_F + dst_head) & (dst_lane == FOX_BIAS_LANE + i),
                     1.0, 0.0).astype(BF16) for i in range(3)]

    def block(carry, k_blk, small_blk, is_meta):
        log_f = -_softplus(-(small_blk + fb_ref[...]))
        if is_meta:
            log_f = jnp.where(row < PAD, 0.0, log_f)
        f1, f2, f3 = _split3(log_f)
        c_blk = carry + _dot(tri_b, f1) + _dot(tri_b, f2) + _dot(tri_b, f3)
        bias = -c_blk
        if is_meta:
            bias = jnp.where(row < PAD, NEG_BIG, bias)
        b1, b2, b3 = _split3(bias)
        placed = _dot(b1, sel[0]) + _dot(b2, sel[1]) + _dot(b3, sel[2])
        return c_blk[T - 1:T, :], (k_blk.astype(F32) + placed).astype(BF16)

    carry, om_ref[...] = block(jnp.zeros((1, LANES), F32), km_ref[...], sm_ref[...], True)
    for j in range(k_ref.shape[0] // T):
        rows = slice(j * T, (j + 1) * T)
        carry, o_ref[rows, :] = block(carry, k_ref[rows, :], s_ref[rows, :], False)


def _fox_keys(k, small, fb, bsz, seq):
    n_rows, width = k.shape
    meta_block = n_rows // BLOCK - 1
    main = lambda w: pl.BlockSpec((seq, w), lambda b: (b, 0))
    meta = lambda w: pl.BlockSpec((BLOCK, w), lambda b: (meta_block, 0))
    return pl.pallas_call(
        _fox_keys_kernel,
        grid=(bsz,),
        in_specs=[meta(width), main(width), meta(LANES), main(LANES), _resident(fb.shape)],
        out_specs=[pl.BlockSpec((BLOCK, width), lambda b: (0, 0)), main(width)],
        out_shape=[jax.ShapeDtypeStruct((BLOCK, width), BF16),
                   jax.ShapeDtypeStruct((bsz * seq, width), BF16)],
        compiler_params=pltpu.CompilerParams(
            dimension_semantics=("arbitrary",), vmem_limit_bytes=VMEM_LIMIT),
        name="fox_keys",
    )(k, k, small, small, fb)


def _attn_kernel(n_heads, q_ref, km_ref, k_ref, vtm_ref, vt_ref, zero_fill_ref, o_ref,
                 acc_scr, s0_scr, s1_scr):
    del zero_fill_ref
    t = pl.program_id(1)
    dv = acc_scr.shape[1]
    heads = range(n_heads)
    hsl = [slice(hd * HEAD_W, (hd + 1) * HEAD_W) for hd in heads]
    vsl = [slice(hd * dv, (hd + 1) * dv) for hd in heads]
    qs = [q_ref[:, hsl[hd]] for hd in heads]
    ahead = (lax.broadcasted_iota(jnp.int32, (TQ, TQ), 0)
             - lax.broadcasted_iota(jnp.int32, (TQ, TQ), 1))
    acc_scr[...] = jnp.zeros(acc_scr.shape, F32)

    def softmax_pv(ss, vts, carry, mask):
        ms, ls = carry
        if mask is not None:
            ss = [jnp.where(mask, s, NEG_BIG) for s in ss]
        new_ms = [jnp.maximum(ms[hd], jnp.max(ss[hd], axis=0, keepdims=True)) for hd in heads]
        corrs = [jnp.exp(ms[hd] - new_ms[hd]) for hd in heads]
        prs = [jnp.exp(ss[hd] - new_ms[hd]) for hd in heads]
        new_ls = [corrs[hd] * ls[hd] + jnp.sum(prs[hd], axis=0, keepdims=True) for hd in heads]
        for hd in heads:
            acc_scr[hd] = corrs[hd] * acc_scr[hd] + _dot(vts[hd], prs[hd].astype(BF16))
        return tuple(new_ms), tuple(new_ls)

    def meta_chunk(carry, mask):
        return softmax_pv([_dot_nt(km_ref[:, hsl[hd]], qs[hd]) for hd in heads],
                          [vtm_ref[0, vsl[hd], :] for hd in heads], carry, mask)

    def scores(c, slot_scr):
        r0 = pl.multiple_of(c * TQ, TQ)
        for hd in heads:
            slot_scr[hd] = _dot_nt(k_ref[pl.ds(r0, TQ), hsl[hd]], qs[hd])

    def consume(c, slot_scr, carry, mask):
        n_sub = TQ // BLOCK
        vts = [jnp.concatenate([vt_ref[n_sub * c + i, vsl[hd], :] for i in range(n_sub)], axis=1)
               for hd in heads]
        return softmax_pv([slot_scr[hd] for hd in heads], vts, carry, mask)

    def finish(ls):
        for p in range(n_heads // 2):
            y_t = jnp.concatenate([acc_scr[2 * p] * (1.0 / ls[2 * p]),
                                   acc_scr[2 * p + 1] * (1.0 / ls[2 * p + 1])], axis=0)
            o_ref[:, p * LANES:(p + 1) * LANES] = y_t.T.astype(BF16)

    init = (tuple(jnp.full((1, TQ), M_INIT, F32) for _ in heads),
            tuple(jnp.zeros((1, TQ), F32) for _ in heads))

    @pl.when((t == 0) & (pl.program_id(0) == 0))
    def _():
        _, ls = meta_chunk(init, ahead[:BLOCK, :] <= -(TQ - BLOCK))
        finish(ls)

    @pl.when(t > 0)
    def _():
        scores(0, s0_scr)
        carry = meta_chunk(init, None)

        def pair(i, cr):
            c = 2 * i
            scores(c + 1, s1_scr)
            cr = consume(c, s0_scr, cr, None)
            scores(c + 2, s0_scr)
            return consume(c + 1, s1_scr, cr, None)

        n_unmasked = t - 1
        carry = lax.fori_loop(0, n_unmasked // 2, pair, carry)
        c = 2 * (n_unmasked // 2)
        diagonal = ahead <= 0

        def odd_tail(cr):
            scores(c + 1, s1_scr)
            cr = consume(c, s0_scr, cr, None)
            return consume(c + 1, s1_scr, cr, diagonal)

        _, ls = lax.cond(n_unmasked % 2 == 1, odd_tail,
                         lambda cr: consume(c, s0_scr, cr, diagonal), carry)
        finish(ls)


def _attention(q, k_meta, k_main, vt, n_heads, bsz, seq, meta_k_block, name):
    n_rows, qw = q.shape
    dv = vt.shape[1] // n_heads
    tiles_per_seq = seq // TQ
    meta_q_tile = n_rows // TQ - 1
    meta_block = n_rows // BLOCK - 1

    def q_tile(b, t):
        first = jnp.where(b == 0, meta_q_tile, b * tiles_per_seq)
        return (jnp.where(t == 0, first, b * tiles_per_seq + t - 1), 0)

    return pl.pallas_call(
        functools.partial(_attn_kernel, n_heads),
        grid=(bsz, tiles_per_seq + 1),
        in_specs=[pl.BlockSpec((TQ, qw), q_tile),
                  pl.BlockSpec((BLOCK, qw), lambda b, t: (meta_k_block, 0)),
                  pl.BlockSpec((seq, qw), lambda b, t: (b, 0)),
                  pl.BlockSpec((1,) + vt.shape[1:], lambda b, t: (meta_block, 0, 0)),
                  pl.BlockSpec((seq // BLOCK,) + vt.shape[1:], lambda b, t: (b, 0, 0)),
                  pl.BlockSpec(memory_space=pl.ANY)],
        out_specs=pl.BlockSpec((TQ, n_heads * dv), q_tile),
        out_shape=jax.ShapeDtypeStruct((n_rows, n_heads * dv), BF16),
        input_output_aliases={5: 0},
        scratch_shapes=[pltpu.VMEM((n_heads, dv, TQ), F32),
                        pltpu.VMEM((n_heads, TQ, TQ), F32),
                        pltpu.VMEM((n_heads, TQ, TQ), F32)],
        compiler_params=pltpu.CompilerParams(
            dimension_semantics=("arbitrary", "arbitrary"), vmem_limit_bytes=VMEM_LIMIT),
        name=name,
    )(q, k_meta, k_main, vt, vt, jnp.zeros((n_rows, n_heads * dv), BF16))


def _out_ln_kernel(h_ref, ys_ref, yf_ref, ym_ref, w_ref, g_ref, b_ref, o_ref):
    mix = (_dot(ys_ref[...], w_ref[0:SSD_D, :])
           + _dot(yf_ref[...], w_ref[SSD_D:SSD_D + FOX_D, :])
           + _dot(ym_ref[...], w_ref[SSD_D + FOX_D:D_MIX, :]))
    o_ref[...] = _layer_norm_rows(ALPHA * h_ref[...] + mix, g_ref[...], b_ref[...])


def _out_ln(h, ys, yf, ym, w, g, b, tm):
    n_rows = h.shape[0]

    def rows(width):
        return pl.BlockSpec((tm, width), lambda i: (i, 0))

    return pl.pallas_call(
        _out_ln_kernel,
        grid=(n_rows // tm,),
        in_specs=[rows(D_MODEL), rows(SSD_D), rows(FOX_D), rows(MLA_D),
                  _resident(w.shape), _resident(g.shape), _resident(b.shape)],
        out_specs=rows(D_MODEL),
        out_shape=jax.ShapeDtypeStruct((n_rows, D_MODEL), F32),
        compiler_params=pltpu.CompilerParams(
            dimension_semantics=("arbitrary",), vmem_limit_bytes=VMEM_LIMIT),
        name="out_proj_ln",
    )(h, ys, yf, ym, w, g, b)


def _rot_half_cols(w):
    half = MLA_ROPE // 2
    return jnp.concatenate([-w[..., half:], w[..., :half]], axis=-1)


def _in_proj_weights(w_in):
    sizes = [SSD_D, SSD_CONV_DIM, SSD_HEADS, FOX_D, FOX_D, FOX_D, FOX_HEADS,
             MLA_Q_LORA, MLA_KV_LORA, MLA_ROPE]
    splits = [int(s) for s in np.cumsum(sizes)[:-1]]
    z, xbc, dt, fq, fk, fv, fr, cq, ckv, kr = jnp.split(w_in, splits, axis=-1)
    zeros = lambda n: jnp.zeros((D_MODEL, n), w_in.dtype)
    rope_pad = LANES - MLA_NOPE - MLA_ROPE
    kr128 = jnp.concatenate([zeros(MLA_NOPE), kr, zeros(rope_pad)], axis=-1)
    krr128 = jnp.concatenate([zeros(MLA_NOPE), _rot_half_cols(kr), zeros(rope_pad)], axis=-1)
    small = jnp.concatenate([dt, fr, zeros(LANES - SMALL_F - FOX_HEADS)], axis=-1)

    def per_head(w):
        w = w.reshape(D_MODEL, FOX_HEADS, FOX_HEAD_DIM)
        return jnp.pad(w, ((0, 0), (0, 0), (0, HEAD_W - FOX_HEAD_DIM))).reshape(D_MODEL, -1)

    win = jnp.concatenate([z, xbc, per_head(fq), per_head(fk), cq, ckv, kr128, krr128, small],
                          axis=-1).astype(BF16)
    return win, fv.T.astype(BF16)


def _mla_weights(w_uq, w_ukv):
    rope_pad = LANES - MLA_NOPE - MLA_ROPE
    wq = w_uq.reshape(MLA_Q_LORA, MLA_HEADS, MLA_NOPE + MLA_ROPE)
    zq = jnp.zeros((MLA_Q_LORA, MLA_HEADS, rope_pad), w_uq.dtype)
    wuq = jnp.concatenate([wq, zq], axis=-1).reshape(MLA_Q_LORA, MLA_HEADS * LANES)
    wuqr = jnp.concatenate([jnp.zeros_like(wq[..., :MLA_NOPE]),
                            _rot_half_cols(wq[..., MLA_NOPE:]), zq],
                           axis=-1).reshape(MLA_Q_LORA, MLA_HEADS * LANES)
    wkv = w_ukv.reshape(MLA_KV_LORA, MLA_HEADS, MLA_NOPE + MLA_V)
    wkk = jnp.concatenate([wkv[..., :MLA_NOPE], jnp.zeros_like(wkv[..., MLA_NOPE:])],
                          axis=-1).reshape(MLA_KV_LORA, MLA_HEADS * LANES)
    wvt = wkv[..., MLA_NOPE:].reshape(MLA_KV_LORA, MLA_D).T
    return wuq.astype(BF16), wuqr.astype(BF16), wkk.astype(BF16), wvt.astype(BF16)


def _lane_row(v, width=LANES):
    v = v.astype(F32)
    return jnp.pad(v, (0, width - v.shape[0]))[None, :]


def _position_tables(seq, tm):
    pos = jnp.concatenate([N_META + jnp.arange(seq, dtype=F32),
                           jnp.arange(tm, dtype=F32) - (tm - N_META)])
    is_pad = jnp.concatenate([jnp.zeros((seq,), bool), jnp.arange(tm) < tm - N_META])
    inv_freq = 1.0 / (ROPE_THETA ** (jnp.arange(0, MLA_ROPE, 2, dtype=F32) / MLA_ROPE))
    ang = pos[:, None] * inv_freq[None, :]
    cos, sin = jnp.cos(ang), jnp.sin(ang)
    n = seq + tm
    rope_pad = LANES - MLA_NOPE - MLA_ROPE
    cos128 = jnp.concatenate([jnp.ones((n, MLA_NOPE), F32), cos, cos,
                              jnp.zeros((n, rope_pad), F32)], axis=-1)
    sin128 = jnp.concatenate([jnp.zeros((n, MLA_NOPE), F32), sin, sin,
                              jnp.zeros((n, rope_pad), F32)], axis=-1)
    kadd = jnp.where(is_pad[:, None] & (jnp.arange(LANES)[None, :] == MLA_MASK_LANE),
                     NEG_BIG, 0.0).astype(F32)
    return cos128, sin128, kadd


def kernel(x, meta, ffn1_w_gate, ffn1_w_up, ffn1_w_down, ln1_g, ln1_b, w_in, conv_w, conv_b, dt_bias, a_log, d_skip, ssd_norm_g, fox_f_b, mla_q_norm_g, mla_w_uq, mla_kv_norm_g, mla_w_ukv, w_out, ln2_g, ln2_b, ffn2_w_gate, ffn2_w_up, ffn2_w_down, ln3_g, ln3_b):
    bsz, seq, _ = x.shape
    assert seq % TQ == 0
    tm = 2 * TQ if seq % (2 * TQ) == 0 else TQ
    n_main = bsz * seq
    n_rows = n_main + tm
    meta_block = n_rows // BLOCK - 1

    h = x.reshape(n_main, D_MODEL)
    meta_tile = jnp.concatenate([jnp.zeros((tm - N_META, D_MODEL), x.dtype),
                                 meta.astype(x.dtype)], axis=0)
    tables = _position_tables(seq, tm)
    row = lambda v: v.astype(F32)[None, :]
    bf = lambda w: w.astype(BF16)

    for l in range(DEPTH):
        h = _ffn_ln(h, meta_tile if l == 0 else None, bf(ffn1_w_gate[l]), bf(ffn1_w_up[l]),
                    bf(ffn1_w_down[l]), row(ln1_g[l]), row(ln1_b[l]), tm, n_rows)

        win, wfvt = _in_proj_weights(w_in[l])
        wuq, wuqr, wkk, wkvt = _mla_weights(mla_w_uq[l], mla_w_ukv[l])
        zx, small, fq, fk, fvt, mq, mk, mvt = _proj(
            h, win, wfvt, wuq, wuqr, wkk, wkvt, row(mla_q_norm_g[l]), row(mla_kv_norm_g[l]),
            tables, tm, seq)

        a_row = _lane_row(-jnp.exp(a_log[l].astype(F32)))
        dsk = jnp.repeat(d_skip[l].astype(F32), SSD_HEAD_DIM)[None, :]
        y_ssd = _ssd(zx, small, conv_w[l].astype(F32), row(conv_b[l]), _lane_row(dt_bias[l]),
                     a_row, dsk, row(ssd_norm_g[l]), bsz, seq)

        fb = jnp.pad(fox_f_b[l].astype(F32), (SMALL_F, LANES - SMALL_F - FOX_HEADS))[None, :]
        fk_meta, fk_main = _fox_keys(fk, small, fb, bsz, seq)
        y_fox = _attention(fq, fk_meta, fk_main, fvt, FOX_HEADS, bsz, seq, 0, "fox_attention")
        y_mla = _attention(mq, mk, mk, mvt, MLA_HEADS, bsz, seq, meta_block, "mla_attention")

        h = _out_ln(h, y_ssd, y_fox, y_mla, bf(w_out[l]), row(ln2_g[l]), row(ln2_b[l]), tm)

        h = _ffn_ln(h, None, bf(ffn2_w_gate[l]), bf(ffn2_w_up[l]), bf(ffn2_w_down[l]),
                    row(ln3_g[l]), row(ln3_b[l]), tm, n_rows if l < DEPTH - 1 else n_main)

    return h.reshape(bsz, seq, D_MODEL)
```

```python
import functools

import numpy as np
import jax
import jax.numpy as jnp
from jax import lax
from jax.experimental import pallas as pl
from jax.experimental.pallas import tpu as pltpu

F32 = jnp.float32
BF16 = jnp.bfloat16

D_MODEL = 1024
DEPTH = 2
N_META = 16
BLOCK = 128
SSD_HEADS = 8
SSD_HEAD_DIM = 64
SSD_D = SSD_HEADS * SSD_HEAD_DIM
SSD_GROUPS = 2
SSD_STATE = 64
SSD_CONV = 4
SSD_CONV_DIM = SSD_D + 2 * SSD_GROUPS * SSD_STATE
FOX_HEADS = 4
FOX_HEAD_DIM = 64
FOX_D = FOX_HEADS * FOX_HEAD_DIM
MLA_HEADS = 4
MLA_Q_LORA = 256
MLA_KV_LORA = 128
MLA_NOPE = 64
MLA_ROPE = 32
MLA_V = 64
MLA_D = MLA_HEADS * MLA_V
ROPE_THETA = 10000.0
D_MIX = SSD_D + FOX_D + MLA_D
D_FF = 2816
ALPHA = (2 * DEPTH) ** 0.25
EPS = 1e-5

LANES = 128
MXU_W = 256
FF_CHUNK = MXU_W
N_FF_CHUNKS = D_FF // FF_CHUNK
N_OUT_CHUNKS = D_MODEL // FF_CHUNK
TQ = 2 * BLOCK
PAD = BLOCK - N_META
NEG_BIG = -1e30
M_INIT = 2 * NEG_BIG
VMEM_LIMIT = 56 * 1024 * 1024

C_Z = 0
C_XBC = C_Z + SSD_D
C_FQ = C_XBC + SSD_CONV_DIM
C_FK = C_FQ + FOX_HEADS * LANES
C_CQ = C_FK + FOX_HEADS * LANES
C_CKV = C_CQ + MLA_Q_LORA
C_SMALL = C_CKV + MLA_KV_LORA
C_KR = C_SMALL + LANES
C_KRR = C_KR + LANES
N_IN_ARR = C_KRR + LANES
SMALL_DT = 0
SMALL_F = 8
HEAD_W = LANES
FOX_BIAS_LANE = FOX_HEAD_DIM
MLA_MASK_LANE = MLA_NOPE + MLA_ROPE


def _sigmoid(x):
    return 1.0 / (1.0 + jnp.exp(-x))


def _softplus(x):
    return jnp.maximum(x, 0.0) + jnp.log(1.0 + jnp.exp(-jnp.abs(x)))


def _layer_norm_rows(y, g, b):
    mu = jnp.mean(y, axis=-1, keepdims=True)
    yc = y - mu
    var = jnp.mean(yc * yc, axis=-1, keepdims=True)
    return yc * lax.rsqrt(var + EPS) * g + b


def _split3(x):
    x1 = x.astype(BF16)
    r1 = x - x1.astype(F32)
    x2 = r1.astype(BF16)
    r2 = r1 - x2.astype(F32)
    return x1, x2, r2.astype(BF16)


def _dot(a, b):
    return jnp.dot(a, b, preferred_element_type=F32)


def _dot_nt(a, b):
    return lax.dot_general(a, b, (((1,), (1,)), ((), ())), preferred_element_type=F32)


def _resident(shape):
    return pl.BlockSpec(shape, lambda *_: (0,) * len(shape), pipeline_mode=pl.Buffered(1))


def _layer(arr, l):
    tail = (0,) * (arr.ndim - 1)
    return pl.BlockSpec((None,) + arr.shape[1:], lambda *_: (l,) + tail,
                        pipeline_mode=pl.Buffered(1))


def _ffn_ln_kernel(prologue, n_main_tiles, *refs):
    wg_ref, wu_ref, wd_ref, g_ref, b_ref, o_ref, a_scr, y_scr = refs[-8:]
    if prologue == "plain":
        x = refs[0][...]
    elif prologue == "meta":
        x_ref, m_ref = refs[:2]
        x = jnp.where(pl.program_id(0) < n_main_tiles, x_ref[...], m_ref[...])
    else:
        h_ref, ys_ref, yf_ref, ym_ref, wo_ref, g2_ref, b2_ref = refs[:7]
        mix = (_dot(ys_ref[...], wo_ref[0:SSD_D, :])
               + _dot(yf_ref[...], wo_ref[SSD_D:SSD_D + FOX_D, :])
               + _dot(ym_ref[...], wo_ref[SSD_D + FOX_D:D_MIX, :]))
        x = _layer_norm_rows(ALPHA * h_ref[...] + mix, g2_ref[...], b2_ref[...])
    xb = x.astype(BF16)

    for c in range(N_FF_CHUNKS):
        cols = slice(c * FF_CHUNK, (c + 1) * FF_CHUNK)
        gate = _dot(xb, wg_ref[:, cols])
        up = _dot(xb, wu_ref[:, cols])
        a_scr[c] = (gate * _sigmoid(gate) * up).astype(BF16)

    for n in range(N_OUT_CHUNKS):
        cols = slice(n * FF_CHUNK, (n + 1) * FF_CHUNK)
        acc = _dot(a_scr[0], wd_ref[0:FF_CHUNK, cols])
        for c in range(1, N_FF_CHUNKS):
            acc = acc + _dot(a_scr[c], wd_ref[c * FF_CHUNK:(c + 1) * FF_CHUNK, cols])
        y_scr[:, cols] = ALPHA * x[:, cols] + 0.5 * acc
    o_ref[...] = _layer_norm_rows(y_scr[...], g_ref[...], b_ref[...])


def _ffn_ln(l, h, ffn, tm, n_out_rows, meta_tile=None, mix=None):
    def rows(width):
        return pl.BlockSpec((tm, width), lambda i: (i, 0))

    if mix is not None:
        prologue, n_main = "mix", None
        x_args = [h, *mix]
        x_specs = ([rows(D_MODEL), rows(SSD_D), rows(FOX_D), rows(MLA_D)]
                   + [_layer(p, l) for p in mix[3:]])
    elif meta_tile is not None:
        prologue, n_main = "meta", h.shape[0] // tm
        x_args = [h, meta_tile]
        x_specs = [pl.BlockSpec((tm, D_MODEL), lambda i: (jnp.minimum(i, n_main - 1), 0)),
                   _resident(meta_tile.shape)]
    else:
        prologue, n_main, x_args, x_specs = "plain", None, [h], [rows(D_MODEL)]
    return pl.pallas_call(
        functools.partial(_ffn_ln_kernel, prologue, n_main),
        grid=(n_out_rows // tm,),
        in_specs=x_specs + [_layer(p, l) for p in ffn],
        out_specs=rows(D_MODEL),
        out_shape=jax.ShapeDtypeStruct((n_out_rows, D_MODEL), F32),
        scratch_shapes=[pltpu.VMEM((N_FF_CHUNKS, tm, FF_CHUNK), BF16),
                        pltpu.VMEM((tm, D_MODEL), F32)],
        compiler_params=pltpu.CompilerParams(
            dimension_semantics=("arbitrary",), vmem_limit_bytes=VMEM_LIMIT),
        name="ffn_ln",
    )(*x_args, *ffn)


def _rms_rows(x, g):
    return x * lax.rsqrt(jnp.mean(x * x, axis=-1, keepdims=True) + EPS) * g


def _store_t_blocks(out_ref, val_t):
    for r in range(out_ref.shape[0]):
        out_ref[r] = val_t[:, r * BLOCK:(r + 1) * BLOCK].astype(out_ref.dtype)


def _proj_kernel(h_ref, win_ref, wfvt_ref, wuq_ref, wuqr_ref, wkk_ref, wkvt_ref, qg_ref, kvg_ref,
                 cos_ref, sin_ref, kadd_ref,
                 zx_ref, small_ref, fq_ref, fk_ref, fvt_ref, mq_ref, mk_ref, mvt_ref):
    hb = h_ref[...].astype(BF16)
    zx_ref[...] = _dot(hb, win_ref[:, C_Z:C_FQ])
    ckv_small = _dot(hb, win_ref[:, C_CKV:C_KR])
    small_ref[...] = ckv_small[:, MLA_KV_LORA:]
    kr_both = _dot(hb, win_ref[:, C_KR:N_IN_ARR])

    lane = lax.broadcasted_iota(jnp.int32, (1, LANES), 1)
    fox_one = jnp.where((lane >= FOX_BIAS_LANE) & (lane < FOX_BIAS_LANE + 3), 1.0, 0.0)
    mla_one = jnp.where(lane == MLA_MASK_LANE, 1.0, 0.0)

    fq = _dot(hb, win_ref[:, C_FQ:C_FK]) * (FOX_HEAD_DIM ** -0.5)
    for hd in range(FOX_HEADS):
        sl = slice(hd * HEAD_W, (hd + 1) * HEAD_W)
        fq_ref[:, sl] = (fq[:, sl] + fox_one).astype(BF16)
    fk_ref[...] = _dot(hb, win_ref[:, C_FK:C_CQ]).astype(BF16)
    _store_t_blocks(fvt_ref, _dot_nt(wfvt_ref[...], hb))

    cos = cos_ref[...]
    sin = sin_ref[...]
    cqn = _rms_rows(_dot(hb, win_ref[:, C_CQ:C_CKV]), qg_ref[...]).astype(BF16)
    q = _dot(cqn, wuq_ref[...])
    qr = _dot(cqn, wuqr_ref[...])
    kvn = _rms_rows(ckv_small[:, :MLA_KV_LORA], kvg_ref[...]).astype(BF16)
    kn = _dot(kvn, wkk_ref[...])
    _store_t_blocks(mvt_ref, _dot_nt(wkvt_ref[...], kvn))
    krope = kr_both[:, :LANES] * cos + kr_both[:, LANES:] * sin + kadd_ref[...]
    scale = (MLA_NOPE + MLA_ROPE) ** -0.5
    for hd in range(MLA_HEADS):
        sl = slice(hd * HEAD_W, (hd + 1) * HEAD_W)
        mq_ref[:, sl] = ((q[:, sl] * cos + qr[:, sl] * sin) * scale + mla_one).astype(BF16)
        mk_ref[:, sl] = (kn[:, sl] + krope).astype(BF16)


def _proj(l, h, weights, tables, tm, seq):
    n_rows = h.shape[0]
    n_main_tiles = n_rows // tm - 1
    tiles_per_seq = seq // tm

    def rows(width):
        return pl.BlockSpec((tm, width), lambda i: (i, 0))

    tab = pl.BlockSpec((tm, LANES), lambda i: (
        jnp.where(i < n_main_tiles, i % tiles_per_seq, tiles_per_seq), 0))
    t_spec = pl.BlockSpec((tm // BLOCK, FOX_D, BLOCK), lambda i: (i, 0, 0))
    t_shape = jax.ShapeDtypeStruct((n_rows // BLOCK, FOX_D, BLOCK), BF16)
    qk_w = FOX_HEADS * HEAD_W
    row_outs = [(C_FQ, F32), (LANES, F32), (qk_w, BF16), (qk_w, BF16)]
    return pl.pallas_call(
        _proj_kernel,
        grid=(n_rows // tm,),
        in_specs=[rows(D_MODEL)] + [_layer(w, l) for w in weights] + [tab, tab, tab],
        out_specs=([rows(w) for w, _ in row_outs] + [t_spec, rows(qk_w), rows(qk_w), t_spec]),
        out_shape=([jax.ShapeDtypeStruct((n_rows, w), dt) for w, dt in row_outs]
                   + [t_shape, jax.ShapeDtypeStruct((n_rows, qk_w), BF16),
                      jax.ShapeDtypeStruct((n_rows, qk_w), BF16), t_shape]),
        compiler_params=pltpu.CompilerParams(
            dimension_semantics=("arbitrary",), vmem_limit_bytes=VMEM_LIMIT),
        name="in_proj",
    )(h, *weights, *tables)


def _ssd_kernel(zx_ref, small_ref, cw_ref, cb_ref, dtb_ref, a_ref, dsk_ref, ng_ref,
                zero_fill_ref, o_ref, conv_scr, s_scr, meta_conv_scr, meta_s_scr):
    del zero_fill_ref
    b = pl.program_id(0)
    c = pl.program_id(1)

    @pl.when((c == 0) & (b == 0))
    def _():
        conv_scr[0:8, :] = jnp.zeros((8, SSD_CONV_DIM), F32)
        s_scr[...] = jnp.zeros(s_scr.shape, F32)

    @pl.when((c == 0) & (b > 0))
    def _():
        conv_scr[0:8, :] = meta_conv_scr[...]
        s_scr[...] = meta_s_scr[...]

    @pl.when((c > 0) | (b == 0))
    def _():
        _ssd_chunk(c, zx_ref, small_ref, cw_ref, cb_ref, dtb_ref, a_ref, dsk_ref, ng_ref,
                   o_ref, conv_scr, s_scr)

    @pl.when((c == 0) & (b == 0))
    def _():
        meta_conv_scr[...] = conv_scr[0:8, :]
        meta_s_scr[...] = s_scr[...]


def _ssd_chunk(c, zx_ref, small_ref, cw_ref, cb_ref, dtb_ref, a_ref, dsk_ref, ng_ref,
               o_ref, conv_scr, s_scr):
    Q = BLOCK
    row = lax.broadcasted_iota(jnp.int32, (Q, 1), 0)
    valid = jnp.logical_or(c > 0, row >= PAD)
    lane = lax.broadcasted_iota(jnp.int32, (1, LANES), 1)
    lane_lo = lane < SSD_HEAD_DIM
    sub = lax.broadcasted_iota(jnp.int32, (LANES, 1), 0)

    conv_scr[8:8 + Q, :] = jnp.where(valid, zx_ref[:, C_XBC:C_FQ], 0.0)
    acc = cb_ref[...]
    for k in range(SSD_CONV):
        off = 8 - (SSD_CONV - 1) + k
        acc = acc + cw_ref[k:k + 1, :] * conv_scr[off:off + Q, :]
    conv_scr[0:8, :] = conv_scr[Q:Q + 8, :]
    xbc = acc * _sigmoid(acc)
    bm = xbc[:, SSD_D:SSD_D + LANES]
    cm = xbc[:, SSD_D + LANES:SSD_D + 2 * LANES]

    dt = jnp.where(valid, _softplus(small_ref[...] + dtb_ref[...]), 0.0)
    a = dt * a_ref[...]
    tri = (lax.broadcasted_iota(jnp.int32, (Q, Q), 0)
           >= lax.broadcasted_iota(jnp.int32, (Q, Q), 1))
    tri_b = jnp.where(tri, 1.0, 0.0).astype(BF16)
    a1, a2, a3 = _split3(a)
    a_cum = _dot(tri_b, a1) + _dot(tri_b, a2) + _dot(tri_b, a3)
    a_cum_t = a_cum.T
    bm_t = bm.T

    cm_b = cm.astype(BF16)
    bm_b = bm.astype(BF16)
    cb_g = [_dot_nt(jnp.where(lane_lo, cm, 0.0).astype(BF16), bm_b),
            _dot_nt(jnp.where(lane_lo, 0.0, cm).astype(BF16), bm_b)]
    rows_g = [sub < SSD_STATE, sub >= SSD_STATE]

    pairs_per_group = SSD_HEADS // 2 // SSD_GROUPS
    y_pairs = []
    for p in range(SSD_HEADS // 2):
        g = p // pairs_per_group
        psl = slice(p * LANES, (p + 1) * LANES)
        xs_p = xbc[:, psl]
        dt_pair = jnp.where(lane_lo, dt[:, 2 * p:2 * p + 1], dt[:, 2 * p + 1:2 * p + 2])
        xdt = (xs_p * dt_pair).astype(BF16)
        s_old = s_scr[p]
        yd, upd, e_col, e_last = [], [], [], []
        for par in range(2):
            hd = 2 * p + par
            col = a_cum[:, hd:hd + 1]
            rowv = a_cum_t[hd:hd + 1, :]
            last = a_cum_t[hd:hd + 1, Q - 1:Q]
            seg = jnp.exp(jnp.where(tri, col - rowv, NEG_BIG))
            yd.append(_dot((cb_g[g] * seg).astype(BF16), xdt))
            upd.append(_dot((bm_t * jnp.exp(last - rowv)).astype(BF16), xdt))
            e_col.append(jnp.exp(col))
            e_last.append(jnp.exp(last))
        y_off = _dot(cm_b, s_old.astype(BF16)) * jnp.where(lane_lo, e_col[0], e_col[1])
        s_new = (jnp.where(lane_lo, e_last[0], e_last[1]) * s_old
                 + jnp.where(rows_g[g], jnp.where(lane_lo, upd[0], upd[1]), 0.0))
        s_scr[p] = s_new
        y_p = jnp.where(lane_lo, yd[0], yd[1]) + y_off + dsk_ref[:, psl] * xs_p
        z_p = zx_ref[:, psl]
        y_pairs.append(y_p * (z_p * _sigmoid(z_p)))

    for g in range(SSD_GROUPS):
        ps = range(g * pairs_per_group, (g + 1) * pairs_per_group)
        ss = sum(jnp.sum(y_pairs[p] * y_pairs[p], axis=-1, keepdims=True) for p in ps)
        inv = lax.rsqrt(ss * (1.0 / (pairs_per_group * LANES)) + EPS)
        for p in ps:
            psl = slice(p * LANES, (p + 1) * LANES)
            o_ref[:, psl] = (y_pairs[p] * inv * ng_ref[:, psl]).astype(BF16)


def _ssd(l, zx, small, params, bsz, seq):
    n_rows = zx.shape[0]
    blocks_per_seq = seq // BLOCK
    meta_block = n_rows // BLOCK - 1

    def block(b, c):
        first = jnp.where(b == 0, meta_block, b * blocks_per_seq)
        return (jnp.where(c == 0, first, b * blocks_per_seq + c - 1), 0)

    return pl.pallas_call(
        _ssd_kernel,
        grid=(bsz, blocks_per_seq + 1),
        in_specs=[pl.BlockSpec((BLOCK, C_FQ), block), pl.BlockSpec((BLOCK, LANES), block)]
                 + [_layer(p, l) for p in params] + [pl.BlockSpec(memory_space=pl.ANY)],
        out_specs=pl.BlockSpec((BLOCK, SSD_D), block),
        out_shape=jax.ShapeDtypeStruct((n_rows, SSD_D), BF16),
        input_output_aliases={2 + len(params): 0},
        scratch_shapes=[pltpu.VMEM((BLOCK + 8, SSD_CONV_DIM), F32),
                        pltpu.VMEM((SSD_HEADS // 2, LANES, LANES), F32),
                        pltpu.VMEM((8, SSD_CONV_DIM), F32),
                        pltpu.VMEM((SSD_HEADS // 2, LANES, LANES), F32)],
        compiler_params=pltpu.CompilerParams(
            dimension_semantics=("arbitrary", "arbitrary"), vmem_limit_bytes=VMEM_LIMIT),
        name="ssd_mixer",
    )(zx, small, *params, jnp.zeros((n_rows, SSD_D), BF16))


def _fox_keys_kernel(km_ref, k_ref, sm_ref, s_ref, fb_ref, om_ref, o_ref):
    T = BLOCK
    width = k_ref.shape[1]
    row = lax.broadcasted_iota(jnp.int32, (T, 1), 0)
    tri = (lax.broadcasted_iota(jnp.int32, (T, T), 0)
           >= lax.broadcasted_iota(jnp.int32, (T, T), 1))
    tri_b = jnp.where(tri, 1.0, 0.0).astype(BF16)
    src = lax.broadcasted_iota(jnp.int32, (LANES, width), 0)
    dst = lax.broadcasted_iota(jnp.int32, (LANES, width), 1)
    dst_head = jnp.right_shift(dst, HEAD_W.bit_length() - 1)
    dst_lane = jnp.bitwise_and(dst, HEAD_W - 1)
    sel = [jnp.where((src == SMALL_F + dst_head) & (dst_lane == FOX_BIAS_LANE + i),
                     1.0, 0.0).astype(BF16) for i in range(3)]

    def block(carry, k_blk, small_blk, is_meta):
        log_f = -_softplus(-(small_blk + fb_ref[...]))
        if is_meta:
            log_f = jnp.where(row < PAD, 0.0, log_f)
        f1, f2, f3 = _split3(log_f)
        c_blk = carry + _dot(tri_b, f1) + _dot(tri_b, f2) + _dot(tri_b, f3)
        bias = -c_blk
        if is_meta:
            bias = jnp.where(row < PAD, NEG_BIG, bias)
        b1, b2, b3 = _split3(bias)
        placed = _dot(b1, sel[0]) + _dot(b2, sel[1]) + _dot(b3, sel[2])
        return c_blk[T - 1:T, :], (k_blk.astype(F32) + placed).astype(BF16)

    carry, om_ref[...] = block(jnp.zeros((1, LANES), F32), km_ref[...], sm_ref[...], True)
    for j in range(k_ref.shape[0] // T):
        rows = slice(j * T, (j + 1) * T)
        carry, o_ref[rows, :] = block(carry, k_ref[rows, :], s_ref[rows, :], False)


def _fox_keys(l, k, small, fb, bsz, seq):
    n_rows, width = k.shape
    meta_block = n_rows // BLOCK - 1
    main = lambda w: pl.BlockSpec((seq, w), lambda b: (b, 0))
    meta = lambda w: pl.BlockSpec((BLOCK, w), lambda b: (meta_block, 0))
    return pl.pallas_call(
        _fox_keys_kernel,
        grid=(bsz,),
        in_specs=[meta(width), main(width), meta(LANES), main(LANES), _layer(fb, l)],
        out_specs=[pl.BlockSpec((BLOCK, width), lambda b: (0, 0)), main(width)],
        out_shape=[jax.ShapeDtypeStruct((BLOCK, width), BF16),
                   jax.ShapeDtypeStruct((bsz * seq, width), BF16)],
        compiler_params=pltpu.CompilerParams(
            dimension_semantics=("arbitrary",), vmem_limit_bytes=VMEM_LIMIT),
        name="fox_keys",
    )(k, k, small, small, fb)


def _attn_kernel(n_heads, q_ref, km_ref, k_ref, vtm_ref, vt_ref, zero_fill_ref, o_ref,
                 acc_scr, s0_scr, s1_scr):
    del zero_fill_ref
    t = pl.program_id(1)
    dv = acc_scr.shape[1]
    heads = range(n_heads)
    hsl = [slice(hd * HEAD_W, (hd + 1) * HEAD_W) for hd in heads]
    vsl = [slice(hd * dv, (hd + 1) * dv) for hd in heads]
    qs = [q_ref[:, hsl[hd]] for hd in heads]
    ahead = (lax.broadcasted_iota(jnp.int32, (TQ, TQ), 0)
             - lax.broadcasted_iota(jnp.int32, (TQ, TQ), 1))
    acc_scr[...] = jnp.zeros(acc_scr.shape, F32)

    def softmax_pv(ss, vts, carry, mask):
        ms, ls = carry
        if mask is not None:
            ss = [jnp.where(mask, s, NEG_BIG) for s in ss]
        new_ms = [jnp.maximum(ms[hd], jnp.max(ss[hd], axis=0, keepdims=True)) for hd in heads]
        corrs = [jnp.exp(ms[hd] - new_ms[hd]) for hd in heads]
        prs = [jnp.exp(ss[hd] - new_ms[hd]) for hd in heads]
        new_ls = [corrs[hd] * ls[hd] + jnp.sum(prs[hd], axis=0, keepdims=True) for hd in heads]
        for hd in heads:
            acc_scr[hd] = corrs[hd] * acc_scr[hd] + _dot(vts[hd], prs[hd].astype(BF16))
        return tuple(new_ms), tuple(new_ls)

    def meta_chunk(carry, mask):
        return softmax_pv([_dot_nt(km_ref[:, hsl[hd]], qs[hd]) for hd in heads],
                          [vtm_ref[0, vsl[hd], :] for hd in heads], carry, mask)

    def scores(c, slot_scr):
        r0 = pl.multiple_of(c * TQ, TQ)
        for hd in heads:
            slot_scr[hd] = _dot_nt(k_ref[pl.ds(r0, TQ), hsl[hd]], qs[hd])

    def consume(c, slot_scr, carry, mask):
        n_sub = TQ // BLOCK
        vts = [jnp.concatenate([vt_ref[n_sub * c + i, vsl[hd], :] for i in range(n_sub)], axis=1)
               for hd in heads]
        return softmax_pv([slot_scr[hd] for hd in heads], vts, carry, mask)

    def finish(ls):
        for p in range(n_heads // 2):
            y_t = jnp.concatenate([acc_scr[2 * p] * (1.0 / ls[2 * p]),
                                   acc_scr[2 * p + 1] * (1.0 / ls[2 * p + 1])], axis=0)
            o_ref[:, p * LANES:(p + 1) * LANES] = y_t.T.astype(BF16)

    init = (tuple(jnp.full((1, TQ), M_INIT, F32) for _ in heads),
            tuple(jnp.zeros((1, TQ), F32) for _ in heads))

    @pl.when((t == 0) & (pl.program_id(0) == 0))
    def _():
        _, ls = meta_chunk(init, ahead[:BLOCK, :] <= -(TQ - BLOCK))
        finish(ls)

    @pl.when(t > 0)
    def _():
        scores(0, s0_scr)
        carry = meta_chunk(init, None)

        def pair(i, cr):
            c = 2 * i
            scores(c + 1, s1_scr)
            cr = consume(c, s0_scr, cr, None)
            scores(c + 2, s0_scr)
            return consume(c + 1, s1_scr, cr, None)

        n_unmasked = t - 1
        carry = lax.fori_loop(0, n_unmasked // 2, pair, carry)
        c = 2 * (n_unmasked // 2)
        diagonal = ahead <= 0

        def odd_tail(cr):
            scores(c + 1, s1_scr)
            cr = consume(c, s0_scr, cr, None)
            return consume(c + 1, s1_scr, cr, diagonal)

        _, ls = lax.cond(n_unmasked % 2 == 1, odd_tail,
                         lambda cr: consume(c, s0_scr, cr, diagonal), carry)
        finish(ls)


def _attention(q, k_meta, k_main, vt, n_heads, bsz, seq, meta_k_block, name):
    n_rows, qw = q.shape
    dv = vt.shape[1] // n_heads
    tiles_per_seq = seq // TQ
    meta_q_tile = n_rows // TQ - 1
    meta_block = n_rows // BLOCK - 1

    def q_tile(b, t):
        first = jnp.where(b == 0, meta_q_tile, b * tiles_per_seq)
        return (jnp.where(t == 0, first, b * tiles_per_seq + t - 1), 0)

    return pl.pallas_call(
        functools.partial(_attn_kernel, n_heads),
        grid=(bsz, tiles_per_seq + 1),
        in_specs=[pl.BlockSpec((TQ, qw), q_tile),
                  pl.BlockSpec((BLOCK, qw), lambda b, t: (meta_k_block, 0)),
                  pl.BlockSpec((seq, qw), lambda b, t: (b, 0)),
                  pl.BlockSpec((1,) + vt.shape[1:], lambda b, t: (meta_block, 0, 0)),
                  pl.BlockSpec((seq // BLOCK,) + vt.shape[1:], lambda b, t: (b, 0, 0)),
                  pl.BlockSpec(memory_space=pl.ANY)],
        out_specs=pl.BlockSpec((TQ, n_heads * dv), q_tile),
        out_shape=jax.ShapeDtypeStruct((n_rows, n_heads * dv), BF16),
        input_output_aliases={5: 0},
        scratch_shapes=[pltpu.VMEM((n_heads, dv, TQ), F32),
                        pltpu.VMEM((n_heads, TQ, TQ), F32),
                        pltpu.VMEM((n_heads, TQ, TQ), F32)],
        compiler_params=pltpu.CompilerParams(
            dimension_semantics=("arbitrary", "arbitrary"), vmem_limit_bytes=VMEM_LIMIT),
        name=name,
    )(q, k_meta, k_main, vt, vt, jnp.zeros((n_rows, n_heads * dv), BF16))


def _rot_half_cols(w):
    half = MLA_ROPE // 2
    return jnp.concatenate([-w[..., half:], w[..., :half]], axis=-1)


def _in_proj_weights(w_in):
    sizes = [SSD_D, SSD_CONV_DIM, SSD_HEADS, FOX_D, FOX_D, FOX_D, FOX_HEADS,
             MLA_Q_LORA, MLA_KV_LORA, MLA_ROPE]
    splits = [int(s) for s in np.cumsum(sizes)[:-1]]
    w_in = w_in.astype(BF16)
    z, xbc, dt, fq, fk, fv, fr, cq, ckv, kr = jnp.split(w_in, splits, axis=-1)
    zeros = lambda n: jnp.zeros(w_in.shape[:-1] + (n,), w_in.dtype)
    rope_pad = LANES - MLA_NOPE - MLA_ROPE
    kr128 = jnp.concatenate([zeros(MLA_NOPE), kr, zeros(rope_pad)], axis=-1)
    krr128 = jnp.concatenate([zeros(MLA_NOPE), _rot_half_cols(kr), zeros(rope_pad)], axis=-1)
    small = jnp.concatenate([dt, fr, zeros(LANES - SMALL_F - FOX_HEADS)], axis=-1)

    def per_head(w):
        w = w.reshape(w.shape[:-1] + (FOX_HEADS, FOX_HEAD_DIM))
        pad = [(0, 0)] * (w.ndim - 1) + [(0, HEAD_W - FOX_HEAD_DIM)]
        return jnp.pad(w, pad).reshape(w.shape[:-2] + (FOX_HEADS * HEAD_W,))

    win = jnp.concatenate([z, xbc, per_head(fq), per_head(fk), cq, ckv, small, kr128, krr128],
                          axis=-1)
    return win, jnp.swapaxes(fv, -1, -2)


def _mla_weights(w_uq, w_ukv):
    rope_pad = LANES - MLA_NOPE - MLA_ROPE
    lead = w_uq.shape[:-1]
    wq = w_uq.astype(BF16).reshape(lead + (MLA_HEADS, MLA_NOPE + MLA_ROPE))
    zq = jnp.zeros(lead + (MLA_HEADS, rope_pad), BF16)
    wuq = jnp.concatenate([wq, zq], axis=-1).reshape(lead + (MLA_HEADS * LANES,))
    wuqr = jnp.concatenate([jnp.zeros_like(wq[..., :MLA_NOPE]),
                            _rot_half_cols(wq[..., MLA_NOPE:]), zq],
                           axis=-1).reshape(lead + (MLA_HEADS * LANES,))
    lead = w_ukv.shape[:-1]
    wkv = w_ukv.astype(BF16).reshape(lead + (MLA_HEADS, MLA_NOPE + MLA_V))
    wkk = jnp.concatenate([wkv[..., :MLA_NOPE], jnp.zeros_like(wkv[..., MLA_NOPE:])],
                          axis=-1).reshape(lead + (MLA_HEADS * LANES,))
    wvt = jnp.swapaxes(wkv[..., MLA_NOPE:].reshape(lead + (MLA_D,)), -1, -2)
    return wuq, wuqr, wkk, wvt


def _rows(v, width=None, offset=0):
    v = v.astype(F32)
    if width is not None:
        v = jnp.pad(v, ((0, 0), (offset, width - offset - v.shape[-1])))
    return v[:, None, :]


def _position_tables(seq, tm):
    pos = jnp.concatenate([N_META + jnp.arange(seq, dtype=F32),
                           jnp.arange(tm, dtype=F32) - (tm - N_META)])
    is_pad = jnp.concatenate([jnp.zeros((seq,), bool), jnp.arange(tm) < tm - N_META])
    inv_freq = 1.0 / (ROPE_THETA ** (jnp.arange(0, MLA_ROPE, 2, dtype=F32) / MLA_ROPE))
    ang = pos[:, None] * inv_freq[None, :]
    cos, sin = jnp.cos(ang), jnp.sin(ang)
    n = seq + tm
    rope_pad = LANES - MLA_NOPE - MLA_ROPE
    cos128 = jnp.concatenate([jnp.ones((n, MLA_NOPE), F32), cos, cos,
                              jnp.zeros((n, rope_pad), F32)], axis=-1)
    sin128 = jnp.concatenate([jnp.zeros((n, MLA_NOPE), F32), sin, sin,
                              jnp.zeros((n, rope_pad), F32)], axis=-1)
    kadd = jnp.where(is_pad[:, None] & (jnp.arange(LANES)[None, :] == MLA_MASK_LANE),
                     NEG_BIG, 0.0).astype(F32)
    return cos128, sin128, kadd


def kernel(x, meta, ffn1_w_gate, ffn1_w_up, ffn1_w_down, ln1_g, ln1_b, w_in, conv_w, conv_b, dt_bias, a_log, d_skip, ssd_norm_g, fox_f_b, mla_q_norm_g, mla_w_uq, mla_kv_norm_g, mla_w_ukv, w_out, ln2_g, ln2_b, ffn2_w_gate, ffn2_w_up, ffn2_w_down, ln3_g, ln3_b):
    bsz, seq, _ = x.shape
    assert seq % TQ == 0
    tm = 2 * TQ if seq % (2 * TQ) == 0 else TQ
    n_main = bsz * seq
    n_rows = n_main + tm
    meta_block = n_rows // BLOCK - 1

    h = x.reshape(n_main, D_MODEL)
    meta_tile = jnp.concatenate([jnp.zeros((tm - N_META, D_MODEL), x.dtype),
                                 meta.astype(x.dtype)], axis=0)
    tables = _position_tables(seq, tm)
    bf = lambda w: w.astype(BF16)

    ffn1 = (bf(ffn1_w_gate), bf(ffn1_w_up), bf(ffn1_w_down), _rows(ln1_g), _rows(ln1_b))
    ffn2 = (bf(ffn2_w_gate), bf(ffn2_w_up), bf(ffn2_w_down), _rows(ln3_g), _rows(ln3_b))
    proj_w = (*_in_proj_weights(w_in), *_mla_weights(mla_w_uq, mla_w_ukv),
              _rows(mla_q_norm_g), _rows(mla_kv_norm_g))
    ssd_p = (conv_w.astype(F32), _rows(conv_b), _rows(dt_bias, LANES),
             _rows(-jnp.exp(a_log.astype(F32)), LANES),
             _rows(jnp.repeat(d_skip, SSD_HEAD_DIM, axis=-1)), _rows(ssd_norm_g))
    fb = _rows(fox_f_b, LANES, SMALL_F)
    out_p = (bf(w_out), _rows(ln2_g), _rows(ln2_b))

    for l in range(DEPTH):
        h = _ffn_ln(l, h, ffn1, tm, n_rows, meta_tile=meta_tile if l == 0 else None)
        zx, small, fq, fk, fvt, mq, mk, mvt = _proj(l, h, proj_w, tables, tm, seq)
        y_ssd = _ssd(l, zx, small, ssd_p, bsz, seq)
        fk_meta, fk_main = _fox_keys(l, fk, small, fb, bsz, seq)
        y_fox = _attention(fq, fk_meta, fk_main, fvt, FOX_HEADS, bsz, seq, 0, "fox_attention")
        y_mla = _attention(mq, mk, mk, mvt, MLA_HEADS, bsz, seq, meta_block, "mla_attention")
        h = _ffn_ln(l, h, ffn2, tm, n_rows if l < DEPTH - 1 else n_main,
                    mix=(y_ssd, y_fox, y_mla, *out_p))

    return h.reshape(bsz, seq, D_MODEL)
```

```python
import functools

import numpy as np
import jax
import jax.numpy as jnp
from jax import lax
from jax.experimental import pallas as pl
from jax.experimental.pallas import tpu as pltpu

F32 = jnp.float32
BF16 = jnp.bfloat16

D_MODEL = 1024
DEPTH = 2
N_META = 16
BLOCK = 128
SSD_HEADS = 8
SSD_HEAD_DIM = 64
SSD_D = SSD_HEADS * SSD_HEAD_DIM
SSD_GROUPS = 2
SSD_STATE = 64
SSD_CONV = 4
SSD_CONV_DIM = SSD_D + 2 * SSD_GROUPS * SSD_STATE
FOX_HEADS = 4
FOX_HEAD_DIM = 64
FOX_D = FOX_HEADS * FOX_HEAD_DIM
MLA_HEADS = 4
MLA_Q_LORA = 256
MLA_KV_LORA = 128
MLA_NOPE = 64
MLA_ROPE = 32
MLA_V = 64
MLA_D = MLA_HEADS * MLA_V
ROPE_THETA = 10000.0
D_MIX = SSD_D + FOX_D + MLA_D
D_FF = 2816
ALPHA = (2 * DEPTH) ** 0.25
EPS = 1e-5

LANES = 128
MXU_W = 256
FF_CHUNK = MXU_W
N_FF_CHUNKS = D_FF // FF_CHUNK
N_OUT_CHUNKS = D_MODEL // FF_CHUNK
TQ = 2 * BLOCK
PAD = BLOCK - N_META
NEG_BIG = -1e30
M_INIT = 2 * NEG_BIG
VMEM_LIMIT = 56 * 1024 * 1024

C_XBC = SSD_D
C_ZX = SSD_D + SSD_CONV_DIM
C_FQ = 0
C_FK = C_FQ + FOX_HEADS * LANES
C_CQ = C_FK + FOX_HEADS * LANES
C_CKV = C_CQ + MLA_Q_LORA
C_SMALL = C_CKV + MLA_KV_LORA
C_KR = C_SMALL + LANES
C_KRR = C_KR + LANES
N_IN_ARR = C_KRR + LANES
LOG2E = 1.4426950408889634
DV_AUG = MLA_V + 16
SMALL_DT = 0
SMALL_F = 8
HEAD_W = LANES
FOX_BIAS_LANE = FOX_HEAD_DIM
MLA_MASK_LANE = MLA_NOPE + MLA_ROPE


def _sigmoid(x):
    return 1.0 / (1.0 + jnp.exp(-x))


def _softplus(x):
    return jnp.maximum(x, 0.0) + jnp.log(1.0 + jnp.exp(-jnp.abs(x)))


def _layer_norm_rows(y, g, b):
    mu = jnp.mean(y, axis=-1, keepdims=True)
    yc = y - mu
    var = jnp.mean(yc * yc, axis=-1, keepdims=True)
    return yc * lax.rsqrt(var + EPS) * g + b


def _split3(x):
    x1 = x.astype(BF16)
    r1 = x - x1.astype(F32)
    x2 = r1.astype(BF16)
    r2 = r1 - x2.astype(F32)
    return x1, x2, r2.astype(BF16)


def _dot(a, b):
    return jnp.dot(a, b, preferred_element_type=F32)


def _dot_nt(a, b):
    return lax.dot_general(a, b, (((1,), (1,)), ((), ())), preferred_element_type=F32)


def _resident(shape):
    return pl.BlockSpec(shape, lambda *_: (0,) * len(shape), pipeline_mode=pl.Buffered(1))


def _layer(arr, l):
    tail = (0,) * (arr.ndim - 1)
    return pl.BlockSpec((None,) + arr.shape[1:], lambda *_: (l,) + tail,
                        pipeline_mode=pl.Buffered(1))


def _ffn_ln_kernel(prologue, n_main_tiles, *refs):
    wg_ref, wu_ref, wd_ref, g_ref, b_ref, o_ref, a_scr, y_scr = refs[-8:]
    if prologue == "plain":
        x = refs[0][...]
    elif prologue == "meta":
        x_ref, m_ref = refs[:2]
        x = jnp.where(pl.program_id(0) < n_main_tiles, x_ref[...], m_ref[...])
    else:
        h_ref, ys_ref, yf_ref, ym_ref, wo_ref, g2_ref, b2_ref = refs[:7]
        mix = (_dot(ys_ref[...], wo_ref[0:SSD_D, :])
               + _dot(yf_ref[...], wo_ref[SSD_D:SSD_D + FOX_D, :])
               + _dot(ym_ref[...], wo_ref[SSD_D + FOX_D:D_MIX, :]))
        x = _layer_norm_rows(ALPHA * h_ref[...] + mix, g2_ref[...], b2_ref[...])
    xb = x.astype(BF16)

    for c in range(N_FF_CHUNKS):
        cols = slice(c * FF_CHUNK, (c + 1) * FF_CHUNK)
        gate = _dot(xb, wg_ref[:, cols])
        up = _dot(xb, wu_ref[:, cols])
        a_scr[c] = (gate * _sigmoid(gate) * up).astype(BF16)

    for n in range(N_OUT_CHUNKS):
        cols = slice(n * FF_CHUNK, (n + 1) * FF_CHUNK)
        acc = _dot(a_scr[0], wd_ref[0:FF_CHUNK, cols])
        for c in range(1, N_FF_CHUNKS):
            acc = acc + _dot(a_scr[c], wd_ref[c * FF_CHUNK:(c + 1) * FF_CHUNK, cols])
        y_scr[:, cols] = ALPHA * x[:, cols] + 0.5 * acc
    o_ref[...] = _layer_norm_rows(y_scr[...], g_ref[...], b_ref[...])


def _ffn_ln(l, h, ffn, tm, n_out_rows, meta_tile=None, mix=None):
    def rows(width):
        return pl.BlockSpec((tm, width), lambda i: (i, 0))

    if mix is not None:
        prologue, n_main = "mix", None
        x_args = [h, *mix]
        x_specs = ([rows(D_MODEL), rows(SSD_D), rows(FOX_D), rows(MLA_D)]
                   + [_layer(p, l) for p in mix[3:]])
    elif meta_tile is not None:
        prologue, n_main = "meta", h.shape[0] // tm
        x_args = [h, meta_tile]
        x_specs = [pl.BlockSpec((tm, D_MODEL), lambda i: (jnp.minimum(i, n_main - 1), 0)),
                   _resident(meta_tile.shape)]
    else:
        prologue, n_main, x_args, x_specs = "plain", None, [h], [rows(D_MODEL)]
    return pl.pallas_call(
        functools.partial(_ffn_ln_kernel, prologue, n_main),
        grid=(n_out_rows // tm,),
        in_specs=x_specs + [_layer(p, l) for p in ffn],
        out_specs=rows(D_MODEL),
        out_shape=jax.ShapeDtypeStruct((n_out_rows, D_MODEL), F32),
        scratch_shapes=[pltpu.VMEM((N_FF_CHUNKS, tm, FF_CHUNK), BF16),
                        pltpu.VMEM((tm, D_MODEL), F32)],
        compiler_params=pltpu.CompilerParams(
            dimension_semantics=("arbitrary",), vmem_limit_bytes=VMEM_LIMIT),
        name="ffn_ln",
    )(*x_args, *ffn)


def _rms_rows(x, g):
    return x * lax.rsqrt(jnp.mean(x * x, axis=-1, keepdims=True) + EPS) * g


def _store_vt_blocks(out_ref, val_t, n_heads):
    dv = val_t.shape[0] // n_heads
    ones = jnp.ones((DV_AUG - dv, BLOCK), out_ref.dtype)
    for r in range(out_ref.shape[0]):
        for hd in range(n_heads):
            out_ref[r, hd * DV_AUG:hd * DV_AUG + dv, :] = (
                val_t[hd * dv:(hd + 1) * dv, r * BLOCK:(r + 1) * BLOCK].astype(out_ref.dtype))
            out_ref[r, hd * DV_AUG + dv:(hd + 1) * DV_AUG, :] = ones


def _proj_kernel(h_ref, wzx_ref, win_ref, wfvt_ref, wuq_ref, wuqr_ref, wkk_ref, wkvt_ref,
                 qg_ref, kvg_ref, cos_ref, sin_ref, kadd_ref,
                 zx_ref, small_ref, fq_ref, fk_ref, fvt_ref, mq_ref, mk_ref, mvt_ref):
    hb = h_ref[...].astype(BF16)
    zx_ref[...] = _dot(hb, wzx_ref[...])
    ckv_small = _dot(hb, win_ref[:, C_CKV:C_KR])
    small_ref[...] = ckv_small[:, MLA_KV_LORA:]
    kr_both = _dot(hb, win_ref[:, C_KR:N_IN_ARR])

    lane = lax.broadcasted_iota(jnp.int32, (1, LANES), 1)
    fox_one = jnp.where((lane >= FOX_BIAS_LANE) & (lane < FOX_BIAS_LANE + 3), 1.0, 0.0)
    mla_one = jnp.where(lane == MLA_MASK_LANE, 1.0, 0.0)

    fq = _dot(hb, win_ref[:, C_FQ:C_FK]) * (FOX_HEAD_DIM ** -0.5 * LOG2E)
    for hd in range(FOX_HEADS):
        sl = slice(hd * HEAD_W, (hd + 1) * HEAD_W)
        fq_ref[:, sl] = (fq[:, sl] + fox_one).astype(BF16)
    fk_ref[...] = _dot(hb, win_ref[:, C_FK:C_CQ]).astype(BF16)
    _store_vt_blocks(fvt_ref, _dot_nt(wfvt_ref[...], hb), FOX_HEADS)

    cos = cos_ref[...]
    sin = sin_ref[...]
    cqn = _rms_rows(_dot(hb, win_ref[:, C_CQ:C_CKV]), qg_ref[...]).astype(BF16)
    q = _dot(cqn, wuq_ref[...])
    qr = _dot(cqn, wuqr_ref[...])
    kvn = _rms_rows(ckv_small[:, :MLA_KV_LORA], kvg_ref[...]).astype(BF16)
    kn = _dot(kvn, wkk_ref[...])
    _store_vt_blocks(mvt_ref, _dot_nt(wkvt_ref[...], kvn), MLA_HEADS)
    krope = kr_both[:, :LANES] * cos + kr_both[:, LANES:] * sin + kadd_ref[...]
    scale = (MLA_NOPE + MLA_ROPE) ** -0.5 * LOG2E
    for hd in range(MLA_HEADS):
        sl = slice(hd * HEAD_W, (hd + 1) * HEAD_W)
        mq_ref[:, sl] = ((q[:, sl] * cos + qr[:, sl] * sin) * scale + mla_one).astype(BF16)
        mk_ref[:, sl] = (kn[:, sl] + krope).astype(BF16)


def _proj(l, h, w_in_bf, weights, tables, tm, seq):
    n_rows = h.shape[0]
    n_main_tiles = n_rows // tm - 1
    tiles_per_seq = seq // tm

    def rows(width):
        return pl.BlockSpec((tm, width), lambda i: (i, 0))

    tab = pl.BlockSpec((tm, LANES), lambda i: (
        jnp.where(i < n_main_tiles, i % tiles_per_seq, tiles_per_seq), 0))
    wzx = pl.BlockSpec((None, D_MODEL, C_ZX), lambda i: (l, 0, 0), pipeline_mode=pl.Buffered(1))
    vt_rows = FOX_HEADS * DV_AUG
    t_spec = pl.BlockSpec((tm // BLOCK, vt_rows, BLOCK), lambda i: (i, 0, 0))
    t_shape = jax.ShapeDtypeStruct((n_rows // BLOCK, vt_rows, BLOCK), BF16)
    qk_w = FOX_HEADS * HEAD_W
    row_outs = [(C_ZX, F32), (LANES, F32), (qk_w, BF16), (qk_w, BF16)]
    return pl.pallas_call(
        _proj_kernel,
        grid=(n_rows // tm,),
        in_specs=[rows(D_MODEL), wzx] + [_layer(w, l) for w in weights] + [tab, tab, tab],
        out_specs=([rows(w) for w, _ in row_outs] + [t_spec, rows(qk_w), rows(qk_w), t_spec]),
        out_shape=([jax.ShapeDtypeStruct((n_rows, w), dt) for w, dt in row_outs]
                   + [t_shape, jax.ShapeDtypeStruct((n_rows, qk_w), BF16),
                      jax.ShapeDtypeStruct((n_rows, qk_w), BF16), t_shape]),
        compiler_params=pltpu.CompilerParams(
            dimension_semantics=("arbitrary",), vmem_limit_bytes=VMEM_LIMIT),
        name="in_proj",
    )(h, w_in_bf, *weights, *tables)


def _ssd_kernel(zx_ref, small_ref, cw_ref, cb_ref, dtb_ref, a_ref, dsk_ref, ng_ref,
                zero_fill_ref, o_ref, conv_scr, s_scr, meta_conv_scr, meta_s_scr):
    del zero_fill_ref
    b = pl.program_id(0)
    c = pl.program_id(1)

    @pl.when((c == 0) & (b == 0))
    def _():
        conv_scr[0:8, :] = jnp.zeros((8, SSD_CONV_DIM), F32)
        s_scr[...] = jnp.zeros(s_scr.shape, F32)

    @pl.when((c == 0) & (b > 0))
    def _():
        conv_scr[0:8, :] = meta_conv_scr[...]
        s_scr[...] = meta_s_scr[...]

    @pl.when((c > 0) | (b == 0))
    def _():
        _ssd_chunk(c, zx_ref, small_ref, cw_ref, cb_ref, dtb_ref, a_ref, dsk_ref, ng_ref,
                   o_ref, conv_scr, s_scr)

    @pl.when((c == 0) & (b == 0))
    def _():
        meta_conv_scr[...] = conv_scr[0:8, :]
        meta_s_scr[...] = s_scr[...]


def _ssd_chunk(c, zx_ref, small_ref, cw_ref, cb_ref, dtb_ref, a_ref, dsk_ref, ng_ref,
               o_ref, conv_scr, s_scr):
    Q = BLOCK
    row = lax.broadcasted_iota(jnp.int32, (Q, 1), 0)
    valid = jnp.logical_or(c > 0, row >= PAD)
    lane = lax.broadcasted_iota(jnp.int32, (1, LANES), 1)
    lane_lo = lane < SSD_HEAD_DIM
    sub = lax.broadcasted_iota(jnp.int32, (LANES, 1), 0)

    conv_scr[8:8 + Q, :] = jnp.where(valid, zx_ref[:, C_XBC:C_ZX], 0.0)
    acc = cb_ref[...]
    for k in range(SSD_CONV):
        off = 8 - (SSD_CONV - 1) + k
        acc = acc + cw_ref[k:k + 1, :] * conv_scr[off:off + Q, :]
    conv_scr[0:8, :] = conv_scr[Q:Q + 8, :]
    xbc = acc * _sigmoid(acc)
    bm = xbc[:, SSD_D:SSD_D + LANES]
    cm = xbc[:, SSD_D + LANES:SSD_D + 2 * LANES]

    dt = jnp.where(valid, _softplus(small_ref[...] + dtb_ref[...]), 0.0)
    a = dt * a_ref[...]
    tri = (lax.broadcasted_iota(jnp.int32, (Q, Q), 0)
           >= lax.broadcasted_iota(jnp.int32, (Q, Q), 1))
    tri_b = jnp.where(tri, 1.0, 0.0).astype(BF16)
    a1, a2, a3 = _split3(a)
    a_cum = _dot(tri_b, a1) + _dot(tri_b, a2) + _dot(tri_b, a3)
    a_cum_t = a_cum.T
    bm_t = bm.T

    cm_b = cm.astype(BF16)
    bm_b = bm.astype(BF16)
    cb_g = [_dot_nt(jnp.where(lane_lo, cm, 0.0).astype(BF16), bm_b),
            _dot_nt(jnp.where(lane_lo, 0.0, cm).astype(BF16), bm_b)]
    rows_g = [sub < SSD_STATE, sub >= SSD_STATE]

    pairs_per_group = SSD_HEADS // 2 // SSD_GROUPS
    y_pairs = []
    for p in range(SSD_HEADS // 2):
        g = p // pairs_per_group
        psl = slice(p * LANES, (p + 1) * LANES)
        xs_p = xbc[:, psl]
        dt_pair = jnp.where(lane_lo, dt[:, 2 * p:2 * p + 1], dt[:, 2 * p + 1:2 * p + 2])
        xdt = (xs_p * dt_pair).astype(BF16)
        s_old = s_scr[p]
        yd, upd, e_col, e_last = [], [], [], []
        for par in range(2):
            hd = 2 * p + par
            col = a_cum[:, hd:hd + 1]
            rowv = a_cum_t[hd:hd + 1, :]
            last = a_cum_t[hd:hd + 1, Q - 1:Q]
            seg = jnp.exp(jnp.where(tri, col - rowv, NEG_BIG))
            yd.append(_dot((cb_g[g] * seg).astype(BF16), xdt))
            upd.append(_dot((bm_t * jnp.exp(last - rowv)).astype(BF16), xdt))
            e_col.append(jnp.exp(col))
            e_last.append(jnp.exp(last))
        y_off = _dot(cm_b, s_old.astype(BF16)) * jnp.where(lane_lo, e_col[0], e_col[1])
        s_new = (jnp.where(lane_lo, e_last[0], e_last[1]) * s_old
                 + jnp.where(rows_g[g], jnp.where(lane_lo, upd[0], upd[1]), 0.0))
        s_scr[p] = s_new
        y_p = jnp.where(lane_lo, yd[0], yd[1]) + y_off + dsk_ref[:, psl] * xs_p
        z_p = zx_ref[:, psl]
        y_pairs.append(y_p * (z_p * _sigmoid(z_p)))

    for g in range(SSD_GROUPS):
        ps = range(g * pairs_per_group, (g + 1) * pairs_per_group)
        ss = sum(jnp.sum(y_pairs[p] * y_pairs[p], axis=-1, keepdims=True) for p in ps)
        inv = lax.rsqrt(ss * (1.0 / (pairs_per_group * LANES)) + EPS)
        for p in ps:
            psl = slice(p * LANES, (p + 1) * LANES)
            o_ref[:, psl] = (y_pairs[p] * inv * ng_ref[:, psl]).astype(BF16)


def _ssd(l, zx, small, params, bsz, seq):
    n_rows = zx.shape[0]
    blocks_per_seq = seq // BLOCK
    meta_block = n_rows // BLOCK - 1

    def block(b, c):
        first = jnp.where(b == 0, meta_block, b * blocks_per_seq)
        return (jnp.where(c == 0, first, b * blocks_per_seq + c - 1), 0)

    return pl.pallas_call(
        _ssd_kernel,
        grid=(bsz, blocks_per_seq + 1),
        in_specs=[pl.BlockSpec((BLOCK, C_ZX), block), pl.BlockSpec((BLOCK, LANES), block)]
                 + [_layer(p, l) for p in params] + [pl.BlockSpec(memory_space=pl.ANY)],
        out_specs=pl.BlockSpec((BLOCK, SSD_D), block),
        out_shape=jax.ShapeDtypeStruct((n_rows, SSD_D), BF16),
        input_output_aliases={2 + len(params): 0},
        scratch_shapes=[pltpu.VMEM((BLOCK + 8, SSD_CONV_DIM), F32),
                        pltpu.VMEM((SSD_HEADS // 2, LANES, LANES), F32),
                        pltpu.VMEM((8, SSD_CONV_DIM), F32),
                        pltpu.VMEM((SSD_HEADS // 2, LANES, LANES), F32)],
        compiler_params=pltpu.CompilerParams(
            dimension_semantics=("arbitrary", "arbitrary"), vmem_limit_bytes=VMEM_LIMIT),
        name="ssd_mixer",
    )(zx, small, *params, jnp.zeros((n_rows, SSD_D), BF16))


def _fox_keys_kernel(km_ref, k_ref, sm_ref, s_ref, fb_ref, om_ref, o_ref):
    T = BLOCK
    width = k_ref.shape[1]
    row = lax.broadcasted_iota(jnp.int32, (T, 1), 0)
    tri = (lax.broadcasted_iota(jnp.int32, (T, T), 0)
           >= lax.broadcasted_iota(jnp.int32, (T, T), 1))
    tri_b = jnp.where(tri, 1.0, 0.0).astype(BF16)
    src = lax.broadcasted_iota(jnp.int32, (LANES, width), 0)
    dst = lax.broadcasted_iota(jnp.int32, (LANES, width), 1)
    dst_head = jnp.right_shift(dst, HEAD_W.bit_length() - 1)
    dst_lane = jnp.bitwise_and(dst, HEAD_W - 1)
    sel = [jnp.where((src == SMALL_F + dst_head) & (dst_lane == FOX_BIAS_LANE + i),
                     1.0, 0.0).astype(BF16) for i in range(3)]

    def local_cumsum(small_blk, is_meta):
        log_f = -_softplus(-(small_blk + fb_ref[...]))
        if is_meta:
            log_f = jnp.where(row < PAD, 0.0, log_f)
        f1, f2, f3 = _split3(log_f)
        return _dot(tri_b, f1) + _dot(tri_b, f2) + _dot(tri_b, f3)

    def keys(c_blk, k_blk, is_meta):
        bias = c_blk * (-LOG2E)
        if is_meta:
            bias = jnp.where(row < PAD, NEG_BIG, bias)
        b1, b2, b3 = _split3(bias)
        placed = _dot(b1, sel[0]) + _dot(b2, sel[1]) + _dot(b3, sel[2])
        return (k_blk.astype(F32) + placed).astype(BF16)

    n_blocks = k_ref.shape[0] // T
    rows = [slice(j * T, (j + 1) * T) for j in range(n_blocks)]
    c_meta = local_cumsum(sm_ref[...], True)
    local = [local_cumsum(s_ref[rows[j], :], False) for j in range(n_blocks)]
    om_ref[...] = keys(c_meta, km_ref[...], True)
    carry = c_meta[T - 1:T, :]
    for j in range(n_blocks):
        c_blk = local[j] + carry
        carry = c_blk[T - 1:T, :]
        o_ref[rows[j], :] = keys(c_blk, k_ref[rows[j], :], False)


def _fox_keys(l, k, small, fb, bsz, seq):
    n_rows, width = k.shape
    meta_block = n_rows // BLOCK - 1
    main = lambda w: pl.BlockSpec((seq, w), lambda b: (b, 0))
    meta = lambda w: pl.BlockSpec((BLOCK, w), lambda b: (meta_block, 0))
    return pl.pallas_call(
        _fox_keys_kernel,
        grid=(bsz,),
        in_specs=[meta(width), main(width), meta(LANES), main(LANES), _layer(fb, l)],
        out_specs=[pl.BlockSpec((BLOCK, width), lambda b: (0, 0)), main(width)],
        out_shape=[jax.ShapeDtypeStruct((BLOCK, width), BF16),
                   jax.ShapeDtypeStruct((bsz * seq, width), BF16)],
        compiler_params=pltpu.CompilerParams(
            dimension_semantics=("arbitrary",), vmem_limit_bytes=VMEM_LIMIT),
        name="fox_keys",
    )(k, k, small, small, fb)


def _attn_kernel(n_heads, q_ref, km_ref, k_ref, vtm_ref, vt_ref, zero_fill_ref, o_ref,
                 acc_scr, s0_scr, s1_scr):
    del zero_fill_ref
    t = pl.program_id(1)
    dv = acc_scr.shape[1]
    heads = range(n_heads)
    hsl = [slice(hd * HEAD_W, (hd + 1) * HEAD_W) for hd in heads]
    vsl = [slice(hd * dv, (hd + 1) * dv) for hd in heads]
    qs = [q_ref[:, hsl[hd]] for hd in heads]
    ahead = (lax.broadcasted_iota(jnp.int32, (TQ, TQ), 0)
             - lax.broadcasted_iota(jnp.int32, (TQ, TQ), 1))
    acc_scr[...] = jnp.zeros(acc_scr.shape, F32)

    def softmax_pv(ss, vts, ms, mask):
        if mask is not None:
            ss = [jnp.where(mask, s, NEG_BIG) for s in ss]
        new_ms = [jnp.maximum(ms[hd], jnp.max(ss[hd], axis=0, keepdims=True)) for hd in heads]
        corrs = [jnp.exp2(ms[hd] - new_ms[hd]) for hd in heads]
        prs = [jnp.exp2(ss[hd] - new_ms[hd]).astype(BF16) for hd in heads]
        for hd in heads:
            acc_scr[hd] = corrs[hd] * acc_scr[hd] + _dot(vts[hd], prs[hd])
        return tuple(new_ms)

    def meta_chunk(carry, mask):
        return softmax_pv([_dot_nt(km_ref[:, hsl[hd]], qs[hd]) for hd in heads],
                          [vtm_ref[0, vsl[hd], :] for hd in heads], carry, mask)

    def scores(c, slot_scr):
        r0 = pl.multiple_of(c * TQ, TQ)
        for hd in heads:
            slot_scr[hd] = _dot_nt(k_ref[pl.ds(r0, TQ), hsl[hd]], qs[hd])

    def consume(c, slot_scr, carry, mask):
        n_sub = TQ // BLOCK
        vts = [jnp.concatenate([vt_ref[n_sub * c + i, vsl[hd], :] for i in range(n_sub)], axis=1)
               for hd in heads]
        return softmax_pv([slot_scr[hd] for hd in heads], vts, carry, mask)

    def finish():
        dv_out = o_ref.shape[1] // n_heads
        ys = [acc_scr[hd, 0:dv_out, :] * (1.0 / acc_scr[hd, dv_out:dv_out + 1, :])
              for hd in heads]
        for p in range(n_heads // 2):
            y_t = jnp.concatenate([ys[2 * p], ys[2 * p + 1]], axis=0)
            o_ref[:, p * LANES:(p + 1) * LANES] = y_t.T.astype(BF16)

    init = tuple(jnp.full((1, TQ), M_INIT, F32) for _ in heads)

    @pl.when((t == 0) & (pl.program_id(0) == 0))
    def _():
        meta_chunk(init, ahead[:BLOCK, :] <= -(TQ - BLOCK))
        finish()

    @pl.when(t > 0)
    def _():
        scores(0, s0_scr)
        carry = meta_chunk(init, None)

        def pair(i, cr):
            c = 2 * i
            scores(c + 1, s1_scr)
            cr = consume(c, s0_scr, cr, None)
            scores(c + 2, s0_scr)
            return consume(c + 1, s1_scr, cr, None)

        n_unmasked = t - 1
        carry = lax.fori_loop(0, n_unmasked // 2, pair, carry)
        c = 2 * (n_unmasked // 2)
        diagonal = ahead <= 0

        def odd_tail(cr):
            scores(c + 1, s1_scr)
            cr = consume(c, s0_scr, cr, None)
            return consume(c + 1, s1_scr, cr, diagonal)

        lax.cond(n_unmasked % 2 == 1, odd_tail,
                 lambda cr: consume(c, s0_scr, cr, diagonal), carry)
        finish()


def _attention(q, k_meta, k_main, vt, n_heads, bsz, seq, meta_k_block, name):
    n_rows, qw = q.shape
    dv_aug = vt.shape[1] // n_heads
    dv = dv_aug - (DV_AUG - MLA_V)
    tiles_per_seq = seq // TQ
    meta_q_tile = n_rows // TQ - 1
    meta_block = n_rows // BLOCK - 1

    def q_tile(b, t):
        first = jnp.where(b == 0, meta_q_tile, b * tiles_per_seq)
        return (jnp.where(t == 0, first, b * tiles_per_seq + t - 1), 0)

    return pl.pallas_call(
        functools.partial(_attn_kernel, n_heads),
        grid=(bsz, tiles_per_seq + 1),
        in_specs=[pl.BlockSpec((TQ, qw), q_tile),
                  pl.BlockSpec((BLOCK, qw), lambda b, t: (meta_k_block, 0)),
                  pl.BlockSpec((seq, qw), lambda b, t: (b, 0)),
                  pl.BlockSpec((1,) + vt.shape[1:], lambda b, t: (meta_block, 0, 0)),
                  pl.BlockSpec((seq // BLOCK,) + vt.shape[1:], lambda b, t: (b, 0, 0)),
                  pl.BlockSpec(memory_space=pl.ANY)],
        out_specs=pl.BlockSpec((TQ, n_heads * dv), q_tile),
        out_shape=jax.ShapeDtypeStruct((n_rows, n_heads * dv), BF16),
        input_output_aliases={5: 0},
        scratch_shapes=[pltpu.VMEM((n_heads, dv_aug, TQ), F32),
                        pltpu.VMEM((n_heads, TQ, TQ), F32),
                        pltpu.VMEM((n_heads, TQ, TQ), F32)],
        compiler_params=pltpu.CompilerParams(
            dimension_semantics=("arbitrary", "arbitrary"), vmem_limit_bytes=VMEM_LIMIT),
        name=name,
    )(q, k_meta, k_main, vt, vt, jnp.zeros((n_rows, n_heads * dv), BF16))


def _rot_half_cols(w):
    half = MLA_ROPE // 2
    return jnp.concatenate([-w[..., half:], w[..., :half]], axis=-1)


def _in_proj_weights(w_in):
    sizes = [SSD_D, SSD_CONV_DIM, SSD_HEADS, FOX_D, FOX_D, FOX_D, FOX_HEADS,
             MLA_Q_LORA, MLA_KV_LORA, MLA_ROPE]
    splits = [int(s) for s in np.cumsum(sizes)[:-1]]
    z, xbc, dt, fq, fk, fv, fr, cq, ckv, kr = jnp.split(w_in, splits, axis=-1)
    zeros = lambda n: jnp.zeros(w_in.shape[:-1] + (n,), w_in.dtype)
    rope_pad = LANES - MLA_NOPE - MLA_ROPE
    kr128 = jnp.concatenate([zeros(MLA_NOPE), kr, zeros(rope_pad)], axis=-1)
    krr128 = jnp.concatenate([zeros(MLA_NOPE), _rot_half_cols(kr), zeros(rope_pad)], axis=-1)
    small = jnp.concatenate([dt, fr, zeros(LANES - SMALL_F - FOX_HEADS)], axis=-1)

    def per_head(w):
        w = w.reshape(w.shape[:-1] + (FOX_HEADS, FOX_HEAD_DIM))
        pad = [(0, 0)] * (w.ndim - 1) + [(0, HEAD_W - FOX_HEAD_DIM)]
        return jnp.pad(w, pad).reshape(w.shape[:-2] + (FOX_HEADS * HEAD_W,))

    del z, xbc
    win = jnp.concatenate([per_head(fq), per_head(fk), cq, ckv, small, kr128, krr128], axis=-1)
    return win, jnp.swapaxes(fv, -1, -2)


def _mla_weights(w_uq, w_ukv):
    rope_pad = LANES - MLA_NOPE - MLA_ROPE
    lead = w_uq.shape[:-1]
    wq = w_uq.astype(BF16).reshape(lead + (MLA_HEADS, MLA_NOPE + MLA_ROPE))
    zq = jnp.zeros(lead + (MLA_HEADS, rope_pad), BF16)
    wuq = jnp.concatenate([wq, zq], axis=-1).reshape(lead + (MLA_HEADS * LANES,))
    wuqr = jnp.concatenate([jnp.zeros_like(wq[..., :MLA_NOPE]),
                            _rot_half_cols(wq[..., MLA_NOPE:]), zq],
                           axis=-1).reshape(lead + (MLA_HEADS * LANES,))
    lead = w_ukv.shape[:-1]
    wkv = w_ukv.astype(BF16).reshape(lead + (MLA_HEADS, MLA_NOPE + MLA_V))
    wkk = jnp.concatenate([wkv[..., :MLA_NOPE], jnp.zeros_like(wkv[..., MLA_NOPE:])],
                          axis=-1).reshape(lead + (MLA_HEADS * LANES,))
    wvt = jnp.swapaxes(wkv[..., MLA_NOPE:].reshape(lead + (MLA_D,)), -1, -2)
    return wuq, wuqr, wkk, wvt


def _rows(v, width=None, offset=0):
    v = v.astype(F32)
    if width is not None:
        v = jnp.pad(v, ((0, 0), (offset, width - offset - v.shape[-1])))
    return v[:, None, :]


def _position_tables(seq, tm):
    pos = jnp.concatenate([N_META + jnp.arange(seq, dtype=F32),
                           jnp.arange(tm, dtype=F32) - (tm - N_META)])
    is_pad = jnp.concatenate([jnp.zeros((seq,), bool), jnp.arange(tm) < tm - N_META])
    inv_freq = 1.0 / (ROPE_THETA ** (jnp.arange(0, MLA_ROPE, 2, dtype=F32) / MLA_ROPE))
    ang = pos[:, None] * inv_freq[None, :]
    cos, sin = jnp.cos(ang), jnp.sin(ang)
    n = seq + tm
    rope_pad = LANES - MLA_NOPE - MLA_ROPE
    cos128 = jnp.concatenate([jnp.ones((n, MLA_NOPE), F32), cos, cos,
                              jnp.zeros((n, rope_pad), F32)], axis=-1)
    sin128 = jnp.concatenate([jnp.zeros((n, MLA_NOPE), F32), sin, sin,
                              jnp.zeros((n, rope_pad), F32)], axis=-1)
    kadd = jnp.where(is_pad[:, None] & (jnp.arange(LANES)[None, :] == MLA_MASK_LANE),
                     NEG_BIG, 0.0).astype(F32)
    return cos128, sin128, kadd


def kernel(x, meta, ffn1_w_gate, ffn1_w_up, ffn1_w_down, ln1_g, ln1_b, w_in, conv_w, conv_b, dt_bias, a_log, d_skip, ssd_norm_g, fox_f_b, mla_q_norm_g, mla_w_uq, mla_kv_norm_g, mla_w_ukv, w_out, ln2_g, ln2_b, ffn2_w_gate, ffn2_w_up, ffn2_w_down, ln3_g, ln3_b):
    bsz, seq, _ = x.shape
    assert seq % TQ == 0
    tm = 2 * TQ if seq % (2 * TQ) == 0 else TQ
    n_main = bsz * seq
    n_rows = n_main + tm
    meta_block = n_rows // BLOCK - 1

    h = x.reshape(n_main, D_MODEL)
    meta_tile = jnp.concatenate([jnp.zeros((tm - N_META, D_MODEL), x.dtype),
                                 meta.astype(x.dtype)], axis=0)
    tables = _position_tables(seq, tm)
    bf = lambda w: w.astype(BF16)

    ffn1 = (bf(ffn1_w_gate), bf(ffn1_w_up), bf(ffn1_w_down), _rows(ln1_g), _rows(ln1_b))
    ffn2 = (bf(ffn2_w_gate), bf(ffn2_w_up), bf(ffn2_w_down), _rows(ln3_g), _rows(ln3_b))
    w_in_bf = bf(w_in)
    proj_w = (*_in_proj_weights(w_in_bf), *_mla_weights(mla_w_uq, mla_w_ukv),
              _rows(mla_q_norm_g), _rows(mla_kv_norm_g))
    ssd_p = (conv_w.astype(F32), _rows(conv_b), _rows(dt_bias, LANES),
             _rows(-jnp.exp(a_log.astype(F32)), LANES),
             _rows(jnp.repeat(d_skip, SSD_HEAD_DIM, axis=-1)), _rows(ssd_norm_g))
    fb = _rows(fox_f_b, LANES, SMALL_F)
    out_p = (bf(w_out), _rows(ln2_g), _rows(ln2_b))

    for l in range(DEPTH):
        h = _ffn_ln(l, h, ffn1, tm, n_rows, meta_tile=meta_tile if l == 0 else None)
        zx, small, fq, fk, fvt, mq, mk, mvt = _proj(l, h, w_in_bf, proj_w, tables, tm, seq)
        y_ssd = _ssd(l, zx, small, ssd_p, bsz, seq)
        fk_meta, fk_main = _fox_keys(l, fk, small, fb, bsz, seq)
        y_fox = _attention(fq, fk_meta, fk_main, fvt, FOX_HEADS, bsz, seq, 0, "fox_attention")
        y_mla = _attention(mq, mk, mk, mvt, MLA_HEADS, bsz, seq, meta_block, "mla_attention")
        h = _ffn_ln(l, h, ffn2, tm, n_rows if l < DEPTH - 1 else n_main,
                    mix=(y_ssd, y_fox, y_mla, *out_p))

    return h.reshape(bsz, seq, D_MODEL)
```

```python
import functools

import numpy as np
import jax
import jax.numpy as jnp
from jax import lax
from jax.experimental import pallas as pl
from jax.experimental.pallas import tpu as pltpu

F32 = jnp.float32
BF16 = jnp.bfloat16

D_MODEL = 1024
DEPTH = 2
N_META = 16
BLOCK = 128
SSD_HEADS = 8
SSD_HEAD_DIM = 64
SSD_D = SSD_HEADS * SSD_HEAD_DIM
SSD_GROUPS = 2
SSD_STATE = 64
SSD_CONV = 4
SSD_CONV_DIM = SSD_D + 2 * SSD_GROUPS * SSD_STATE
FOX_HEADS = 4
FOX_HEAD_DIM = 64
FOX_D = FOX_HEADS * FOX_HEAD_DIM
MLA_HEADS = 4
MLA_Q_LORA = 256
MLA_KV_LORA = 128
MLA_NOPE = 64
MLA_ROPE = 32
MLA_V = 64
MLA_D = MLA_HEADS * MLA_V
ROPE_THETA = 10000.0
D_MIX = SSD_D + FOX_D + MLA_D
D_FF = 2816
ALPHA = (2 * DEPTH) ** 0.25
EPS = 1e-5

LANES = 128
MXU_W = 256
FF_CHUNK = MXU_W
N_FF_CHUNKS = D_FF // FF_CHUNK
N_OUT_CHUNKS = D_MODEL // FF_CHUNK
TQ = 2 * BLOCK
PAD = BLOCK - N_META
NEG_BIG = -1e30
M_INIT = 2 * NEG_BIG
VMEM_LIMIT = 56 * 1024 * 1024

C_XBC = SSD_D
C_ZX = SSD_D + SSD_CONV_DIM
C_FQ = 0
C_CQ = C_FQ + 2 * FOX_D
C_MISC = C_CQ + MLA_Q_LORA
N_IN_ARR = C_MISC + MLA_KV_LORA + LANES
MISC_KRR = 32
MISC_KR = MLA_NOPE
LOG2E = 1.4426950408889634
DV_AUG = MLA_V + 16
SMALL_DT = 0
SMALL_F = 8
HEAD_W = LANES
FOX_BIAS_LANE = FOX_HEAD_DIM


def _sigmoid(x):
    return 1.0 / (1.0 + jnp.exp(-x))


def _softplus(x):
    return jnp.maximum(x, 0.0) + jnp.log(1.0 + jnp.exp(-jnp.abs(x)))


def _layer_norm_rows(y, g, b):
    mu = jnp.mean(y, axis=-1, keepdims=True)
    yc = y - mu
    var = jnp.mean(yc * yc, axis=-1, keepdims=True)
    return yc * lax.rsqrt(var + EPS) * g + b


def _split3(x):
    x1 = x.astype(BF16)
    r1 = x - x1.astype(F32)
    x2 = r1.astype(BF16)
    r2 = r1 - x2.astype(F32)
    return x1, x2, r2.astype(BF16)


def _dot(a, b):
    return jnp.dot(a, b, preferred_element_type=F32)


def _dot_nt(a, b):
    return lax.dot_general(a, b, (((1,), (1,)), ((), ())), preferred_element_type=F32)


def _resident(shape):
    return pl.BlockSpec(shape, lambda *_: (0,) * len(shape), pipeline_mode=pl.Buffered(1))


def _layer(arr, l):
    tail = (0,) * (arr.ndim - 1)
    return pl.BlockSpec((None,) + arr.shape[1:], lambda *_: (l,) + tail,
                        pipeline_mode=pl.Buffered(1))


def _ffn_ln_kernel(prologue, n_main_tiles, *refs):
    wg_ref, wu_ref, wd_ref, g_ref, b_ref, o_ref, a_scr, y_scr = refs[-8:]
    if prologue == "plain":
        x = refs[0][...]
    elif prologue == "meta":
        x_ref, m_ref = refs[:2]
        x = jnp.where(pl.program_id(0) < n_main_tiles, x_ref[...], m_ref[...])
    else:
        h_ref, ys_ref, yf_ref, ym_ref, wo_ref, g2_ref, b2_ref = refs[:7]
        mix = (_dot(ys_ref[...], wo_ref[0:SSD_D, :])
               + _dot(yf_ref[...], wo_ref[SSD_D:SSD_D + FOX_D, :])
               + _dot(ym_ref[...], wo_ref[SSD_D + FOX_D:D_MIX, :]))
        x = _layer_norm_rows(ALPHA * h_ref[...] + mix, g2_ref[...], b2_ref[...])
    xb = x.astype(BF16)

    for c in range(N_FF_CHUNKS):
        cols = slice(c * FF_CHUNK, (c + 1) * FF_CHUNK)
        gate = _dot(xb, wg_ref[:, cols])
        up = _dot(xb, wu_ref[:, cols])
        a_scr[c] = (gate * _sigmoid(gate) * up).astype(BF16)

    for n in range(N_OUT_CHUNKS):
        cols = slice(n * FF_CHUNK, (n + 1) * FF_CHUNK)
        acc = _dot(a_scr[0], wd_ref[0:FF_CHUNK, cols])
        for c in range(1, N_FF_CHUNKS):
            acc = acc + _dot(a_scr[c], wd_ref[c * FF_CHUNK:(c + 1) * FF_CHUNK, cols])
        y_scr[:, cols] = ALPHA * x[:, cols] + 0.5 * acc
    o_ref[...] = _layer_norm_rows(y_scr[...], g_ref[...], b_ref[...])


def _ffn_ln(l, h, ffn, tm, n_out_rows, meta_tile=None, mix=None):
    def rows(width):
        return pl.BlockSpec((tm, width), lambda i: (i, 0))

    if mix is not None:
        prologue, n_main = "mix", None
        x_args = [h, *mix]
        x_specs = ([rows(D_MODEL), rows(SSD_D), rows(FOX_D), rows(MLA_D)]
                   + [_layer(p, l) for p in mix[3:]])
    elif meta_tile is not None:
        prologue, n_main = "meta", h.shape[0] // tm
        x_args = [h, meta_tile]
        x_specs = [pl.BlockSpec((tm, D_MODEL), lambda i: (jnp.minimum(i, n_main - 1), 0)),
                   _resident(meta_tile.shape)]
    else:
        prologue, n_main, x_args, x_specs = "plain", None, [h], [rows(D_MODEL)]
    return pl.pallas_call(
        functools.partial(_ffn_ln_kernel, prologue, n_main),
        grid=(n_out_rows // tm,),
        in_specs=x_specs + [_layer(p, l) for p in ffn],
        out_specs=rows(D_MODEL),
        out_shape=jax.ShapeDtypeStruct((n_out_rows, D_MODEL), F32),
        scratch_shapes=[pltpu.VMEM((N_FF_CHUNKS, tm, FF_CHUNK), BF16),
                        pltpu.VMEM((tm, D_MODEL), F32)],
        compiler_params=pltpu.CompilerParams(
            dimension_semantics=("arbitrary",), vmem_limit_bytes=VMEM_LIMIT),
        name="ffn_ln",
    )(*x_args, *ffn)


def _rms_rows(x, g):
    return x * lax.rsqrt(jnp.mean(x * x, axis=-1, keepdims=True) + EPS) * g


def _store_vt_blocks(out_ref, val_t, n_heads):
    dv = val_t.shape[0] // n_heads
    ones = jnp.ones((DV_AUG - dv, BLOCK), out_ref.dtype)
    for r in range(out_ref.shape[0]):
        for hd in range(n_heads):
            out_ref[r, hd * DV_AUG:hd * DV_AUG + dv, :] = (
                val_t[hd * dv:(hd + 1) * dv, r * BLOCK:(r + 1) * BLOCK].astype(out_ref.dtype))
            out_ref[r, hd * DV_AUG + dv:(hd + 1) * DV_AUG, :] = ones


def _proj_kernel(h_ref, wzx_ref, win_ref, wfvt_ref, wuq_ref, wuqr_ref, wkk_ref, wkvt_ref,
                 qg_ref, kvg_ref, cos_ref, sin_ref, cosk_ref,
                 zx_ref, small_ref, fq_ref, fk_ref, fvt_ref, mq_ref, mk_ref, mvt_ref):
    hb = h_ref[...].astype(BF16)
    zx_ref[...] = _dot(hb, wzx_ref[...])
    misc = _dot(hb, win_ref[:, C_MISC:N_IN_ARR])
    small = misc[:, MLA_KV_LORA:]
    small_ref[...] = small

    lane = lax.broadcasted_iota(jnp.int32, (1, LANES), 1)
    lane_lo = lane < FOX_HEAD_DIM
    fox_one = jnp.where((lane >= FOX_BIAS_LANE) & (lane < FOX_BIAS_LANE + 3), 1.0, 0.0)

    fqk = _dot(hb, win_ref[:, C_FQ:C_CQ])
    for half, (ref, scale, extra) in enumerate([(fq_ref, FOX_HEAD_DIM ** -0.5 * LOG2E, fox_one),
                                                (fk_ref, None, None)]):
        for pair in range(FOX_HEADS // 2):
            c0 = half * FOX_D + pair * LANES
            both = fqk[:, c0:c0 + LANES]
            if scale is not None:
                both = both * scale
            for par, val in enumerate([both, pltpu.roll(both, FOX_HEAD_DIM, 1)]):
                val = jnp.where(lane_lo, val, 0.0)
                if extra is not None:
                    val = val + extra
                hd = 2 * pair + par
                ref[:, hd * HEAD_W:(hd + 1) * HEAD_W] = val.astype(BF16)
    _store_vt_blocks(fvt_ref, _dot_nt(wfvt_ref[...], hb), FOX_HEADS)

    cos = cos_ref[...]
    sin = sin_ref[...]
    cqn = _rms_rows(_dot(hb, win_ref[:, C_CQ:C_MISC]), qg_ref[...]).astype(BF16)
    q = _dot(cqn, wuq_ref[...])
    qr = _dot(cqn, wuqr_ref[...])
    kvn = _rms_rows(misc[:, :MLA_KV_LORA], kvg_ref[...]).astype(BF16)
    kn = _dot(kvn, wkk_ref[...])
    _store_vt_blocks(mvt_ref, _dot_nt(wkvt_ref[...], kvn), MLA_HEADS)
    krope = small * cosk_ref[...] + pltpu.roll(small, MISC_KR - MISC_KRR, 1) * sin
    scale = (MLA_NOPE + MLA_ROPE) ** -0.5 * LOG2E
    for hd in range(MLA_HEADS):
        sl = slice(hd * HEAD_W, (hd + 1) * HEAD_W)
        mq_ref[:, sl] = ((q[:, sl] * cos + qr[:, sl] * sin) * scale).astype(BF16)
        mk_ref[:, sl] = (kn[:, sl] + krope).astype(BF16)


def _proj(l, h, w_in_bf, weights, tables, tm, seq):
    n_rows = h.shape[0]
    n_main_tiles = n_rows // tm - 1
    tiles_per_seq = seq // tm

    def rows(width):
        return pl.BlockSpec((tm, width), lambda i: (i, 0))

    tab = pl.BlockSpec((tm, LANES), lambda i: (
        jnp.where(i < n_main_tiles, i % tiles_per_seq, tiles_per_seq), 0))
    wzx = pl.BlockSpec((None, D_MODEL, C_ZX), lambda i: (l, 0, 0), pipeline_mode=pl.Buffered(1))
    vt_rows = FOX_HEADS * DV_AUG
    t_spec = pl.BlockSpec((tm // BLOCK, vt_rows, BLOCK), lambda i: (i, 0, 0))
    t_shape = jax.ShapeDtypeStruct((n_rows // BLOCK, vt_rows, BLOCK), BF16)
    qk_w = FOX_HEADS * HEAD_W
    row_outs = [(C_ZX, F32), (LANES, F32), (qk_w, BF16), (qk_w, BF16)]
    return pl.pallas_call(
        _proj_kernel,
        grid=(n_rows // tm,),
        in_specs=[rows(D_MODEL), wzx] + [_layer(w, l) for w in weights] + [tab, tab, tab],
        out_specs=([rows(w) for w, _ in row_outs] + [t_spec, rows(qk_w), rows(qk_w), t_spec]),
        out_shape=([jax.ShapeDtypeStruct((n_rows, w), dt) for w, dt in row_outs]
                   + [t_shape, jax.ShapeDtypeStruct((n_rows, qk_w), BF16),
                      jax.ShapeDtypeStruct((n_rows, qk_w), BF16), t_shape]),
        compiler_params=pltpu.CompilerParams(
            dimension_semantics=("arbitrary",), vmem_limit_bytes=VMEM_LIMIT),
        name="in_proj",
    )(h, w_in_bf, *weights, *tables)


def _ssd_kernel(zx_ref, small_ref, cw_ref, cb_ref, dtb_ref, a_ref, dsk_ref, ng_ref,
                zero_fill_ref, o_ref, conv_scr, s_scr, meta_conv_scr, meta_s_scr):
    del zero_fill_ref
    b = pl.program_id(0)
    c = pl.program_id(1)

    @pl.when((c == 0) & (b == 0))
    def _():
        conv_scr[0:8, :] = jnp.zeros((8, SSD_CONV_DIM), F32)
        s_scr[...] = jnp.zeros(s_scr.shape, F32)

    @pl.when((c == 0) & (b > 0))
    def _():
        conv_scr[0:8, :] = meta_conv_scr[...]
        s_scr[...] = meta_s_scr[...]

    @pl.when((c > 0) | (b == 0))
    def _():
        _ssd_chunk(c, zx_ref, small_ref, cw_ref, cb_ref, dtb_ref, a_ref, dsk_ref, ng_ref,
                   o_ref, conv_scr, s_scr)

    @pl.when((c == 0) & (b == 0))
    def _():
        meta_conv_scr[...] = conv_scr[0:8, :]
        meta_s_scr[...] = s_scr[...]


def _ssd_chunk(c, zx_ref, small_ref, cw_ref, cb_ref, dtb_ref, a_ref, dsk_ref, ng_ref,
               o_ref, conv_scr, s_scr):
    Q = BLOCK
    row = lax.broadcasted_iota(jnp.int32, (Q, 1), 0)
    valid = jnp.logical_or(c > 0, row >= PAD)
    lane = lax.broadcasted_iota(jnp.int32, (1, LANES), 1)
    lane_lo = lane < SSD_HEAD_DIM
    sub = lax.broadcasted_iota(jnp.int32, (LANES, 1), 0)

    conv_scr[8:8 + Q, :] = jnp.where(valid, zx_ref[:, C_XBC:C_ZX], 0.0)
    acc = cb_ref[...]
    for k in range(SSD_CONV):
        off = 8 - (SSD_CONV - 1) + k
        acc = acc + cw_ref[k:k + 1, :] * conv_scr[off:off + Q, :]
    conv_scr[0:8, :] = conv_scr[Q:Q + 8, :]
    xbc = acc * _sigmoid(acc)
    bm = xbc[:, SSD_D:SSD_D + LANES]
    cm = xbc[:, SSD_D + LANES:SSD_D + 2 * LANES]

    dt = jnp.where(valid, _softplus(small_ref[...] + dtb_ref[...]), 0.0)
    a = dt * a_ref[...]
    tri = (lax.broadcasted_iota(jnp.int32, (Q, Q), 0)
           >= lax.broadcasted_iota(jnp.int32, (Q, Q), 1))
    tri_b = jnp.where(tri, 1.0, 0.0).astype(BF16)
    a1, a2, a3 = _split3(a)
    a_cum = _dot(tri_b, a1) + _dot(tri_b, a2) + _dot(tri_b, a3)
    a_cum_t = a_cum.T
    bm_t = bm.T

    cm_b = cm.astype(BF16)
    bm_b = bm.astype(BF16)
    cb_g = [_dot_nt(jnp.where(lane_lo, cm, 0.0).astype(BF16), bm_b),
            _dot_nt(jnp.where(lane_lo, 0.0, cm).astype(BF16), bm_b)]
    rows_g = [sub < SSD_STATE, sub >= SSD_STATE]

    pairs_per_group = SSD_HEADS // 2 // SSD_GROUPS
    y_pairs = []
    for p in range(SSD_HEADS // 2):
        g = p // pairs_per_group
        psl = slice(p * LANES, (p + 1) * LANES)
        xs_p = xbc[:, psl]
        dt_pair = jnp.where(lane_lo, dt[:, 2 * p:2 * p + 1], dt[:, 2 * p + 1:2 * p + 2])
        xdt = (xs_p * dt_pair).astype(BF16)
        s_old = s_scr[p]
        yd, upd, e_col, e_last = [], [], [], []
        for par in range(2):
            hd = 2 * p + par
            col = a_cum[:, hd:hd + 1]
            rowv = a_cum_t[hd:hd + 1, :]
            last = a_cum_t[hd:hd + 1, Q - 1:Q]
            seg = jnp.exp(jnp.where(tri, col - rowv, NEG_BIG))
            yd.append(_dot((cb_g[g] * seg).astype(BF16), xdt))
            upd.append(_dot((bm_t * jnp.exp(last - rowv)).astype(BF16), xdt))
            e_col.append(jnp.exp(col))
            e_last.append(jnp.exp(last))
        y_off = _dot(cm_b, s_old.astype(BF16)) * jnp.where(lane_lo, e_col[0], e_col[1])
        s_new = (jnp.where(lane_lo, e_last[0], e_last[1]) * s_old
                 + jnp.where(rows_g[g], jnp.where(lane_lo, upd[0], upd[1]), 0.0))
        s_scr[p] = s_new
        y_p = jnp.where(lane_lo, yd[0], yd[1]) + y_off + dsk_ref[:, psl] * xs_p
        z_p = zx_ref[:, psl]
        y_pairs.append(y_p * (z_p * _sigmoid(z_p)))

    for g in range(SSD_GROUPS):
        ps = range(g * pairs_per_group, (g + 1) * pairs_per_group)
        ss = sum(jnp.sum(y_pairs[p] * y_pairs[p], axis=-1, keepdims=True) for p in ps)
        inv = lax.rsqrt(ss * (1.0 / (pairs_per_group * LANES)) + EPS)
        for p in ps:
            psl = slice(p * LANES, (p + 1) * LANES)
            o_ref[:, psl] = (y_pairs[p] * inv * ng_ref[:, psl]).astype(BF16)


def _ssd(l, zx, small, params, bsz, seq):
    n_rows = zx.shape[0]
    blocks_per_seq = seq // BLOCK
    meta_block = n_rows // BLOCK - 1

    def block(b, c):
        first = jnp.where(b == 0, meta_block, b * blocks_per_seq)
        return (jnp.where(c == 0, first, b * blocks_per_seq + c - 1), 0)

    return pl.pallas_call(
        _ssd_kernel,
        grid=(bsz, blocks_per_seq + 1),
        in_specs=[pl.BlockSpec((BLOCK, C_ZX), block), pl.BlockSpec((BLOCK, LANES), block)]
                 + [_layer(p, l) for p in params] + [pl.BlockSpec(memory_space=pl.ANY)],
        out_specs=pl.BlockSpec((BLOCK, SSD_D), block),
        out_shape=jax.ShapeDtypeStruct((n_rows, SSD_D), BF16),
        input_output_aliases={2 + len(params): 0},
        scratch_shapes=[pltpu.VMEM((BLOCK + 8, SSD_CONV_DIM), F32),
                        pltpu.VMEM((SSD_HEADS // 2, LANES, LANES), F32),
                        pltpu.VMEM((8, SSD_CONV_DIM), F32),
                        pltpu.VMEM((SSD_HEADS // 2, LANES, LANES), F32)],
        compiler_params=pltpu.CompilerParams(
            dimension_semantics=("arbitrary", "arbitrary"), vmem_limit_bytes=VMEM_LIMIT),
        name="ssd_mixer",
    )(zx, small, *params, jnp.zeros((n_rows, SSD_D), BF16))


def _fox_keys_kernel(km_ref, k_ref, sm_ref, s_ref, fb_ref, om_ref, o_ref):
    T = BLOCK
    width = k_ref.shape[1]
    row = lax.broadcasted_iota(jnp.int32, (T, 1), 0)
    tri = (lax.broadcasted_iota(jnp.int32, (T, T), 0)
           >= lax.broadcasted_iota(jnp.int32, (T, T), 1))
    tri_b = jnp.where(tri, 1.0, 0.0).astype(BF16)
    src = lax.broadcasted_iota(jnp.int32, (LANES, width), 0)
    dst = lax.broadcasted_iota(jnp.int32, (LANES, width), 1)
    dst_head = jnp.right_shift(dst, HEAD_W.bit_length() - 1)
    dst_lane = jnp.bitwise_and(dst, HEAD_W - 1)
    sel = [jnp.where((src == SMALL_F + dst_head) & (dst_lane == FOX_BIAS_LANE + i),
                     1.0, 0.0).astype(BF16) for i in range(3)]

    def local_cumsum(small_blk, is_meta):
        log_f = -_softplus(-(small_blk + fb_ref[...]))
        if is_meta:
            log_f = jnp.where(row < PAD, 0.0, log_f)
        f1, f2, f3 = _split3(log_f)
        return _dot(tri_b, f1) + _dot(tri_b, f2) + _dot(tri_b, f3)

    def keys(c_blk, k_blk):
        b1, b2, b3 = _split3(c_blk * (-LOG2E))
        placed = _dot(b1, sel[0]) + _dot(b2, sel[1]) + _dot(b3, sel[2])
        return (k_blk.astype(F32) + placed).astype(BF16)

    n_blocks = k_ref.shape[0] // T
    rows = [slice(j * T, (j + 1) * T) for j in range(n_blocks)]
    c_meta = local_cumsum(sm_ref[...], True)
    local = [local_cumsum(s_ref[rows[j], :], False) for j in range(n_blocks)]
    om_ref[...] = keys(c_meta, km_ref[...])
    carry = c_meta[T - 1:T, :]
    for j in range(n_blocks):
        c_blk = local[j] + carry
        carry = c_blk[T - 1:T, :]
        o_ref[rows[j], :] = keys(c_blk, k_ref[rows[j], :])


def _fox_keys(l, k, small, fb, bsz, seq):
    n_rows, width = k.shape
    meta_block = n_rows // BLOCK - 1
    main = lambda w: pl.BlockSpec((seq, w), lambda b: (b, 0))
    meta = lambda w: pl.BlockSpec((BLOCK, w), lambda b: (meta_block, 0))
    return pl.pallas_call(
        _fox_keys_kernel,
        grid=(bsz,),
        in_specs=[meta(width), main(width), meta(LANES), main(LANES), _layer(fb, l)],
        out_specs=[pl.BlockSpec((BLOCK, width), lambda b: (0, 0)), main(width)],
        out_shape=[jax.ShapeDtypeStruct((BLOCK, width), BF16),
                   jax.ShapeDtypeStruct((bsz * seq, width), BF16)],
        compiler_params=pltpu.CompilerParams(
            dimension_semantics=("arbitrary",), vmem_limit_bytes=VMEM_LIMIT),
        name="fox_keys",
    )(k, k, small, small, fb)


def _attn_kernel(n_heads, q_ref, km_ref, k_ref, vtm_ref, vt_ref, zero_fill_ref, o_ref,
                 acc_scr, s0_scr, s1_scr):
    del zero_fill_ref
    t = pl.program_id(1)
    dv = acc_scr.shape[1]
    heads = range(n_heads)
    hsl = [slice(hd * HEAD_W, (hd + 1) * HEAD_W) for hd in heads]
    vsl = [slice(hd * dv, (hd + 1) * dv) for hd in heads]
    qs = [q_ref[:, hsl[hd]] for hd in heads]
    ahead = (lax.broadcasted_iota(jnp.int32, (TQ, TQ), 0)
             - lax.broadcasted_iota(jnp.int32, (TQ, TQ), 1))
    acc_scr[...] = jnp.zeros(acc_scr.shape, F32)

    def softmax_pv(ss, vts, ms, mask):
        if mask is not None:
            ss = [jnp.where(mask, s, NEG_BIG) for s in ss]
        new_ms = [jnp.maximum(ms[hd], jnp.max(ss[hd], axis=0, keepdims=True)) for hd in heads]
        corrs = [jnp.exp2(ms[hd] - new_ms[hd]) for hd in heads]
        prs = [jnp.exp2(ss[hd] - new_ms[hd]).astype(BF16) for hd in heads]
        for hd in heads:
            acc_scr[hd] = corrs[hd] * acc_scr[hd] + _dot(vts[hd], prs[hd])
        return tuple(new_ms)

    def meta_chunk(carry, mask):
        return softmax_pv([_dot_nt(km_ref[PAD:, hsl[hd]], qs[hd]) for hd in heads],
                          [vtm_ref[0, vsl[hd], PAD:] for hd in heads], carry, mask)

    def scores(c, slot_scr):
        r0 = pl.multiple_of(c * TQ, TQ)
        for hd in heads:
            slot_scr[hd] = _dot_nt(k_ref[pl.ds(r0, TQ), hsl[hd]], qs[hd])

    def consume(c, slot_scr, carry, mask):
        n_sub = TQ // BLOCK
        vts = [jnp.concatenate([vt_ref[n_sub * c + i, vsl[hd], :] for i in range(n_sub)], axis=1)
               for hd in heads]
        return softmax_pv([slot_scr[hd] for hd in heads], vts, carry, mask)

    def finish():
        dv_out = o_ref.shape[1] // n_heads
        ys = [acc_scr[hd, 0:dv_out, :] * (1.0 / acc_scr[hd, dv_out:dv_out + 1, :])
              for hd in heads]
        for p in range(n_heads // 2):
            y_t = jnp.concatenate([ys[2 * p], ys[2 * p + 1]], axis=0)
            o_ref[:, p * LANES:(p + 1) * LANES] = y_t.T.astype(BF16)

    init = tuple(jnp.full((1, TQ), M_INIT, F32) for _ in heads)

    @pl.when((t == 0) & (pl.program_id(0) == 0))
    def _():
        meta_chunk(init, ahead[:N_META, :] <= -(TQ - N_META))
        finish()

    @pl.when(t > 0)
    def _():
        scores(0, s0_scr)
        carry = meta_chunk(init, None)

        def pair(i, cr):
            c = 2 * i
            scores(c + 1, s1_scr)
            cr = consume(c, s0_scr, cr, None)
            scores(c + 2, s0_scr)
            return consume(c + 1, s1_scr, cr, None)

        n_unmasked = t - 1
        carry = lax.fori_loop(0, n_unmasked // 2, pair, carry)
        c = 2 * (n_unmasked // 2)
        diagonal = ahead <= 0

        def odd_tail(cr):
            scores(c + 1, s1_scr)
            cr = consume(c, s0_scr, cr, None)
            return consume(c + 1, s1_scr, cr, diagonal)

        lax.cond(n_unmasked % 2 == 1, odd_tail,
                 lambda cr: consume(c, s0_scr, cr, diagonal), carry)
        finish()


def _attention(q, k_meta, k_main, vt, n_heads, bsz, seq, meta_k_block, name):
    n_rows, qw = q.shape
    dv_aug = vt.shape[1] // n_heads
    dv = dv_aug - (DV_AUG - MLA_V)
    tiles_per_seq = seq // TQ
    meta_q_tile = n_rows // TQ - 1
    meta_block = n_rows // BLOCK - 1

    def q_tile(b, t):
        first = jnp.where(b == 0, meta_q_tile, b * tiles_per_seq)
        return (jnp.where(t == 0, first, b * tiles_per_seq + t - 1), 0)

    return pl.pallas_call(
        functools.partial(_attn_kernel, n_heads),
        grid=(bsz, tiles_per_seq + 1),
        in_specs=[pl.BlockSpec((TQ, qw), q_tile),
                  pl.BlockSpec((BLOCK, qw), lambda b, t: (meta_k_block, 0)),
                  pl.BlockSpec((seq, qw), lambda b, t: (b, 0)),
                  pl.BlockSpec((1,) + vt.shape[1:], lambda b, t: (meta_block, 0, 0)),
                  pl.BlockSpec((seq // BLOCK,) + vt.shape[1:], lambda b, t: (b, 0, 0)),
                  pl.BlockSpec(memory_space=pl.ANY)],
        out_specs=pl.BlockSpec((TQ, n_heads * dv), q_tile),
        out_shape=jax.ShapeDtypeStruct((n_rows, n_heads * dv), BF16),
        input_output_aliases={5: 0},
        scratch_shapes=[pltpu.VMEM((n_heads, dv_aug, TQ), F32),
                        pltpu.VMEM((n_heads, TQ, TQ), F32),
                        pltpu.VMEM((n_heads, TQ, TQ), F32)],
        compiler_params=pltpu.CompilerParams(
            dimension_semantics=("arbitrary", "arbitrary"), vmem_limit_bytes=VMEM_LIMIT),
        name=name,
    )(q, k_meta, k_main, vt, vt, jnp.zeros((n_rows, n_heads * dv), BF16))


def _rot_half_cols(w):
    half = MLA_ROPE // 2
    return jnp.concatenate([-w[..., half:], w[..., :half]], axis=-1)


def _in_proj_weights(w_in):
    sizes = [SSD_D, SSD_CONV_DIM, SSD_HEADS, FOX_D, FOX_D, FOX_D, FOX_HEADS,
             MLA_Q_LORA, MLA_KV_LORA, MLA_ROPE]
    splits = [int(s) for s in np.cumsum(sizes)[:-1]]
    z, xbc, dt, fq, fk, fv, fr, cq, ckv, kr = jnp.split(w_in, splits, axis=-1)
    zeros = lambda n: jnp.zeros(w_in.shape[:-1] + (n,), w_in.dtype)
    del z, xbc
    misc = jnp.concatenate([dt, fr, zeros(MISC_KRR - SMALL_F - FOX_HEADS), _rot_half_cols(kr),
                            kr, zeros(LANES - MISC_KR - MLA_ROPE)], axis=-1)
    win = jnp.concatenate([fq, fk, cq, ckv, misc], axis=-1)
    return win, jnp.swapaxes(fv, -1, -2)


def _mla_weights(w_uq, w_ukv):
    rope_pad = LANES - MLA_NOPE - MLA_ROPE
    lead = w_uq.shape[:-1]
    wq = w_uq.astype(BF16).reshape(lead + (MLA_HEADS, MLA_NOPE + MLA_ROPE))
    zq = jnp.zeros(lead + (MLA_HEADS, rope_pad), BF16)
    wuq = jnp.concatenate([wq, zq], axis=-1).reshape(lead + (MLA_HEADS * LANES,))
    wuqr = jnp.concatenate([jnp.zeros_like(wq[..., :MLA_NOPE]),
                            _rot_half_cols(wq[..., MLA_NOPE:]), zq],
                           axis=-1).reshape(lead + (MLA_HEADS * LANES,))
    lead = w_ukv.shape[:-1]
    wkv = w_ukv.astype(BF16).reshape(lead + (MLA_HEADS, MLA_NOPE + MLA_V))
    wkk = jnp.concatenate([wkv[..., :MLA_NOPE], jnp.zeros_like(wkv[..., MLA_NOPE:])],
                          axis=-1).reshape(lead + (MLA_HEADS * LANES,))
    wvt = jnp.swapaxes(wkv[..., MLA_NOPE:].reshape(lead + (MLA_D,)), -1, -2)
    return wuq, wuqr, wkk, wvt


def _rows(v, width=None, offset=0):
    v = v.astype(F32)
    if width is not None:
        v = jnp.pad(v, ((0, 0), (offset, width - offset - v.shape[-1])))
    return v[:, None, :]


def _position_tables(seq, tm):
    pos = jnp.concatenate([N_META + jnp.arange(seq, dtype=F32),
                           jnp.arange(tm, dtype=F32) - (tm - N_META)])
    inv_freq = 1.0 / (ROPE_THETA ** (jnp.arange(0, MLA_ROPE, 2, dtype=F32) / MLA_ROPE))
    ang = pos[:, None] * inv_freq[None, :]
    cos, sin = jnp.cos(ang), jnp.sin(ang)
    n = seq + tm
    rope_pad = LANES - MLA_NOPE - MLA_ROPE
    cos128 = jnp.concatenate([jnp.ones((n, MLA_NOPE), F32), cos, cos,
                              jnp.zeros((n, rope_pad), F32)], axis=-1)
    sin128 = jnp.concatenate([jnp.zeros((n, MLA_NOPE), F32), sin, sin,
                              jnp.zeros((n, rope_pad), F32)], axis=-1)
    cosk128 = jnp.concatenate([jnp.zeros((n, MLA_NOPE), F32), cos, cos,
                               jnp.zeros((n, rope_pad), F32)], axis=-1)
    return cos128, sin128, cosk128


def kernel(x, meta, ffn1_w_gate, ffn1_w_up, ffn1_w_down, ln1_g, ln1_b, w_in, conv_w, conv_b, dt_bias, a_log, d_skip, ssd_norm_g, fox_f_b, mla_q_norm_g, mla_w_uq, mla_kv_norm_g, mla_w_ukv, w_out, ln2_g, ln2_b, ffn2_w_gate, ffn2_w_up, ffn2_w_down, ln3_g, ln3_b):
    bsz, seq, _ = x.shape
    assert seq % TQ == 0
    tm = 2 * TQ if seq % (2 * TQ) == 0 else TQ
    n_main = bsz * seq
    n_rows = n_main + tm
    meta_block = n_rows // BLOCK - 1

    h = x.reshape(n_main, D_MODEL)
    meta_tile = jnp.concatenate([jnp.zeros((tm - N_META, D_MODEL), x.dtype),
                                 meta.astype(x.dtype)], axis=0)
    tables = _position_tables(seq, tm)
    bf = lambda w: w.astype(BF16)

    ffn1 = (bf(ffn1_w_gate), bf(ffn1_w_up), bf(ffn1_w_down), _rows(ln1_g), _rows(ln1_b))
    ffn2 = (bf(ffn2_w_gate), bf(ffn2_w_up), bf(ffn2_w_down), _rows(ln3_g), _rows(ln3_b))
    w_in_bf = bf(w_in)
    proj_w = (*_in_proj_weights(w_in_bf), *_mla_weights(mla_w_uq, mla_w_ukv),
              _rows(mla_q_norm_g), _rows(mla_kv_norm_g))
    ssd_p = (conv_w.astype(F32), _rows(conv_b), _rows(dt_bias, LANES),
             _rows(-jnp.exp(a_log.astype(F32)), LANES),
             _rows(jnp.repeat(d_skip, SSD_HEAD_DIM, axis=-1)), _rows(ssd_norm_g))
    fb = _rows(fox_f_b, LANES, SMALL_F)
    out_p = (bf(w_out), _rows(ln2_g), _rows(ln2_b))

    for l in range(DEPTH):
        h = _ffn_ln(l, h, ffn1, tm, n_rows, meta_tile=meta_tile if l == 0 else None)
        zx, small, fq, fk, fvt, mq, mk, mvt = _proj(l, h, w_in_bf, proj_w, tables, tm, seq)
        y_ssd = _ssd(l, zx, small, ssd_p, bsz, seq)
        fk_meta, fk_main = _fox_keys(l, fk, small, fb, bsz, seq)
        y_fox = _attention(fq, fk_meta, fk_main, fvt, FOX_HEADS, bsz, seq, 0, "fox_attention")
        y_mla = _attention(mq, mk, mk, mvt, MLA_HEADS, bsz, seq, meta_block, "mla_attention")
        h = _ffn_ln(l, h, ffn2, tm, n_rows if l < DEPTH - 1 else n_main,
                    mix=(y_ssd, y_fox, y_mla, *out_p))

    return h.reshape(bsz, seq, D_MODEL)
```

```python
import functools

import numpy as np
import jax
import jax.numpy as jnp
from jax import lax
from jax.experimental import pallas as pl
from jax.experimental.pallas import tpu as pltpu

F32 = jnp.float32
BF16 = jnp.bfloat16

D_MODEL = 1024
DEPTH = 2
N_META = 16
BLOCK = 128
SSD_HEADS = 8
SSD_HEAD_DIM = 64
SSD_D = SSD_HEADS * SSD_HEAD_DIM
SSD_GROUPS = 2
SSD_STATE = 64
SSD_CONV = 4
SSD_CONV_DIM = SSD_D + 2 * SSD_GROUPS * SSD_STATE
FOX_HEADS = 4
FOX_HEAD_DIM = 64
FOX_D = FOX_HEADS * FOX_HEAD_DIM
MLA_HEADS = 4
MLA_Q_LORA = 256
MLA_KV_LORA = 128
MLA_NOPE = 64
MLA_ROPE = 32
MLA_V = 64
MLA_D = MLA_HEADS * MLA_V
ROPE_THETA = 10000.0
D_MIX = SSD_D + FOX_D + MLA_D
D_FF = 2816
ALPHA = (2 * DEPTH) ** 0.25
EPS = 1e-5

LANES = 128
MXU_W = 256
FF_CHUNK = MXU_W
N_FF_CHUNKS = D_FF // FF_CHUNK
N_OUT_CHUNKS = D_MODEL // FF_CHUNK
TQ = 2 * BLOCK
PAD = BLOCK - N_META
NEG_BIG = -1e30
M_INIT = 2 * NEG_BIG
VMEM_LIMIT = 56 * 1024 * 1024

C_XBC = SSD_D
C_ZX = SSD_D + SSD_CONV_DIM
C_FQ = 0
C_CQ = C_FQ + 2 * FOX_D
C_MISC = C_CQ + MLA_Q_LORA
N_IN_ARR = C_MISC + MLA_KV_LORA + LANES
MISC_KRR = 32
MISC_KR = MLA_NOPE
LOG2E = 1.4426950408889634
DV_AUG = MLA_V + 16
SMALL_DT = 0
SMALL_F = 8
HEAD_W = LANES
FOX_BIAS_LANE = FOX_HEAD_DIM


def _sigmoid(x):
    return 1.0 / (1.0 + jnp.exp(-x))


def _softplus(x):
    return jnp.maximum(x, 0.0) + jnp.log(1.0 + jnp.exp(-jnp.abs(x)))


def _layer_norm_rows(y, g, b):
    mu = jnp.mean(y, axis=-1, keepdims=True)
    yc = y - mu
    var = jnp.mean(yc * yc, axis=-1, keepdims=True)
    return yc * lax.rsqrt(var + EPS) * g + b


def _split3(x):
    x1 = x.astype(BF16)
    r1 = x - x1.astype(F32)
    x2 = r1.astype(BF16)
    r2 = r1 - x2.astype(F32)
    return x1, x2, r2.astype(BF16)


def _dot(a, b):
    return jnp.dot(a, b, preferred_element_type=F32)


def _dot_nt(a, b):
    return lax.dot_general(a, b, (((1,), (1,)), ((), ())), preferred_element_type=F32)


def _resident(shape):
    return pl.BlockSpec(shape, lambda *_: (0,) * len(shape), pipeline_mode=pl.Buffered(1))


def _layer(arr, l):
    tail = (0,) * (arr.ndim - 1)
    return pl.BlockSpec((None,) + arr.shape[1:], lambda *_: (l,) + tail,
                        pipeline_mode=pl.Buffered(1))


N_PROJ_IN = 12
N_PROJ_OUT = 8


def _ffn_ln_kernel(prologue, n_main_tiles, with_proj, *refs):
    a_scr, y_scr = refs[-2:]
    refs = refs[:-2]
    if with_proj:
        proj_out, refs = refs[-N_PROJ_OUT:], refs[:-N_PROJ_OUT]
        o_ref = refs[-1]
        proj_in, refs = refs[-1 - N_PROJ_IN:-1], refs[:-1 - N_PROJ_IN]
    else:
        o_ref, refs = refs[-1], refs[:-1]
    wg_ref, wu_ref, wd_ref, g_ref, b_ref = refs[-5:]
    if prologue == "plain":
        x = refs[0][...]
    elif prologue == "meta":
        x_ref, m_ref = refs[:2]
        x = jnp.where(pl.program_id(0) < n_main_tiles, x_ref[...], m_ref[...])
    else:
        h_ref, ys_ref, yf_ref, ym_ref, wo_ref, g2_ref, b2_ref = refs[:7]
        mix = (_dot(ys_ref[...], wo_ref[0:SSD_D, :])
               + _dot(yf_ref[...], wo_ref[SSD_D:SSD_D + FOX_D, :])
               + _dot(ym_ref[...], wo_ref[SSD_D + FOX_D:D_MIX, :]))
        x = _layer_norm_rows(ALPHA * h_ref[...] + mix, g2_ref[...], b2_ref[...])
    xb = x.astype(BF16)

    for c in range(N_FF_CHUNKS):
        cols = slice(c * FF_CHUNK, (c + 1) * FF_CHUNK)
        gate = _dot(xb, wg_ref[:, cols])
        up = _dot(xb, wu_ref[:, cols])
        a_scr[c] = (gate * _sigmoid(gate) * up).astype(BF16)

    for n in range(N_OUT_CHUNKS):
        cols = slice(n * FF_CHUNK, (n + 1) * FF_CHUNK)
        acc = _dot(a_scr[0], wd_ref[0:FF_CHUNK, cols])
        for c in range(1, N_FF_CHUNKS):
            acc = acc + _dot(a_scr[c], wd_ref[c * FF_CHUNK:(c + 1) * FF_CHUNK, cols])
        y_scr[:, cols] = ALPHA * x[:, cols] + 0.5 * acc
    out = _layer_norm_rows(y_scr[...], g_ref[...], b_ref[...])
    o_ref[...] = out
    if with_proj:
        _proj_body(out, *proj_in, *proj_out)


def _ffn_ln(l, h, ffn, tm, n_out_rows, meta_tile=None, mix=None, proj=None):
    def rows(width):
        return pl.BlockSpec((tm, width), lambda i: (i, 0))

    out_specs = rows(D_MODEL)
    out_shape = jax.ShapeDtypeStruct((n_out_rows, D_MODEL), F32)
    p_args, p_specs = [], []
    if proj is not None:
        p_args, p_specs, p_out_specs, p_out_shape = _proj_specs(l, n_out_rows, tm, *proj)
        out_specs, out_shape = [out_specs] + p_out_specs, [out_shape] + p_out_shape

    if mix is not None:
        prologue, n_main = "mix", None
        x_args = [h, *mix]
        x_specs = ([rows(D_MODEL), rows(SSD_D), rows(FOX_D), rows(MLA_D)]
                   + [_layer(p, l) for p in mix[3:]])
    elif meta_tile is not None:
        prologue, n_main = "meta", h.shape[0] // tm
        x_args = [h, meta_tile]
        x_specs = [pl.BlockSpec((tm, D_MODEL), lambda i: (jnp.minimum(i, n_main - 1), 0)),
                   _resident(meta_tile.shape)]
    else:
        prologue, n_main, x_args, x_specs = "plain", None, [h], [rows(D_MODEL)]
    return pl.pallas_call(
        functools.partial(_ffn_ln_kernel, prologue, n_main, proj is not None),
        grid=(n_out_rows // tm,),
        in_specs=x_specs + [_layer(p, l) for p in ffn] + p_specs,
        out_specs=out_specs,
        out_shape=out_shape,
        scratch_shapes=[pltpu.VMEM((N_FF_CHUNKS, tm, FF_CHUNK), BF16),
                        pltpu.VMEM((tm, D_MODEL), F32)],
        compiler_params=pltpu.CompilerParams(
            dimension_semantics=("arbitrary",), vmem_limit_bytes=VMEM_LIMIT),
        name="ffn_ln_proj" if proj is not None else "ffn_ln",
    )(*x_args, *ffn, *p_args)


def _rms_rows(x, g):
    return x * lax.rsqrt(jnp.mean(x * x, axis=-1, keepdims=True) + EPS) * g


def _store_vt_blocks(out_ref, val_t, n_heads):
    dv = val_t.shape[0] // n_heads
    ones = jnp.ones((DV_AUG - dv, BLOCK), out_ref.dtype)
    for r in range(out_ref.shape[0]):
        for hd in range(n_heads):
            out_ref[r, hd * DV_AUG:hd * DV_AUG + dv, :] = (
                val_t[hd * dv:(hd + 1) * dv, r * BLOCK:(r + 1) * BLOCK].astype(out_ref.dtype))
            out_ref[r, hd * DV_AUG + dv:(hd + 1) * DV_AUG, :] = ones


def _proj_body(h, wzx_ref, win_ref, wfvt_ref, wuq_ref, wuqr_ref, wkk_ref, wkvt_ref,
               qg_ref, kvg_ref, cos_ref, sin_ref, cosk_ref,
               zx_ref, small_ref, fq_ref, fk_ref, fvt_ref, mq_ref, mk_ref, mvt_ref):
    hb = h.astype(BF16)
    zx_ref[...] = _dot(hb, wzx_ref[...])
    misc = _dot(hb, win_ref[:, C_MISC:N_IN_ARR])
    small = misc[:, MLA_KV_LORA:]
    small_ref[...] = small

    lane = lax.broadcasted_iota(jnp.int32, (1, LANES), 1)
    lane_lo = lane < FOX_HEAD_DIM
    fox_one = jnp.where((lane >= FOX_BIAS_LANE) & (lane < FOX_BIAS_LANE + 3), 1.0, 0.0)

    fqk = _dot(hb, win_ref[:, C_FQ:C_CQ])
    for half, (ref, scale, extra) in enumerate([(fq_ref, FOX_HEAD_DIM ** -0.5 * LOG2E, fox_one),
                                                (fk_ref, None, None)]):
        for pair in range(FOX_HEADS // 2):
            c0 = half * FOX_D + pair * LANES
            both = fqk[:, c0:c0 + LANES]
            if scale is not None:
                both = both * scale
            for par, val in enumerate([both, pltpu.roll(both, FOX_HEAD_DIM, 1)]):
                val = jnp.where(lane_lo, val, 0.0)
                if extra is not None:
                    val = val + extra
                hd = 2 * pair + par
                ref[:, hd * HEAD_W:(hd + 1) * HEAD_W] = val.astype(BF16)
    _store_vt_blocks(fvt_ref, _dot_nt(wfvt_ref[...], hb), FOX_HEADS)

    cos = cos_ref[...]
    sin = sin_ref[...]
    cqn = _rms_rows(_dot(hb, win_ref[:, C_CQ:C_MISC]), qg_ref[...]).astype(BF16)
    q = _dot(cqn, wuq_ref[...])
    qr = _dot(cqn, wuqr_ref[...])
    kvn = _rms_rows(misc[:, :MLA_KV_LORA], kvg_ref[...]).astype(BF16)
    kn = _dot(kvn, wkk_ref[...])
    _store_vt_blocks(mvt_ref, _dot_nt(wkvt_ref[...], kvn), MLA_HEADS)
    krope = small * cosk_ref[...] + pltpu.roll(small, MISC_KR - MISC_KRR, 1) * sin
    scale = (MLA_NOPE + MLA_ROPE) ** -0.5 * LOG2E
    for hd in range(MLA_HEADS):
        sl = slice(hd * HEAD_W, (hd + 1) * HEAD_W)
        mq_ref[:, sl] = ((q[:, sl] * cos + qr[:, sl] * sin) * scale).astype(BF16)
        mk_ref[:, sl] = (kn[:, sl] + krope).astype(BF16)


def _proj_specs(l, n_rows, tm, w_in_bf, weights, tables, seq):
    n_main_tiles = n_rows // tm - 1
    tiles_per_seq = seq // tm

    def rows(width):
        return pl.BlockSpec((tm, width), lambda i: (i, 0))

    tab = pl.BlockSpec((tm, LANES), lambda i: (
        jnp.where(i < n_main_tiles, i % tiles_per_seq, tiles_per_seq), 0))
    wzx = pl.BlockSpec((None, D_MODEL, C_ZX), lambda i: (l, 0, 0), pipeline_mode=pl.Buffered(1))
    vt_rows = FOX_HEADS * DV_AUG
    t_spec = pl.BlockSpec((tm // BLOCK, vt_rows, BLOCK), lambda i: (i, 0, 0))
    t_shape = jax.ShapeDtypeStruct((n_rows // BLOCK, vt_rows, BLOCK), BF16)
    qk_w = FOX_HEADS * HEAD_W
    row_outs = [(C_ZX, F32), (LANES, F32), (qk_w, BF16), (qk_w, BF16)]
    args = [w_in_bf, *weights, *tables]
    in_specs = [wzx] + [_layer(w, l) for w in weights] + [tab, tab, tab]
    out_specs = [rows(w) for w, _ in row_outs] + [t_spec, rows(qk_w), rows(qk_w), t_spec]
    out_shape = ([jax.ShapeDtypeStruct((n_rows, w), dt) for w, dt in row_outs]
                 + [t_shape, jax.ShapeDtypeStruct((n_rows, qk_w), BF16),
                    jax.ShapeDtypeStruct((n_rows, qk_w), BF16), t_shape])
    assert len(args) == N_PROJ_IN and len(out_shape) == N_PROJ_OUT
    return args, in_specs, out_specs, out_shape


def _ssd_kernel(zx_ref, small_ref, cw_ref, cb_ref, dtb_ref, a_ref, dsk_ref, ng_ref,
                zero_fill_ref, o_ref, conv_scr, s_scr, meta_conv_scr, meta_s_scr):
    del zero_fill_ref
    b = pl.program_id(0)
    c = pl.program_id(1)

    @pl.when((c == 0) & (b == 0))
    def _():
        conv_scr[0:8, :] = jnp.zeros((8, SSD_CONV_DIM), F32)
        s_scr[...] = jnp.zeros(s_scr.shape, F32)

    @pl.when((c == 0) & (b > 0))
    def _():
        conv_scr[0:8, :] = meta_conv_scr[...]
        s_scr[...] = meta_s_scr[...]

    @pl.when((c > 0) | (b == 0))
    def _():
        _ssd_chunk(c, zx_ref, small_ref, cw_ref, cb_ref, dtb_ref, a_ref, dsk_ref, ng_ref,
                   o_ref, conv_scr, s_scr)

    @pl.when((c == 0) & (b == 0))
    def _():
        meta_conv_scr[...] = conv_scr[0:8, :]
        meta_s_scr[...] = s_scr[...]


def _ssd_chunk(c, zx_ref, small_ref, cw_ref, cb_ref, dtb_ref, a_ref, dsk_ref, ng_ref,
               o_ref, conv_scr, s_scr):
    Q = BLOCK
    row = lax.broadcasted_iota(jnp.int32, (Q, 1), 0)
    valid = jnp.logical_or(c > 0, row >= PAD)
    lane = lax.broadcasted_iota(jnp.int32, (1, LANES), 1)
    lane_lo = lane < SSD_HEAD_DIM
    sub = lax.broadcasted_iota(jnp.int32, (LANES, 1), 0)

    conv_scr[8:8 + Q, :] = jnp.where(valid, zx_ref[:, C_XBC:C_ZX], 0.0)
    acc = cb_ref[...]
    for k in range(SSD_CONV):
        off = 8 - (SSD_CONV - 1) + k
        acc = acc + cw_ref[k:k + 1, :] * conv_scr[off:off + Q, :]
    conv_scr[0:8, :] = conv_scr[Q:Q + 8, :]
    xbc = acc * _sigmoid(acc)
    bm = xbc[:, SSD_D:SSD_D + LANES]
    cm = xbc[:, SSD_D + LANES:SSD_D + 2 * LANES]

    dt = jnp.where(valid, _softplus(small_ref[...] + dtb_ref[...]), 0.0)
    a = dt * a_ref[...]
    tri = (lax.broadcasted_iota(jnp.int32, (Q, Q), 0)
           >= lax.broadcasted_iota(jnp.int32, (Q, Q), 1))
    tri_b = jnp.where(tri, 1.0, 0.0).astype(BF16)
    a1, a2, a3 = _split3(a)
    a_cum = _dot(tri_b, a1) + _dot(tri_b, a2) + _dot(tri_b, a3)
    a_cum_t = a_cum.T
    bm_t = bm.T

    cm_b = cm.astype(BF16)
    bm_b = bm.astype(BF16)
    cb_g = [_dot_nt(jnp.where(lane_lo, cm, 0.0).astype(BF16), bm_b),
            _dot_nt(jnp.where(lane_lo, 0.0, cm).astype(BF16), bm_b)]
    rows_g = [sub < SSD_STATE, sub >= SSD_STATE]

    pairs_per_group = SSD_HEADS // 2 // SSD_GROUPS
    y_pairs = []
    for p in range(SSD_HEADS // 2):
        g = p // pairs_per_group
        psl = slice(p * LANES, (p + 1) * LANES)
        xs_p = xbc[:, psl]
        dt_pair = jnp.where(lane_lo, dt[:, 2 * p:2 * p + 1], dt[:, 2 * p + 1:2 * p + 2])
        xdt = (xs_p * dt_pair).astype(BF16)
        s_old = s_scr[p]
        yd, upd, e_col, e_last = [], [], [], []
        for par in range(2):
            hd = 2 * p + par
            col = a_cum[:, hd:hd + 1]
            rowv = a_cum_t[hd:hd + 1, :]
            last = a_cum_t[hd:hd + 1, Q - 1:Q]
            seg = jnp.exp(jnp.where(tri, col - rowv, NEG_BIG))
            yd.append(_dot((cb_g[g] * seg).astype(BF16), xdt))
            upd.append(_dot((bm_t * jnp.exp(last - rowv)).astype(BF16), xdt))
            e_col.append(jnp.exp(col))
            e_last.append(jnp.exp(last))
        y_off = _dot(cm_b, s_old.astype(BF16)) * jnp.where(lane_lo, e_col[0], e_col[1])
        s_new = (jnp.where(lane_lo, e_last[0], e_last[1]) * s_old
                 + jnp.where(rows_g[g], jnp.where(lane_lo, upd[0], upd[1]), 0.0))
        s_scr[p] = s_new
        y_p = jnp.where(lane_lo, yd[0], yd[1]) + y_off + dsk_ref[:, psl] * xs_p
        z_p = zx_ref[:, psl]
        y_pairs.append(y_p * (z_p * _sigmoid(z_p)))

    for g in range(SSD_GROUPS):
        ps = range(g * pairs_per_group, (g + 1) * pairs_per_group)
        ss = sum(jnp.sum(y_pairs[p] * y_pairs[p], axis=-1, keepdims=True) for p in ps)
        inv = lax.rsqrt(ss * (1.0 / (pairs_per_group * LANES)) + EPS)
        for p in ps:
            psl = slice(p * LANES, (p + 1) * LANES)
            o_ref[:, psl] = (y_pairs[p] * inv * ng_ref[:, psl]).astype(BF16)


def _ssd(l, zx, small, params, bsz, seq):
    n_rows = zx.shape[0]
    blocks_per_seq = seq // BLOCK
    meta_block = n_rows // BLOCK - 1

    def block(b, c):
        first = jnp.where(b == 0, meta_block, b * blocks_per_seq)
        return (jnp.where(c == 0, first, b * blocks_per_seq + c - 1), 0)

    return pl.pallas_call(
        _ssd_kernel,
        grid=(bsz, blocks_per_seq + 1),
        in_specs=[pl.BlockSpec((BLOCK, C_ZX), block), pl.BlockSpec((BLOCK, LANES), block)]
                 + [_layer(p, l) for p in params] + [pl.BlockSpec(memory_space=pl.ANY)],
        out_specs=pl.BlockSpec((BLOCK, SSD_D), block),
        out_shape=jax.ShapeDtypeStruct((n_rows, SSD_D), BF16),
        input_output_aliases={2 + len(params): 0},
        scratch_shapes=[pltpu.VMEM((BLOCK + 8, SSD_CONV_DIM), F32),
                        pltpu.VMEM((SSD_HEADS // 2, LANES, LANES), F32),
                        pltpu.VMEM((8, SSD_CONV_DIM), F32),
                        pltpu.VMEM((SSD_HEADS // 2, LANES, LANES), F32)],
        compiler_params=pltpu.CompilerParams(
            dimension_semantics=("arbitrary", "arbitrary"), vmem_limit_bytes=VMEM_LIMIT),
        name="ssd_mixer",
    )(zx, small, *params, jnp.zeros((n_rows, SSD_D), BF16))


def _fox_keys_kernel(km_ref, k_ref, sm_ref, s_ref, fb_ref, om_ref, o_ref):
    T = BLOCK
    width = k_ref.shape[1]
    row = lax.broadcasted_iota(jnp.int32, (T, 1), 0)
    tri = (lax.broadcasted_iota(jnp.int32, (T, T), 0)
           >= lax.broadcasted_iota(jnp.int32, (T, T), 1))
    tri_b = jnp.where(tri, 1.0, 0.0).astype(BF16)
    src = lax.broadcasted_iota(jnp.int32, (LANES, width), 0)
    dst = lax.broadcasted_iota(jnp.int32, (LANES, width), 1)
    dst_head = jnp.right_shift(dst, HEAD_W.bit_length() - 1)
    dst_lane = jnp.bitwise_and(dst, HEAD_W - 1)
    sel = [jnp.where((src == SMALL_F + dst_head) & (dst_lane == FOX_BIAS_LANE + i),
                     1.0, 0.0).astype(BF16) for i in range(3)]

    def local_cumsum(small_blk, is_meta):
        log_f = -_softplus(-(small_blk + fb_ref[...]))
        if is_meta:
            log_f = jnp.where(row < PAD, 0.0, log_f)
        f1, f2, f3 = _split3(log_f)
        return _dot(tri_b, f1) + _dot(tri_b, f2) + _dot(tri_b, f3)

    def keys(c_blk, k_blk):
        b1, b2, b3 = _split3(c_blk * (-LOG2E))
        placed = _dot(b1, sel[0]) + _dot(b2, sel[1]) + _dot(b3, sel[2])
        return (k_blk.astype(F32) + placed).astype(BF16)

    n_blocks = k_ref.shape[0] // T
    rows = [slice(j * T, (j + 1) * T) for j in range(n_blocks)]
    c_meta = local_cumsum(sm_ref[...], True)
    local = [local_cumsum(s_ref[rows[j], :], False) for j in range(n_blocks)]
    om_ref[...] = keys(c_meta, km_ref[...])
    carry = c_meta[T - 1:T, :]
    for j in range(n_blocks):
        c_blk = local[j] + carry
        carry = c_blk[T - 1:T, :]
        o_ref[rows[j], :] = keys(c_blk, k_ref[rows[j], :])


def _fox_keys(l, k, small, fb, bsz, seq):
    n_rows, width = k.shape
    meta_block = n_rows // BLOCK - 1
    main = lambda w: pl.BlockSpec((seq, w), lambda b: (b, 0))
    meta = lambda w: pl.BlockSpec((BLOCK, w), lambda b: (meta_block, 0))
    return pl.pallas_call(
        _fox_keys_kernel,
        grid=(bsz,),
        in_specs=[meta(width), main(width), meta(LANES), main(LANES), _layer(fb, l)],
        out_specs=[pl.BlockSpec((BLOCK, width), lambda b: (0, 0)), main(width)],
        out_shape=[jax.ShapeDtypeStruct((BLOCK, width), BF16),
                   jax.ShapeDtypeStruct((bsz * seq, width), BF16)],
        compiler_params=pltpu.CompilerParams(
            dimension_semantics=("arbitrary",), vmem_limit_bytes=VMEM_LIMIT),
        name="fox_keys",
    )(k, k, small, small, fb)


def _attn_kernel(n_heads, q_ref, km_ref, k_ref, vtm_ref, vt_ref, zero_fill_ref, o_ref,
                 acc_scr, s0_scr, s1_scr):
    del zero_fill_ref
    t = pl.program_id(1)
    dv = acc_scr.shape[1]
    heads = range(n_heads)
    hsl = [slice(hd * HEAD_W, (hd + 1) * HEAD_W) for hd in heads]
    vsl = [slice(hd * dv, (hd + 1) * dv) for hd in heads]
    qs = [q_ref[:, hsl[hd]] for hd in heads]
    ahead = (lax.broadcasted_iota(jnp.int32, (TQ, TQ), 0)
             - lax.broadcasted_iota(jnp.int32, (TQ, TQ), 1))
    acc_scr[...] = jnp.zeros(acc_scr.shape, F32)

    def softmax_pv(ss, vts, ms, mask):
        if mask is not None:
            ss = [jnp.where(mask, s, NEG_BIG) for s in ss]
        new_ms = [jnp.maximum(ms[hd], jnp.max(ss[hd], axis=0, keepdims=True)) for hd in heads]
        corrs = [jnp.exp2(ms[hd] - new_ms[hd]) for hd in heads]
        prs = [jnp.exp2(ss[hd] - new_ms[hd]).astype(BF16) for hd in heads]
        for hd in heads:
            acc_scr[hd] = corrs[hd] * acc_scr[hd] + _dot(vts[hd], prs[hd])
        return tuple(new_ms)

    def meta_chunk(carry, mask):
        return softmax_pv([_dot_nt(km_ref[PAD:, hsl[hd]], qs[hd]) for hd in heads],
                          [vtm_ref[0, vsl[hd], PAD:] for hd in heads], carry, mask)

    def scores(c, slot_scr):
        r0 = pl.multiple_of(c * TQ, TQ)
        for hd in heads:
            slot_scr[hd] = _dot_nt(k_ref[pl.ds(r0, TQ), hsl[hd]], qs[hd])

    def consume(c, slot_scr, carry, mask):
        n_sub = TQ // BLOCK
        vts = [jnp.concatenate([vt_ref[n_sub * c + i, vsl[hd], :] for i in range(n_sub)], axis=1)
               for hd in heads]
        return softmax_pv([slot_scr[hd] for hd in heads], vts, carry, mask)

    def finish():
        dv_out = o_ref.shape[1] // n_heads
        ys = [acc_scr[hd, 0:dv_out, :] * (1.0 / acc_scr[hd, dv_out:dv_out + 1, :])
              for hd in heads]
        for p in range(n_heads // 2):
            y_t = jnp.concatenate([ys[2 * p], ys[2 * p + 1]], axis=0)
            o_ref[:, p * LANES:(p + 1) * LANES] = y_t.T.astype(BF16)

    init = tuple(jnp.full((1, TQ), M_INIT, F32) for _ in heads)

    @pl.when((t == 0) & (pl.program_id(0) == 0))
    def _():
        meta_chunk(init, ahead[:N_META, :] <= -(TQ - N_META))
        finish()

    @pl.when(t > 0)
    def _():
        scores(0, s0_scr)
        carry = meta_chunk(init, None)

        def pair(i, cr):
            c = 2 * i
            scores(c + 1, s1_scr)
            cr = consume(c, s0_scr, cr, None)
            scores(c + 2, s0_scr)
            return consume(c + 1, s1_scr, cr, None)

        n_unmasked = t - 1
        carry = lax.fori_loop(0, n_unmasked // 2, pair, carry)
        c = 2 * (n_unmasked // 2)
        diagonal = ahead <= 0

        def odd_tail(cr):
            scores(c + 1, s1_scr)
            cr = consume(c, s0_scr, cr, None)
            return consume(c + 1, s1_scr, cr, diagonal)

        lax.cond(n_unmasked % 2 == 1, odd_tail,
                 lambda cr: consume(c, s0_scr, cr, diagonal), carry)
        finish()


def _attention(q, k_meta, k_main, vt, n_heads, bsz, seq, meta_k_block, name):
    n_rows, qw = q.shape
    dv_aug = vt.shape[1] // n_heads
    dv = dv_aug - (DV_AUG - MLA_V)
    tiles_per_seq = seq // TQ
    meta_q_tile = n_rows // TQ - 1
    meta_block = n_rows // BLOCK - 1

    def q_tile(b, t):
        first = jnp.where(b == 0, meta_q_tile, b * tiles_per_seq)
        return (jnp.where(t == 0, first, b * tiles_per_seq + t - 1), 0)

    return pl.pallas_call(
        functools.partial(_attn_kernel, n_heads),
        grid=(bsz, tiles_per_seq + 1),
        in_specs=[pl.BlockSpec((TQ, qw), q_tile),
                  pl.BlockSpec((BLOCK, qw), lambda b, t: (meta_k_block, 0)),
                  pl.BlockSpec((seq, qw), lambda b, t: (b, 0)),
                  pl.BlockSpec((1,) + vt.shape[1:], lambda b, t: (meta_block, 0, 0)),
                  pl.BlockSpec((seq // BLOCK,) + vt.shape[1:], lambda b, t: (b, 0, 0)),
                  pl.BlockSpec(memory_space=pl.ANY)],
        out_specs=pl.BlockSpec((TQ, n_heads * dv), q_tile),
        out_shape=jax.ShapeDtypeStruct((n_rows, n_heads * dv), BF16),
        input_output_aliases={5: 0},
        scratch_shapes=[pltpu.VMEM((n_heads, dv_aug, TQ), F32),
                        pltpu.VMEM((n_heads, TQ, TQ), F32),
                        pltpu.VMEM((n_heads, TQ, TQ), F32)],
        compiler_params=pltpu.CompilerParams(
            dimension_semantics=("arbitrary", "arbitrary"), vmem_limit_bytes=VMEM_LIMIT),
        name=name,
    )(q, k_meta, k_main, vt, vt, jnp.zeros((n_rows, n_heads * dv), BF16))


def _rot_half_cols(w):
    half = MLA_ROPE // 2
    return jnp.concatenate([-w[..., half:], w[..., :half]], axis=-1)


def _in_proj_weights(w_in):
    sizes = [SSD_D, SSD_CONV_DIM, SSD_HEADS, FOX_D, FOX_D, FOX_D, FOX_HEADS,
             MLA_Q_LORA, MLA_KV_LORA, MLA_ROPE]
    splits = [int(s) for s in np.cumsum(sizes)[:-1]]
    z, xbc, dt, fq, fk, fv, fr, cq, ckv, kr = jnp.split(w_in, splits, axis=-1)
    zeros = lambda n: jnp.zeros(w_in.shape[:-1] + (n,), w_in.dtype)
    del z, xbc
    misc = jnp.concatenate([dt, fr, zeros(MISC_KRR - SMALL_F - FOX_HEADS), _rot_half_cols(kr),
                            kr, zeros(LANES - MISC_KR - MLA_ROPE)], axis=-1)
    win = jnp.concatenate([fq, fk, cq, ckv, misc], axis=-1)
    return win, jnp.swapaxes(fv, -1, -2)


def _mla_weights(w_uq, w_ukv):
    rope_pad = LANES - MLA_NOPE - MLA_ROPE
    lead = w_uq.shape[:-1]
    wq = w_uq.astype(BF16).reshape(lead + (MLA_HEADS, MLA_NOPE + MLA_ROPE))
    zq = jnp.zeros(lead + (MLA_HEADS, rope_pad), BF16)
    wuq = jnp.concatenate([wq, zq], axis=-1).reshape(lead + (MLA_HEADS * LANES,))
    wuqr = jnp.concatenate([jnp.zeros_like(wq[..., :MLA_NOPE]),
                            _rot_half_cols(wq[..., MLA_NOPE:]), zq],
                           axis=-1).reshape(lead + (MLA_HEADS * LANES,))
    lead = w_ukv.shape[:-1]
    wkv = w_ukv.astype(BF16).reshape(lead + (MLA_HEADS, MLA_NOPE + MLA_V))
    wkk = jnp.concatenate([wkv[..., :MLA_NOPE], jnp.zeros_like(wkv[..., MLA_NOPE:])],
                          axis=-1).reshape(lead + (MLA_HEADS * LANES,))
    wvt = jnp.swapaxes(wkv[..., MLA_NOPE:].reshape(lead + (MLA_D,)), -1, -2)
    return wuq, wuqr, wkk, wvt


def _rows(v, width=None, offset=0):
    v = v.astype(F32)
    if width is not None:
        v = jnp.pad(v, ((0, 0), (offset, width - offset - v.shape[-1])))
    return v[:, None, :]


def _position_tables(seq, tm):
    pos = jnp.concatenate([N_META + jnp.arange(seq, dtype=F32),
                           jnp.arange(tm, dtype=F32) - (tm - N_META)])
    inv_freq = 1.0 / (ROPE_THETA ** (jnp.arange(0, MLA_ROPE, 2, dtype=F32) / MLA_ROPE))
    ang = pos[:, None] * inv_freq[None, :]
    cos, sin = jnp.cos(ang), jnp.sin(ang)
    n = seq + tm
    rope_pad = LANES - MLA_NOPE - MLA_ROPE
    cos128 = jnp.concatenate([jnp.ones((n, MLA_NOPE), F32), cos, cos,
                              jnp.zeros((n, rope_pad), F32)], axis=-1)
    sin128 = jnp.concatenate([jnp.zeros((n, MLA_NOPE), F32), sin, sin,
                              jnp.zeros((n, rope_pad), F32)], axis=-1)
    cosk128 = jnp.concatenate([jnp.zeros((n, MLA_NOPE), F32), cos, cos,
                               jnp.zeros((n, rope_pad), F32)], axis=-1)
    return cos128, sin128, cosk128


def kernel(x, meta, ffn1_w_gate, ffn1_w_up, ffn1_w_down, ln1_g, ln1_b, w_in, conv_w, conv_b, dt_bias, a_log, d_skip, ssd_norm_g, fox_f_b, mla_q_norm_g, mla_w_uq, mla_kv_norm_g, mla_w_ukv, w_out, ln2_g, ln2_b, ffn2_w_gate, ffn2_w_up, ffn2_w_down, ln3_g, ln3_b):
    bsz, seq, _ = x.shape
    assert seq % TQ == 0
    tm = 2 * TQ if seq % (2 * TQ) == 0 else TQ
    n_main = bsz * seq
    n_rows = n_main + tm
    meta_block = n_rows // BLOCK - 1

    h = x.reshape(n_main, D_MODEL)
    meta_tile = jnp.concatenate([jnp.zeros((tm - N_META, D_MODEL), x.dtype),
                                 meta.astype(x.dtype)], axis=0)
    tables = _position_tables(seq, tm)
    bf = lambda w: w.astype(BF16)

    ffn1 = (bf(ffn1_w_gate), bf(ffn1_w_up), bf(ffn1_w_down), _rows(ln1_g), _rows(ln1_b))
    ffn2 = (bf(ffn2_w_gate), bf(ffn2_w_up), bf(ffn2_w_down), _rows(ln3_g), _rows(ln3_b))
    w_in_bf = bf(w_in)
    proj_w = (*_in_proj_weights(w_in_bf), *_mla_weights(mla_w_uq, mla_w_ukv),
              _rows(mla_q_norm_g), _rows(mla_kv_norm_g))
    ssd_p = (conv_w.astype(F32), _rows(conv_b), _rows(dt_bias, LANES),
             _rows(-jnp.exp(a_log.astype(F32)), LANES),
             _rows(jnp.repeat(d_skip, SSD_HEAD_DIM, axis=-1)), _rows(ssd_norm_g))
    fb = _rows(fox_f_b, LANES, SMALL_F)
    out_p = (bf(w_out), _rows(ln2_g), _rows(ln2_b))

    for l in range(DEPTH):
        h, zx, small, fq, fk, fvt, mq, mk, mvt = _ffn_ln(
            l, h, ffn1, tm, n_rows, meta_tile=meta_tile if l == 0 else None,
            proj=(w_in_bf, proj_w, tables, seq))
        y_ssd = _ssd(l, zx, small, ssd_p, bsz, seq)
        fk_meta, fk_main = _fox_keys(l, fk, small, fb, bsz, seq)
        y_fox = _attention(fq, fk_meta, fk_main, fvt, FOX_HEADS, bsz, seq, 0, "fox_attention")
        y_mla = _attention(mq, mk, mk, mvt, MLA_HEADS, bsz, seq, meta_block, "mla_attention")
        h = _ffn_ln(l, h, ffn2, tm, n_rows if l < DEPTH - 1 else n_main,
                    mix=(y_ssd, y_fox, y_mla, *out_p))

    return h.reshape(bsz, seq, D_MODEL)
```

```python
import functools

import numpy as np
import jax
import jax.numpy as jnp
from jax import lax
from jax.experimental import pallas as pl
from jax.experimental.pallas import tpu as pltpu

F32 = jnp.float32
BF16 = jnp.bfloat16

D_MODEL = 1024
DEPTH = 2
N_META = 16
BLOCK = 128
SSD_HEADS = 8
SSD_HEAD_DIM = 64
SSD_D = SSD_HEADS * SSD_HEAD_DIM
SSD_GROUPS = 2
SSD_STATE = 64
SSD_CONV = 4
SSD_CONV_DIM = SSD_D + 2 * SSD_GROUPS * SSD_STATE
FOX_HEADS = 4
FOX_HEAD_DIM = 64
FOX_D = FOX_HEADS * FOX_HEAD_DIM
MLA_HEADS = 4
MLA_Q_LORA = 256
MLA_KV_LORA = 128
MLA_NOPE = 64
MLA_ROPE = 32
MLA_V = 64
MLA_D = MLA_HEADS * MLA_V
ROPE_THETA = 10000.0
D_MIX = SSD_D + FOX_D + MLA_D
D_FF = 2816
ALPHA = (2 * DEPTH) ** 0.25
EPS = 1e-5

LANES = 128
MXU_W = 256
FF_CHUNK = MXU_W
N_FF_CHUNKS = D_FF // FF_CHUNK
N_OUT_CHUNKS = D_MODEL // FF_CHUNK
TQ = 2 * BLOCK
SSD_STEP_ROWS = 2 * BLOCK
PAD = BLOCK - N_META
NEG_BIG = -1e30
M_INIT = 2 * NEG_BIG
VMEM_LIMIT = 56 * 1024 * 1024

C_XBC = SSD_D
C_ZX = SSD_D + SSD_CONV_DIM
C_FQ = 0
C_CQ = C_FQ + 2 * FOX_D
C_MISC = C_CQ + MLA_Q_LORA
N_IN_ARR = C_MISC + MLA_KV_LORA + LANES
MISC_KRR = 32
MISC_KR = MLA_NOPE
LOG2E = 1.4426950408889634
DV_AUG = MLA_V + 16
SMALL_DT = 0
SMALL_F = 8
HEAD_W = LANES
FOX_BIAS_LANE = FOX_HEAD_DIM


def _sigmoid(x):
    return 1.0 / (1.0 + jnp.exp(-x))


def _softplus(x):
    return jnp.maximum(x, 0.0) + jnp.log(1.0 + jnp.exp(-jnp.abs(x)))


def _layer_norm_rows(y, g, b):
    mu = jnp.mean(y, axis=-1, keepdims=True)
    yc = y - mu
    var = jnp.mean(yc * yc, axis=-1, keepdims=True)
    return yc * lax.rsqrt(var + EPS) * g + b


def _split3(x):
    x1 = x.astype(BF16)
    r1 = x - x1.astype(F32)
    x2 = r1.astype(BF16)
    r2 = r1 - x2.astype(F32)
    return x1, x2, r2.astype(BF16)


def _dot(a, b):
    return jnp.dot(a, b, preferred_element_type=F32)


def _dot_nt(a, b):
    return lax.dot_general(a, b, (((1,), (1,)), ((), ())), preferred_element_type=F32)


def _resident(shape):
    return pl.BlockSpec(shape, lambda *_: (0,) * len(shape), pipeline_mode=pl.Buffered(1))


def _layer(arr, l):
    tail = (0,) * (arr.ndim - 1)
    return pl.BlockSpec((None,) + arr.shape[1:], lambda *_: (l,) + tail,
                        pipeline_mode=pl.Buffered(1))


N_PROJ_IN = 12
N_PROJ_OUT = 8


def _ffn_ln_kernel(prologue, n_main_tiles, with_proj, *refs):
    a_scr, y_scr = refs[-2:]
    refs = refs[:-2]
    if with_proj:
        proj_out, refs = refs[-N_PROJ_OUT:], refs[:-N_PROJ_OUT]
        o_ref = refs[-1]
        proj_in, refs = refs[-1 - N_PROJ_IN:-1], refs[:-1 - N_PROJ_IN]
    else:
        o_ref, refs = refs[-1], refs[:-1]
    wg_ref, wu_ref, wd_ref, g_ref, b_ref = refs[-5:]
    if prologue == "plain":
        x = refs[0][...]
    elif prologue == "meta":
        x_ref, m_ref = refs[:2]
        x = jnp.where(pl.program_id(0) < n_main_tiles, x_ref[...], m_ref[...])
    else:
        h_ref, ys_ref, yf_ref, ym_ref, wo_ref, g2_ref, b2_ref = refs[:7]
        mix = (_dot(ys_ref[...], wo_ref[0:SSD_D, :])
               + _dot(yf_ref[...], wo_ref[SSD_D:SSD_D + FOX_D, :])
               + _dot(ym_ref[...], wo_ref[SSD_D + FOX_D:D_MIX, :]))
        x = _layer_norm_rows(ALPHA * h_ref[...] + mix, g2_ref[...], b2_ref[...])
    xb = x.astype(BF16)

    for c in range(N_FF_CHUNKS):
        cols = slice(c * FF_CHUNK, (c + 1) * FF_CHUNK)
        gate = _dot(xb, wg_ref[:, cols])
        up = _dot(xb, wu_ref[:, cols])
        a_scr[c] = (gate * _sigmoid(gate) * up).astype(BF16)

    for n in range(N_OUT_CHUNKS):
        cols = slice(n * FF_CHUNK, (n + 1) * FF_CHUNK)
        acc = _dot(a_scr[0], wd_ref[0:FF_CHUNK, cols])
        for c in range(1, N_FF_CHUNKS):
            acc = acc + _dot(a_scr[c], wd_ref[c * FF_CHUNK:(c + 1) * FF_CHUNK, cols])
        y_scr[:, cols] = ALPHA * x[:, cols] + 0.5 * acc
    out = _layer_norm_rows(y_scr[...], g_ref[...], b_ref[...])
    o_ref[...] = out
    if with_proj:
        _proj_body(out, *proj_in, *proj_out)


def _ffn_ln(l, h, ffn, tm, n_out_rows, meta_tile=None, mix=None, proj=None):
    def rows(width):
        return pl.BlockSpec((tm, width), lambda i: (i, 0))

    out_specs = rows(D_MODEL)
    out_shape = jax.ShapeDtypeStruct((n_out_rows, D_MODEL), F32)
    p_args, p_specs = [], []
    if proj is not None:
        p_args, p_specs, p_out_specs, p_out_shape = _proj_specs(l, n_out_rows, tm, *proj)
        out_specs, out_shape = [out_specs] + p_out_specs, [out_shape] + p_out_shape

    if mix is not None:
        prologue, n_main = "mix", None
        x_args = [h, *mix]
        x_specs = ([rows(D_MODEL), rows(SSD_D), rows(FOX_D), rows(MLA_D)]
                   + [_layer(p, l) for p in mix[3:]])
    elif meta_tile is not None:
        prologue, n_main = "meta", h.shape[0] // tm
        x_args = [h, meta_tile]
        x_specs = [pl.BlockSpec((tm, D_MODEL), lambda i: (jnp.minimum(i, n_main - 1), 0)),
                   _resident(meta_tile.shape)]
    else:
        prologue, n_main, x_args, x_specs = "plain", None, [h], [rows(D_MODEL)]
    return pl.pallas_call(
        functools.partial(_ffn_ln_kernel, prologue, n_main, proj is not None),
        grid=(n_out_rows // tm,),
        in_specs=x_specs + [_layer(p, l) for p in ffn] + p_specs,
        out_specs=out_specs,
        out_shape=out_shape,
        scratch_shapes=[pltpu.VMEM((N_FF_CHUNKS, tm, FF_CHUNK), BF16),
                        pltpu.VMEM((tm, D_MODEL), F32)],
        compiler_params=pltpu.CompilerParams(
            dimension_semantics=("arbitrary",), vmem_limit_bytes=VMEM_LIMIT),
        name="ffn_ln_proj" if proj is not None else "ffn_ln",
    )(*x_args, *ffn, *p_args)


def _rms_rows(x, g):
    return x * lax.rsqrt(jnp.mean(x * x, axis=-1, keepdims=True) + EPS) * g


def _store_vt_blocks(out_ref, val_t, n_heads):
    dv = val_t.shape[0] // n_heads
    ones = jnp.ones((DV_AUG - dv, BLOCK), out_ref.dtype)
    for r in range(out_ref.shape[0]):
        for hd in range(n_heads):
            out_ref[r, hd * DV_AUG:hd * DV_AUG + dv, :] = (
                val_t[hd * dv:(hd + 1) * dv, r * BLOCK:(r + 1) * BLOCK].astype(out_ref.dtype))
            out_ref[r, hd * DV_AUG + dv:(hd + 1) * DV_AUG, :] = ones


def _proj_body(h, wzx_ref, win_ref, wfvt_ref, wuq_ref, wuqr_ref, wkk_ref, wkvt_ref,
               qg_ref, kvg_ref, cos_ref, sin_ref, cosk_ref,
               zx_ref, small_ref, fq_ref, fk_ref, fvt_ref, mq_ref, mk_ref, mvt_ref):
    hb = h.astype(BF16)
    zx_ref[...] = _dot(hb, wzx_ref[...])
    misc = _dot(hb, win_ref[:, C_MISC:N_IN_ARR])
    small = misc[:, MLA_KV_LORA:]
    small_ref[...] = small

    lane = lax.broadcasted_iota(jnp.int32, (1, LANES), 1)
    lane_lo = lane < FOX_HEAD_DIM
    fox_one = jnp.where((lane >= FOX_BIAS_LANE) & (lane < FOX_BIAS_LANE + 3), 1.0, 0.0)

    fqk = _dot(hb, win_ref[:, C_FQ:C_CQ])
    for half, (ref, scale, extra) in enumerate([(fq_ref, FOX_HEAD_DIM ** -0.5 * LOG2E, fox_one),
                                                (fk_ref, None, None)]):
        for pair in range(FOX_HEADS // 2):
            c0 = half * FOX_D + pair * LANES
            both = fqk[:, c0:c0 + LANES]
            if scale is not None:
                both = both * scale
            for par, val in enumerate([both, pltpu.roll(both, FOX_HEAD_DIM, 1)]):
                val = jnp.where(lane_lo, val, 0.0)
                if extra is not None:
                    val = val + extra
                hd = 2 * pair + par
                ref[:, hd * HEAD_W:(hd + 1) * HEAD_W] = val.astype(BF16)
    _store_vt_blocks(fvt_ref, _dot_nt(wfvt_ref[...], hb), FOX_HEADS)

    cos = cos_ref[...]
    sin = sin_ref[...]
    cqn = _rms_rows(_dot(hb, win_ref[:, C_CQ:C_MISC]), qg_ref[...]).astype(BF16)
    q = _dot(cqn, wuq_ref[...])
    qr = _dot(cqn, wuqr_ref[...])
    kvn = _rms_rows(misc[:, :MLA_KV_LORA], kvg_ref[...]).astype(BF16)
    kn = _dot(kvn, wkk_ref[...])
    _store_vt_blocks(mvt_ref, _dot_nt(wkvt_ref[...], kvn), MLA_HEADS)
    krope = small * cosk_ref[...] + pltpu.roll(small, MISC_KR - MISC_KRR, 1) * sin
    scale = (MLA_NOPE + MLA_ROPE) ** -0.5 * LOG2E
    for hd in range(MLA_HEADS):
        sl = slice(hd * HEAD_W, (hd + 1) * HEAD_W)
        mq_ref[:, sl] = ((q[:, sl] * cos + qr[:, sl] * sin) * scale).astype(BF16)
        mk_ref[:, sl] = (kn[:, sl] + krope).astype(BF16)


def _proj_specs(l, n_rows, tm, w_in_bf, weights, tables, seq):
    n_main_tiles = n_rows // tm - 1
    tiles_per_seq = seq // tm

    def rows(width):
        return pl.BlockSpec((tm, width), lambda i: (i, 0))

    tab = pl.BlockSpec((tm, LANES), lambda i: (
        jnp.where(i < n_main_tiles, i % tiles_per_seq, tiles_per_seq), 0))
    wzx = pl.BlockSpec((None, D_MODEL, C_ZX), lambda i: (l, 0, 0), pipeline_mode=pl.Buffered(1))
    vt_rows = FOX_HEADS * DV_AUG
    t_spec = pl.BlockSpec((tm // BLOCK, vt_rows, BLOCK), lambda i: (i, 0, 0))
    t_shape = jax.ShapeDtypeStruct((n_rows // BLOCK, vt_rows, BLOCK), BF16)
    qk_w = FOX_HEADS * HEAD_W
    row_outs = [(C_ZX, F32), (LANES, F32), (qk_w, BF16), (qk_w, BF16)]
    args = [w_in_bf, *weights, *tables]
    in_specs = [wzx] + [_layer(w, l) for w in weights] + [tab, tab, tab]
    out_specs = [rows(w) for w, _ in row_outs] + [t_spec, rows(qk_w), rows(qk_w), t_spec]
    out_shape = ([jax.ShapeDtypeStruct((n_rows, w), dt) for w, dt in row_outs]
                 + [t_shape, jax.ShapeDtypeStruct((n_rows, qk_w), BF16),
                    jax.ShapeDtypeStruct((n_rows, qk_w), BF16), t_shape])
    assert len(args) == N_PROJ_IN and len(out_shape) == N_PROJ_OUT
    return args, in_specs, out_specs, out_shape


def _first_step_block(b, meta_block, n_fill, own_first_block):
    return jnp.where(b == 0, meta_block,
                     jnp.where(b <= n_fill, meta_block - b, own_first_block))


def _n_fill_blocks(n_rows, bsz, seq, block_rows):
    n_fill = (n_rows - bsz * seq) // block_rows - 1
    assert bsz > n_fill, "needs one batch row per meta-tile block to zero-fill"
    return n_fill


def _ssd_kernel(n_fill, zx_ref, small_ref, cw_ref, cb_ref, dtb_ref, a_ref, dsk_ref, ng_ref,
                o_ref, conv_scr, s_scr, meta_conv_scr, meta_s_scr):
    b = pl.program_id(0)
    c = pl.program_id(1)

    @pl.when((c == 0) & (b >= 1) & (b <= n_fill))
    def _():
        o_ref[...] = jnp.zeros(o_ref.shape, o_ref.dtype)

    @pl.when((c == 0) & (b == 0))
    def _():
        conv_scr[0:8, :] = jnp.zeros((8, SSD_CONV_DIM), F32)
        s_scr[...] = jnp.zeros(s_scr.shape, F32)

    @pl.when((c == 0) & (b > 0))
    def _():
        conv_scr[0:8, :] = meta_conv_scr[...]
        s_scr[...] = meta_s_scr[...]

    @pl.when((c > 0) | (b == 0))
    def _():
        _ssd_step(c, zx_ref, small_ref, cw_ref, cb_ref, dtb_ref, a_ref, dsk_ref, ng_ref,
                  o_ref, conv_scr, s_scr)

    @pl.when((c == 0) & (b == 0))
    def _():
        meta_conv_scr[...] = conv_scr[0:8, :]
        meta_s_scr[...] = s_scr[...]


def _ssd_step(c, zx_ref, small_ref, cw_ref, cb_ref, dtb_ref, a_ref, dsk_ref, ng_ref,
              o_ref, conv_scr, s_scr):
    R = SSD_STEP_ROWS
    row = lax.broadcasted_iota(jnp.int32, (R, 1), 0)
    valid = jnp.logical_or(c > 0, row >= R - N_META)

    conv_scr[8:8 + R, :] = jnp.where(valid, zx_ref[:, C_XBC:C_ZX], 0.0)
    acc = cb_ref[...]
    for k in range(SSD_CONV):
        off = 8 - (SSD_CONV - 1) + k
        acc = acc + cw_ref[k:k + 1, :] * conv_scr[off:off + R, :]
    conv_scr[0:8, :] = conv_scr[R:R + 8, :]
    xbc = acc * _sigmoid(acc)
    dt = jnp.where(valid, _softplus(small_ref[...] + dtb_ref[...]), 0.0)
    for ch in range(R // BLOCK):
        rows = slice(ch * BLOCK, (ch + 1) * BLOCK)
        _ssd_chunk(xbc[rows], dt[rows], zx_ref.at[rows], a_ref, dsk_ref, ng_ref,
                   o_ref.at[rows], s_scr)


def _ssd_chunk(xbc, dt, zx_ref, a_ref, dsk_ref, ng_ref, o_ref, s_scr):
    Q = BLOCK
    lane = lax.broadcasted_iota(jnp.int32, (1, LANES), 1)
    lane_lo = lane < SSD_HEAD_DIM
    sub = lax.broadcasted_iota(jnp.int32, (LANES, 1), 0)
    bm = xbc[:, SSD_D:SSD_D + LANES]
    cm = xbc[:, SSD_D + LANES:SSD_D + 2 * LANES]

    a = dt * a_ref[...]
    tri = (lax.broadcasted_iota(jnp.int32, (Q, Q), 0)
           >= lax.broadcasted_iota(jnp.int32, (Q, Q), 1))
    tri_b = jnp.where(tri, 1.0, 0.0).astype(BF16)
    a1, a2, a3 = _split3(a)
    a_cum = _dot(tri_b, a1) + _dot(tri_b, a2) + _dot(tri_b, a3)
    a_cum_t = a_cum.T
    bm_t = bm.T

    cm_b = cm.astype(BF16)
    bm_b = bm.astype(BF16)
    cb_g = [_dot_nt(jnp.where(lane_lo, cm, 0.0).astype(BF16), bm_b),
            _dot_nt(jnp.where(lane_lo, 0.0, cm).astype(BF16), bm_b)]
    rows_g = [sub < SSD_STATE, sub >= SSD_STATE]

    pairs_per_group = SSD_HEADS // 2 // SSD_GROUPS
    y_pairs = []
    for p in range(SSD_HEADS // 2):
        g = p // pairs_per_group
        psl = slice(p * LANES, (p + 1) * LANES)
        xs_p = xbc[:, psl]
        dt_pair = jnp.where(lane_lo, dt[:, 2 * p:2 * p + 1], dt[:, 2 * p + 1:2 * p + 2])
        xdt = (xs_p * dt_pair).astype(BF16)
        s_old = s_scr[p]
        yd, upd, e_col, e_last = [], [], [], []
        for par in range(2):
            hd = 2 * p + par
            col = a_cum[:, hd:hd + 1]
            rowv = a_cum_t[hd:hd + 1, :]
            last = a_cum_t[hd:hd + 1, Q - 1:Q]
            seg = jnp.exp(jnp.where(tri, col - rowv, NEG_BIG))
            yd.append(_dot((cb_g[g] * seg).astype(BF16), xdt))
            upd.append(_dot((bm_t * jnp.exp(last - rowv)).astype(BF16), xdt))
            e_col.append(jnp.exp(col))
            e_last.append(jnp.exp(last))
        y_off = _dot(cm_b, s_old.astype(BF16)) * jnp.where(lane_lo, e_col[0], e_col[1])
        s_new = (jnp.where(lane_lo, e_last[0], e_last[1]) * s_old
                 + jnp.where(rows_g[g], jnp.where(lane_lo, upd[0], upd[1]), 0.0))
        s_scr[p] = s_new
        y_p = jnp.where(lane_lo, yd[0], yd[1]) + y_off + dsk_ref[:, psl] * xs_p
        z_p = zx_ref[:, psl]
        y_pairs.append(y_p * (z_p * _sigmoid(z_p)))

    for g in range(SSD_GROUPS):
        ps = range(g * pairs_per_group, (g + 1) * pairs_per_group)
        ss = sum(jnp.sum(y_pairs[p] * y_pairs[p], axis=-1, keepdims=True) for p in ps)
        inv = lax.rsqrt(ss * (1.0 / (pairs_per_group * LANES)) + EPS)
        for p in ps:
            psl = slice(p * LANES, (p + 1) * LANES)
            o_ref[:, psl] = (y_pairs[p] * inv * ng_ref[:, psl]).astype(BF16)


def _ssd(l, zx, small, params, bsz, seq):
    n_rows = zx.shape[0]
    R = SSD_STEP_ROWS
    steps_per_seq = seq // R
    meta_block = n_rows // R - 1
    n_fill = _n_fill_blocks(n_rows, bsz, seq, R)

    def block(b, c):
        first = _first_step_block(b, meta_block, n_fill, b * steps_per_seq)
        return (jnp.where(c == 0, first, b * steps_per_seq + c - 1), 0)

    return pl.pallas_call(
        functools.partial(_ssd_kernel, n_fill),
        grid=(bsz, steps_per_seq + 1),
        in_specs=[pl.BlockSpec((R, C_ZX), block), pl.BlockSpec((R, LANES), block)]
                 + [_layer(p, l) for p in params],
        out_specs=pl.BlockSpec((R, SSD_D), block),
        out_shape=jax.ShapeDtypeStruct((n_rows, SSD_D), BF16),
        scratch_shapes=[pltpu.VMEM((R + 8, SSD_CONV_DIM), F32),
                        pltpu.VMEM((SSD_HEADS // 2, LANES, LANES), F32),
                        pltpu.VMEM((8, SSD_CONV_DIM), F32),
                        pltpu.VMEM((SSD_HEADS // 2, LANES, LANES), F32)],
        compiler_params=pltpu.CompilerParams(
            dimension_semantics=("arbitrary", "arbitrary"), vmem_limit_bytes=VMEM_LIMIT),
        name="ssd_mixer",
    )(zx, small, *params)


def _fox_keys_kernel(km_ref, k_ref, sm_ref, s_ref, fb_ref, om_ref, o_ref):
    T = BLOCK
    width = k_ref.shape[1]
    row = lax.broadcasted_iota(jnp.int32, (T, 1), 0)
    tri = (lax.broadcasted_iota(jnp.int32, (T, T), 0)
           >= lax.broadcasted_iota(jnp.int32, (T, T), 1))
    tri_b = jnp.where(tri, 1.0, 0.0).astype(BF16)
    src = lax.broadcasted_iota(jnp.int32, (LANES, width), 0)
    dst = lax.broadcasted_iota(jnp.int32, (LANES, width), 1)
    dst_head = jnp.right_shift(dst, HEAD_W.bit_length() - 1)
    dst_lane = jnp.bitwise_and(dst, HEAD_W - 1)
    sel = [jnp.where((src == SMALL_F + dst_head) & (dst_lane == FOX_BIAS_LANE + i),
                     1.0, 0.0).astype(BF16) for i in range(3)]

    def local_cumsum(small_blk, is_meta):
        log_f = -_softplus(-(small_blk + fb_ref[...]))
        if is_meta:
            log_f = jnp.where(row < PAD, 0.0, log_f)
        f1, f2, f3 = _split3(log_f)
        return _dot(tri_b, f1) + _dot(tri_b, f2) + _dot(tri_b, f3)

    def keys(c_blk, k_blk):
        b1, b2, b3 = _split3(c_blk * (-LOG2E))
        placed = _dot(b1, sel[0]) + _dot(b2, sel[1]) + _dot(b3, sel[2])
        return (k_blk.astype(F32) + placed).astype(BF16)

    n_blocks = k_ref.shape[0] // T
    rows = [slice(j * T, (j + 1) * T) for j in range(n_blocks)]
    c_meta = local_cumsum(sm_ref[...], True)
    local = [local_cumsum(s_ref[rows[j], :], False) for j in range(n_blocks)]
    om_ref[...] = keys(c_meta, km_ref[...])
    carry = c_meta[T - 1:T, :]
    for j in range(n_blocks):
        c_blk = local[j] + carry
        carry = c_blk[T - 1:T, :]
        o_ref[rows[j], :] = keys(c_blk, k_ref[rows[j], :])


def _fox_keys(l, k, small, fb, bsz, seq):
    n_rows, width = k.shape
    meta_block = n_rows // BLOCK - 1
    main = lambda w: pl.BlockSpec((seq, w), lambda b: (b, 0))
    meta = lambda w: pl.BlockSpec((BLOCK, w), lambda b: (meta_block, 0))
    return pl.pallas_call(
        _fox_keys_kernel,
        grid=(bsz,),
        in_specs=[meta(width), main(width), meta(LANES), main(LANES), _layer(fb, l)],
        out_specs=[pl.BlockSpec((BLOCK, width), lambda b: (0, 0)), main(width)],
        out_shape=[jax.ShapeDtypeStruct((BLOCK, width), BF16),
                   jax.ShapeDtypeStruct((bsz * seq, width), BF16)],
        compiler_params=pltpu.CompilerParams(
            dimension_semantics=("arbitrary",), vmem_limit_bytes=VMEM_LIMIT),
        name="fox_keys",
    )(k, k, small, small, fb)


def _attn_kernel(n_heads, n_fill, q_ref, km_ref, k_ref, vtm_ref, vt_ref, o_ref,
                 acc_scr, s0_scr, s1_scr):
    t = pl.program_id(1)
    b = pl.program_id(0)
    dv = acc_scr.shape[1]

    @pl.when((t == 0) & (b >= 1) & (b <= n_fill))
    def _():
        o_ref[...] = jnp.zeros(o_ref.shape, o_ref.dtype)

    heads = range(n_heads)
    hsl = [slice(hd * HEAD_W, (hd + 1) * HEAD_W) for hd in heads]
    vsl = [slice(hd * dv, (hd + 1) * dv) for hd in heads]
    qs = [q_ref[:, hsl[hd]] for hd in heads]
    ahead = (lax.broadcasted_iota(jnp.int32, (TQ, TQ), 0)
             - lax.broadcasted_iota(jnp.int32, (TQ, TQ), 1))
    acc_scr[...] = jnp.zeros(acc_scr.shape, F32)

    def softmax_pv(ss, vts, ms, mask):
        if mask is not None:
            ss = [jnp.where(mask, s, NEG_BIG) for s in ss]
        new_ms = [jnp.maximum(ms[hd], jnp.max(ss[hd], axis=0, keepdims=True)) for hd in heads]
        corrs = [jnp.exp2(ms[hd] - new_ms[hd]) for hd in heads]
        prs = [jnp.exp2(ss[hd] - new_ms[hd]).astype(BF16) for hd in heads]
        for hd in heads:
            acc_scr[hd] = corrs[hd] * acc_scr[hd] + _dot(vts[hd], prs[hd])
        return tuple(new_ms)

    def meta_chunk(carry, mask):
        return softmax_pv([_dot_nt(km_ref[PAD:, hsl[hd]], qs[hd]) for hd in heads],
                          [vtm_ref[0, vsl[hd], PAD:] for hd in heads], carry, mask)

    def scores(c, slot_scr):
        r0 = pl.multiple_of(c * TQ, TQ)
        for hd in heads:
            slot_scr[hd] = _dot_nt(k_ref[pl.ds(r0, TQ), hsl[hd]], qs[hd])

    def consume(c, slot_scr, carry, mask):
        n_sub = TQ // BLOCK
        vts = [jnp.concatenate([vt_ref[n_sub * c + i, vsl[hd], :] for i in range(n_sub)], axis=1)
               for hd in heads]
        return softmax_pv([slot_scr[hd] for hd in heads], vts, carry, mask)

    def finish():
        dv_out = o_ref.shape[1] // n_heads
        ys = [acc_scr[hd, 0:dv_out, :] * (1.0 / acc_scr[hd, dv_out:dv_out + 1, :])
              for hd in heads]
        for p in range(n_heads // 2):
            y_t = jnp.concatenate([ys[2 * p], ys[2 * p + 1]], axis=0)
            o_ref[:, p * LANES:(p + 1) * LANES] = y_t.T.astype(BF16)

    init = tuple(jnp.full((1, TQ), M_INIT, F32) for _ in heads)

    @pl.when((t == 0) & (b == 0))
    def _():
        meta_chunk(init, ahead[:N_META, :] <= -(TQ - N_META))
        finish()

    @pl.when(t > 0)
    def _():
        scores(0, s0_scr)
        carry = meta_chunk(init, None)

        def pair(i, cr):
            c = 2 * i
            scores(c + 1, s1_scr)
            cr = consume(c, s0_scr, cr, None)
            scores(c + 2, s0_scr)
            return consume(c + 1, s1_scr, cr, None)

        n_unmasked = t - 1
        carry = lax.fori_loop(0, n_unmasked // 2, pair, carry)
        c = 2 * (n_unmasked // 2)
        diagonal = ahead <= 0

        def odd_tail(cr):
            scores(c + 1, s1_scr)
            cr = consume(c, s0_scr, cr, None)
            return consume(c + 1, s1_scr, cr, diagonal)

        lax.cond(n_unmasked % 2 == 1, odd_tail,
                 lambda cr: consume(c, s0_scr, cr, diagonal), carry)
        finish()


def _attention(q, k_meta, k_main, vt, n_heads, bsz, seq, meta_k_block, name):
    n_rows, qw = q.shape
    dv_aug = vt.shape[1] // n_heads
    dv = dv_aug - (DV_AUG - MLA_V)
    tiles_per_seq = seq // TQ
    meta_q_tile = n_rows // TQ - 1
    meta_block = n_rows // BLOCK - 1
    n_fill = _n_fill_blocks(n_rows, bsz, seq, TQ)

    def q_tile(b, t):
        first = _first_step_block(b, meta_q_tile, n_fill, b * tiles_per_seq)
        return (jnp.where(t == 0, first, b * tiles_per_seq + t - 1), 0)

    return pl.pallas_call(
        functools.partial(_attn_kernel, n_heads, n_fill),
        grid=(bsz, tiles_per_seq + 1),
        in_specs=[pl.BlockSpec((TQ, qw), q_tile),
                  pl.BlockSpec((BLOCK, qw), lambda b, t: (meta_k_block, 0)),
                  pl.BlockSpec((seq, qw), lambda b, t: (b, 0)),
                  pl.BlockSpec((1,) + vt.shape[1:], lambda b, t: (meta_block, 0, 0)),
                  pl.BlockSpec((seq // BLOCK,) + vt.shape[1:], lambda b, t: (b, 0, 0))],
        out_specs=pl.BlockSpec((TQ, n_heads * dv), q_tile),
        out_shape=jax.ShapeDtypeStruct((n_rows, n_heads * dv), BF16),
        scratch_shapes=[pltpu.VMEM((n_heads, dv_aug, TQ), F32),
                        pltpu.VMEM((n_heads, TQ, TQ), F32),
                        pltpu.VMEM((n_heads, TQ, TQ), F32)],
        compiler_params=pltpu.CompilerParams(
            dimension_semantics=("arbitrary", "arbitrary"), vmem_limit_bytes=VMEM_LIMIT),
        name=name,
    )(q, k_meta, k_main, vt, vt)


def _rot_half_cols(w):
    half = MLA_ROPE // 2
    return jnp.concatenate([-w[..., half:], w[..., :half]], axis=-1)


def _in_proj_weights(w_in):
    sizes = [SSD_D, SSD_CONV_DIM, SSD_HEADS, FOX_D, FOX_D, FOX_D, FOX_HEADS,
             MLA_Q_LORA, MLA_KV_LORA, MLA_ROPE]
    splits = [int(s) for s in np.cumsum(sizes)[:-1]]
    z, xbc, dt, fq, fk, fv, fr, cq, ckv, kr = jnp.split(w_in, splits, axis=-1)
    zeros = lambda n: jnp.zeros(w_in.shape[:-1] + (n,), w_in.dtype)
    del z, xbc
    misc = jnp.concatenate([dt, fr, zeros(MISC_KRR - SMALL_F - FOX_HEADS), _rot_half_cols(kr),
                            kr, zeros(LANES - MISC_KR - MLA_ROPE)], axis=-1)
    win = jnp.concatenate([fq, fk, cq, ckv, misc], axis=-1)
    return win, jnp.swapaxes(fv, -1, -2)


def _mla_weights(w_uq, w_ukv):
    rope_pad = LANES - MLA_NOPE - MLA_ROPE
    lead = w_uq.shape[:-1]
    wq = w_uq.astype(BF16).reshape(lead + (MLA_HEADS, MLA_NOPE + MLA_ROPE))
    zq = jnp.zeros(lead + (MLA_HEADS, rope_pad), BF16)
    wuq = jnp.concatenate([wq, zq], axis=-1).reshape(lead + (MLA_HEADS * LANES,))
    wuqr = jnp.concatenate([jnp.zeros_like(wq[..., :MLA_NOPE]),
                            _rot_half_cols(wq[..., MLA_NOPE:]), zq],
                           axis=-1).reshape(lead + (MLA_HEADS * LANES,))
    lead = w_ukv.shape[:-1]
    wkv = w_ukv.astype(BF16).reshape(lead + (MLA_HEADS, MLA_NOPE + MLA_V))
    wkk = jnp.concatenate([wkv[..., :MLA_NOPE], jnp.zeros_like(wkv[..., MLA_NOPE:])],
                          axis=-1).reshape(lead + (MLA_HEADS * LANES,))
    wvt = jnp.swapaxes(wkv[..., MLA_NOPE:].reshape(lead + (MLA_D,)), -1, -2)
    return wuq, wuqr, wkk, wvt


def _rows(v, width=None, offset=0):
    v = v.astype(F32)
    if width is not None:
        v = jnp.pad(v, ((0, 0), (offset, width - offset - v.shape[-1])))
    return v[:, None, :]


def _position_tables(seq, tm):
    pos = jnp.concatenate([N_META + jnp.arange(seq, dtype=F32),
                           jnp.arange(tm, dtype=F32) - (tm - N_META)])
    inv_freq = 1.0 / (ROPE_THETA ** (jnp.arange(0, MLA_ROPE, 2, dtype=F32) / MLA_ROPE))
    ang = pos[:, None] * inv_freq[None, :]
    cos, sin = jnp.cos(ang), jnp.sin(ang)
    n = seq + tm
    rope_pad = LANES - MLA_NOPE - MLA_ROPE
    cos128 = jnp.concatenate([jnp.ones((n, MLA_NOPE), F32), cos, cos,
                              jnp.zeros((n, rope_pad), F32)], axis=-1)
    sin128 = jnp.concatenate([jnp.zeros((n, MLA_NOPE), F32), sin, sin,
                              jnp.zeros((n, rope_pad), F32)], axis=-1)
    cosk128 = jnp.concatenate([jnp.zeros((n, MLA_NOPE), F32), cos, cos,
                               jnp.zeros((n, rope_pad), F32)], axis=-1)
    return cos128, sin128, cosk128


def kernel(x, meta, ffn1_w_gate, ffn1_w_up, ffn1_w_down, ln1_g, ln1_b, w_in, conv_w, conv_b, dt_bias, a_log, d_skip, ssd_norm_g, fox_f_b, mla_q_norm_g, mla_w_uq, mla_kv_norm_g, mla_w_ukv, w_out, ln2_g, ln2_b, ffn2_w_gate, ffn2_w_up, ffn2_w_down, ln3_g, ln3_b):
    bsz, seq, _ = x.shape
    assert seq % TQ == 0
    tm = 2 * TQ if seq % (2 * TQ) == 0 else TQ
    n_main = bsz * seq
    n_rows = n_main + tm
    meta_block = n_rows // BLOCK - 1

    h = x.reshape(n_main, D_MODEL)
    meta_tile = jnp.concatenate([jnp.zeros((tm - N_META, D_MODEL), x.dtype),
                                 meta.astype(x.dtype)], axis=0)
    tables = _position_tables(seq, tm)
    bf = lambda w: w.astype(BF16)

    ffn1 = (bf(ffn1_w_gate), bf(ffn1_w_up), bf(ffn1_w_down), _rows(ln1_g), _rows(ln1_b))
    ffn2 = (bf(ffn2_w_gate), bf(ffn2_w_up), bf(ffn2_w_down), _rows(ln3_g), _rows(ln3_b))
    w_in_bf = bf(w_in)
    proj_w = (*_in_proj_weights(w_in_bf), *_mla_weights(mla_w_uq, mla_w_ukv),
              _rows(mla_q_norm_g), _rows(mla_kv_norm_g))
    ssd_p = (conv_w.astype(F32), _rows(conv_b), _rows(dt_bias, LANES),
             _rows(-jnp.exp(a_log.astype(F32)), LANES),
             _rows(jnp.repeat(d_skip, SSD_HEAD_DIM, axis=-1)), _rows(ssd_norm_g))
    fb = _rows(fox_f_b, LANES, SMALL_F)
    out_p = (bf(w_out), _rows(ln2_g), _rows(ln2_b))

    for l in range(DEPTH):
        h, zx, small, fq, fk, fvt, mq, mk, mvt = _ffn_ln(
            l, h, ffn1, tm, n_rows, meta_tile=meta_tile if l == 0 else None,
            proj=(w_in_bf, proj_w, tables, seq))
        y_ssd = _ssd(l, zx, small, ssd_p, bsz, seq)
        fk_meta, fk_main = _fox_keys(l, fk, small, fb, bsz, seq)
        y_fox = _attention(fq, fk_meta, fk_main, fvt, FOX_HEADS, bsz, seq, 0, "fox_attention")
        y_mla = _attention(mq, mk, mk, mvt, MLA_HEADS, bsz, seq, meta_block, "mla_attention")
        h = _ffn_ln(l, h, ffn2, tm, n_rows if l < DEPTH - 1 else n_main,
                    mix=(y_ssd, y_fox, y_mla, *out_p))

    return h.reshape(bsz, seq, D_MODEL)
```

```python
import functools

import numpy as np
import jax
import jax.numpy as jnp
from jax import lax
from jax.experimental import pallas as pl
from jax.experimental.pallas import tpu as pltpu

F32 = jnp.float32
BF16 = jnp.bfloat16

D_MODEL = 1024
DEPTH = 2
N_META = 16
BLOCK = 128
SSD_HEADS = 8
SSD_HEAD_DIM = 64
SSD_D = SSD_HEADS * SSD_HEAD_DIM
SSD_GROUPS = 2
SSD_STATE = 64
SSD_CONV = 4
SSD_CONV_DIM = SSD_D + 2 * SSD_GROUPS * SSD_STATE
FOX_HEADS = 4
FOX_HEAD_DIM = 64
FOX_D = FOX_HEADS * FOX_HEAD_DIM
MLA_HEADS = 4
MLA_Q_LORA = 256
MLA_KV_LORA = 128
MLA_NOPE = 64
MLA_ROPE = 32
MLA_V = 64
MLA_D = MLA_HEADS * MLA_V
ROPE_THETA = 10000.0
D_MIX = SSD_D + FOX_D + MLA_D
D_FF = 2816
ALPHA = (2 * DEPTH) ** 0.25
EPS = 1e-5

LANES = 128
MXU_W = 256
FF_CHUNK = MXU_W
N_FF_CHUNKS = D_FF // FF_CHUNK
N_OUT_CHUNKS = D_MODEL // FF_CHUNK
TQ = 2 * BLOCK
SSD_STEP_ROWS = 2 * BLOCK
PAD = BLOCK - N_META
NEG_BIG = -1e30
M_INIT = 2 * NEG_BIG
VMEM_LIMIT = 56 * 1024 * 1024

C_XBC = SSD_D
C_ZX = SSD_D + SSD_CONV_DIM
C_FQ = 0
C_CQ = C_FQ + 2 * FOX_D
C_MISC = C_CQ + MLA_Q_LORA
N_IN_ARR = C_MISC + MLA_KV_LORA + LANES
MISC_KRR = 32
MISC_KR = MLA_NOPE
LOG2E = 1.4426950408889634
DV_AUG = MLA_V + 16
SMALL_DT = 0
SMALL_F = 8
HEAD_W = LANES
FOX_BIAS_LANE = FOX_HEAD_DIM


def _sigmoid(x):
    return 1.0 / (1.0 + jnp.exp(-x))


def _softplus(x):
    return jnp.maximum(x, 0.0) + jnp.log(1.0 + jnp.exp(-jnp.abs(x)))


def _layer_norm_rows(y, g, b):
    mu = jnp.mean(y, axis=-1, keepdims=True)
    yc = y - mu
    var = jnp.mean(yc * yc, axis=-1, keepdims=True)
    return yc * lax.rsqrt(var + EPS) * g + b


def _split3(x):
    x1 = x.astype(BF16)
    r1 = x - x1.astype(F32)
    x2 = r1.astype(BF16)
    r2 = r1 - x2.astype(F32)
    return x1, x2, r2.astype(BF16)


def _dot(a, b):
    return jnp.dot(a, b, preferred_element_type=F32)


def _dot_nt(a, b):
    return lax.dot_general(a, b, (((1,), (1,)), ((), ())), preferred_element_type=F32)


def _resident(shape):
    return pl.BlockSpec(shape, lambda *_: (0,) * len(shape), pipeline_mode=pl.Buffered(1))


def _layer(arr, l):
    tail = (0,) * (arr.ndim - 1)
    return pl.BlockSpec((None,) + arr.shape[1:], lambda *_: (l,) + tail,
                        pipeline_mode=pl.Buffered(1))


N_PROJ_IN = 12
N_PROJ_OUT = 8


def _ffn_ln_kernel(prologue, n_main_tiles, with_proj, *refs):
    a_scr, y_scr = refs[-2:]
    refs = refs[:-2]
    if with_proj:
        proj_out, refs = refs[-N_PROJ_OUT:], refs[:-N_PROJ_OUT]
        o_ref = refs[-1]
        proj_in, refs = refs[-1 - N_PROJ_IN:-1], refs[:-1 - N_PROJ_IN]
    else:
        o_ref, refs = refs[-1], refs[:-1]
    wg_ref, wu_ref, wd_ref, g_ref, b_ref = refs[-5:]
    if prologue == "plain":
        x = refs[0][...]
    elif prologue == "meta":
        x_ref, m_ref = refs[:2]
        x = jnp.where(pl.program_id(0) < n_main_tiles, x_ref[...], m_ref[...])
    else:
        h_ref, ys_ref, yf_ref, ym_ref, wo_ref, g2_ref, b2_ref = refs[:7]
        mix = (_dot(ys_ref[...], wo_ref[0:SSD_D, :])
               + _dot(yf_ref[...], wo_ref[SSD_D:SSD_D + FOX_D, :])
               + _dot(ym_ref[...], wo_ref[SSD_D + FOX_D:D_MIX, :]))
        x = _layer_norm_rows(ALPHA * h_ref[...] + mix, g2_ref[...], b2_ref[...])
    xb = x.astype(BF16)

    for c in range(N_FF_CHUNKS):
        cols = slice(c * FF_CHUNK, (c + 1) * FF_CHUNK)
        gate = _dot(xb, wg_ref[:, cols])
        up = _dot(xb, wu_ref[:, cols])
        a_scr[c] = (gate * _sigmoid(gate) * up).astype(BF16)

    for n in range(N_OUT_CHUNKS):
        cols = slice(n * FF_CHUNK, (n + 1) * FF_CHUNK)
        acc = _dot(a_scr[0], wd_ref[0:FF_CHUNK, cols])
        for c in range(1, N_FF_CHUNKS):
            acc = acc + _dot(a_scr[c], wd_ref[c * FF_CHUNK:(c + 1) * FF_CHUNK, cols])
        y_scr[:, cols] = ALPHA * x[:, cols] + 0.5 * acc
    out = _layer_norm_rows(y_scr[...], g_ref[...], b_ref[...])
    o_ref[...] = out
    if with_proj:
        _proj_body(out, *proj_in, *proj_out)


def _ffn_ln(l, h, ffn, tm, n_out_rows, meta_tile=None, mix=None, proj=None):
    def rows(width):
        return pl.BlockSpec((tm, width), lambda i: (i, 0))

    out_specs = rows(D_MODEL)
    out_shape = jax.ShapeDtypeStruct((n_out_rows, D_MODEL), F32)
    p_args, p_specs = [], []
    if proj is not None:
        p_args, p_specs, p_out_specs, p_out_shape = _proj_specs(l, n_out_rows, tm, *proj)
        out_specs, out_shape = [out_specs] + p_out_specs, [out_shape] + p_out_shape

    if mix is not None:
        prologue, n_main = "mix", None
        x_args = [h, *mix]
        x_specs = ([rows(D_MODEL), rows(SSD_D), rows(FOX_D), rows(MLA_D)]
                   + [_layer(p, l) for p in mix[3:]])
    elif meta_tile is not None:
        prologue, n_main = "meta", h.shape[0] // tm
        x_args = [h, meta_tile]
        x_specs = [pl.BlockSpec((tm, D_MODEL), lambda i: (jnp.minimum(i, n_main - 1), 0)),
                   _resident(meta_tile.shape)]
    else:
        prologue, n_main, x_args, x_specs = "plain", None, [h], [rows(D_MODEL)]
    return pl.pallas_call(
        functools.partial(_ffn_ln_kernel, prologue, n_main, proj is not None),
        grid=(n_out_rows // tm,),
        in_specs=x_specs + [_layer(p, l) for p in ffn] + p_specs,
        out_specs=out_specs,
        out_shape=out_shape,
        scratch_shapes=[pltpu.VMEM((N_FF_CHUNKS, tm, FF_CHUNK), BF16),
                        pltpu.VMEM((tm, D_MODEL), F32)],
        compiler_params=pltpu.CompilerParams(
            dimension_semantics=("arbitrary",), vmem_limit_bytes=VMEM_LIMIT),
        name="ffn_ln_proj" if proj is not None else "ffn_ln",
    )(*x_args, *ffn, *p_args)


def _rms_rows(x, g):
    return x * lax.rsqrt(jnp.mean(x * x, axis=-1, keepdims=True) + EPS) * g


def _store_vt_blocks(out_ref, val_t, n_heads):
    dv = val_t.shape[0] // n_heads
    ones = jnp.ones((DV_AUG - dv, BLOCK), out_ref.dtype)
    for r in range(out_ref.shape[0]):
        for hd in range(n_heads):
            out_ref[r, hd * DV_AUG:hd * DV_AUG + dv, :] = (
                val_t[hd * dv:(hd + 1) * dv, r * BLOCK:(r + 1) * BLOCK].astype(out_ref.dtype))
            out_ref[r, hd * DV_AUG + dv:(hd + 1) * DV_AUG, :] = ones


def _proj_body(h, wzx_ref, win_ref, wfvt_ref, wuq_ref, wuqr_ref, wkk_ref, wkvt_ref,
               qg_ref, kvg_ref, cos_ref, sin_ref, cosk_ref,
               zx_ref, small_ref, fq_ref, fk_ref, fvt_ref, mq_ref, mk_ref, mvt_ref):
    hb = h.astype(BF16)
    zx_ref[...] = _dot(hb, wzx_ref[...])
    misc = _dot(hb, win_ref[:, C_MISC:N_IN_ARR])
    small = misc[:, MLA_KV_LORA:]
    small_ref[...] = small

    lane = lax.broadcasted_iota(jnp.int32, (1, LANES), 1)
    lane_lo = lane < FOX_HEAD_DIM
    fox_one = jnp.where((lane >= FOX_BIAS_LANE) & (lane < FOX_BIAS_LANE + 3), 1.0, 0.0)

    fqk = _dot(hb, win_ref[:, C_FQ:C_CQ])
    for half, (ref, scale, extra) in enumerate([(fq_ref, FOX_HEAD_DIM ** -0.5 * LOG2E, fox_one),
                                                (fk_ref, None, None)]):
        for pair in range(FOX_HEADS // 2):
            c0 = half * FOX_D + pair * LANES
            both = fqk[:, c0:c0 + LANES]
            if scale is not None:
                both = both * scale
            for par, val in enumerate([both, pltpu.roll(both, FOX_HEAD_DIM, 1)]):
                val = jnp.where(lane_lo, val, 0.0)
                if extra is not None:
                    val = val + extra
                hd = 2 * pair + par
                ref[:, hd * HEAD_W:(hd + 1) * HEAD_W] = val.astype(BF16)
    _store_vt_blocks(fvt_ref, _dot_nt(wfvt_ref[...], hb), FOX_HEADS)

    cos = cos_ref[...]
    sin = sin_ref[...]
    cqn = _rms_rows(_dot(hb, win_ref[:, C_CQ:C_MISC]), qg_ref[...]).astype(BF16)
    q = _dot(cqn, wuq_ref[...])
    qr = _dot(cqn, wuqr_ref[...])
    kvn = _rms_rows(misc[:, :MLA_KV_LORA], kvg_ref[...]).astype(BF16)
    kn = _dot(kvn, wkk_ref[...])
    _store_vt_blocks(mvt_ref, _dot_nt(wkvt_ref[...], kvn), MLA_HEADS)
    krope = small * cosk_ref[...] + pltpu.roll(small, MISC_KR - MISC_KRR, 1) * sin
    scale = (MLA_NOPE + MLA_ROPE) ** -0.5 * LOG2E
    for hd in range(MLA_HEADS):
        sl = slice(hd * HEAD_W, (hd + 1) * HEAD_W)
        mq_ref[:, sl] = ((q[:, sl] * cos + qr[:, sl] * sin) * scale).astype(BF16)
        mk_ref[:, sl] = (kn[:, sl] + krope).astype(BF16)


def _proj_specs(l, n_rows, tm, w_in_bf, weights, tables, seq):
    n_main_tiles = n_rows // tm - 1
    tiles_per_seq = seq // tm

    def rows(width):
        return pl.BlockSpec((tm, width), lambda i: (i, 0))

    tab = pl.BlockSpec((tm, LANES), lambda i: (
        jnp.where(i < n_main_tiles, i % tiles_per_seq, tiles_per_seq), 0))
    wzx = pl.BlockSpec((None, D_MODEL, C_ZX), lambda i: (l, 0, 0), pipeline_mode=pl.Buffered(1))
    vt_rows = FOX_HEADS * DV_AUG
    t_spec = pl.BlockSpec((tm // BLOCK, vt_rows, BLOCK), lambda i: (i, 0, 0))
    t_shape = jax.ShapeDtypeStruct((n_rows // BLOCK, vt_rows, BLOCK), BF16)
    qk_w = FOX_HEADS * HEAD_W
    row_outs = [(C_ZX, F32), (LANES, F32), (qk_w, BF16), (qk_w, BF16)]
    args = [w_in_bf, *weights, *tables]
    in_specs = [wzx] + [_layer(w, l) for w in weights] + [tab, tab, tab]
    out_specs = [rows(w) for w, _ in row_outs] + [t_spec, rows(qk_w), rows(qk_w), t_spec]
    out_shape = ([jax.ShapeDtypeStruct((n_rows, w), dt) for w, dt in row_outs]
                 + [t_shape, jax.ShapeDtypeStruct((n_rows, qk_w), BF16),
                    jax.ShapeDtypeStruct((n_rows, qk_w), BF16), t_shape])
    assert len(args) == N_PROJ_IN and len(out_shape) == N_PROJ_OUT
    return args, in_specs, out_specs, out_shape


def _first_step_block(b, meta_block, n_fill, own_first_block):
    return jnp.where(b == 0, meta_block,
                     jnp.where(b <= n_fill, meta_block - b, own_first_block))


def _n_fill_blocks(n_rows, bsz, seq, block_rows):
    n_fill = (n_rows - bsz * seq) // block_rows - 1
    assert bsz > n_fill, "needs one batch row per meta-tile block to zero-fill"
    return n_fill


def _ssd_kernel(n_fill, zx_ref, small_ref, cw_ref, cb_ref, dtb_ref, a_ref, dsk_ref, ng_ref,
                o_ref, conv_scr, s_scr, meta_conv_scr, meta_s_scr):
    b = pl.program_id(0)
    c = pl.program_id(1)

    @pl.when((c == 0) & (b >= 1) & (b <= n_fill))
    def _():
        o_ref[...] = jnp.zeros(o_ref.shape, o_ref.dtype)

    @pl.when((c == 0) & (b == 0))
    def _():
        conv_scr[0:8, :] = jnp.zeros((8, SSD_CONV_DIM), F32)
        s_scr[...] = jnp.zeros(s_scr.shape, F32)

    @pl.when((c == 0) & (b > 0))
    def _():
        conv_scr[0:8, :] = meta_conv_scr[...]
        s_scr[...] = meta_s_scr[...]

    @pl.when((c > 0) | (b == 0))
    def _():
        _ssd_step(c, zx_ref, small_ref, cw_ref, cb_ref, dtb_ref, a_ref, dsk_ref, ng_ref,
                  o_ref, conv_scr, s_scr)

    @pl.when((c == 0) & (b == 0))
    def _():
        meta_conv_scr[...] = conv_scr[0:8, :]
        meta_s_scr[...] = s_scr[...]


def _ssd_step(c, zx_ref, small_ref, cw_ref, cb_ref, dtb_ref, a_ref, dsk_ref, ng_ref,
              o_ref, conv_scr, s_scr):
    R = SSD_STEP_ROWS
    row = lax.broadcasted_iota(jnp.int32, (R, 1), 0)
    valid = jnp.logical_or(c > 0, row >= R - N_META)

    conv_scr[8:8 + R, :] = jnp.where(valid, zx_ref[:, C_XBC:C_ZX], 0.0)
    acc = cb_ref[...]
    for k in range(SSD_CONV):
        off = 8 - (SSD_CONV - 1) + k
        acc = acc + cw_ref[k:k + 1, :] * conv_scr[off:off + R, :]
    conv_scr[0:8, :] = conv_scr[R:R + 8, :]
    xbc = acc * _sigmoid(acc)
    dt = jnp.where(valid, _softplus(small_ref[...] + dtb_ref[...]), 0.0)
    for ch in range(R // BLOCK):
        rows = slice(ch * BLOCK, (ch + 1) * BLOCK)
        _ssd_chunk(xbc[rows], dt[rows], zx_ref.at[rows], a_ref, dsk_ref, ng_ref,
                   o_ref.at[rows], s_scr)


def _ssd_chunk(xbc, dt, zx_ref, a_ref, dsk_ref, ng_ref, o_ref, s_scr):
    Q = BLOCK
    lane = lax.broadcasted_iota(jnp.int32, (1, LANES), 1)
    lane_lo = lane < SSD_HEAD_DIM
    sub = lax.broadcasted_iota(jnp.int32, (LANES, 1), 0)
    bm = xbc[:, SSD_D:SSD_D + LANES]
    cm = xbc[:, SSD_D + LANES:SSD_D + 2 * LANES]

    a = dt * a_ref[...]
    tri = (lax.broadcasted_iota(jnp.int32, (Q, Q), 0)
           >= lax.broadcasted_iota(jnp.int32, (Q, Q), 1))
    tri_b = jnp.where(tri, 1.0, 0.0).astype(BF16)
    a1, a2, a3 = _split3(a)
    a_cum = _dot(tri_b, a1) + _dot(tri_b, a2) + _dot(tri_b, a3)
    a_cum_t = a_cum.T
    bm_t = bm.T

    cm_b = cm.astype(BF16)
    bm_b = bm.astype(BF16)
    cb_g = [_dot_nt(jnp.where(lane_lo, cm, 0.0).astype(BF16), bm_b),
            _dot_nt(jnp.where(lane_lo, 0.0, cm).astype(BF16), bm_b)]
    rows_g = [sub < SSD_STATE, sub >= SSD_STATE]

    pairs_per_group = SSD_HEADS // 2 // SSD_GROUPS
    y_pairs = []
    for p in range(SSD_HEADS // 2):
        g = p // pairs_per_group
        psl = slice(p * LANES, (p + 1) * LANES)
        xs_p = xbc[:, psl]
        dt_pair = jnp.where(lane_lo, dt[:, 2 * p:2 * p + 1], dt[:, 2 * p + 1:2 * p + 2])
        xdt = (xs_p * dt_pair).astype(BF16)
        s_old = s_scr[p]
        yd, upd, e_col, e_last = [], [], [], []
        for par in range(2):
            hd = 2 * p + par
            col = a_cum[:, hd:hd + 1]
            rowv = a_cum_t[hd:hd + 1, :]
            last = a_cum_t[hd:hd + 1, Q - 1:Q]
            seg = jnp.exp(jnp.where(tri, col - rowv, NEG_BIG))
            yd.append(_dot((cb_g[g] * seg).astype(BF16), xdt))
            upd.append(_dot((bm_t * jnp.exp(last - rowv)).astype(BF16), xdt))
            e_col.append(jnp.exp(col))
            e_last.append(jnp.exp(last))
        y_off = _dot(cm_b, s_old.astype(BF16)) * jnp.where(lane_lo, e_col[0], e_col[1])
        s_new = (jnp.where(lane_lo, e_last[0], e_last[1]) * s_old
                 + jnp.where(rows_g[g], jnp.where(lane_lo, upd[0], upd[1]), 0.0))
        s_scr[p] = s_new
        y_p = jnp.where(lane_lo, yd[0], yd[1]) + y_off + dsk_ref[:, psl] * xs_p
        z_p = zx_ref[:, psl]
        y_pairs.append(y_p * (z_p * _sigmoid(z_p)))

    for g in range(SSD_GROUPS):
        ps = range(g * pairs_per_group, (g + 1) * pairs_per_group)
        ss = sum(jnp.sum(y_pairs[p] * y_pairs[p], axis=-1, keepdims=True) for p in ps)
        inv = lax.rsqrt(ss * (1.0 / (pairs_per_group * LANES)) + EPS)
        for p in ps:
            psl = slice(p * LANES, (p + 1) * LANES)
            o_ref[:, psl] = (y_pairs[p] * inv * ng_ref[:, psl]).astype(BF16)


def _ssd(l, zx, small, params, bsz, seq):
    n_rows = zx.shape[0]
    R = SSD_STEP_ROWS
    steps_per_seq = seq // R
    meta_block = n_rows // R - 1
    n_fill = _n_fill_blocks(n_rows, bsz, seq, R)

    def block(b, c):
        first = _first_step_block(b, meta_block, n_fill, b * steps_per_seq)
        return (jnp.where(c == 0, first, b * steps_per_seq + c - 1), 0)

    return pl.pallas_call(
        functools.partial(_ssd_kernel, n_fill),
        grid=(bsz, steps_per_seq + 1),
        in_specs=[pl.BlockSpec((R, C_ZX), block), pl.BlockSpec((R, LANES), block)]
                 + [_layer(p, l) for p in params],
        out_specs=pl.BlockSpec((R, SSD_D), block),
        out_shape=jax.ShapeDtypeStruct((n_rows, SSD_D), BF16),
        scratch_shapes=[pltpu.VMEM((R + 8, SSD_CONV_DIM), F32),
                        pltpu.VMEM((SSD_HEADS // 2, LANES, LANES), F32),
                        pltpu.VMEM((8, SSD_CONV_DIM), F32),
                        pltpu.VMEM((SSD_HEADS // 2, LANES, LANES), F32)],
        compiler_params=pltpu.CompilerParams(
            dimension_semantics=("arbitrary", "arbitrary"), vmem_limit_bytes=VMEM_LIMIT),
        name="ssd_mixer",
    )(zx, small, *params)


def _fox_keys_kernel(km_ref, k_ref, sm_ref, s_ref, fb_ref, om_ref, o_ref):
    T = BLOCK
    width = k_ref.shape[1]
    row = lax.broadcasted_iota(jnp.int32, (T, 1), 0)
    tri = (lax.broadcasted_iota(jnp.int32, (T, T), 0)
           >= lax.broadcasted_iota(jnp.int32, (T, T), 1))
    tri_b = jnp.where(tri, 1.0, 0.0).astype(BF16)
    src = lax.broadcasted_iota(jnp.int32, (LANES, width), 0)
    dst = lax.broadcasted_iota(jnp.int32, (LANES, width), 1)
    dst_head = jnp.right_shift(dst, HEAD_W.bit_length() - 1)
    dst_lane = jnp.bitwise_and(dst, HEAD_W - 1)
    sel = [jnp.where((src == SMALL_F + dst_head) & (dst_lane == FOX_BIAS_LANE + i),
                     1.0, 0.0).astype(BF16) for i in range(3)]

    def local_cumsum(small_blk, is_meta):
        log_f = -_softplus(-(small_blk + fb_ref[...]))
        if is_meta:
            log_f = jnp.where(row < PAD, 0.0, log_f)
        f1, f2, f3 = _split3(log_f)
        return _dot(tri_b, f1) + _dot(tri_b, f2) + _dot(tri_b, f3)

    def keys(c_blk, k_blk):
        b1, b2, b3 = _split3(c_blk * (-LOG2E))
        placed = _dot(b1, sel[0]) + _dot(b2, sel[1]) + _dot(b3, sel[2])
        return (k_blk.astype(F32) + placed).astype(BF16)

    n_blocks = k_ref.shape[0] // T
    rows = [slice(j * T, (j + 1) * T) for j in range(n_blocks)]
    c_meta = local_cumsum(sm_ref[...], True)
    local = [local_cumsum(s_ref[rows[j], :], False) for j in range(n_blocks)]
    om_ref[...] = keys(c_meta, km_ref[...])
    carry = c_meta[T - 1:T, :]
    for j in range(n_blocks):
        c_blk = local[j] + carry
        carry = c_blk[T - 1:T, :]
        o_ref[rows[j], :] = keys(c_blk, k_ref[rows[j], :])


def _fox_keys(l, k, small, fb, bsz, seq):
    n_rows, width = k.shape
    meta_block = n_rows // BLOCK - 1
    main = lambda w: pl.BlockSpec((seq, w), lambda b: (b, 0))
    meta = lambda w: pl.BlockSpec((BLOCK, w), lambda b: (meta_block, 0))
    return pl.pallas_call(
        _fox_keys_kernel,
        grid=(bsz,),
        in_specs=[meta(width), main(width), meta(LANES), main(LANES), _layer(fb, l)],
        out_specs=[pl.BlockSpec((BLOCK, width), lambda b: (0, 0)), main(width)],
        out_shape=[jax.ShapeDtypeStruct((BLOCK, width), BF16),
                   jax.ShapeDtypeStruct((bsz * seq, width), BF16)],
        compiler_params=pltpu.CompilerParams(
            dimension_semantics=("arbitrary",), vmem_limit_bytes=VMEM_LIMIT),
        name="fox_keys",
    )(k, k, small, small, fb)


def _attn_kernel(n_heads, n_fill, q_ref, km_ref, k_ref, vtm_ref, vt_ref, o_ref,
                 acc_scr, s0_scr, s1_scr):
    t = pl.program_id(1)
    b = pl.program_id(0)
    dv = acc_scr.shape[1]

    @pl.when((t == 0) & (b >= 1) & (b <= n_fill))
    def _():
        o_ref[...] = jnp.zeros(o_ref.shape, o_ref.dtype)

    heads = range(n_heads)
    hsl = [slice(hd * HEAD_W, (hd + 1) * HEAD_W) for hd in heads]
    vsl = [slice(hd * dv, (hd + 1) * dv) for hd in heads]
    qs = [q_ref[:, hsl[hd]] for hd in heads]
    ahead = (lax.broadcasted_iota(jnp.int32, (TQ, TQ), 0)
             - lax.broadcasted_iota(jnp.int32, (TQ, TQ), 1))

    def softmax_pv(ss, vts, ms, mask):
        if mask is not None:
            ss = [jnp.where(mask, s, NEG_BIG) for s in ss]
        new_ms = [jnp.maximum(ms[hd], jnp.max(ss[hd], axis=0, keepdims=True)) for hd in heads]
        corrs = [jnp.exp2(ms[hd] - new_ms[hd]) for hd in heads]
        prs = [jnp.exp2(ss[hd] - new_ms[hd]).astype(BF16) for hd in heads]
        for hd in heads:
            acc_scr[hd] = corrs[hd] * acc_scr[hd] + _dot(vts[hd], prs[hd])
        return tuple(new_ms)

    def meta_chunk(carry, mask):
        return softmax_pv([_dot_nt(km_ref[PAD:, hsl[hd]], qs[hd]) for hd in heads],
                          [vtm_ref[0, vsl[hd], PAD:] for hd in heads], carry, mask)

    def scores(c, slot_scr):
        r0 = pl.multiple_of(c * TQ, TQ)
        for hd in heads:
            slot_scr[hd] = _dot_nt(k_ref[pl.ds(r0, TQ), hsl[hd]], qs[hd])

    def chunk_vt(c, hd):
        n_sub = TQ // BLOCK
        return jnp.concatenate([vt_ref[n_sub * c + i, vsl[hd], :] for i in range(n_sub)], axis=1)

    def consume(c, slot_scr, carry, mask):
        return softmax_pv([slot_scr[hd] for hd in heads], [chunk_vt(c, hd) for hd in heads],
                          carry, mask)

    def finish():
        dv_out = o_ref.shape[1] // n_heads
        ys = [acc_scr[hd, 0:dv_out, :] * (1.0 / acc_scr[hd, dv_out:dv_out + 1, :])
              for hd in heads]
        for p in range(n_heads // 2):
            y_t = jnp.concatenate([ys[2 * p], ys[2 * p + 1]], axis=0)
            o_ref[:, p * LANES:(p + 1) * LANES] = y_t.T.astype(BF16)

    init = tuple(jnp.full((1, TQ), M_INIT, F32) for _ in heads)

    @pl.when((t == 0) & (b == 0))
    def _():
        acc_scr[...] = jnp.zeros(acc_scr.shape, F32)
        meta_chunk(init, ahead[:N_META, :] <= -(TQ - N_META))
        finish()

    @pl.when(t > 0)
    def _():
        scores(0, s0_scr)
        s_meta = [_dot_nt(km_ref[PAD:, hsl[hd]], qs[hd]) for hd in heads]
        scores(jnp.minimum(1, k_ref.shape[0] // TQ - 1), s1_scr)
        diagonal = ahead <= 0

        s_first = [jnp.where(ahead <= (t - 1) * TQ, s0_scr[hd], NEG_BIG) for hd in heads]
        ms = tuple(jnp.maximum(jnp.max(s_first[hd], axis=0, keepdims=True),
                               jnp.max(s_meta[hd], axis=0, keepdims=True)) for hd in heads)
        for hd in heads:
            acc_scr[hd] = (_dot(chunk_vt(0, hd), jnp.exp2(s_first[hd] - ms[hd]).astype(BF16))
                           + _dot(vtm_ref[0, vsl[hd], PAD:],
                                  jnp.exp2(s_meta[hd] - ms[hd]).astype(BF16)))

        @pl.when(t > 1)
        def _():
            def pair(i, cr):
                c = 2 * i + 1
                scores(c + 1, s0_scr)
                cr = consume(c, s1_scr, cr, None)
                scores(c + 2, s1_scr)
                return consume(c + 1, s0_scr, cr, None)

            n_unmasked = t - 2
            carry = lax.fori_loop(0, n_unmasked // 2, pair, ms)
            c = 2 * (n_unmasked // 2) + 1

            def odd_tail(cr):
                scores(c + 1, s0_scr)
                cr = consume(c, s1_scr, cr, None)
                return consume(c + 1, s0_scr, cr, diagonal)

            lax.cond(n_unmasked % 2 == 1, odd_tail,
                     lambda cr: consume(c, s1_scr, cr, diagonal), carry)

        finish()


def _attention(q, k_meta, k_main, vt, n_heads, bsz, seq, meta_k_block, name):
    n_rows, qw = q.shape
    dv_aug = vt.shape[1] // n_heads
    dv = dv_aug - (DV_AUG - MLA_V)
    tiles_per_seq = seq // TQ
    meta_q_tile = n_rows // TQ - 1
    meta_block = n_rows // BLOCK - 1
    n_fill = _n_fill_blocks(n_rows, bsz, seq, TQ)

    def q_tile(b, t):
        first = _first_step_block(b, meta_q_tile, n_fill, b * tiles_per_seq)
        return (jnp.where(t == 0, first, b * tiles_per_seq + t - 1), 0)

    return pl.pallas_call(
        functools.partial(_attn_kernel, n_heads, n_fill),
        grid=(bsz, tiles_per_seq + 1),
        in_specs=[pl.BlockSpec((TQ, qw), q_tile),
                  pl.BlockSpec((BLOCK, qw), lambda b, t: (meta_k_block, 0)),
                  pl.BlockSpec((seq, qw), lambda b, t: (b, 0)),
                  pl.BlockSpec((1,) + vt.shape[1:], lambda b, t: (meta_block, 0, 0)),
                  pl.BlockSpec((seq // BLOCK,) + vt.shape[1:], lambda b, t: (b, 0, 0))],
        out_specs=pl.BlockSpec((TQ, n_heads * dv), q_tile),
        out_shape=jax.ShapeDtypeStruct((n_rows, n_heads * dv), BF16),
        scratch_shapes=[pltpu.VMEM((n_heads, dv_aug, TQ), F32),
                        pltpu.VMEM((n_heads, TQ, TQ), F32),
                        pltpu.VMEM((n_heads, TQ, TQ), F32)],
        compiler_params=pltpu.CompilerParams(
            dimension_semantics=("arbitrary", "arbitrary"), vmem_limit_bytes=VMEM_LIMIT),
        name=name,
    )(q, k_meta, k_main, vt, vt)


def _rot_half_cols(w):
    half = MLA_ROPE // 2
    return jnp.concatenate([-w[..., half:], w[..., :half]], axis=-1)


def _in_proj_weights(w_in):
    sizes = [SSD_D, SSD_CONV_DIM, SSD_HEADS, FOX_D, FOX_D, FOX_D, FOX_HEADS,
             MLA_Q_LORA, MLA_KV_LORA, MLA_ROPE]
    splits = [int(s) for s in np.cumsum(sizes)[:-1]]
    z, xbc, dt, fq, fk, fv, fr, cq, ckv, kr = jnp.split(w_in, splits, axis=-1)
    zeros = lambda n: jnp.zeros(w_in.shape[:-1] + (n,), w_in.dtype)
    del z, xbc
    misc = jnp.concatenate([dt, fr, zeros(MISC_KRR - SMALL_F - FOX_HEADS), _rot_half_cols(kr),
                            kr, zeros(LANES - MISC_KR - MLA_ROPE)], axis=-1)
    win = jnp.concatenate([fq, fk, cq, ckv, misc], axis=-1)
    return win, jnp.swapaxes(fv, -1, -2)


def _mla_weights(w_uq, w_ukv):
    rope_pad = LANES - MLA_NOPE - MLA_ROPE
    lead = w_uq.shape[:-1]
    wq = w_uq.astype(BF16).reshape(lead + (MLA_HEADS, MLA_NOPE + MLA_ROPE))
    zq = jnp.zeros(lead + (MLA_HEADS, rope_pad), BF16)
    wuq = jnp.concatenate([wq, zq], axis=-1).reshape(lead + (MLA_HEADS * LANES,))
    wuqr = jnp.concatenate([jnp.zeros_like(wq[..., :MLA_NOPE]),
                            _rot_half_cols(wq[..., MLA_NOPE:]), zq],
                           axis=-1).reshape(lead + (MLA_HEADS * LANES,))
    lead = w_ukv.shape[:-1]
    wkv = w_ukv.astype(BF16).reshape(lead + (MLA_HEADS, MLA_NOPE + MLA_V))
    wkk = jnp.concatenate([wkv[..., :MLA_NOPE], jnp.zeros_like(wkv[..., MLA_NOPE:])],
                          axis=-1).reshape(lead + (MLA_HEADS * LANES,))
    wvt = jnp.swapaxes(wkv[..., MLA_NOPE:].reshape(lead + (MLA_D,)), -1, -2)
    return wuq, wuqr, wkk, wvt


def _rows(v, width=None, offset=0):
    v = v.astype(F32)
    if width is not None:
        v = jnp.pad(v, ((0, 0), (offset, width - offset - v.shape[-1])))
    return v[:, None, :]


def _position_tables(seq, tm):
    pos = jnp.concatenate([N_META + jnp.arange(seq, dtype=F32),
                           jnp.arange(tm, dtype=F32) - (tm - N_META)])
    inv_freq = 1.0 / (ROPE_THETA ** (jnp.arange(0, MLA_ROPE, 2, dtype=F32) / MLA_ROPE))
    ang = pos[:, None] * inv_freq[None, :]
    cos, sin = jnp.cos(ang), jnp.sin(ang)
    n = seq + tm
    rope_pad = LANES - MLA_NOPE - MLA_ROPE
    cos128 = jnp.concatenate([jnp.ones((n, MLA_NOPE), F32), cos, cos,
                              jnp.zeros((n, rope_pad), F32)], axis=-1)
    sin128 = jnp.concatenate([jnp.zeros((n, MLA_NOPE), F32), sin, sin,
                              jnp.zeros((n, rope_pad), F32)], axis=-1)
    cosk128 = jnp.concatenate([jnp.zeros((n, MLA_NOPE), F32), cos, cos,
                               jnp.zeros((n, rope_pad), F32)], axis=-1)
    return cos128, sin128, cosk128


def kernel(x, meta, ffn1_w_gate, ffn1_w_up, ffn1_w_down, ln1_g, ln1_b, w_in, conv_w, conv_b, dt_bias, a_log, d_skip, ssd_norm_g, fox_f_b, mla_q_norm_g, mla_w_uq, mla_kv_norm_g, mla_w_ukv, w_out, ln2_g, ln2_b, ffn2_w_gate, ffn2_w_up, ffn2_w_down, ln3_g, ln3_b):
    bsz, seq, _ = x.shape
    assert seq % TQ == 0
    tm = 2 * TQ if seq % (2 * TQ) == 0 else TQ
    n_main = bsz * seq
    n_rows = n_main + tm
    meta_block = n_rows // BLOCK - 1

    h = x.reshape(n_main, D_MODEL)
    meta_tile = jnp.concatenate([jnp.zeros((tm - N_META, D_MODEL), x.dtype),
                                 meta.astype(x.dtype)], axis=0)
    tables = _position_tables(seq, tm)
    bf = lambda w: w.astype(BF16)

    ffn1 = (bf(ffn1_w_gate), bf(ffn1_w_up), bf(ffn1_w_down), _rows(ln1_g), _rows(ln1_b))
    ffn2 = (bf(ffn2_w_gate), bf(ffn2_w_up), bf(ffn2_w_down), _rows(ln3_g), _rows(ln3_b))
    w_in_bf = bf(w_in)
    proj_w = (*_in_proj_weights(w_in_bf), *_mla_weights(mla_w_uq, mla_w_ukv),
              _rows(mla_q_norm_g), _rows(mla_kv_norm_g))
    ssd_p = (conv_w.astype(F32), _rows(conv_b), _rows(dt_bias, LANES),
             _rows(-jnp.exp(a_log.astype(F32)), LANES),
             _rows(jnp.repeat(d_skip, SSD_HEAD_DIM, axis=-1)), _rows(ssd_norm_g))
    fb = _rows(fox_f_b, LANES, SMALL_F)
    out_p = (bf(w_out), _rows(ln2_g), _rows(ln2_b))

    for l in range(DEPTH):
        h, zx, small, fq, fk, fvt, mq, mk, mvt = _ffn_ln(
            l, h, ffn1, tm, n_rows, meta_tile=meta_tile if l == 0 else None,
            proj=(w_in_bf, proj_w, tables, seq))
        y_ssd = _ssd(l, zx, small, ssd_p, bsz, seq)
        fk_meta, fk_main = _fox_keys(l, fk, small, fb, bsz, seq)
        y_fox = _attention(fq, fk_meta, fk_main, fvt, FOX_HEADS, bsz, seq, 0, "fox_attention")
        y_mla = _attention(mq, mk, mk, mvt, MLA_HEADS, bsz, seq, meta_block, "mla_attention")
        h = _ffn_ln(l, h, ffn2, tm, n_rows if l < DEPTH - 1 else n_main,
                    mix=(y_ssd, y_fox, y_mla, *out_p))

    return h.reshape(bsz, seq, D_MODEL)
```

```python
import functools

import numpy as np
import jax
import jax.numpy as jnp
from jax import lax
from jax.experimental import pallas as pl
from jax.experimental.pallas import tpu as pltpu

F32 = jnp.float32
BF16 = jnp.bfloat16

D_MODEL = 1024
DEPTH = 2
N_META = 16
BLOCK = 128
SSD_HEADS = 8
SSD_HEAD_DIM = 64
SSD_D = SSD_HEADS * SSD_HEAD_DIM
SSD_GROUPS = 2
SSD_STATE = 64
SSD_CONV = 4
SSD_CONV_DIM = SSD_D + 2 * SSD_GROUPS * SSD_STATE
FOX_HEADS = 4
FOX_HEAD_DIM = 64
FOX_D = FOX_HEADS * FOX_HEAD_DIM
MLA_HEADS = 4
MLA_Q_LORA = 256
MLA_KV_LORA = 128
MLA_NOPE = 64
MLA_ROPE = 32
MLA_V = 64
MLA_D = MLA_HEADS * MLA_V
ROPE_THETA = 10000.0
D_MIX = SSD_D + FOX_D + MLA_D
D_FF = 2816
ALPHA = (2 * DEPTH) ** 0.25
EPS = 1e-5

LANES = 128
MXU_W = 256
FF_CHUNK = MXU_W
N_FF_CHUNKS = D_FF // FF_CHUNK
N_OUT_CHUNKS = D_MODEL // FF_CHUNK
TQ = 2 * BLOCK
SSD_STEP_ROWS = 2 * BLOCK
PAD = BLOCK - N_META
NEG_BIG = -1e30
M_INIT = 2 * NEG_BIG
VMEM_LIMIT = 56 * 1024 * 1024

C_XBC = SSD_D
C_ZX = SSD_D + SSD_CONV_DIM
C_FQ = 0
C_CQ = C_FQ + 2 * FOX_D
C_MISC = C_CQ + MLA_Q_LORA
N_IN_ARR = C_MISC + MLA_KV_LORA + LANES
MISC_KRR = 32
MISC_KR = MLA_NOPE
LOG2E = 1.4426950408889634
DV_AUG = MLA_V + 16
SMALL_DT = 0
SMALL_F = 8
HEAD_W = LANES
FOX_BIAS_LANE = FOX_HEAD_DIM


def _sigmoid(x):
    return 1.0 / (1.0 + jnp.exp(-x))


def _softplus(x):
    return jnp.maximum(x, 0.0) + jnp.log(1.0 + jnp.exp(-jnp.abs(x)))


def _layer_norm_rows(y, g, b):
    mu = jnp.mean(y, axis=-1, keepdims=True)
    yc = y - mu
    var = jnp.mean(yc * yc, axis=-1, keepdims=True)
    return yc * lax.rsqrt(var + EPS) * g + b


def _split3(x):
    x1 = x.astype(BF16)
    r1 = x - x1.astype(F32)
    x2 = r1.astype(BF16)
    r2 = r1 - x2.astype(F32)
    return x1, x2, r2.astype(BF16)


def _dot(a, b):
    return jnp.dot(a, b, preferred_element_type=F32)


def _dot_nt(a, b):
    return lax.dot_general(a, b, (((1,), (1,)), ((), ())), preferred_element_type=F32)


def _resident(shape):
    return pl.BlockSpec(shape, lambda *_: (0,) * len(shape), pipeline_mode=pl.Buffered(1))


def _layer(arr, l):
    tail = (0,) * (arr.ndim - 1)
    return pl.BlockSpec((None,) + arr.shape[1:], lambda *_: (l,) + tail,
                        pipeline_mode=pl.Buffered(1))


N_PROJ_IN = 12
N_PROJ_OUT = 8


def _ffn_ln_kernel(prologue, n_main_tiles, with_proj, *refs):
    a_scr, y_scr = refs[-2:]
    refs = refs[:-2]
    if with_proj:
        proj_out, refs = refs[-N_PROJ_OUT:], refs[:-N_PROJ_OUT]
        o_ref = refs[-1]
        proj_in, refs = refs[-1 - N_PROJ_IN:-1], refs[:-1 - N_PROJ_IN]
    else:
        o_ref, refs = refs[-1], refs[:-1]
    wg_ref, wu_ref, wd_ref, g_ref, b_ref = refs[-5:]
    if prologue == "plain":
        x = refs[0][...]
    elif prologue == "meta":
        x_ref, m_ref = refs[:2]
        x = jnp.where(pl.program_id(0) < n_main_tiles, x_ref[...], m_ref[...])
    else:
        h_ref, ys_ref, yf_ref, ym_ref, wo_ref, g2_ref, b2_ref = refs[:7]
        mix = (_dot(ys_ref[...], wo_ref[0:SSD_D, :])
               + _dot(yf_ref[...], wo_ref[SSD_D:SSD_D + FOX_D, :])
               + _dot(ym_ref[...], wo_ref[SSD_D + FOX_D:D_MIX, :]))
        x = _layer_norm_rows(ALPHA * h_ref[...] + mix, g2_ref[...], b2_ref[...])
    xb = x.astype(BF16)

    for c in range(N_FF_CHUNKS):
        cols = slice(c * FF_CHUNK, (c + 1) * FF_CHUNK)
        gate = _dot(xb, wg_ref[:, cols])
        up = _dot(xb, wu_ref[:, cols])
        a_scr[c] = (gate * _sigmoid(gate) * up).astype(BF16)

    for n in range(N_OUT_CHUNKS):
        cols = slice(n * FF_CHUNK, (n + 1) * FF_CHUNK)
        acc = _dot(a_scr[0], wd_ref[0:FF_CHUNK, cols])
        for c in range(1, N_FF_CHUNKS):
            acc = acc + _dot(a_scr[c], wd_ref[c * FF_CHUNK:(c + 1) * FF_CHUNK, cols])
        y_scr[:, cols] = ALPHA * x[:, cols] + 0.5 * acc
    out = _layer_norm_rows(y_scr[...], g_ref[...], b_ref[...])
    o_ref[...] = out
    if with_proj:
        _proj_body(out, *proj_in, *proj_out)


def _ffn_ln(l, h, ffn, tm, n_out_rows, meta_tile=None, mix=None, proj=None):
    def rows(width):
        return pl.BlockSpec((tm, width), lambda i: (i, 0))

    out_specs = rows(D_MODEL)
    out_shape = jax.ShapeDtypeStruct((n_out_rows, D_MODEL), F32)
    p_args, p_specs = [], []
    if proj is not None:
        p_args, p_specs, p_out_specs, p_out_shape = _proj_specs(l, n_out_rows, tm, *proj)
        out_specs, out_shape = [out_specs] + p_out_specs, [out_shape] + p_out_shape

    if mix is not None:
        prologue, n_main = "mix", None
        x_args = [h, *mix]
        x_specs = ([rows(D_MODEL), rows(SSD_D), rows(FOX_D), rows(MLA_D)]
                   + [_layer(p, l) for p in mix[3:]])
    elif meta_tile is not None:
        prologue, n_main = "meta", h.shape[0] // tm
        x_args = [h, meta_tile]
        x_specs = [pl.BlockSpec((tm, D_MODEL), lambda i: (jnp.minimum(i, n_main - 1), 0)),
                   _resident(meta_tile.shape)]
    else:
        prologue, n_main, x_args, x_specs = "plain", None, [h], [rows(D_MODEL)]
    return pl.pallas_call(
        functools.partial(_ffn_ln_kernel, prologue, n_main, proj is not None),
        grid=(n_out_rows // tm,),
        in_specs=x_specs + [_layer(p, l) for p in ffn] + p_specs,
        out_specs=out_specs,
        out_shape=out_shape,
        scratch_shapes=[pltpu.VMEM((N_FF_CHUNKS, tm, FF_CHUNK), BF16),
                        pltpu.VMEM((tm, D_MODEL), F32)],
        compiler_params=pltpu.CompilerParams(
            dimension_semantics=("arbitrary",), vmem_limit_bytes=VMEM_LIMIT),
        name="ffn_ln_proj" if proj is not None else "ffn_ln",
    )(*x_args, *ffn, *p_args)


def _rms_rows(x, g):
    return x * lax.rsqrt(jnp.mean(x * x, axis=-1, keepdims=True) + EPS) * g


def _store_vt_blocks(out_ref, val_t, n_heads):
    dv = val_t.shape[0] // n_heads
    ones = jnp.ones((DV_AUG - dv, BLOCK), out_ref.dtype)
    for r in range(out_ref.shape[0]):
        for hd in range(n_heads):
            out_ref[r, hd * DV_AUG:hd * DV_AUG + dv, :] = (
                val_t[hd * dv:(hd + 1) * dv, r * BLOCK:(r + 1) * BLOCK].astype(out_ref.dtype))
            out_ref[r, hd * DV_AUG + dv:(hd + 1) * DV_AUG, :] = ones


def _proj_body(h, wzx_ref, win_ref, wfvt_ref, wuq_ref, wuqr_ref, wkk_ref, wkvt_ref,
               qg_ref, kvg_ref, cos_ref, sin_ref, cosk_ref,
               zx_ref, small_ref, fq_ref, fk_ref, fvt_ref, mq_ref, mk_ref, mvt_ref):
    hb = h.astype(BF16)
    zx_ref[...] = _dot(hb, wzx_ref[...])
    misc = _dot(hb, win_ref[:, C_MISC:N_IN_ARR])
    small = misc[:, MLA_KV_LORA:]
    small_ref[...] = small

    lane = lax.broadcasted_iota(jnp.int32, (1, LANES), 1)
    lane_lo = lane < FOX_HEAD_DIM
    fox_one = jnp.where((lane >= FOX_BIAS_LANE) & (lane < FOX_BIAS_LANE + 3), 1.0, 0.0)

    fqk = _dot(hb, win_ref[:, C_FQ:C_CQ])
    for half, (ref, scale, extra) in enumerate([(fq_ref, FOX_HEAD_DIM ** -0.5 * LOG2E, fox_one),
                                                (fk_ref, None, None)]):
        for pair in range(FOX_HEADS // 2):
            c0 = half * FOX_D + pair * LANES
            both = fqk[:, c0:c0 + LANES]
            if scale is not None:
                both = both * scale
            for par, val in enumerate([both, pltpu.roll(both, FOX_HEAD_DIM, 1)]):
                val = jnp.where(lane_lo, val, 0.0)
                if extra is not None:
                    val = val + extra
                hd = 2 * pair + par
                ref[:, hd * HEAD_W:(hd + 1) * HEAD_W] = val.astype(BF16)
    _store_vt_blocks(fvt_ref, _dot_nt(wfvt_ref[...], hb), FOX_HEADS)

    cos = cos_ref[...]
    sin = sin_ref[...]
    cqn = _rms_rows(_dot(hb, win_ref[:, C_CQ:C_MISC]), qg_ref[...]).astype(BF16)
    q = _dot(cqn, wuq_ref[...])
    qr = _dot(cqn, wuqr_ref[...])
    kvn = _rms_rows(misc[:, :MLA_KV_LORA], kvg_ref[...]).astype(BF16)
    kn = _dot(kvn, wkk_ref[...])
    _store_vt_blocks(mvt_ref, _dot_nt(wkvt_ref[...], kvn), MLA_HEADS)
    krope = small * cosk_ref[...] + pltpu.roll(small, MISC_KR - MISC_KRR, 1) * sin
    scale = (MLA_NOPE + MLA_ROPE) ** -0.5 * LOG2E
    for hd in range(MLA_HEADS):
        sl = slice(hd * HEAD_W, (hd + 1) * HEAD_W)
        mq_ref[:, sl] = ((q[:, sl] * cos + qr[:, sl] * sin) * scale).astype(BF16)
        mk_ref[:, sl] = (kn[:, sl] + krope).astype(BF16)


def _proj_specs(l, n_rows, tm, weights, tables, seq):
    n_main_tiles = n_rows // tm - 1
    tiles_per_seq = seq // tm

    def rows(width):
        return pl.BlockSpec((tm, width), lambda i: (i, 0))

    tab = pl.BlockSpec((tm, LANES), lambda i: (
        jnp.where(i < n_main_tiles, i % tiles_per_seq, tiles_per_seq), 0))
    vt_rows = FOX_HEADS * DV_AUG
    t_spec = pl.BlockSpec((tm // BLOCK, vt_rows, BLOCK), lambda i: (i, 0, 0))
    t_shape = jax.ShapeDtypeStruct((n_rows // BLOCK, vt_rows, BLOCK), BF16)
    qk_w = FOX_HEADS * HEAD_W
    row_outs = [(C_ZX, F32), (LANES, F32), (qk_w, BF16), (qk_w, BF16)]
    args = [*weights, *tables]
    in_specs = [_layer(w, l) for w in weights] + [tab, tab, tab]
    out_specs = [rows(w) for w, _ in row_outs] + [t_spec, rows(qk_w), rows(qk_w), t_spec]
    out_shape = ([jax.ShapeDtypeStruct((n_rows, w), dt) for w, dt in row_outs]
                 + [t_shape, jax.ShapeDtypeStruct((n_rows, qk_w), BF16),
                    jax.ShapeDtypeStruct((n_rows, qk_w), BF16), t_shape])
    assert len(args) == N_PROJ_IN and len(out_shape) == N_PROJ_OUT
    return args, in_specs, out_specs, out_shape


def _first_step_block(b, meta_block, n_fill, own_first_block):
    return jnp.where(b == 0, meta_block,
                     jnp.where(b <= n_fill, meta_block - b, own_first_block))


def _n_fill_blocks(n_rows, bsz, seq, block_rows):
    n_fill = (n_rows - bsz * seq) // block_rows - 1
    assert bsz > n_fill, "needs one batch row per meta-tile block to zero-fill"
    return n_fill


def _ssd_kernel(n_fill, zx_ref, small_ref, cw_ref, cb_ref, dtb_ref, a_ref, dsk_ref, ng_ref,
                o_ref, conv_scr, s_scr, meta_conv_scr, meta_s_scr):
    b = pl.program_id(0)
    c = pl.program_id(1)

    @pl.when((c == 0) & (b >= 1) & (b <= n_fill))
    def _():
        o_ref[...] = jnp.zeros(o_ref.shape, o_ref.dtype)

    @pl.when((c == 0) & (b == 0))
    def _():
        conv_scr[0:8, :] = jnp.zeros((8, SSD_CONV_DIM), F32)
        s_scr[...] = jnp.zeros(s_scr.shape, F32)

    @pl.when((c == 0) & (b > 0))
    def _():
        conv_scr[0:8, :] = meta_conv_scr[...]
        s_scr[...] = meta_s_scr[...]

    @pl.when((c > 0) | (b == 0))
    def _():
        _ssd_step(c, zx_ref, small_ref, cw_ref, cb_ref, dtb_ref, a_ref, dsk_ref, ng_ref,
                  o_ref, conv_scr, s_scr)

    @pl.when((c == 0) & (b == 0))
    def _():
        meta_conv_scr[...] = conv_scr[0:8, :]
        meta_s_scr[...] = s_scr[...]


def _ssd_step(c, zx_ref, small_ref, cw_ref, cb_ref, dtb_ref, a_ref, dsk_ref, ng_ref,
              o_ref, conv_scr, s_scr):
    R = SSD_STEP_ROWS
    row = lax.broadcasted_iota(jnp.int32, (R, 1), 0)
    valid = jnp.logical_or(c > 0, row >= R - N_META)

    conv_scr[8:8 + R, :] = jnp.where(valid, zx_ref[:, C_XBC:C_ZX], 0.0)
    acc = cb_ref[...]
    for k in range(SSD_CONV):
        off = 8 - (SSD_CONV - 1) + k
        acc = acc + cw_ref[k:k + 1, :] * conv_scr[off:off + R, :]
    conv_scr[0:8, :] = conv_scr[R:R + 8, :]
    xbc = acc * _sigmoid(acc)
    dt = jnp.where(valid, _softplus(small_ref[...] + dtb_ref[...]), 0.0)
    for ch in range(R // BLOCK):
        rows = slice(ch * BLOCK, (ch + 1) * BLOCK)
        _ssd_chunk(xbc[rows], dt[rows], zx_ref.at[rows], a_ref, dsk_ref, ng_ref,
                   o_ref.at[rows], s_scr)


def _ssd_chunk(xbc, dt, zx_ref, a_ref, dsk_ref, ng_ref, o_ref, s_scr):
    Q = BLOCK
    lane = lax.broadcasted_iota(jnp.int32, (1, LANES), 1)
    lane_lo = lane < SSD_HEAD_DIM
    sub = lax.broadcasted_iota(jnp.int32, (LANES, 1), 0)
    bm = xbc[:, SSD_D:SSD_D + LANES]
    cm = xbc[:, SSD_D + LANES:SSD_D + 2 * LANES]

    a = dt * a_ref[...]
    tri = (lax.broadcasted_iota(jnp.int32, (Q, Q), 0)
           >= lax.broadcasted_iota(jnp.int32, (Q, Q), 1))
    tri_b = jnp.where(tri, 1.0, 0.0).astype(BF16)
    a1, a2, a3 = _split3(a)
    a_cum = _dot(tri_b, a1) + _dot(tri_b, a2) + _dot(tri_b, a3)
    a_cum_t = a_cum.T
    bm_t = bm.T

    cm_b = cm.astype(BF16)
    bm_b = bm.astype(BF16)
    cb_g = [_dot_nt(jnp.where(lane_lo, cm, 0.0).astype(BF16), bm_b),
            _dot_nt(jnp.where(lane_lo, 0.0, cm).astype(BF16), bm_b)]
    rows_g = [sub < SSD_STATE, sub >= SSD_STATE]

    pairs_per_group = SSD_HEADS // 2 // SSD_GROUPS
    y_pairs = []
    for p in range(SSD_HEADS // 2):
        g = p // pairs_per_group
        psl = slice(p * LANES, (p + 1) * LANES)
        xs_p = xbc[:, psl]
        dt_pair = jnp.where(lane_lo, dt[:, 2 * p:2 * p + 1], dt[:, 2 * p + 1:2 * p + 2])
        xdt = (xs_p * dt_pair).astype(BF16)
        s_old = s_scr[p]
        yd, upd, e_col, e_last = [], [], [], []
        for par in range(2):
            hd = 2 * p + par
            col = a_cum[:, hd:hd + 1]
            rowv = a_cum_t[hd:hd + 1, :]
            last = a_cum_t[hd:hd + 1, Q - 1:Q]
            seg = jnp.exp(jnp.where(tri, col - rowv, NEG_BIG))
            yd.append(_dot((cb_g[g] * seg).astype(BF16), xdt))
            upd.append(_dot((bm_t * jnp.exp(last - rowv)).astype(BF16), xdt))
            e_col.append(jnp.exp(col))
            e_last.append(jnp.exp(last))
        y_off = _dot(cm_b, s_old.astype(BF16)) * jnp.where(lane_lo, e_col[0], e_col[1])
        s_new = (jnp.where(lane_lo, e_last[0], e_last[1]) * s_old
                 + jnp.where(rows_g[g], jnp.where(lane_lo, upd[0], upd[1]), 0.0))
        s_scr[p] = s_new
        y_p = jnp.where(lane_lo, yd[0], yd[1]) + y_off + dsk_ref[:, psl] * xs_p
        z_p = zx_ref[:, psl]
        y_pairs.append(y_p * (z_p * _sigmoid(z_p)))

    for g in range(SSD_GROUPS):
        ps = range(g * pairs_per_group, (g + 1) * pairs_per_group)
        ss = sum(jnp.sum(y_pairs[p] * y_pairs[p], axis=-1, keepdims=True) for p in ps)
        inv = lax.rsqrt(ss * (1.0 / (pairs_per_group * LANES)) + EPS)
        for p in ps:
            psl = slice(p * LANES, (p + 1) * LANES)
            o_ref[:, psl] = (y_pairs[p] * inv * ng_ref[:, psl]).astype(BF16)


def _ssd(l, zx, small, params, bsz, seq):
    n_rows = zx.shape[0]
    R = SSD_STEP_ROWS
    steps_per_seq = seq // R
    meta_block = n_rows // R - 1
    n_fill = _n_fill_blocks(n_rows, bsz, seq, R)

    def block(b, c):
        first = _first_step_block(b, meta_block, n_fill, b * steps_per_seq)
        return (jnp.where(c == 0, first, b * steps_per_seq + c - 1), 0)

    return pl.pallas_call(
        functools.partial(_ssd_kernel, n_fill),
        grid=(bsz, steps_per_seq + 1),
        in_specs=[pl.BlockSpec((R, C_ZX), block), pl.BlockSpec((R, LANES), block)]
                 + [_layer(p, l) for p in params],
        out_specs=pl.BlockSpec((R, SSD_D), block),
        out_shape=jax.ShapeDtypeStruct((n_rows, SSD_D), BF16),
        scratch_shapes=[pltpu.VMEM((R + 8, SSD_CONV_DIM), F32),
                        pltpu.VMEM((SSD_HEADS // 2, LANES, LANES), F32),
                        pltpu.VMEM((8, SSD_CONV_DIM), F32),
                        pltpu.VMEM((SSD_HEADS // 2, LANES, LANES), F32)],
        compiler_params=pltpu.CompilerParams(
            dimension_semantics=("arbitrary", "arbitrary"), vmem_limit_bytes=VMEM_LIMIT),
        name="ssd_mixer",
    )(zx, small, *params)


def _fox_keys_kernel(km_ref, k_ref, sm_ref, s_ref, fb_ref, om_ref, o_ref):
    T = BLOCK
    width = k_ref.shape[1]
    row = lax.broadcasted_iota(jnp.int32, (T, 1), 0)
    tri = (lax.broadcasted_iota(jnp.int32, (T, T), 0)
           >= lax.broadcasted_iota(jnp.int32, (T, T), 1))
    tri_b = jnp.where(tri, 1.0, 0.0).astype(BF16)
    src = lax.broadcasted_iota(jnp.int32, (LANES, width), 0)
    dst = lax.broadcasted_iota(jnp.int32, (LANES, width), 1)
    dst_head = jnp.right_shift(dst, HEAD_W.bit_length() - 1)
    dst_lane = jnp.bitwise_and(dst, HEAD_W - 1)
    sel = jnp.concatenate(
        [jnp.where((src == SMALL_F + dst_head) & (dst_lane == FOX_BIAS_LANE + i),
                   1.0, 0.0).astype(BF16) for i in range(3)], axis=0)

    def local_cumsum(small_blk, is_meta):
        log_f = -_softplus(-(small_blk + fb_ref[...]))
        if is_meta:
            log_f = jnp.where(row < PAD, 0.0, log_f)
        parts = _dot(tri_b, jnp.concatenate(_split3(log_f), axis=1))
        return parts[:, :LANES] + parts[:, LANES:2 * LANES] + parts[:, 2 * LANES:]

    def keys(c_blk, k_blk):
        pieces = jnp.concatenate(_split3(c_blk * (-LOG2E)), axis=1)
        return (k_blk.astype(F32) + _dot(pieces, sel)).astype(BF16)

    n_blocks = k_ref.shape[0] // T
    rows = [slice(j * T, (j + 1) * T) for j in range(n_blocks)]
    c_meta = local_cumsum(sm_ref[...], True)
    local = [local_cumsum(s_ref[rows[j], :], False) for j in range(n_blocks)]
    om_ref[...] = keys(c_meta, km_ref[...])
    carry = c_meta[T - 1:T, :]
    for j in range(n_blocks):
        c_blk = local[j] + carry
        carry = c_blk[T - 1:T, :]
        o_ref[rows[j], :] = keys(c_blk, k_ref[rows[j], :])


def _fox_keys(l, k, small, fb, bsz, seq):
    n_rows, width = k.shape
    meta_block = n_rows // BLOCK - 1
    main = lambda w: pl.BlockSpec((seq, w), lambda b: (b, 0))
    meta = lambda w: pl.BlockSpec((BLOCK, w), lambda b: (meta_block, 0))
    return pl.pallas_call(
        _fox_keys_kernel,
        grid=(bsz,),
        in_specs=[meta(width), main(width), meta(LANES), main(LANES), _layer(fb, l)],
        out_specs=[pl.BlockSpec((BLOCK, width), lambda b: (0, 0)), main(width)],
        out_shape=[jax.ShapeDtypeStruct((BLOCK, width), BF16),
                   jax.ShapeDtypeStruct((bsz * seq, width), BF16)],
        compiler_params=pltpu.CompilerParams(
            dimension_semantics=("arbitrary",), vmem_limit_bytes=VMEM_LIMIT),
        name="fox_keys",
    )(k, k, small, small, fb)


def _attn_kernel(n_heads, n_fill, q_ref, km_ref, k_ref, vtm_ref, vt_ref, o_ref,
                 acc_scr, s0_scr, s1_scr):
    t = pl.program_id(1)
    b = pl.program_id(0)
    dv = acc_scr.shape[1]

    @pl.when((t == 0) & (b >= 1) & (b <= n_fill))
    def _():
        o_ref[...] = jnp.zeros(o_ref.shape, o_ref.dtype)

    heads = range(n_heads)
    hsl = [slice(hd * HEAD_W, (hd + 1) * HEAD_W) for hd in heads]
    vsl = [slice(hd * dv, (hd + 1) * dv) for hd in heads]
    qs = [q_ref[:, hsl[hd]] for hd in heads]
    ahead = (lax.broadcasted_iota(jnp.int32, (TQ, TQ), 0)
             - lax.broadcasted_iota(jnp.int32, (TQ, TQ), 1))

    def softmax_pv(ss, vts, ms, mask):
        if mask is not None:
            ss = [jnp.where(mask, s, NEG_BIG) for s in ss]
        new_ms = [jnp.maximum(ms[hd], jnp.max(ss[hd], axis=0, keepdims=True)) for hd in heads]
        corrs = [jnp.exp2(ms[hd] - new_ms[hd]) for hd in heads]
        prs = [jnp.exp2(ss[hd] - new_ms[hd]).astype(BF16) for hd in heads]
        for hd in heads:
            acc_scr[hd] = corrs[hd] * acc_scr[hd] + _dot(vts[hd], prs[hd])
        return tuple(new_ms)

    def meta_chunk(carry, mask):
        return softmax_pv([_dot_nt(km_ref[PAD:, hsl[hd]], qs[hd]) for hd in heads],
                          [vtm_ref[0, vsl[hd], PAD:] for hd in heads], carry, mask)

    def scores(c, slot_scr):
        r0 = pl.multiple_of(c * TQ, TQ)
        for hd in heads:
            slot_scr[hd] = _dot_nt(k_ref[pl.ds(r0, TQ), hsl[hd]], qs[hd])

    def chunk_vt(c, hd):
        n_sub = TQ // BLOCK
        return jnp.concatenate([vt_ref[n_sub * c + i, vsl[hd], :] for i in range(n_sub)], axis=1)

    def consume(c, slot_scr, carry, mask):
        return softmax_pv([slot_scr[hd] for hd in heads], [chunk_vt(c, hd) for hd in heads],
                          carry, mask)

    def finish():
        dv_out = o_ref.shape[1] // n_heads
        ys = [acc_scr[hd, 0:dv_out, :] * (1.0 / acc_scr[hd, dv_out:dv_out + 1, :])
              for hd in heads]
        for p in range(n_heads // 2):
            y_t = jnp.concatenate([ys[2 * p], ys[2 * p + 1]], axis=0)
            o_ref[:, p * LANES:(p + 1) * LANES] = y_t.T.astype(BF16)

    init = tuple(jnp.full((1, TQ), M_INIT, F32) for _ in heads)

    @pl.when((t == 0) & (b == 0))
    def _():
        acc_scr[...] = jnp.zeros(acc_scr.shape, F32)
        meta_chunk(init, ahead[:N_META, :] <= -(TQ - N_META))
        finish()

    @pl.when(t > 0)
    def _():
        scores(0, s0_scr)
        s_meta = [_dot_nt(km_ref[PAD:, hsl[hd]], qs[hd]) for hd in heads]
        scores(jnp.minimum(1, k_ref.shape[0] // TQ - 1), s1_scr)
        diagonal = ahead <= 0

        s_first = [jnp.where(ahead <= (t - 1) * TQ, s0_scr[hd], NEG_BIG) for hd in heads]
        ms = tuple(jnp.maximum(jnp.max(s_first[hd], axis=0, keepdims=True),
                               jnp.max(s_meta[hd], axis=0, keepdims=True)) for hd in heads)
        for hd in heads:
            acc_scr[hd] = (_dot(chunk_vt(0, hd), jnp.exp2(s_first[hd] - ms[hd]).astype(BF16))
                           + _dot(vtm_ref[0, vsl[hd], PAD:],
                                  jnp.exp2(s_meta[hd] - ms[hd]).astype(BF16)))

        @pl.when(t > 1)
        def _():
            def pair(i, cr):
                c = 2 * i + 1
                scores(c + 1, s0_scr)
                cr = consume(c, s1_scr, cr, None)
                scores(c + 2, s1_scr)
                return consume(c + 1, s0_scr, cr, None)

            n_unmasked = t - 2
            carry = lax.fori_loop(0, n_unmasked // 2, pair, ms)
            c = 2 * (n_unmasked // 2) + 1

            def odd_tail(cr):
                scores(c + 1, s0_scr)
                cr = consume(c, s1_scr, cr, None)
                return consume(c + 1, s0_scr, cr, diagonal)

            lax.cond(n_unmasked % 2 == 1, odd_tail,
                     lambda cr: consume(c, s1_scr, cr, diagonal), carry)

        finish()


def _attention(q, k_meta, k_main, vt, n_heads, bsz, seq, meta_k_block, name):
    n_rows, qw = q.shape
    dv_aug = vt.shape[1] // n_heads
    dv = dv_aug - (DV_AUG - MLA_V)
    tiles_per_seq = seq // TQ
    meta_q_tile = n_rows // TQ - 1
    meta_block = n_rows // BLOCK - 1
    n_fill = _n_fill_blocks(n_rows, bsz, seq, TQ)

    def q_tile(b, t):
        first = _first_step_block(b, meta_q_tile, n_fill, b * tiles_per_seq)
        return (jnp.where(t == 0, first, b * tiles_per_seq + t - 1), 0)

    return pl.pallas_call(
        functools.partial(_attn_kernel, n_heads, n_fill),
        grid=(bsz, tiles_per_seq + 1),
        in_specs=[pl.BlockSpec((TQ, qw), q_tile),
                  pl.BlockSpec((BLOCK, qw), lambda b, t: (meta_k_block, 0)),
                  pl.BlockSpec((seq, qw), lambda b, t: (b, 0)),
                  pl.BlockSpec((1,) + vt.shape[1:], lambda b, t: (meta_block, 0, 0)),
                  pl.BlockSpec((seq // BLOCK,) + vt.shape[1:], lambda b, t: (b, 0, 0))],
        out_specs=pl.BlockSpec((TQ, n_heads * dv), q_tile),
        out_shape=jax.ShapeDtypeStruct((n_rows, n_heads * dv), BF16),
        scratch_shapes=[pltpu.VMEM((n_heads, dv_aug, TQ), F32),
                        pltpu.VMEM((n_heads, TQ, TQ), F32),
                        pltpu.VMEM((n_heads, TQ, TQ), F32)],
        compiler_params=pltpu.CompilerParams(
            dimension_semantics=("arbitrary", "arbitrary"), vmem_limit_bytes=VMEM_LIMIT),
        name=name,
    )(q, k_meta, k_main, vt, vt)


def _select_cols(w, cols, transpose=False):
    place = np.zeros((w.shape[-1], len(cols)), np.float32)
    for j, entry in enumerate(cols):
        if entry is not None:
            place[entry[0], j] = entry[1]
    out = jnp.einsum('lkn,nm->lmk' if transpose else 'lkn,nm->lkm', w.astype(BF16),
                     jnp.asarray(place, BF16), preferred_element_type=F32)
    return out.astype(BF16)


def _span(start, n, sign=1.0):
    return [(start + i, sign) for i in range(n)]


def _rot_half_span(start):
    half = MLA_ROPE // 2
    return _span(start + half, half, -1.0) + _span(start, half)


def _in_proj_weights(w_in):
    sizes = [SSD_D, SSD_CONV_DIM, SSD_HEADS, FOX_D, FOX_D, FOX_D, FOX_HEADS,
             MLA_Q_LORA, MLA_KV_LORA, MLA_ROPE]
    z, xbc, dt, fq, fk, fv, fr, cq, ckv, kr = [int(o) - C_ZX for o in
                                               np.cumsum([0] + sizes[:-1])]
    del z, xbc
    misc = (_span(dt, SSD_HEADS) + _span(fr, FOX_HEADS)
            + [None] * (MISC_KRR - SMALL_F - FOX_HEADS) + _rot_half_span(kr)
            + _span(kr, MLA_ROPE) + [None] * (LANES - MISC_KR - MLA_ROPE))
    rest = w_in[..., C_ZX:]
    w_rest = _select_cols(rest, _span(fq, FOX_D) + _span(fk, FOX_D) + _span(cq, MLA_Q_LORA)
                          + _span(ckv, MLA_KV_LORA) + misc)
    return w_in[..., :C_ZX].astype(BF16), w_rest, _select_cols(rest, _span(fv, FOX_D), True)


def _mla_weights(w_uq, w_ukv):
    qd, kvd = MLA_NOPE + MLA_ROPE, MLA_NOPE + MLA_V
    pad_q = [None] * (HEAD_W - qd)
    wuq = _select_cols(w_uq, sum([_span(h * qd, qd) + pad_q for h in range(MLA_HEADS)], []))
    wuqr = _select_cols(w_uq, sum([[None] * MLA_NOPE + _rot_half_span(h * qd + MLA_NOPE) + pad_q
                                   for h in range(MLA_HEADS)], []))
    wkk = _select_cols(w_ukv, sum([_span(h * kvd, MLA_NOPE) + [None] * (HEAD_W - MLA_NOPE)
                                   for h in range(MLA_HEADS)], []))
    wvt = _select_cols(w_ukv, sum([_span(h * kvd + MLA_NOPE, MLA_V) for h in range(MLA_HEADS)],
                                  []), True)
    return wuq, wuqr, wkk, wvt


def _rows(v, width=None, offset=0):
    v = v.astype(F32)
    if width is not None:
        v = jnp.pad(v, ((0, 0), (offset, width - offset - v.shape[-1])))
    return v[:, None, :]


def _position_tables(seq, tm):
    pos = jnp.concatenate([N_META + jnp.arange(seq, dtype=F32),
                           jnp.arange(tm, dtype=F32) - (tm - N_META)])
    inv_freq = 1.0 / (ROPE_THETA ** (jnp.arange(0, MLA_ROPE, 2, dtype=F32) / MLA_ROPE))
    ang = pos[:, None] * inv_freq[None, :]
    cos, sin = jnp.cos(ang), jnp.sin(ang)
    n = seq + tm
    rope_pad = LANES - MLA_NOPE - MLA_ROPE
    cos128 = jnp.concatenate([jnp.ones((n, MLA_NOPE), F32), cos, cos,
                              jnp.zeros((n, rope_pad), F32)], axis=-1)
    sin128 = jnp.concatenate([jnp.zeros((n, MLA_NOPE), F32), sin, sin,
                              jnp.zeros((n, rope_pad), F32)], axis=-1)
    cosk128 = jnp.concatenate([jnp.zeros((n, MLA_NOPE), F32), cos, cos,
                               jnp.zeros((n, rope_pad), F32)], axis=-1)
    return cos128, sin128, cosk128


def kernel(x, meta, ffn1_w_gate, ffn1_w_up, ffn1_w_down, ln1_g, ln1_b, w_in, conv_w, conv_b, dt_bias, a_log, d_skip, ssd_norm_g, fox_f_b, mla_q_norm_g, mla_w_uq, mla_kv_norm_g, mla_w_ukv, w_out, ln2_g, ln2_b, ffn2_w_gate, ffn2_w_up, ffn2_w_down, ln3_g, ln3_b):
    bsz, seq, _ = x.shape
    assert seq % TQ == 0
    tm = 2 * TQ if seq % (2 * TQ) == 0 else TQ
    n_main = bsz * seq
    n_rows = n_main + tm
    meta_block = n_rows // BLOCK - 1

    h = x.reshape(n_main, D_MODEL)
    meta_tile = jnp.concatenate([jnp.zeros((tm - N_META, D_MODEL), x.dtype),
                                 meta.astype(x.dtype)], axis=0)
    tables = _position_tables(seq, tm)
    bf = lambda w: w.astype(BF16)

    ffn1 = (bf(ffn1_w_gate), bf(ffn1_w_up), bf(ffn1_w_down), _rows(ln1_g), _rows(ln1_b))
    ffn2 = (bf(ffn2_w_gate), bf(ffn2_w_up), bf(ffn2_w_down), _rows(ln3_g), _rows(ln3_b))
    proj_w = (*_in_proj_weights(w_in), *_mla_weights(mla_w_uq, mla_w_ukv),
              _rows(mla_q_norm_g), _rows(mla_kv_norm_g))
    ssd_p = (conv_w.astype(F32), _rows(conv_b), _rows(dt_bias, LANES),
             _rows(-jnp.exp(a_log.astype(F32)), LANES),
             _rows(jnp.repeat(d_skip, SSD_HEAD_DIM, axis=-1)), _rows(ssd_norm_g))
    fb = _rows(fox_f_b, LANES, SMALL_F)
    out_p = (bf(w_out), _rows(ln2_g), _rows(ln2_b))

    for l in range(DEPTH):
        h, zx, small, fq, fk, fvt, mq, mk, mvt = _ffn_ln(
            l, h, ffn1, tm, n_rows, meta_tile=meta_tile if l == 0 else None,
            proj=(proj_w, tables, seq))
        y_ssd = _ssd(l, zx, small, ssd_p, bsz, seq)
        fk_meta, fk_main = _fox_keys(l, fk, small, fb, bsz, seq)
        y_fox = _attention(fq, fk_meta, fk_main, fvt, FOX_HEADS, bsz, seq, 0, "fox_attention")
        y_mla = _attention(mq, mk, mk, mvt, MLA_HEADS, bsz, seq, meta_block, "mla_attention")
        h = _ffn_ln(l, h, ffn2, tm, n_rows if l < DEPTH - 1 else n_main,
                    mix=(y_ssd, y_fox, y_mla, *out_p))

    return h.reshape(bsz, seq, D_MODEL)
```

```python
import functools

import numpy as np
import jax
import jax.numpy as jnp
from jax import lax
from jax.experimental import pallas as pl
from jax.experimental.pallas import tpu as pltpu

F32 = jnp.float32
BF16 = jnp.bfloat16

D_MODEL = 1024
DEPTH = 2
N_META = 16
BLOCK = 128
SSD_HEADS = 8
SSD_HEAD_DIM = 64
SSD_D = SSD_HEADS * SSD_HEAD_DIM
SSD_GROUPS = 2
SSD_STATE = 64
SSD_CONV = 4
SSD_CONV_DIM = SSD_D + 2 * SSD_GROUPS * SSD_STATE
FOX_HEADS = 4
FOX_HEAD_DIM = 64
FOX_D = FOX_HEADS * FOX_HEAD_DIM
MLA_HEADS = 4
MLA_Q_LORA = 256
MLA_KV_LORA = 128
MLA_NOPE = 64
MLA_ROPE = 32
MLA_V = 64
MLA_D = MLA_HEADS * MLA_V
ROPE_THETA = 10000.0
D_MIX = SSD_D + FOX_D + MLA_D
D_FF = 2816
ALPHA = (2 * DEPTH) ** 0.25
EPS = 1e-5

LANES = 128
MXU_W = 256
FF_CHUNK = MXU_W
N_FF_CHUNKS = D_FF // FF_CHUNK
N_OUT_CHUNKS = D_MODEL // FF_CHUNK
TQ = 2 * BLOCK
SSD_STEP_ROWS = 2 * BLOCK
PAD = BLOCK - N_META
NEG_BIG = -1e30
M_INIT = 2 * NEG_BIG
VMEM_LIMIT = 56 * 1024 * 1024

C_XBC = SSD_D
C_ZX = SSD_D + SSD_CONV_DIM
C_FQ = 0
C_CQ = C_FQ + 2 * FOX_D
C_MISC = C_CQ + MLA_Q_LORA
N_IN_ARR = C_MISC + MLA_KV_LORA + LANES
MISC_KRR = 32
MISC_KR = MLA_NOPE
LOG2E = 1.4426950408889634
DV_AUG = MLA_V + 16
SMALL_DT = 0
SMALL_F = 8
HEAD_W = LANES
FOX_BIAS_LANE = FOX_HEAD_DIM


def _sigmoid(x):
    return 1.0 / (1.0 + jnp.exp(-x))


def _softplus(x):
    return jnp.maximum(x, 0.0) + jnp.log(1.0 + jnp.exp(-jnp.abs(x)))


def _layer_norm_rows(y, g, b):
    mu = jnp.mean(y, axis=-1, keepdims=True)
    yc = y - mu
    var = jnp.mean(yc * yc, axis=-1, keepdims=True)
    return yc * lax.rsqrt(var + EPS) * g + b


def _split3(x):
    x1 = x.astype(BF16)
    r1 = x - x1.astype(F32)
    x2 = r1.astype(BF16)
    r2 = r1 - x2.astype(F32)
    return x1, x2, r2.astype(BF16)


def _dot(a, b):
    return jnp.dot(a, b, preferred_element_type=F32)


def _dot_nt(a, b):
    return lax.dot_general(a, b, (((1,), (1,)), ((), ())), preferred_element_type=F32)


def _resident(shape):
    return pl.BlockSpec(shape, lambda *_: (0,) * len(shape), pipeline_mode=pl.Buffered(1))


def _layer(arr, l):
    tail = (0,) * (arr.ndim - 1)
    return pl.BlockSpec((None,) + arr.shape[1:], lambda *_: (l,) + tail,
                        pipeline_mode=pl.Buffered(1))


N_PROJ_IN = 12
N_PROJ_OUT = 8


def _ffn_ln_kernel(prologue, n_main_tiles, with_proj, *refs):
    a_scr, y_scr = refs[-2:]
    refs = refs[:-2]
    if with_proj:
        proj_out, refs = refs[-N_PROJ_OUT:], refs[:-N_PROJ_OUT]
        o_ref = refs[-1]
        proj_in, refs = refs[-1 - N_PROJ_IN:-1], refs[:-1 - N_PROJ_IN]
    else:
        o_ref, refs = refs[-1], refs[:-1]
    wg_ref, wu_ref, wd_ref, g_ref, b_ref = refs[-5:]
    if prologue == "plain":
        x = refs[0][...]
    elif prologue == "meta":
        x_ref, m_ref = refs[:2]
        x = jnp.where(pl.program_id(0) < n_main_tiles, x_ref[...], m_ref[...])
    else:
        h_ref, ys_ref, yf_ref, ym_ref, wo_ref, g2_ref, b2_ref = refs[:7]
        mix = (_dot(ys_ref[...], wo_ref[0:SSD_D, :])
               + _dot(yf_ref[...], wo_ref[SSD_D:SSD_D + FOX_D, :])
               + _dot(ym_ref[...], wo_ref[SSD_D + FOX_D:D_MIX, :]))
        x = _layer_norm_rows(ALPHA * h_ref[...] + mix, g2_ref[...], b2_ref[...])
    xb = x.astype(BF16)

    for c in range(N_FF_CHUNKS):
        cols = slice(c * FF_CHUNK, (c + 1) * FF_CHUNK)
        gate = _dot(xb, wg_ref[:, cols])
        up = _dot(xb, wu_ref[:, cols])
        a_scr[c] = (gate * _sigmoid(gate) * up).astype(BF16)

    for n in range(N_OUT_CHUNKS):
        cols = slice(n * FF_CHUNK, (n + 1) * FF_CHUNK)
        acc = _dot(a_scr[0], wd_ref[0:FF_CHUNK, cols])
        for c in range(1, N_FF_CHUNKS):
            acc = acc + _dot(a_scr[c], wd_ref[c * FF_CHUNK:(c + 1) * FF_CHUNK, cols])
        y_scr[:, cols] = ALPHA * x[:, cols] + 0.5 * acc
    out = _layer_norm_rows(y_scr[...], g_ref[...], b_ref[...])
    o_ref[...] = out
    if with_proj:
        _proj_body(out, *proj_in, *proj_out)


def _ffn_ln(l, h, ffn, tm, n_out_rows, meta_tile=None, mix=None, proj=None):
    def rows(width):
        return pl.BlockSpec((tm, width), lambda i: (i, 0))

    out_specs = rows(D_MODEL)
    out_shape = jax.ShapeDtypeStruct((n_out_rows, D_MODEL), F32)
    p_args, p_specs = [], []
    if proj is not None:
        p_args, p_specs, p_out_specs, p_out_shape = _proj_specs(l, n_out_rows, tm, *proj)
        out_specs, out_shape = [out_specs] + p_out_specs, [out_shape] + p_out_shape

    if mix is not None:
        prologue, n_main = "mix", None
        x_args = [h, *mix]
        x_specs = ([rows(D_MODEL), rows(SSD_D), rows(FOX_D), rows(MLA_D)]
                   + [_layer(p, l) for p in mix[3:]])
    elif meta_tile is not None:
        prologue, n_main = "meta", h.shape[0] // tm
        x_args = [h, meta_tile]
        x_specs = [pl.BlockSpec((tm, D_MODEL), lambda i: (jnp.minimum(i, n_main - 1), 0)),
                   _resident(meta_tile.shape)]
    else:
        prologue, n_main, x_args, x_specs = "plain", None, [h], [rows(D_MODEL)]
    return pl.pallas_call(
        functools.partial(_ffn_ln_kernel, prologue, n_main, proj is not None),
        grid=(n_out_rows // tm,),
        in_specs=x_specs + [_layer(p, l) for p in ffn] + p_specs,
        out_specs=out_specs,
        out_shape=out_shape,
        scratch_shapes=[pltpu.VMEM((N_FF_CHUNKS, tm, FF_CHUNK), BF16),
                        pltpu.VMEM((tm, D_MODEL), F32)],
        compiler_params=pltpu.CompilerParams(
            dimension_semantics=("arbitrary",), vmem_limit_bytes=VMEM_LIMIT),
        name="ffn_ln_proj" if proj is not None else "ffn_ln",
    )(*x_args, *ffn, *p_args)


def _rms_rows(x, g):
    return x * lax.rsqrt(jnp.mean(x * x, axis=-1, keepdims=True) + EPS) * g


def _store_vt_blocks(out_ref, val_t, n_heads):
    dv = val_t.shape[0] // n_heads
    ones = jnp.ones((DV_AUG - dv, BLOCK), out_ref.dtype)
    for r in range(out_ref.shape[0]):
        for hd in range(n_heads):
            out_ref[r, hd * DV_AUG:hd * DV_AUG + dv, :] = (
                val_t[hd * dv:(hd + 1) * dv, r * BLOCK:(r + 1) * BLOCK].astype(out_ref.dtype))
            out_ref[r, hd * DV_AUG + dv:(hd + 1) * DV_AUG, :] = ones


def _proj_body(h, wzx_ref, win_ref, wfvt_ref, wuq_ref, wuqr_ref, wkk_ref, wkvt_ref,
               qg_ref, kvg_ref, cos_ref, sin_ref, cosk_ref,
               zx_ref, small_ref, fq_ref, fk_ref, fvt_ref, mq_ref, mk_ref, mvt_ref):
    hb = h.astype(BF16)
    zx_ref[...] = _dot(hb, wzx_ref[...])
    misc = _dot(hb, win_ref[:, C_MISC:N_IN_ARR])
    small = misc[:, MLA_KV_LORA:]
    small_ref[...] = small

    lane = lax.broadcasted_iota(jnp.int32, (1, LANES), 1)
    lane_lo = lane < FOX_HEAD_DIM
    fox_one = jnp.where((lane >= FOX_BIAS_LANE) & (lane < FOX_BIAS_LANE + 3), 1.0, 0.0)

    fqk = _dot(hb, win_ref[:, C_FQ:C_CQ])
    for half, (ref, scale, extra) in enumerate([(fq_ref, FOX_HEAD_DIM ** -0.5 * LOG2E, fox_one),
                                                (fk_ref, None, None)]):
        for pair in range(FOX_HEADS // 2):
            c0 = half * FOX_D + pair * LANES
            both = fqk[:, c0:c0 + LANES]
            if scale is not None:
                both = both * scale
            for par, val in enumerate([both, pltpu.roll(both, FOX_HEAD_DIM, 1)]):
                val = jnp.where(lane_lo, val, 0.0)
                if extra is not None:
                    val = val + extra
                hd = 2 * pair + par
                ref[:, hd * HEAD_W:(hd + 1) * HEAD_W] = val.astype(BF16)
    _store_vt_blocks(fvt_ref, _dot_nt(wfvt_ref[...], hb), FOX_HEADS)

    cos = cos_ref[...]
    sin = sin_ref[...]
    cqn = _rms_rows(_dot(hb, win_ref[:, C_CQ:C_MISC]), qg_ref[...]).astype(BF16)
    q = _dot(cqn, wuq_ref[...])
    qr = _dot(cqn, wuqr_ref[...])
    kvn = _rms_rows(misc[:, :MLA_KV_LORA], kvg_ref[...]).astype(BF16)
    kn = _dot(kvn, wkk_ref[...])
    _store_vt_blocks(mvt_ref, _dot_nt(wkvt_ref[...], kvn), MLA_HEADS)
    krope = small * cosk_ref[...] + pltpu.roll(small, MISC_KR - MISC_KRR, 1) * sin
    scale = (MLA_NOPE + MLA_ROPE) ** -0.5 * LOG2E
    for hd in range(MLA_HEADS):
        sl = slice(hd * HEAD_W, (hd + 1) * HEAD_W)
        mq_ref[:, sl] = ((q[:, sl] * cos + qr[:, sl] * sin) * scale).astype(BF16)
        mk_ref[:, sl] = (kn[:, sl] + krope).astype(BF16)


def _proj_specs(l, n_rows, tm, weights, tables, seq):
    n_main_tiles = n_rows // tm - 1
    tiles_per_seq = seq // tm

    def rows(width):
        return pl.BlockSpec((tm, width), lambda i: (i, 0))

    tab = pl.BlockSpec((tm, LANES), lambda i: (
        jnp.where(i < n_main_tiles, i % tiles_per_seq, tiles_per_seq), 0))
    vt_rows = FOX_HEADS * DV_AUG
    t_spec = pl.BlockSpec((tm // BLOCK, vt_rows, BLOCK), lambda i: (i, 0, 0))
    t_shape = jax.ShapeDtypeStruct((n_rows // BLOCK, vt_rows, BLOCK), BF16)
    qk_w = FOX_HEADS * HEAD_W
    row_outs = [(C_ZX, F32), (LANES, F32), (qk_w, BF16), (qk_w, BF16)]
    args = [*weights, *tables]
    in_specs = [_layer(w, l) for w in weights] + [tab, tab, tab]
    out_specs = [rows(w) for w, _ in row_outs] + [t_spec, rows(qk_w), rows(qk_w), t_spec]
    out_shape = ([jax.ShapeDtypeStruct((n_rows, w), dt) for w, dt in row_outs]
                 + [t_shape, jax.ShapeDtypeStruct((n_rows, qk_w), BF16),
                    jax.ShapeDtypeStruct((n_rows, qk_w), BF16), t_shape])
    assert len(args) == N_PROJ_IN and len(out_shape) == N_PROJ_OUT
    return args, in_specs, out_specs, out_shape


def _first_step_block(b, meta_block, n_fill, own_first_block):
    return jnp.where(b == 0, meta_block,
                     jnp.where(b <= n_fill, meta_block - b, own_first_block))


def _n_fill_blocks(n_rows, bsz, seq, block_rows):
    n_fill = (n_rows - bsz * seq) // block_rows - 1
    assert bsz > n_fill, "needs one batch row per meta-tile block to zero-fill"
    return n_fill


def _ssd_kernel(n_fill, zx_ref, small_ref, cw_ref, cb_ref, dtb_ref, a_ref, dsk_ref, ng_ref,
                o_ref, conv_scr, s_scr, meta_conv_scr, meta_s_scr):
    b = pl.program_id(0)
    c = pl.program_id(1)

    @pl.when((c == 0) & (b >= 1) & (b <= n_fill))
    def _():
        o_ref[...] = jnp.zeros(o_ref.shape, o_ref.dtype)

    @pl.when((c == 0) & (b == 0))
    def _():
        conv_scr[0:8, :] = jnp.zeros((8, SSD_CONV_DIM), F32)
        s_scr[...] = jnp.zeros(s_scr.shape, F32)

    @pl.when((c == 0) & (b > 0))
    def _():
        conv_scr[0:8, :] = meta_conv_scr[...]
        s_scr[...] = meta_s_scr[...]

    @pl.when((c > 0) | (b == 0))
    def _():
        _ssd_step(c, zx_ref, small_ref, cw_ref, cb_ref, dtb_ref, a_ref, dsk_ref, ng_ref,
                  o_ref, conv_scr, s_scr)

    @pl.when((c == 0) & (b == 0))
    def _():
        meta_conv_scr[...] = conv_scr[0:8, :]
        meta_s_scr[...] = s_scr[...]


def _ssd_step(c, zx_ref, small_ref, cw_ref, cb_ref, dtb_ref, a_ref, dsk_ref, ng_ref,
              o_ref, conv_scr, s_scr):
    R = SSD_STEP_ROWS
    row = lax.broadcasted_iota(jnp.int32, (R, 1), 0)
    valid = jnp.logical_or(c > 0, row >= R - N_META)

    conv_scr[8:8 + R, :] = jnp.where(valid, zx_ref[:, C_XBC:C_ZX], 0.0)
    acc = cb_ref[...]
    for k in range(SSD_CONV):
        off = 8 - (SSD_CONV - 1) + k
        acc = acc + cw_ref[k:k + 1, :] * conv_scr[off:off + R, :]
    conv_scr[0:8, :] = conv_scr[R:R + 8, :]
    xbc = acc * _sigmoid(acc)
    dt = jnp.where(valid, _softplus(small_ref[...] + dtb_ref[...]), 0.0)
    for ch in range(R // BLOCK):
        rows = slice(ch * BLOCK, (ch + 1) * BLOCK)
        _ssd_chunk(xbc[rows], dt[rows], zx_ref.at[rows], a_ref, dsk_ref, ng_ref,
                   o_ref.at[rows], s_scr)


def _ssd_chunk(xbc, dt, zx_ref, a_ref, dsk_ref, ng_ref, o_ref, s_scr):
    Q = BLOCK
    lane = lax.broadcasted_iota(jnp.int32, (1, LANES), 1)
    lane_lo = lane < SSD_HEAD_DIM
    sub = lax.broadcasted_iota(jnp.int32, (LANES, 1), 0)
    bm = xbc[:, SSD_D:SSD_D + LANES]
    cm = xbc[:, SSD_D + LANES:SSD_D + 2 * LANES]

    a = dt * a_ref[...]
    tri = (lax.broadcasted_iota(jnp.int32, (Q, Q), 0)
           >= lax.broadcasted_iota(jnp.int32, (Q, Q), 1))
    tri_b = jnp.where(tri, 1.0, 0.0).astype(BF16)
    a1, a2, a3 = _split3(a)
    a_cum = _dot(tri_b, a1) + _dot(tri_b, a2) + _dot(tri_b, a3)
    a_cum_t = a_cum.T
    bm_t = bm.T

    cm_b = cm.astype(BF16)
    bm_b = bm.astype(BF16)
    cb_g = [_dot_nt(jnp.where(lane_lo, cm, 0.0).astype(BF16), bm_b),
            _dot_nt(jnp.where(lane_lo, 0.0, cm).astype(BF16), bm_b)]
    rows_g = [sub < SSD_STATE, sub >= SSD_STATE]

    pairs_per_group = SSD_HEADS // 2 // SSD_GROUPS
    y_pairs = []
    for p in range(SSD_HEADS // 2):
        g = p // pairs_per_group
        psl = slice(p * LANES, (p + 1) * LANES)
        xs_p = xbc[:, psl]
        dt_pair = jnp.where(lane_lo, dt[:, 2 * p:2 * p + 1], dt[:, 2 * p + 1:2 * p + 2])
        xdt = (xs_p * dt_pair).astype(BF16)
        s_old = s_scr[p]
        yd, upd, e_col, e_last = [], [], [], []
        for par in range(2):
            hd = 2 * p + par
            col = a_cum[:, hd:hd + 1]
            rowv = a_cum_t[hd:hd + 1, :]
            last = a_cum_t[hd:hd + 1, Q - 1:Q]
            seg = jnp.exp(jnp.where(tri, col - rowv, NEG_BIG))
            yd.append(_dot((cb_g[g] * seg).astype(BF16), xdt))
            upd.append(_dot((bm_t * jnp.exp(last - rowv)).astype(BF16), xdt))
            e_col.append(jnp.exp(col))
            e_last.append(jnp.exp(last))
        y_off = _dot(cm_b, s_old.astype(BF16)) * jnp.where(lane_lo, e_col[0], e_col[1])
        s_new = (jnp.where(lane_lo, e_last[0], e_last[1]) * s_old
                 + jnp.where(rows_g[g], jnp.where(lane_lo, upd[0], upd[1]), 0.0))
        s_scr[p] = s_new
        y_p = jnp.where(lane_lo, yd[0], yd[1]) + y_off + dsk_ref[:, psl] * xs_p
        z_p = zx_ref[:, psl]
        y_pairs.append(y_p * (z_p * _sigmoid(z_p)))

    for g in range(SSD_GROUPS):
        ps = range(g * pairs_per_group, (g + 1) * pairs_per_group)
        ss = sum(jnp.sum(y_pairs[p] * y_pairs[p], axis=-1, keepdims=True) for p in ps)
        inv = lax.rsqrt(ss * (1.0 / (pairs_per_group * LANES)) + EPS)
        for p in ps:
            psl = slice(p * LANES, (p + 1) * LANES)
            o_ref[:, psl] = (y_pairs[p] * inv * ng_ref[:, psl]).astype(BF16)


def _ssd(l, zx, small, params, bsz, seq):
    n_rows = zx.shape[0]
    R = SSD_STEP_ROWS
    steps_per_seq = seq // R
    meta_block = n_rows // R - 1
    n_fill = _n_fill_blocks(n_rows, bsz, seq, R)

    def block(b, c):
        first = _first_step_block(b, meta_block, n_fill, b * steps_per_seq)
        return (jnp.where(c == 0, first, b * steps_per_seq + c - 1), 0)

    return pl.pallas_call(
        functools.partial(_ssd_kernel, n_fill),
        grid=(bsz, steps_per_seq + 1),
        in_specs=[pl.BlockSpec((R, C_ZX), block), pl.BlockSpec((R, LANES), block)]
                 + [_layer(p, l) for p in params],
        out_specs=pl.BlockSpec((R, SSD_D), block),
        out_shape=jax.ShapeDtypeStruct((n_rows, SSD_D), BF16),
        scratch_shapes=[pltpu.VMEM((R + 8, SSD_CONV_DIM), F32),
                        pltpu.VMEM((SSD_HEADS // 2, LANES, LANES), F32),
                        pltpu.VMEM((8, SSD_CONV_DIM), F32),
                        pltpu.VMEM((SSD_HEADS // 2, LANES, LANES), F32)],
        compiler_params=pltpu.CompilerParams(
            dimension_semantics=("arbitrary", "arbitrary"), vmem_limit_bytes=VMEM_LIMIT),
        name="ssd_mixer",
    )(zx, small, *params)


def _fox_keys_kernel(km_ref, k_ref, sm_ref, s_ref, fb_ref, om_ref, o_ref):
    T = BLOCK
    width = k_ref.shape[1]
    row = lax.broadcasted_iota(jnp.int32, (T, 1), 0)
    tri = (lax.broadcasted_iota(jnp.int32, (T, T), 0)
           >= lax.broadcasted_iota(jnp.int32, (T, T), 1))
    tri_b = jnp.where(tri, 1.0, 0.0).astype(BF16)
    src = lax.broadcasted_iota(jnp.int32, (LANES, width), 0)
    dst = lax.broadcasted_iota(jnp.int32, (LANES, width), 1)
    dst_head = jnp.right_shift(dst, HEAD_W.bit_length() - 1)
    dst_lane = jnp.bitwise_and(dst, HEAD_W - 1)
    sel = jnp.concatenate(
        [jnp.where((src == SMALL_F + dst_head) & (dst_lane == FOX_BIAS_LANE + i),
                   1.0, 0.0).astype(BF16) for i in range(3)], axis=0)

    def local_cumsum(small_blk, is_meta):
        log_f = -_softplus(-(small_blk + fb_ref[...]))
        if is_meta:
            log_f = jnp.where(row < PAD, 0.0, log_f)
        parts = _dot(tri_b, jnp.concatenate(_split3(log_f), axis=1))
        return parts[:, :LANES] + parts[:, LANES:2 * LANES] + parts[:, 2 * LANES:]

    def keys(c_blk, k_blk):
        pieces = jnp.concatenate(_split3(c_blk * (-LOG2E)), axis=1)
        return (k_blk.astype(F32) + _dot(pieces, sel)).astype(BF16)

    n_blocks = k_ref.shape[0] // T
    rows = [slice(j * T, (j + 1) * T) for j in range(n_blocks)]
    c_meta = local_cumsum(sm_ref[...], True)
    local = [local_cumsum(s_ref[rows[j], :], False) for j in range(n_blocks)]
    om_ref[...] = keys(c_meta, km_ref[...])
    carry = c_meta[T - 1:T, :]
    for j in range(n_blocks):
        c_blk = local[j] + carry
        carry = c_blk[T - 1:T, :]
        o_ref[rows[j], :] = keys(c_blk, k_ref[rows[j], :])


def _fox_keys(l, k, small, fb, bsz, seq):
    n_rows, width = k.shape
    meta_block = n_rows // BLOCK - 1
    main = lambda w: pl.BlockSpec((seq, w), lambda b: (b, 0))
    meta = lambda w: pl.BlockSpec((BLOCK, w), lambda b: (meta_block, 0))
    return pl.pallas_call(
        _fox_keys_kernel,
        grid=(bsz,),
        in_specs=[meta(width), main(width), meta(LANES), main(LANES), _layer(fb, l)],
        out_specs=[pl.BlockSpec((BLOCK, width), lambda b: (0, 0)), main(width)],
        out_shape=[jax.ShapeDtypeStruct((BLOCK, width), BF16),
                   jax.ShapeDtypeStruct((bsz * seq, width), BF16)],
        compiler_params=pltpu.CompilerParams(
            dimension_semantics=("arbitrary",), vmem_limit_bytes=VMEM_LIMIT),
        name="fox_keys",
    )(k, k, small, small, fb)


def _attn_kernel(n_heads, n_fill, q_ref, km_ref, k_ref, vtm_ref, vt_ref, o_ref,
                 acc_scr, s0_scr, s1_scr):
    t = pl.program_id(1)
    b = pl.program_id(0)
    dv = acc_scr.shape[1]

    @pl.when((t == 0) & (b >= 1) & (b <= n_fill))
    def _():
        o_ref[...] = jnp.zeros(o_ref.shape, o_ref.dtype)

    heads = range(n_heads)
    hsl = [slice(hd * HEAD_W, (hd + 1) * HEAD_W) for hd in heads]
    vsl = [slice(hd * dv, (hd + 1) * dv) for hd in heads]
    qs = [q_ref[:, hsl[hd]] for hd in heads]
    ahead = (lax.broadcasted_iota(jnp.int32, (TQ, TQ), 0)
             - lax.broadcasted_iota(jnp.int32, (TQ, TQ), 1))

    def softmax_pv(ss, vts, ms, mask):
        if mask is not None:
            ss = [jnp.where(mask, s, NEG_BIG) for s in ss]
        new_ms = [jnp.maximum(ms[hd], jnp.max(ss[hd], axis=0, keepdims=True)) for hd in heads]
        corrs = [jnp.exp2(ms[hd] - new_ms[hd]) for hd in heads]
        prs = [jnp.exp2(ss[hd] - new_ms[hd]).astype(BF16) for hd in heads]
        for hd in heads:
            acc_scr[hd] = corrs[hd] * acc_scr[hd] + _dot(vts[hd], prs[hd])
        return tuple(new_ms)

    def meta_chunk(carry, mask):
        return softmax_pv([_dot_nt(km_ref[PAD:, hsl[hd]], qs[hd]) for hd in heads],
                          [vtm_ref[0, vsl[hd], PAD:] for hd in heads], carry, mask)

    def scores(c, slot_scr):
        r0 = pl.multiple_of(c * TQ, TQ)
        for hd in heads:
            slot_scr[hd] = _dot_nt(k_ref[pl.ds(r0, TQ), hsl[hd]], qs[hd])

    def chunk_vt(c, hd):
        n_sub = TQ // BLOCK
        return jnp.concatenate([vt_ref[n_sub * c + i, vsl[hd], :] for i in range(n_sub)], axis=1)

    def consume(c, slot_scr, carry, mask):
        return softmax_pv([slot_scr[hd] for hd in heads], [chunk_vt(c, hd) for hd in heads],
                          carry, mask)

    def finish():
        dv_out = o_ref.shape[1] // n_heads
        ys = [acc_scr[hd, 0:dv_out, :] * (1.0 / acc_scr[hd, dv_out:dv_out + 1, :])
              for hd in heads]
        for p in range(n_heads // 2):
            y_t = jnp.concatenate([ys[2 * p], ys[2 * p + 1]], axis=0)
            o_ref[:, p * LANES:(p + 1) * LANES] = y_t.T.astype(BF16)

    init = tuple(jnp.full((1, TQ), M_INIT, F32) for _ in heads)

    @pl.when((t == 0) & (b == 0))
    def _():
        acc_scr[...] = jnp.zeros(acc_scr.shape, F32)
        meta_chunk(init, ahead[:N_META, :] <= -(TQ - N_META))
        finish()

    @pl.when(t > 0)
    def _():
        scores(0, s0_scr)
        s_meta = [_dot_nt(km_ref[PAD:, hsl[hd]], qs[hd]) for hd in heads]
        scores(jnp.minimum(1, k_ref.shape[0] // TQ - 1), s1_scr)
        diagonal = ahead <= 0

        s_first = [jnp.where(ahead <= (t - 1) * TQ, s0_scr[hd], NEG_BIG) for hd in heads]
        ms = tuple(jnp.maximum(jnp.max(s_first[hd], axis=0, keepdims=True),
                               jnp.max(s_meta[hd], axis=0, keepdims=True)) for hd in heads)
        for hd in heads:
            acc_scr[hd] = (_dot(chunk_vt(0, hd), jnp.exp2(s_first[hd] - ms[hd]).astype(BF16))
                           + _dot(vtm_ref[0, vsl[hd], PAD:],
                                  jnp.exp2(s_meta[hd] - ms[hd]).astype(BF16)))

        @pl.when(t > 1)
        def _():
            def pair(i, cr):
                c = 2 * i + 1
                scores(c + 1, s0_scr)
                cr = consume(c, s1_scr, cr, None)
                scores(c + 2, s1_scr)
                return consume(c + 1, s0_scr, cr, None)

            n_unmasked = t - 2
            carry = lax.fori_loop(0, n_unmasked // 2, pair, ms)
            c = 2 * (n_unmasked // 2) + 1

            def odd_tail(cr):
                scores(c + 1, s0_scr)
                cr = consume(c, s1_scr, cr, None)
                return consume(c + 1, s0_scr, cr, diagonal)

            lax.cond(n_unmasked % 2 == 1, odd_tail,
                     lambda cr: consume(c, s1_scr, cr, diagonal), carry)

        finish()


def _attention(q, k_meta, k_main, vt, n_heads, bsz, seq, meta_k_block, name):
    n_rows, qw = q.shape
    dv_aug = vt.shape[1] // n_heads
    dv = dv_aug - (DV_AUG - MLA_V)
    tiles_per_seq = seq // TQ
    meta_q_tile = n_rows // TQ - 1
    meta_block = n_rows // BLOCK - 1
    n_fill = _n_fill_blocks(n_rows, bsz, seq, TQ)

    def q_tile(b, t):
        first = _first_step_block(b, meta_q_tile, n_fill, b * tiles_per_seq)
        return (jnp.where(t == 0, first, b * tiles_per_seq + t - 1), 0)

    return pl.pallas_call(
        functools.partial(_attn_kernel, n_heads, n_fill),
        grid=(bsz, tiles_per_seq + 1),
        in_specs=[pl.BlockSpec((TQ, qw), q_tile),
                  pl.BlockSpec((BLOCK, qw), lambda b, t: (meta_k_block, 0)),
                  pl.BlockSpec((seq, qw), lambda b, t: (b, 0)),
                  pl.BlockSpec((1,) + vt.shape[1:], lambda b, t: (meta_block, 0, 0)),
                  pl.BlockSpec((seq // BLOCK,) + vt.shape[1:], lambda b, t: (b, 0, 0))],
        out_specs=pl.BlockSpec((TQ, n_heads * dv), q_tile),
        out_shape=jax.ShapeDtypeStruct((n_rows, n_heads * dv), BF16),
        scratch_shapes=[pltpu.VMEM((n_heads, dv_aug, TQ), F32),
                        pltpu.VMEM((n_heads, TQ, TQ), F32),
                        pltpu.VMEM((n_heads, TQ, TQ), F32)],
        compiler_params=pltpu.CompilerParams(
            dimension_semantics=("arbitrary", "arbitrary"), vmem_limit_bytes=VMEM_LIMIT),
        name=name,
    )(q, k_meta, k_main, vt, vt)


def _rot_half_cols(w):
    half = MLA_ROPE // 2
    return jnp.concatenate([-w[..., half:], w[..., :half]], axis=-1)


def _in_proj_weights(w_in):
    sizes = [SSD_D, SSD_CONV_DIM, SSD_HEADS, FOX_D, FOX_D, FOX_D, FOX_HEADS,
             MLA_Q_LORA, MLA_KV_LORA, MLA_ROPE]
    splits = [int(s) - C_ZX for s in np.cumsum(sizes)[2:-1]]
    rest = w_in[..., C_ZX:].astype(BF16)
    dt, fq, fk, fv, fr, cq, ckv, kr = jnp.split(rest, splits, axis=-1)
    zeros = lambda n: jnp.zeros(rest.shape[:-1] + (n,), rest.dtype)
    misc = jnp.concatenate([dt, fr, zeros(MISC_KRR - SMALL_F - FOX_HEADS), _rot_half_cols(kr),
                            kr, zeros(LANES - MISC_KR - MLA_ROPE)], axis=-1)
    w_rest = jnp.concatenate([fq, fk, cq, ckv, misc], axis=-1)
    return w_in[..., :C_ZX].astype(BF16), w_rest, jnp.swapaxes(fv, -1, -2)


def _mla_weights(w_uq, w_ukv):
    rope_pad = LANES - MLA_NOPE - MLA_ROPE
    lead = w_uq.shape[:-1]
    wq = w_uq.astype(BF16).reshape(lead + (MLA_HEADS, MLA_NOPE + MLA_ROPE))
    zq = jnp.zeros(lead + (MLA_HEADS, rope_pad), BF16)
    wuq = jnp.concatenate([wq, zq], axis=-1).reshape(lead + (MLA_HEADS * LANES,))
    wuqr = jnp.concatenate([jnp.zeros_like(wq[..., :MLA_NOPE]),
                            _rot_half_cols(wq[..., MLA_NOPE:]), zq],
                           axis=-1).reshape(lead + (MLA_HEADS * LANES,))
    lead = w_ukv.shape[:-1]
    wkv = w_ukv.astype(BF16).reshape(lead + (MLA_HEADS, MLA_NOPE + MLA_V))
    wkk = jnp.concatenate([wkv[..., :MLA_NOPE], jnp.zeros_like(wkv[..., MLA_NOPE:])],
                          axis=-1).reshape(lead + (MLA_HEADS * LANES,))
    wvt = jnp.swapaxes(wkv[..., MLA_NOPE:].reshape(lead + (MLA_D,)), -1, -2)
    return wuq, wuqr, wkk, wvt


def _rows(v, width=None, offset=0):
    v = v.astype(F32)
    if width is not None:
        v = jnp.pad(v, ((0, 0), (offset, width - offset - v.shape[-1])))
    return v[:, None, :]


def _position_tables(seq, tm):
    pos = jnp.concatenate([N_META + jnp.arange(seq, dtype=F32),
                           jnp.arange(tm, dtype=F32) - (tm - N_META)])
    inv_freq = 1.0 / (ROPE_THETA ** (jnp.arange(0, MLA_ROPE, 2, dtype=F32) / MLA_ROPE))
    ang = pos[:, None] * inv_freq[None, :]
    cos, sin = jnp.cos(ang), jnp.sin(ang)
    n = seq + tm
    rope_pad = LANES - MLA_NOPE - MLA_ROPE
    cos128 = jnp.concatenate([jnp.ones((n, MLA_NOPE), F32), cos, cos,
                              jnp.zeros((n, rope_pad), F32)], axis=-1)
    sin128 = jnp.concatenate([jnp.zeros((n, MLA_NOPE), F32), sin, sin,
                              jnp.zeros((n, rope_pad), F32)], axis=-1)
    cosk128 = jnp.concatenate([jnp.zeros((n, MLA_NOPE), F32), cos, cos,
                               jnp.zeros((n, rope_pad), F32)], axis=-1)
    return cos128, sin128, cosk128


def kernel(x, meta, ffn1_w_gate, ffn1_w_up, ffn1_w_down, ln1_g, ln1_b, w_in, conv_w, conv_b, dt_bias, a_log, d_skip, ssd_norm_g, fox_f_b, mla_q_norm_g, mla_w_uq, mla_kv_norm_g, mla_w_ukv, w_out, ln2_g, ln2_b, ffn2_w_gate, ffn2_w_up, ffn2_w_down, ln3_g, ln3_b):
    bsz, seq, _ = x.shape
    assert seq % TQ == 0
    tm = 2 * TQ if seq % (2 * TQ) == 0 else TQ
    n_main = bsz * seq
    n_rows = n_main + tm
    meta_block = n_rows // BLOCK - 1

    h = x.reshape(n_main, D_MODEL)
    meta_tile = jnp.concatenate([jnp.zeros((tm - N_META, D_MODEL), x.dtype),
                                 meta.astype(x.dtype)], axis=0)
    tables = _position_tables(seq, tm)
    bf = lambda w: w.astype(BF16)

    ffn1 = (bf(ffn1_w_gate), bf(ffn1_w_up), bf(ffn1_w_down), _rows(ln1_g), _rows(ln1_b))
    ffn2 = (bf(ffn2_w_gate), bf(ffn2_w_up), bf(ffn2_w_down), _rows(ln3_g), _rows(ln3_b))
    proj_w = (*_in_proj_weights(w_in), *_mla_weights(mla_w_uq, mla_w_ukv),
              _rows(mla_q_norm_g), _rows(mla_kv_norm_g))
    ssd_p = (conv_w.astype(F32), _rows(conv_b), _rows(dt_bias, LANES),
             _rows(-jnp.exp(a_log.astype(F32)), LANES),
             _rows(jnp.repeat(d_skip, SSD_HEAD_DIM, axis=-1)), _rows(ssd_norm_g))
    fb = _rows(fox_f_b, LANES, SMALL_F)
    out_p = (bf(w_out), _rows(ln2_g), _rows(ln2_b))

    for l in range(DEPTH):
        h, zx, small, fq, fk, fvt, mq, mk, mvt = _ffn_ln(
            l, h, ffn1, tm, n_rows, meta_tile=meta_tile if l == 0 else None,
            proj=(proj_w, tables, seq))
        y_ssd = _ssd(l, zx, small, ssd_p, bsz, seq)
        fk_meta, fk_main = _fox_keys(l, fk, small, fb, bsz, seq)
        y_fox = _attention(fq, fk_meta, fk_main, fvt, FOX_HEADS, bsz, seq, 0, "fox_attention")
        y_mla = _attention(mq, mk, mk, mvt, MLA_HEADS, bsz, seq, meta_block, "mla_attention")
        h = _ffn_ln(l, h, ffn2, tm, n_rows if l < DEPTH - 1 else n_main,
                    mix=(y_ssd, y_fox, y_mla, *out_p))

    return h.reshape(bsz, seq, D_MODEL)
```

```python
import functools

import numpy as np
import jax
import jax.numpy as jnp
from jax import lax
from jax.experimental import pallas as pl
from jax.experimental.pallas import tpu as pltpu

F32 = jnp.float32
BF16 = jnp.bfloat16

D_MODEL = 1024
DEPTH = 2
N_META = 16
BLOCK = 128
SSD_HEADS = 8
SSD_HEAD_DIM = 64
SSD_D = SSD_HEADS * SSD_HEAD_DIM
SSD_GROUPS = 2
SSD_STATE = 64
SSD_CONV = 4
SSD_CONV_DIM = SSD_D + 2 * SSD_GROUPS * SSD_STATE
FOX_HEADS = 4
FOX_HEAD_DIM = 64
FOX_D = FOX_HEADS * FOX_HEAD_DIM
MLA_HEADS = 4
MLA_Q_LORA = 256
MLA_KV_LORA = 128
MLA_NOPE = 64
MLA_ROPE = 32
MLA_V = 64
MLA_D = MLA_HEADS * MLA_V
ROPE_THETA = 10000.0
D_MIX = SSD_D + FOX_D + MLA_D
D_FF = 2816
ALPHA = (2 * DEPTH) ** 0.25
EPS = 1e-5

LANES = 128
MXU_W = 256
FF_CHUNK = MXU_W
N_FF_CHUNKS = D_FF // FF_CHUNK
N_OUT_CHUNKS = D_MODEL // FF_CHUNK
TQ = 2 * BLOCK
SSD_STEP_ROWS = 2 * BLOCK
PAD = BLOCK - N_META
NEG_BIG = -1e30
M_INIT = 2 * NEG_BIG
VMEM_LIMIT = 56 * 1024 * 1024

C_XBC = SSD_D
C_ZX = SSD_D + SSD_CONV_DIM
C_FQ = 0
C_CQ = C_FQ + 2 * FOX_D
C_MISC = C_CQ + MLA_Q_LORA
N_IN_ARR = C_MISC + MLA_KV_LORA + LANES
MISC_KRR = 32
MISC_KR = MLA_NOPE
LOG2E = 1.4426950408889634
DV_AUG = MLA_V + 16
SMALL_DT = 0
SMALL_F = 8
HEAD_W = LANES
FOX_BIAS_LANE = FOX_HEAD_DIM


def _sigmoid(x):
    return 1.0 / (1.0 + jnp.exp(-x))


def _softplus(x):
    return jnp.maximum(x, 0.0) + jnp.log(1.0 + jnp.exp(-jnp.abs(x)))


def _layer_norm_rows(y, g, b):
    mu = jnp.mean(y, axis=-1, keepdims=True)
    yc = y - mu
    var = jnp.mean(yc * yc, axis=-1, keepdims=True)
    return yc * lax.rsqrt(var + EPS) * g + b


def _split3(x):
    x1 = x.astype(BF16)
    r1 = x - x1.astype(F32)
    x2 = r1.astype(BF16)
    r2 = r1 - x2.astype(F32)
    return x1, x2, r2.astype(BF16)


def _dot(a, b):
    return jnp.dot(a, b, preferred_element_type=F32)


def _dot_nt(a, b):
    return lax.dot_general(a, b, (((1,), (1,)), ((), ())), preferred_element_type=F32)


def _resident(shape):
    return pl.BlockSpec(shape, lambda *_: (0,) * len(shape), pipeline_mode=pl.Buffered(1))


def _layer(arr, l):
    tail = (0,) * (arr.ndim - 1)
    return pl.BlockSpec((None,) + arr.shape[1:], lambda *_: (l,) + tail,
                        pipeline_mode=pl.Buffered(1))


N_PROJ_IN = 12
N_PROJ_OUT = 8


def _ffn_ln_kernel(prologue, n_main_tiles, with_proj, *refs):
    a_scr, y_scr = refs[-2:]
    refs = refs[:-2]
    if with_proj:
        proj_out, refs = refs[-N_PROJ_OUT:], refs[:-N_PROJ_OUT]
        o_ref = refs[-1]
        proj_in, refs = refs[-1 - N_PROJ_IN:-1], refs[:-1 - N_PROJ_IN]
    else:
        o_ref, refs = refs[-1], refs[:-1]
    wg_ref, wu_ref, wd_ref, g_ref, b_ref = refs[-5:]
    if prologue == "plain":
        x = refs[0][...]
    elif prologue == "meta":
        x_ref, m_ref = refs[:2]
        x = jnp.where(pl.program_id(0) < n_main_tiles, x_ref[...], m_ref[...])
    else:
        h_ref, ys_ref, yf_ref, ym_ref, wo_ref, g2_ref, b2_ref = refs[:7]
        mix = (_dot(ys_ref[...], wo_ref[0:SSD_D, :])
               + _dot(yf_ref[...], wo_ref[SSD_D:SSD_D + FOX_D, :])
               + _dot(ym_ref[...], wo_ref[SSD_D + FOX_D:D_MIX, :]))
        x = _layer_norm_rows(ALPHA * h_ref[...] + mix, g2_ref[...], b2_ref[...])
    xb = x.astype(BF16)

    for c in range(N_FF_CHUNKS):
        cols = slice(c * FF_CHUNK, (c + 1) * FF_CHUNK)
        gate = _dot(xb, wg_ref[:, cols])
        up = _dot(xb, wu_ref[:, cols])
        a_scr[c] = (gate * _sigmoid(gate) * up).astype(BF16)

    tm = x.shape[0]
    outs = []
    for rows in (slice(0, tm // 2), slice(tm // 2, tm)):
        for n in range(N_OUT_CHUNKS):
            cols = slice(n * FF_CHUNK, (n + 1) * FF_CHUNK)
            acc = _dot(a_scr[0, rows], wd_ref[0:FF_CHUNK, cols])
            for c in range(1, N_FF_CHUNKS):
                acc = acc + _dot(a_scr[c, rows], wd_ref[c * FF_CHUNK:(c + 1) * FF_CHUNK, cols])
            y_scr[rows, cols] = ALPHA * x[rows, cols] + 0.5 * acc
        outs.append(_layer_norm_rows(y_scr[rows, :], g_ref[...], b_ref[...]))
        o_ref[rows, :] = outs[-1]
    if with_proj:
        _proj_body(jnp.concatenate(outs, axis=0), *proj_in, *proj_out)


def _ffn_ln(l, h, ffn, tm, n_out_rows, meta_tile=None, mix=None, proj=None):
    def rows(width):
        return pl.BlockSpec((tm, width), lambda i: (i, 0))

    out_specs = rows(D_MODEL)
    out_shape = jax.ShapeDtypeStruct((n_out_rows, D_MODEL), F32)
    p_args, p_specs = [], []
    if proj is not None:
        p_args, p_specs, p_out_specs, p_out_shape = _proj_specs(l, n_out_rows, tm, *proj)
        out_specs, out_shape = [out_specs] + p_out_specs, [out_shape] + p_out_shape

    if mix is not None:
        prologue, n_main = "mix", None
        x_args = [h, *mix]
        x_specs = ([rows(D_MODEL), rows(SSD_D), rows(FOX_D), rows(MLA_D)]
                   + [_layer(p, l) for p in mix[3:]])
    elif meta_tile is not None:
        prologue, n_main = "meta", h.shape[0] // tm
        x_args = [h, meta_tile]
        x_specs = [pl.BlockSpec((tm, D_MODEL), lambda i: (jnp.minimum(i, n_main - 1), 0)),
                   _resident(meta_tile.shape)]
    else:
        prologue, n_main, x_args, x_specs = "plain", None, [h], [rows(D_MODEL)]
    return pl.pallas_call(
        functools.partial(_ffn_ln_kernel, prologue, n_main, proj is not None),
        grid=(n_out_rows // tm,),
        in_specs=x_specs + [_layer(p, l) for p in ffn] + p_specs,
        out_specs=out_specs,
        out_shape=out_shape,
        scratch_shapes=[pltpu.VMEM((N_FF_CHUNKS, tm, FF_CHUNK), BF16),
                        pltpu.VMEM((tm, D_MODEL), F32)],
        compiler_params=pltpu.CompilerParams(
            dimension_semantics=("arbitrary",), vmem_limit_bytes=VMEM_LIMIT),
        name="ffn_ln_proj" if proj is not None else "ffn_ln",
    )(*x_args, *ffn, *p_args)


def _rms_rows(x, g):
    return x * lax.rsqrt(jnp.mean(x * x, axis=-1, keepdims=True) + EPS) * g


def _store_vt_blocks(out_ref, val_t, n_heads):
    dv = val_t.shape[0] // n_heads
    ones = jnp.ones((DV_AUG - dv, BLOCK), out_ref.dtype)
    for r in range(out_ref.shape[0]):
        for hd in range(n_heads):
            out_ref[r, hd * DV_AUG:hd * DV_AUG + dv, :] = (
                val_t[hd * dv:(hd + 1) * dv, r * BLOCK:(r + 1) * BLOCK].astype(out_ref.dtype))
            out_ref[r, hd * DV_AUG + dv:(hd + 1) * DV_AUG, :] = ones


def _proj_body(h, wzx_ref, win_ref, wfvt_ref, wuq_ref, wuqr_ref, wkk_ref, wkvt_ref,
               qg_ref, kvg_ref, cos_ref, sin_ref, cosk_ref,
               zx_ref, small_ref, fq_ref, fk_ref, fvt_ref, mq_ref, mk_ref, mvt_ref):
    hb = h.astype(BF16)
    zx_ref[...] = _dot(hb, wzx_ref[...])
    misc = _dot(hb, win_ref[:, C_MISC:N_IN_ARR])
    small = misc[:, MLA_KV_LORA:]
    small_ref[...] = small

    lane = lax.broadcasted_iota(jnp.int32, (1, LANES), 1)
    lane_lo = lane < FOX_HEAD_DIM
    fox_one = jnp.where((lane >= FOX_BIAS_LANE) & (lane < FOX_BIAS_LANE + 3), 1.0, 0.0)

    fqk = _dot(hb, win_ref[:, C_FQ:C_CQ])
    for half, (ref, scale, extra) in enumerate([(fq_ref, FOX_HEAD_DIM ** -0.5 * LOG2E, fox_one),
                                                (fk_ref, None, None)]):
        for pair in range(FOX_HEADS // 2):
            c0 = half * FOX_D + pair * LANES
            both = fqk[:, c0:c0 + LANES]
            if scale is not None:
                both = both * scale
            for par, val in enumerate([both, pltpu.roll(both, FOX_HEAD_DIM, 1)]):
                val = jnp.where(lane_lo, val, 0.0)
                if extra is not None:
                    val = val + extra
                hd = 2 * pair + par
                ref[:, hd * HEAD_W:(hd + 1) * HEAD_W] = val.astype(BF16)
    _store_vt_blocks(fvt_ref, _dot_nt(wfvt_ref[...], hb), FOX_HEADS)

    cos = cos_ref[...]
    sin = sin_ref[...]
    cqn = _rms_rows(_dot(hb, win_ref[:, C_CQ:C_MISC]), qg_ref[...]).astype(BF16)
    q = _dot(cqn, wuq_ref[...])
    qr = _dot(cqn, wuqr_ref[...])
    kvn = _rms_rows(misc[:, :MLA_KV_LORA], kvg_ref[...]).astype(BF16)
    kn = _dot(kvn, wkk_ref[...])
    _store_vt_blocks(mvt_ref, _dot_nt(wkvt_ref[...], kvn), MLA_HEADS)
    krope = small * cosk_ref[...] + pltpu.roll(small, MISC_KR - MISC_KRR, 1) * sin
    scale = (MLA_NOPE + MLA_ROPE) ** -0.5 * LOG2E
    for hd in range(MLA_HEADS):
        sl = slice(hd * HEAD_W, (hd + 1) * HEAD_W)
        mq_ref[:, sl] = ((q[:, sl] * cos + qr[:, sl] * sin) * scale).astype(BF16)
        mk_ref[:, sl] = (kn[:, sl] + krope).astype(BF16)


def _proj_specs(l, n_rows, tm, weights, tables, seq):
    n_main_tiles = n_rows // tm - 1
    tiles_per_seq = seq // tm

    def rows(width):
        return pl.BlockSpec((tm, width), lambda i: (i, 0))

    tab = pl.BlockSpec((tm, LANES), lambda i: (
        jnp.where(i < n_main_tiles, i % tiles_per_seq, tiles_per_seq), 0))
    vt_rows = FOX_HEADS * DV_AUG
    t_spec = pl.BlockSpec((tm // BLOCK, vt_rows, BLOCK), lambda i: (i, 0, 0))
    t_shape = jax.ShapeDtypeStruct((n_rows // BLOCK, vt_rows, BLOCK), BF16)
    qk_w = FOX_HEADS * HEAD_W
    row_outs = [(C_ZX, F32), (LANES, F32), (qk_w, BF16), (qk_w, BF16)]
    args = [*weights, *tables]
    in_specs = [_layer(w, l) for w in weights] + [tab, tab, tab]
    out_specs = [rows(w) for w, _ in row_outs] + [t_spec, rows(qk_w), rows(qk_w), t_spec]
    out_shape = ([jax.ShapeDtypeStruct((n_rows, w), dt) for w, dt in row_outs]
                 + [t_shape, jax.ShapeDtypeStruct((n_rows, qk_w), BF16),
                    jax.ShapeDtypeStruct((n_rows, qk_w), BF16), t_shape])
    assert len(args) == N_PROJ_IN and len(out_shape) == N_PROJ_OUT
    return args, in_specs, out_specs, out_shape


def _first_step_block(b, meta_block, n_fill, own_first_block):
    return jnp.where(b == 0, meta_block,
                     jnp.where(b <= n_fill, meta_block - b, own_first_block))


def _n_fill_blocks(n_rows, bsz, seq, block_rows):
    n_fill = (n_rows - bsz * seq) // block_rows - 1
    assert bsz > n_fill, "needs one batch row per meta-tile block to zero-fill"
    return n_fill


def _ssd_kernel(n_fill, zx_ref, small_ref, cw_ref, cb_ref, dtb_ref, a_ref, dsk_ref, ng_ref,
                o_ref, conv_scr, s_scr, meta_conv_scr, meta_s_scr):
    b = pl.program_id(0)
    c = pl.program_id(1)

    @pl.when((c == 0) & (b >= 1) & (b <= n_fill))
    def _():
        o_ref[...] = jnp.zeros(o_ref.shape, o_ref.dtype)

    @pl.when((c == 0) & (b == 0))
    def _():
        conv_scr[0:8, :] = jnp.zeros((8, SSD_CONV_DIM), F32)
        s_scr[...] = jnp.zeros(s_scr.shape, F32)

    @pl.when((c == 0) & (b > 0))
    def _():
        conv_scr[0:8, :] = meta_conv_scr[...]
        s_scr[...] = meta_s_scr[...]

    @pl.when((c > 0) | (b == 0))
    def _():
        _ssd_step(c, zx_ref, small_ref, cw_ref, cb_ref, dtb_ref, a_ref, dsk_ref, ng_ref,
                  o_ref, conv_scr, s_scr)

    @pl.when((c == 0) & (b == 0))
    def _():
        meta_conv_scr[...] = conv_scr[0:8, :]
        meta_s_scr[...] = s_scr[...]


def _ssd_step(c, zx_ref, small_ref, cw_ref, cb_ref, dtb_ref, a_ref, dsk_ref, ng_ref,
              o_ref, conv_scr, s_scr):
    R = SSD_STEP_ROWS
    row = lax.broadcasted_iota(jnp.int32, (R, 1), 0)
    valid = jnp.logical_or(c > 0, row >= R - N_META)

    conv_scr[8:8 + R, :] = jnp.where(valid, zx_ref[:, C_XBC:C_ZX], 0.0)
    acc = cb_ref[...]
    for k in range(SSD_CONV):
        off = 8 - (SSD_CONV - 1) + k
        acc = acc + cw_ref[k:k + 1, :] * conv_scr[off:off + R, :]
    conv_scr[0:8, :] = conv_scr[R:R + 8, :]
    xbc = acc * _sigmoid(acc)
    dt = jnp.where(valid, _softplus(small_ref[...] + dtb_ref[...]), 0.0)
    for ch in range(R // BLOCK):
        rows = slice(ch * BLOCK, (ch + 1) * BLOCK)
        _ssd_chunk(xbc[rows], dt[rows], zx_ref.at[rows], a_ref, dsk_ref, ng_ref,
                   o_ref.at[rows], s_scr)


def _ssd_chunk(xbc, dt, zx_ref, a_ref, dsk_ref, ng_ref, o_ref, s_scr):
    Q = BLOCK
    lane = lax.broadcasted_iota(jnp.int32, (1, LANES), 1)
    lane_lo = lane < SSD_HEAD_DIM
    sub = lax.broadcasted_iota(jnp.int32, (LANES, 1), 0)
    bm = xbc[:, SSD_D:SSD_D + LANES]
    cm = xbc[:, SSD_D + LANES:SSD_D + 2 * LANES]

    a = dt * a_ref[...]
    tri = (lax.broadcasted_iota(jnp.int32, (Q, Q), 0)
           >= lax.broadcasted_iota(jnp.int32, (Q, Q), 1))
    tri_b = jnp.where(tri, 1.0, 0.0).astype(BF16)
    a1, a2, a3 = _split3(a)
    a_cum = _dot(tri_b, a1) + _dot(tri_b, a2) + _dot(tri_b, a3)
    a_cum_t = a_cum.T
    bm_t = bm.T

    cm_b = cm.astype(BF16)
    bm_b = bm.astype(BF16)
    cb_g = [_dot_nt(jnp.where(lane_lo, cm, 0.0).astype(BF16), bm_b),
            _dot_nt(jnp.where(lane_lo, 0.0, cm).astype(BF16), bm_b)]
    rows_g = [sub < SSD_STATE, sub >= SSD_STATE]

    pairs_per_group = SSD_HEADS // 2 // SSD_GROUPS
    y_pairs = []
    for p in range(SSD_HEADS // 2):
        g = p // pairs_per_group
        psl = slice(p * LANES, (p + 1) * LANES)
        xs_p = xbc[:, psl]
        dt_pair = jnp.where(lane_lo, dt[:, 2 * p:2 * p + 1], dt[:, 2 * p + 1:2 * p + 2])
        xdt = (xs_p * dt_pair).astype(BF16)
        s_old = s_scr[p]
        yd, upd, e_col, e_last = [], [], [], []
        for par in range(2):
            hd = 2 * p + par
            col = a_cum[:, hd:hd + 1]
            rowv = a_cum_t[hd:hd + 1, :]
            last = a_cum_t[hd:hd + 1, Q - 1:Q]
            seg = jnp.exp(jnp.where(tri, col - rowv, NEG_BIG))
            yd.append(_dot((cb_g[g] * seg).astype(BF16), xdt))
            upd.append(_dot((bm_t * jnp.exp(last - rowv)).astype(BF16), xdt))
            e_col.append(jnp.exp(col))
            e_last.append(jnp.exp(last))
        y_off = _dot(cm_b, s_old.astype(BF16)) * jnp.where(lane_lo, e_col[0], e_col[1])
        s_new = (jnp.where(lane_lo, e_last[0], e_last[1]) * s_old
                 + jnp.where(rows_g[g], jnp.where(lane_lo, upd[0], upd[1]), 0.0))
        s_scr[p] = s_new
        y_p = jnp.where(lane_lo, yd[0], yd[1]) + y_off + dsk_ref[:, psl] * xs_p
        z_p = zx_ref[:, psl]
        y_pairs.append(y_p * (z_p * _sigmoid(z_p)))

    for g in range(SSD_GROUPS):
        ps = range(g * pairs_per_group, (g + 1) * pairs_per_group)
        ss = sum(jnp.sum(y_pairs[p] * y_pairs[p], axis=-1, keepdims=True) for p in ps)
        inv = lax.rsqrt(ss * (1.0 / (pairs_per_group * LANES)) + EPS)
        for p in ps:
            psl = slice(p * LANES, (p + 1) * LANES)
            o_ref[:, psl] = (y_pairs[p] * inv * ng_ref[:, psl]).astype(BF16)


def _ssd(l, zx, small, params, bsz, seq):
    n_rows = zx.shape[0]
    R = SSD_STEP_ROWS
    steps_per_seq = seq // R
    meta_block = n_rows // R - 1
    n_fill = _n_fill_blocks(n_rows, bsz, seq, R)

    def block(b, c):
        first = _first_step_block(b, meta_block, n_fill, b * steps_per_seq)
        return (jnp.where(c == 0, first, b * steps_per_seq + c - 1), 0)

    return pl.pallas_call(
        functools.partial(_ssd_kernel, n_fill),
        grid=(bsz, steps_per_seq + 1),
        in_specs=[pl.BlockSpec((R, C_ZX), block), pl.BlockSpec((R, LANES), block)]
                 + [_layer(p, l) for p in params],
        out_specs=pl.BlockSpec((R, SSD_D), block),
        out_shape=jax.ShapeDtypeStruct((n_rows, SSD_D), BF16),
        scratch_shapes=[pltpu.VMEM((R + 8, SSD_CONV_DIM), F32),
                        pltpu.VMEM((SSD_HEADS // 2, LANES, LANES), F32),
                        pltpu.VMEM((8, SSD_CONV_DIM), F32),
                        pltpu.VMEM((SSD_HEADS // 2, LANES, LANES), F32)],
        compiler_params=pltpu.CompilerParams(
            dimension_semantics=("arbitrary", "arbitrary"), vmem_limit_bytes=VMEM_LIMIT),
        name="ssd_mixer",
    )(zx, small, *params)


def _fox_keys_kernel(km_ref, k_ref, sm_ref, s_ref, fb_ref, om_ref, o_ref):
    T = BLOCK
    width = k_ref.shape[1]
    row = lax.broadcasted_iota(jnp.int32, (T, 1), 0)
    tri = (lax.broadcasted_iota(jnp.int32, (T, T), 0)
           >= lax.broadcasted_iota(jnp.int32, (T, T), 1))
    tri_b = jnp.where(tri, 1.0, 0.0).astype(BF16)
    src = lax.broadcasted_iota(jnp.int32, (LANES, width), 0)
    dst = lax.broadcasted_iota(jnp.int32, (LANES, width), 1)
    dst_head = jnp.right_shift(dst, HEAD_W.bit_length() - 1)
    dst_lane = jnp.bitwise_and(dst, HEAD_W - 1)
    sel = jnp.concatenate(
        [jnp.where((src == SMALL_F + dst_head) & (dst_lane == FOX_BIAS_LANE + i),
                   1.0, 0.0).astype(BF16) for i in range(3)], axis=0)

    def local_cumsum(small_blk, is_meta):
        log_f = -_softplus(-(small_blk + fb_ref[...]))
        if is_meta:
            log_f = jnp.where(row < PAD, 0.0, log_f)
        parts = _dot(tri_b, jnp.concatenate(_split3(log_f), axis=1))
        return parts[:, :LANES] + parts[:, LANES:2 * LANES] + parts[:, 2 * LANES:]

    def keys(c_blk, k_blk):
        pieces = jnp.concatenate(_split3(c_blk * (-LOG2E)), axis=1)
        return (k_blk.astype(F32) + _dot(pieces, sel)).astype(BF16)

    n_blocks = k_ref.shape[0] // T
    rows = [slice(j * T, (j + 1) * T) for j in range(n_blocks)]
    c_meta = local_cumsum(sm_ref[...], True)
    local = [local_cumsum(s_ref[rows[j], :], False) for j in range(n_blocks)]
    om_ref[...] = keys(c_meta, km_ref[...])
    carry = c_meta[T - 1:T, :]
    for j in range(n_blocks):
        c_blk = local[j] + carry
        carry = c_blk[T - 1:T, :]
        o_ref[rows[j], :] = keys(c_blk, k_ref[rows[j], :])


def _fox_keys(l, k, small, fb, bsz, seq):
    n_rows, width = k.shape
    meta_block = n_rows // BLOCK - 1
    main = lambda w: pl.BlockSpec((seq, w), lambda b: (b, 0))
    meta = lambda w: pl.BlockSpec((BLOCK, w), lambda b: (meta_block, 0))
    return pl.pallas_call(
        _fox_keys_kernel,
        grid=(bsz,),
        in_specs=[meta(width), main(width), meta(LANES), main(LANES), _layer(fb, l)],
        out_specs=[pl.BlockSpec((BLOCK, width), lambda b: (0, 0)), main(width)],
        out_shape=[jax.ShapeDtypeStruct((BLOCK, width), BF16),
                   jax.ShapeDtypeStruct((bsz * seq, width), BF16)],
        compiler_params=pltpu.CompilerParams(
            dimension_semantics=("arbitrary",), vmem_limit_bytes=VMEM_LIMIT),
        name="fox_keys",
    )(k, k, small, small, fb)


def _attn_kernel(n_heads, n_fill, q_ref, km_ref, k_ref, vtm_ref, vt_ref, o_ref,
                 acc_scr, s0_scr, s1_scr):
    t = pl.program_id(1)
    b = pl.program_id(0)
    dv = acc_scr.shape[1]

    @pl.when((t == 0) & (b >= 1) & (b <= n_fill))
    def _():
        o_ref[...] = jnp.zeros(o_ref.shape, o_ref.dtype)

    heads = range(n_heads)
    hsl = [slice(hd * HEAD_W, (hd + 1) * HEAD_W) for hd in heads]
    vsl = [slice(hd * dv, (hd + 1) * dv) for hd in heads]
    qs = [q_ref[:, hsl[hd]] for hd in heads]
    ahead = (lax.broadcasted_iota(jnp.int32, (TQ, TQ), 0)
             - lax.broadcasted_iota(jnp.int32, (TQ, TQ), 1))

    def softmax_pv(ss, vts, ms, mask):
        if mask is not None:
            ss = [jnp.where(mask, s, NEG_BIG) for s in ss]
        new_ms = [jnp.maximum(ms[hd], jnp.max(ss[hd], axis=0, keepdims=True)) for hd in heads]
        corrs = [jnp.exp2(ms[hd] - new_ms[hd]) for hd in heads]
        prs = [jnp.exp2(ss[hd] - new_ms[hd]).astype(BF16) for hd in heads]
        for hd in heads:
            acc_scr[hd] = corrs[hd] * acc_scr[hd] + _dot(vts[hd], prs[hd])
        return tuple(new_ms)

    def meta_chunk(carry, mask):
        return softmax_pv([_dot_nt(km_ref[PAD:, hsl[hd]], qs[hd]) for hd in heads],
                          [vtm_ref[0, vsl[hd], PAD:] for hd in heads], carry, mask)

    def scores(c, slot_scr):
        r0 = pl.multiple_of(c * TQ, TQ)
        for hd in heads:
            slot_scr[hd] = _dot_nt(k_ref[pl.ds(r0, TQ), hsl[hd]], qs[hd])

    def chunk_vt(c, hd):
        n_sub = TQ // BLOCK
        return jnp.concatenate([vt_ref[n_sub * c + i, vsl[hd], :] for i in range(n_sub)], axis=1)

    def consume(c, slot_scr, carry, mask):
        return softmax_pv([slot_scr[hd] for hd in heads], [chunk_vt(c, hd) for hd in heads],
                          carry, mask)

    def finish():
        dv_out = o_ref.shape[1] // n_heads
        ys = [acc_scr[hd, 0:dv_out, :] * (1.0 / acc_scr[hd, dv_out:dv_out + 1, :])
              for hd in heads]
        for p in range(n_heads // 2):
            y_t = jnp.concatenate([ys[2 * p], ys[2 * p + 1]], axis=0)
            o_ref[:, p * LANES:(p + 1) * LANES] = y_t.T.astype(BF16)

    init = tuple(jnp.full((1, TQ), M_INIT, F32) for _ in heads)

    @pl.when((t == 0) & (b == 0))
    def _():
        acc_scr[...] = jnp.zeros(acc_scr.shape, F32)
        meta_chunk(init, ahead[:N_META, :] <= -(TQ - N_META))
        finish()

    @pl.when(t > 0)
    def _():
        scores(0, s0_scr)
        s_meta = [_dot_nt(km_ref[PAD:, hsl[hd]], qs[hd]) for hd in heads]
        scores(jnp.minimum(1, k_ref.shape[0] // TQ - 1), s1_scr)
        diagonal = ahead <= 0

        s_first = [jnp.where(ahead <= (t - 1) * TQ, s0_scr[hd], NEG_BIG) for hd in heads]
        ms = tuple(jnp.maximum(jnp.max(s_first[hd], axis=0, keepdims=True),
                               jnp.max(s_meta[hd], axis=0, keepdims=True)) for hd in heads)
        for hd in heads:
            acc_scr[hd] = (_dot(chunk_vt(0, hd), jnp.exp2(s_first[hd] - ms[hd]).astype(BF16))
                           + _dot(vtm_ref[0, vsl[hd], PAD:],
                                  jnp.exp2(s_meta[hd] - ms[hd]).astype(BF16)))

        @pl.when(t > 1)
        def _():
            def pair(i, cr):
                c = 2 * i + 1
                scores(c + 1, s0_scr)
                cr = consume(c, s1_scr, cr, None)
                scores(c + 2, s1_scr)
                return consume(c + 1, s0_scr, cr, None)

            n_unmasked = t - 2
            carry = lax.fori_loop(0, n_unmasked // 2, pair, ms)
            c = 2 * (n_unmasked // 2) + 1

            def odd_tail(cr):
                scores(c + 1, s0_scr)
                cr = consume(c, s1_scr, cr, None)
                return consume(c + 1, s0_scr, cr, diagonal)

            lax.cond(n_unmasked % 2 == 1, odd_tail,
                     lambda cr: consume(c, s1_scr, cr, diagonal), carry)

        finish()


def _attention(q, k_meta, k_main, vt, n_heads, bsz, seq, meta_k_block, name):
    n_rows, qw = q.shape
    dv_aug = vt.shape[1] // n_heads
    dv = dv_aug - (DV_AUG - MLA_V)
    tiles_per_seq = seq // TQ
    meta_q_tile = n_rows // TQ - 1
    meta_block = n_rows // BLOCK - 1
    n_fill = _n_fill_blocks(n_rows, bsz, seq, TQ)

    def q_tile(b, t):
        first = _first_step_block(b, meta_q_tile, n_fill, b * tiles_per_seq)
        return (jnp.where(t == 0, first, b * tiles_per_seq + t - 1), 0)

    return pl.pallas_call(
        functools.partial(_attn_kernel, n_heads, n_fill),
        grid=(bsz, tiles_per_seq + 1),
        in_specs=[pl.BlockSpec((TQ, qw), q_tile),
                  pl.BlockSpec((BLOCK, qw), lambda b, t: (meta_k_block, 0)),
                  pl.BlockSpec((seq, qw), lambda b, t: (b, 0)),
                  pl.BlockSpec((1,) + vt.shape[1:], lambda b, t: (meta_block, 0, 0)),
                  pl.BlockSpec((seq // BLOCK,) + vt.shape[1:], lambda b, t: (b, 0, 0))],
        out_specs=pl.BlockSpec((TQ, n_heads * dv), q_tile),
        out_shape=jax.ShapeDtypeStruct((n_rows, n_heads * dv), BF16),
        scratch_shapes=[pltpu.VMEM((n_heads, dv_aug, TQ), F32),
                        pltpu.VMEM((n_heads, TQ, TQ), F32),
                        pltpu.VMEM((n_heads, TQ, TQ), F32)],
        compiler_params=pltpu.CompilerParams(
            dimension_semantics=("arbitrary", "arbitrary"), vmem_limit_bytes=VMEM_LIMIT),
        name=name,
    )(q, k_meta, k_main, vt, vt)


def _select_cols(w, cols, transpose=False):
    place = np.zeros((w.shape[-1], len(cols)), np.float32)
    for j, entry in enumerate(cols):
        if entry is not None:
            place[entry[0], j] = entry[1]
    out = jnp.einsum('lkn,nm->lmk' if transpose else 'lkn,nm->lkm',
                     w.astype(BF16).astype(F32), jnp.asarray(place))
    return out.astype(BF16)


def _span(start, n, sign=1.0):
    return [(start + i, sign) for i in range(n)]


def _rot_half_span(start):
    half = MLA_ROPE // 2
    return _span(start + half, half, -1.0) + _span(start, half)


def _in_proj_weights(w_in):
    sizes = [SSD_D, SSD_CONV_DIM, SSD_HEADS, FOX_D, FOX_D, FOX_D, FOX_HEADS,
             MLA_Q_LORA, MLA_KV_LORA, MLA_ROPE]
    z, xbc, dt, fq, fk, fv, fr, cq, ckv, kr = [int(o) - C_ZX for o in
                                               np.cumsum([0] + sizes[:-1])]
    del z, xbc
    misc = (_span(dt, SSD_HEADS) + _span(fr, FOX_HEADS)
            + [None] * (MISC_KRR - SMALL_F - FOX_HEADS) + _rot_half_span(kr)
            + _span(kr, MLA_ROPE) + [None] * (LANES - MISC_KR - MLA_ROPE))
    rest = w_in[..., C_ZX:]
    w_rest = _select_cols(rest, _span(fq, FOX_D) + _span(fk, FOX_D) + _span(cq, MLA_Q_LORA)
                          + _span(ckv, MLA_KV_LORA) + misc)
    return w_in[..., :C_ZX].astype(BF16), w_rest, _select_cols(rest, _span(fv, FOX_D), True)


def _mla_weights(w_uq, w_ukv):
    qd, kvd = MLA_NOPE + MLA_ROPE, MLA_NOPE + MLA_V
    pad_q = [None] * (HEAD_W - qd)
    wuq = _select_cols(w_uq, sum([_span(h * qd, qd) + pad_q for h in range(MLA_HEADS)], []))
    wuqr = _select_cols(w_uq, sum([[None] * MLA_NOPE + _rot_half_span(h * qd + MLA_NOPE) + pad_q
                                   for h in range(MLA_HEADS)], []))
    wkk = _select_cols(w_ukv, sum([_span(h * kvd, MLA_NOPE) + [None] * (HEAD_W - MLA_NOPE)
                                   for h in range(MLA_HEADS)], []))
    wvt = _select_cols(w_ukv, sum([_span(h * kvd + MLA_NOPE, MLA_V) for h in range(MLA_HEADS)],
                                  []), True)
    return wuq, wuqr, wkk, wvt


def _rows(v, width=None, offset=0):
    v = v.astype(F32)
    if width is not None:
        v = jnp.pad(v, ((0, 0), (offset, width - offset - v.shape[-1])))
    return v[:, None, :]


def _position_tables(seq, tm):
    pos = jnp.concatenate([N_META + jnp.arange(seq, dtype=F32),
                           jnp.arange(tm, dtype=F32) - (tm - N_META)])
    inv_freq = 1.0 / (ROPE_THETA ** (jnp.arange(0, MLA_ROPE, 2, dtype=F32) / MLA_ROPE))
    ang = pos[:, None] * inv_freq[None, :]
    cos, sin = jnp.cos(ang), jnp.sin(ang)
    n = seq + tm
    rope_pad = LANES - MLA_NOPE - MLA_ROPE
    cos128 = jnp.concatenate([jnp.ones((n, MLA_NOPE), F32), cos, cos,
                              jnp.zeros((n, rope_pad), F32)], axis=-1)
    sin128 = jnp.concatenate([jnp.zeros((n, MLA_NOPE), F32), sin, sin,
                              jnp.zeros((n, rope_pad), F32)], axis=-1)
    cosk128 = jnp.concatenate([jnp.zeros((n, MLA_NOPE), F32), cos, cos,
                               jnp.zeros((n, rope_pad), F32)], axis=-1)
    return cos128, sin128, cosk128


def kernel(x, meta, ffn1_w_gate, ffn1_w_up, ffn1_w_down, ln1_g, ln1_b, w_in, conv_w, conv_b, dt_bias, a_log, d_skip, ssd_norm_g, fox_f_b, mla_q_norm_g, mla_w_uq, mla_kv_norm_g, mla_w_ukv, w_out, ln2_g, ln2_b, ffn2_w_gate, ffn2_w_up, ffn2_w_down, ln3_g, ln3_b):
    bsz, seq, _ = x.shape
    assert seq % TQ == 0
    tm = 2 * TQ if seq % (2 * TQ) == 0 else TQ
    n_main = bsz * seq
    n_rows = n_main + tm
    meta_block = n_rows // BLOCK - 1

    h = x.reshape(n_main, D_MODEL)
    meta_tile = jnp.concatenate([jnp.zeros((tm - N_META, D_MODEL), x.dtype),
                                 meta.astype(x.dtype)], axis=0)
    tables = _position_tables(seq, tm)
    bf = lambda w: w.astype(BF16)

    ffn1 = (bf(ffn1_w_gate), bf(ffn1_w_up), bf(ffn1_w_down), _rows(ln1_g), _rows(ln1_b))
    ffn2 = (bf(ffn2_w_gate), bf(ffn2_w_up), bf(ffn2_w_down), _rows(ln3_g), _rows(ln3_b))
    proj_w = (*_in_proj_weights(w_in), *_mla_weights(mla_w_uq, mla_w_ukv),
              _rows(mla_q_norm_g), _rows(mla_kv_norm_g))
    ssd_p = (conv_w.astype(F32), _rows(conv_b), _rows(dt_bias, LANES),
             _rows(-jnp.exp(a_log.astype(F32)), LANES),
             _rows(jnp.repeat(d_skip, SSD_HEAD_DIM, axis=-1)), _rows(ssd_norm_g))
    fb = _rows(fox_f_b, LANES, SMALL_F)
    out_p = (bf(w_out), _rows(ln2_g), _rows(ln2_b))

    for l in range(DEPTH):
        h, zx, small, fq, fk, fvt, mq, mk, mvt = _ffn_ln(
            l, h, ffn1, tm, n_rows, meta_tile=meta_tile if l == 0 else None,
            proj=(proj_w, tables, seq))
        y_ssd = _ssd(l, zx, small, ssd_p, bsz, seq)
        fk_meta, fk_main = _fox_keys(l, fk, small, fb, bsz, seq)
        y_fox = _attention(fq, fk_meta, fk_main, fvt, FOX_HEADS, bsz, seq, 0, "fox_attention")
        y_mla = _attention(mq, mk, mk, mvt, MLA_HEADS, bsz, seq, meta_block, "mla_attention")
        h = _ffn_ln(l, h, ffn2, tm, n_rows if l < DEPTH - 1 else n_main,
                    mix=(y_ssd, y_fox, y_mla, *out_p))

    return h.reshape(bsz, seq, D_MODEL)
```

```python
import functools

import numpy as np
import jax
import jax.numpy as jnp
from jax import lax
from jax.experimental import pallas as pl
from jax.experimental.pallas import tpu as pltpu

F32 = jnp.float32
BF16 = jnp.bfloat16

D_MODEL = 1024
DEPTH = 2
N_META = 16
BLOCK = 128
SSD_HEADS = 8
SSD_HEAD_DIM = 64
SSD_D = SSD_HEADS * SSD_HEAD_DIM
SSD_GROUPS = 2
SSD_STATE = 64
SSD_CONV = 4
SSD_CONV_DIM = SSD_D + 2 * SSD_GROUPS * SSD_STATE
FOX_HEADS = 4
FOX_HEAD_DIM = 64
FOX_D = FOX_HEADS * FOX_HEAD_DIM
MLA_HEADS = 4
MLA_Q_LORA = 256
MLA_KV_LORA = 128
MLA_NOPE = 64
MLA_ROPE = 32
MLA_V = 64
MLA_D = MLA_HEADS * MLA_V
ROPE_THETA = 10000.0
D_MIX = SSD_D + FOX_D + MLA_D
D_FF = 2816
ALPHA = (2 * DEPTH) ** 0.25
EPS = 1e-5

LANES = 128
MXU_W = 256
FF_CHUNK = MXU_W
N_FF_CHUNKS = D_FF // FF_CHUNK
N_OUT_CHUNKS = D_MODEL // FF_CHUNK
TQ = 4 * BLOCK
SSD_STEP_ROWS = 2 * BLOCK
PAD = BLOCK - N_META
NEG_BIG = -1e30
M_INIT = 2 * NEG_BIG
VMEM_LIMIT = 56 * 1024 * 1024

C_XBC = SSD_D
C_ZX = SSD_D + SSD_CONV_DIM
C_FQ = 0
C_CQ = C_FQ + 2 * FOX_D
C_MISC = C_CQ + MLA_Q_LORA
N_IN_ARR = C_MISC + MLA_KV_LORA + LANES
MISC_KRR = 32
MISC_KR = MLA_NOPE
LOG2E = 1.4426950408889634
DV_AUG = MLA_V + 16
SMALL_DT = 0
SMALL_F = 8
HEAD_W = LANES
FOX_BIAS_LANE = FOX_HEAD_DIM


def _sigmoid(x):
    return 1.0 / (1.0 + jnp.exp(-x))


def _softplus(x):
    return jnp.maximum(x, 0.0) + jnp.log(1.0 + jnp.exp(-jnp.abs(x)))


def _layer_norm_rows(y, g, b):
    mu = jnp.mean(y, axis=-1, keepdims=True)
    yc = y - mu
    var = jnp.mean(yc * yc, axis=-1, keepdims=True)
    return yc * lax.rsqrt(var + EPS) * g + b


def _split3(x):
    x1 = x.astype(BF16)
    r1 = x - x1.astype(F32)
    x2 = r1.astype(BF16)
    r2 = r1 - x2.astype(F32)
    return x1, x2, r2.astype(BF16)


def _dot(a, b):
    return jnp.dot(a, b, preferred_element_type=F32)


def _dot_nt(a, b):
    return lax.dot_general(a, b, (((1,), (1,)), ((), ())), preferred_element_type=F32)


def _resident(shape):
    return pl.BlockSpec(shape, lambda *_: (0,) * len(shape), pipeline_mode=pl.Buffered(1))


def _layer(arr, l):
    tail = (0,) * (arr.ndim - 1)
    return pl.BlockSpec((None,) + arr.shape[1:], lambda *_: (l,) + tail,
                        pipeline_mode=pl.Buffered(1))


N_PROJ_IN = 12
N_PROJ_OUT = 8


def _ffn_ln_kernel(prologue, n_main_tiles, with_proj, *refs):
    a_scr, y_scr = refs[-2:]
    refs = refs[:-2]
    if with_proj:
        proj_out, refs = refs[-N_PROJ_OUT:], refs[:-N_PROJ_OUT]
        o_ref = refs[-1]
        proj_in, refs = refs[-1 - N_PROJ_IN:-1], refs[:-1 - N_PROJ_IN]
    else:
        o_ref, refs = refs[-1], refs[:-1]
    wg_ref, wu_ref, wd_ref, g_ref, b_ref = refs[-5:]
    if prologue == "plain":
        x = refs[0][...]
    elif prologue == "meta":
        x_ref, m_ref = refs[:2]
        x = jnp.where(pl.program_id(0) < n_main_tiles, x_ref[...], m_ref[...])
    else:
        h_ref, ys_ref, yf_ref, ym_ref, wo_ref, g2_ref, b2_ref = refs[:7]
        mix = (_dot(ys_ref[...], wo_ref[0:SSD_D, :])
               + _dot(yf_ref[...], wo_ref[SSD_D:SSD_D + FOX_D, :])
               + _dot(ym_ref[...], wo_ref[SSD_D + FOX_D:D_MIX, :]))
        x = _layer_norm_rows(ALPHA * h_ref[...] + mix, g2_ref[...], b2_ref[...])
    xb = x.astype(BF16)

    for c in range(N_FF_CHUNKS):
        cols = slice(c * FF_CHUNK, (c + 1) * FF_CHUNK)
        gate = _dot(xb, wg_ref[:, cols])
        up = _dot(xb, wu_ref[:, cols])
        a_scr[c] = (gate * _sigmoid(gate) * up).astype(BF16)

    tm = x.shape[0]
    outs = []
    for rows in (slice(0, tm // 2), slice(tm // 2, tm)):
        for n in range(N_OUT_CHUNKS):
            cols = slice(n * FF_CHUNK, (n + 1) * FF_CHUNK)
            acc = _dot(a_scr[0, rows], wd_ref[0:FF_CHUNK, cols])
            for c in range(1, N_FF_CHUNKS):
                acc = acc + _dot(a_scr[c, rows], wd_ref[c * FF_CHUNK:(c + 1) * FF_CHUNK, cols])
            y_scr[rows, cols] = ALPHA * x[rows, cols] + 0.5 * acc
        outs.append(_layer_norm_rows(y_scr[rows, :], g_ref[...], b_ref[...]))
        o_ref[rows, :] = outs[-1]
    if with_proj:
        _proj_body(jnp.concatenate(outs, axis=0), *proj_in, *proj_out)


def _ffn_ln(l, h, ffn, tm, n_out_rows, meta_tile=None, mix=None, proj=None):
    def rows(width):
        return pl.BlockSpec((tm, width), lambda i: (i, 0))

    out_specs = rows(D_MODEL)
    out_shape = jax.ShapeDtypeStruct((n_out_rows, D_MODEL), F32)
    p_args, p_specs = [], []
    if proj is not None:
        p_args, p_specs, p_out_specs, p_out_shape = _proj_specs(l, n_out_rows, tm, *proj)
        out_specs, out_shape = [out_specs] + p_out_specs, [out_shape] + p_out_shape

    if mix is not None:
        prologue, n_main = "mix", None
        x_args = [h, *mix]
        x_specs = ([rows(D_MODEL), rows(SSD_D), rows(FOX_D), rows(MLA_D)]
                   + [_layer(p, l) for p in mix[3:]])
    elif meta_tile is not None:
        prologue, n_main = "meta", h.shape[0] // tm
        x_args = [h, meta_tile]
        x_specs = [pl.BlockSpec((tm, D_MODEL), lambda i: (jnp.minimum(i, n_main - 1), 0)),
                   _resident(meta_tile.shape)]
    else:
        prologue, n_main, x_args, x_specs = "plain", None, [h], [rows(D_MODEL)]
    return pl.pallas_call(
        functools.partial(_ffn_ln_kernel, prologue, n_main, proj is not None),
        grid=(n_out_rows // tm,),
        in_specs=x_specs + [_layer(p, l) for p in ffn] + p_specs,
        out_specs=out_specs,
        out_shape=out_shape,
        scratch_shapes=[pltpu.VMEM((N_FF_CHUNKS, tm, FF_CHUNK), BF16),
                        pltpu.VMEM((tm, D_MODEL), F32)],
        compiler_params=pltpu.CompilerParams(
            dimension_semantics=("arbitrary",), vmem_limit_bytes=VMEM_LIMIT),
        name="ffn_ln_proj" if proj is not None else "ffn_ln",
    )(*x_args, *ffn, *p_args)


def _rms_rows(x, g):
    return x * lax.rsqrt(jnp.mean(x * x, axis=-1, keepdims=True) + EPS) * g


def _store_vt_blocks(out_ref, val_t, n_heads):
    dv = val_t.shape[0] // n_heads
    ones = jnp.ones((DV_AUG - dv, BLOCK), out_ref.dtype)
    for r in range(out_ref.shape[0]):
        for hd in range(n_heads):
            out_ref[r, hd * DV_AUG:hd * DV_AUG + dv, :] = (
                val_t[hd * dv:(hd + 1) * dv, r * BLOCK:(r + 1) * BLOCK].astype(out_ref.dtype))
            out_ref[r, hd * DV_AUG + dv:(hd + 1) * DV_AUG, :] = ones


def _proj_body(h, wzx_ref, win_ref, wfvt_ref, wuq_ref, wuqr_ref, wkk_ref, wkvt_ref,
               qg_ref, kvg_ref, cos_ref, sin_ref, cosk_ref,
               zx_ref, small_ref, fq_ref, fk_ref, fvt_ref, mq_ref, mk_ref, mvt_ref):
    hb = h.astype(BF16)
    zx_ref[...] = _dot(hb, wzx_ref[...])
    misc = _dot(hb, win_ref[:, C_MISC:N_IN_ARR])
    small = misc[:, MLA_KV_LORA:]
    small_ref[...] = small

    lane = lax.broadcasted_iota(jnp.int32, (1, LANES), 1)
    lane_lo = lane < FOX_HEAD_DIM
    fox_one = jnp.where((lane >= FOX_BIAS_LANE) & (lane < FOX_BIAS_LANE + 3), 1.0, 0.0)

    fqk = _dot(hb, win_ref[:, C_FQ:C_CQ])
    for half, (ref, scale, extra) in enumerate([(fq_ref, FOX_HEAD_DIM ** -0.5 * LOG2E, fox_one),
                                                (fk_ref, None, None)]):
        for pair in range(FOX_HEADS // 2):
            c0 = half * FOX_D + pair * LANES
            both = fqk[:, c0:c0 + LANES]
            if scale is not None:
                both = both * scale
            for par, val in enumerate([both, pltpu.roll(both, FOX_HEAD_DIM, 1)]):
                val = jnp.where(lane_lo, val, 0.0)
                if extra is not None:
                    val = val + extra
                hd = 2 * pair + par
                ref[:, hd * HEAD_W:(hd + 1) * HEAD_W] = val.astype(BF16)
    _store_vt_blocks(fvt_ref, _dot_nt(wfvt_ref[...], hb), FOX_HEADS)

    cos = cos_ref[...]
    sin = sin_ref[...]
    cqn = _rms_rows(_dot(hb, win_ref[:, C_CQ:C_MISC]), qg_ref[...]).astype(BF16)
    q = _dot(cqn, wuq_ref[...])
    qr = _dot(cqn, wuqr_ref[...])
    kvn = _rms_rows(misc[:, :MLA_KV_LORA], kvg_ref[...]).astype(BF16)
    kn = _dot(kvn, wkk_ref[...])
    _store_vt_blocks(mvt_ref, _dot_nt(wkvt_ref[...], kvn), MLA_HEADS)
    krope = small * cosk_ref[...] + pltpu.roll(small, MISC_KR - MISC_KRR, 1) * sin
    scale = (MLA_NOPE + MLA_ROPE) ** -0.5 * LOG2E
    for hd in range(MLA_HEADS):
        sl = slice(hd * HEAD_W, (hd + 1) * HEAD_W)
        mq_ref[:, sl] = ((q[:, sl] * cos + qr[:, sl] * sin) * scale).astype(BF16)
        mk_ref[:, sl] = (kn[:, sl] + krope).astype(BF16)


def _proj_specs(l, n_rows, tm, weights, tables, seq):
    n_main_tiles = n_rows // tm - 1
    tiles_per_seq = seq // tm

    def rows(width):
        return pl.BlockSpec((tm, width), lambda i: (i, 0))

    tab = pl.BlockSpec((tm, LANES), lambda i: (
        jnp.where(i < n_main_tiles, i % tiles_per_seq, tiles_per_seq), 0))
    vt_rows = FOX_HEADS * DV_AUG
    t_spec = pl.BlockSpec((tm // BLOCK, vt_rows, BLOCK), lambda i: (i, 0, 0))
    t_shape = jax.ShapeDtypeStruct((n_rows // BLOCK, vt_rows, BLOCK), BF16)
    qk_w = FOX_HEADS * HEAD_W
    row_outs = [(C_ZX, F32), (LANES, F32), (qk_w, BF16), (qk_w, BF16)]
    args = [*weights, *tables]
    in_specs = [_layer(w, l) for w in weights] + [tab, tab, tab]
    out_specs = [rows(w) for w, _ in row_outs] + [t_spec, rows(qk_w), rows(qk_w), t_spec]
    out_shape = ([jax.ShapeDtypeStruct((n_rows, w), dt) for w, dt in row_outs]
                 + [t_shape, jax.ShapeDtypeStruct((n_rows, qk_w), BF16),
                    jax.ShapeDtypeStruct((n_rows, qk_w), BF16), t_shape])
    assert len(args) == N_PROJ_IN and len(out_shape) == N_PROJ_OUT
    return args, in_specs, out_specs, out_shape


def _first_step_block(b, meta_block, n_fill, own_first_block):
    return jnp.where(b == 0, meta_block,
                     jnp.where(b <= n_fill, meta_block - b, own_first_block))


def _n_fill_blocks(n_rows, bsz, seq, block_rows):
    n_fill = (n_rows - bsz * seq) // block_rows - 1
    assert bsz > n_fill, "needs one batch row per meta-tile block to zero-fill"
    return n_fill


def _ssd_kernel(n_fill, zx_ref, small_ref, cw_ref, cb_ref, dtb_ref, a_ref, dsk_ref, ng_ref,
                o_ref, conv_scr, s_scr, meta_conv_scr, meta_s_scr):
    b = pl.program_id(0)
    c = pl.program_id(1)

    @pl.when((c == 0) & (b >= 1) & (b <= n_fill))
    def _():
        o_ref[...] = jnp.zeros(o_ref.shape, o_ref.dtype)

    @pl.when((c == 0) & (b == 0))
    def _():
        conv_scr[0:8, :] = jnp.zeros((8, SSD_CONV_DIM), F32)
        s_scr[...] = jnp.zeros(s_scr.shape, F32)

    @pl.when((c == 0) & (b > 0))
    def _():
        conv_scr[0:8, :] = meta_conv_scr[...]
        s_scr[...] = meta_s_scr[...]

    @pl.when((c > 0) | (b == 0))
    def _():
        _ssd_step(c, zx_ref, small_ref, cw_ref, cb_ref, dtb_ref, a_ref, dsk_ref, ng_ref,
                  o_ref, conv_scr, s_scr)

    @pl.when((c == 0) & (b == 0))
    def _():
        meta_conv_scr[...] = conv_scr[0:8, :]
        meta_s_scr[...] = s_scr[...]


def _ssd_step(c, zx_ref, small_ref, cw_ref, cb_ref, dtb_ref, a_ref, dsk_ref, ng_ref,
              o_ref, conv_scr, s_scr):
    R = SSD_STEP_ROWS
    row = lax.broadcasted_iota(jnp.int32, (R, 1), 0)
    valid = jnp.logical_or(c > 0, row >= R - N_META)

    conv_scr[8:8 + R, :] = jnp.where(valid, zx_ref[:, C_XBC:C_ZX], 0.0)
    acc = cb_ref[...]
    for k in range(SSD_CONV):
        off = 8 - (SSD_CONV - 1) + k
        acc = acc + cw_ref[k:k + 1, :] * conv_scr[off:off + R, :]
    conv_scr[0:8, :] = conv_scr[R:R + 8, :]
    xbc = acc * _sigmoid(acc)
    dt = jnp.where(valid, _softplus(small_ref[...] + dtb_ref[...]), 0.0)
    for ch in range(R // BLOCK):
        rows = slice(ch * BLOCK, (ch + 1) * BLOCK)
        _ssd_chunk(xbc[rows], dt[rows], zx_ref.at[rows], a_ref, dsk_ref, ng_ref,
                   o_ref.at[rows], s_scr)


def _ssd_chunk(xbc, dt, zx_ref, a_ref, dsk_ref, ng_ref, o_ref, s_scr):
    Q = BLOCK
    lane = lax.broadcasted_iota(jnp.int32, (1, LANES), 1)
    lane_lo = lane < SSD_HEAD_DIM
    sub = lax.broadcasted_iota(jnp.int32, (LANES, 1), 0)
    bm = xbc[:, SSD_D:SSD_D + LANES]
    cm = xbc[:, SSD_D + LANES:SSD_D + 2 * LANES]

    a = dt * a_ref[...]
    tri = (lax.broadcasted_iota(jnp.int32, (Q, Q), 0)
           >= lax.broadcasted_iota(jnp.int32, (Q, Q), 1))
    tri_b = jnp.where(tri, 1.0, 0.0).astype(BF16)
    a1, a2, a3 = _split3(a)
    a_cum = _dot(tri_b, a1) + _dot(tri_b, a2) + _dot(tri_b, a3)
    a_cum_t = a_cum.T
    bm_t = bm.T

    cm_b = cm.astype(BF16)
    bm_b = bm.astype(BF16)
    cb_g = [_dot_nt(jnp.where(lane_lo, cm, 0.0).astype(BF16), bm_b),
            _dot_nt(jnp.where(lane_lo, 0.0, cm).astype(BF16), bm_b)]
    rows_g = [sub < SSD_STATE, sub >= SSD_STATE]

    pairs_per_group = SSD_HEADS // 2 // SSD_GROUPS
    y_pairs = []
    for p in range(SSD_HEADS // 2):
        g = p // pairs_per_group
        psl = slice(p * LANES, (p + 1) * LANES)
        xs_p = xbc[:, psl]
        dt_pair = jnp.where(lane_lo, dt[:, 2 * p:2 * p + 1], dt[:, 2 * p + 1:2 * p + 2])
        xdt = (xs_p * dt_pair).astype(BF16)
        s_old = s_scr[p]
        yd, upd, e_col, e_last = [], [], [], []
        for par in range(2):
            hd = 2 * p + par
            col = a_cum[:, hd:hd + 1]
            rowv = a_cum_t[hd:hd + 1, :]
            last = a_cum_t[hd:hd + 1, Q - 1:Q]
            seg = jnp.exp(jnp.where(tri, col - rowv, NEG_BIG))
            yd.append(_dot((cb_g[g] * seg).astype(BF16), xdt))
            upd.append(_dot((bm_t * jnp.exp(last - rowv)).astype(BF16), xdt))
            e_col.append(jnp.exp(col))
            e_last.append(jnp.exp(last))
        y_off = _dot(cm_b, s_old.astype(BF16)) * jnp.where(lane_lo, e_col[0], e_col[1])
        s_new = (jnp.where(lane_lo, e_last[0], e_last[1]) * s_old
                 + jnp.where(rows_g[g], jnp.where(lane_lo, upd[0], upd[1]), 0.0))
        s_scr[p] = s_new
        y_p = jnp.where(lane_lo, yd[0], yd[1]) + y_off + dsk_ref[:, psl] * xs_p
        z_p = zx_ref[:, psl]
        y_pairs.append(y_p * (z_p * _sigmoid(z_p)))

    for g in range(SSD_GROUPS):
        ps = range(g * pairs_per_group, (g + 1) * pairs_per_group)
        ss = sum(jnp.sum(y_pairs[p] * y_pairs[p], axis=-1, keepdims=True) for p in ps)
        inv = lax.rsqrt(ss * (1.0 / (pairs_per_group * LANES)) + EPS)
        for p in ps:
            psl = slice(p * LANES, (p + 1) * LANES)
            o_ref[:, psl] = (y_pairs[p] * inv * ng_ref[:, psl]).astype(BF16)


def _ssd(l, zx, small, params, bsz, seq):
    n_rows = zx.shape[0]
    R = SSD_STEP_ROWS
    steps_per_seq = seq // R
    meta_block = n_rows // R - 1
    n_fill = _n_fill_blocks(n_rows, bsz, seq, R)

    def block(b, c):
        first = _first_step_block(b, meta_block, n_fill, b * steps_per_seq)
        return (jnp.where(c == 0, first, b * steps_per_seq + c - 1), 0)

    return pl.pallas_call(
        functools.partial(_ssd_kernel, n_fill),
        grid=(bsz, steps_per_seq + 1),
        in_specs=[pl.BlockSpec((R, C_ZX), block), pl.BlockSpec((R, LANES), block)]
                 + [_layer(p, l) for p in params],
        out_specs=pl.BlockSpec((R, SSD_D), block),
        out_shape=jax.ShapeDtypeStruct((n_rows, SSD_D), BF16),
        scratch_shapes=[pltpu.VMEM((R + 8, SSD_CONV_DIM), F32),
                        pltpu.VMEM((SSD_HEADS // 2, LANES, LANES), F32),
                        pltpu.VMEM((8, SSD_CONV_DIM), F32),
                        pltpu.VMEM((SSD_HEADS // 2, LANES, LANES), F32)],
        compiler_params=pltpu.CompilerParams(
            dimension_semantics=("arbitrary", "arbitrary"), vmem_limit_bytes=VMEM_LIMIT),
        name="ssd_mixer",
    )(zx, small, *params)


def _fox_keys_kernel(km_ref, k_ref, sm_ref, s_ref, fb_ref, om_ref, o_ref):
    T = BLOCK
    width = k_ref.shape[1]
    row = lax.broadcasted_iota(jnp.int32, (T, 1), 0)
    tri = (lax.broadcasted_iota(jnp.int32, (T, T), 0)
           >= lax.broadcasted_iota(jnp.int32, (T, T), 1))
    tri_b = jnp.where(tri, 1.0, 0.0).astype(BF16)
    src = lax.broadcasted_iota(jnp.int32, (LANES, width), 0)
    dst = lax.broadcasted_iota(jnp.int32, (LANES, width), 1)
    dst_head = jnp.right_shift(dst, HEAD_W.bit_length() - 1)
    dst_lane = jnp.bitwise_and(dst, HEAD_W - 1)
    sel = jnp.concatenate(
        [jnp.where((src == SMALL_F + dst_head) & (dst_lane == FOX_BIAS_LANE + i),
                   1.0, 0.0).astype(BF16) for i in range(3)], axis=0)

    def local_cumsum(small_blk, is_meta):
        log_f = -_softplus(-(small_blk + fb_ref[...]))
        if is_meta:
            log_f = jnp.where(row < PAD, 0.0, log_f)
        parts = _dot(tri_b, jnp.concatenate(_split3(log_f), axis=1))
        return parts[:, :LANES] + parts[:, LANES:2 * LANES] + parts[:, 2 * LANES:]

    def keys(c_blk, k_blk):
        pieces = jnp.concatenate(_split3(c_blk * (-LOG2E)), axis=1)
        return (k_blk.astype(F32) + _dot(pieces, sel)).astype(BF16)

    n_blocks = k_ref.shape[0] // T
    rows = [slice(j * T, (j + 1) * T) for j in range(n_blocks)]
    c_meta = local_cumsum(sm_ref[...], True)
    local = [local_cumsum(s_ref[rows[j], :], False) for j in range(n_blocks)]
    om_ref[...] = keys(c_meta, km_ref[...])
    carry = c_meta[T - 1:T, :]
    for j in range(n_blocks):
        c_blk = local[j] + carry
        carry = c_blk[T - 1:T, :]
        o_ref[rows[j], :] = keys(c_blk, k_ref[rows[j], :])


def _fox_keys(l, k, small, fb, bsz, seq):
    n_rows, width = k.shape
    meta_block = n_rows // BLOCK - 1
    main = lambda w: pl.BlockSpec((seq, w), lambda b: (b, 0))
    meta = lambda w: pl.BlockSpec((BLOCK, w), lambda b: (meta_block, 0))
    return pl.pallas_call(
        _fox_keys_kernel,
        grid=(bsz,),
        in_specs=[meta(width), main(width), meta(LANES), main(LANES), _layer(fb, l)],
        out_specs=[pl.BlockSpec((BLOCK, width), lambda b: (0, 0)), main(width)],
        out_shape=[jax.ShapeDtypeStruct((BLOCK, width), BF16),
                   jax.ShapeDtypeStruct((bsz * seq, width), BF16)],
        compiler_params=pltpu.CompilerParams(
            dimension_semantics=("arbitrary",), vmem_limit_bytes=VMEM_LIMIT),
        name="fox_keys",
    )(k, k, small, small, fb)


def _attn_kernel(n_heads, n_fill, q_ref, km_ref, k_ref, vtm_ref, vt_ref, o_ref,
                 acc_scr, s0_scr, s1_scr):
    t = pl.program_id(1)
    b = pl.program_id(0)
    dv = acc_scr.shape[1]

    @pl.when((t == 0) & (b >= 1) & (b <= n_fill))
    def _():
        o_ref[...] = jnp.zeros(o_ref.shape, o_ref.dtype)

    heads = range(n_heads)
    hsl = [slice(hd * HEAD_W, (hd + 1) * HEAD_W) for hd in heads]
    vsl = [slice(hd * dv, (hd + 1) * dv) for hd in heads]
    qs = [q_ref[:, hsl[hd]] for hd in heads]
    ahead = (lax.broadcasted_iota(jnp.int32, (TQ, TQ), 0)
             - lax.broadcasted_iota(jnp.int32, (TQ, TQ), 1))

    def softmax_pv(ss, vts, ms, mask):
        if mask is not None:
            ss = [jnp.where(mask, s, NEG_BIG) for s in ss]
        new_ms = [jnp.maximum(ms[hd], jnp.max(ss[hd], axis=0, keepdims=True)) for hd in heads]
        corrs = [jnp.exp2(ms[hd] - new_ms[hd]) for hd in heads]
        prs = [jnp.exp2(ss[hd] - new_ms[hd]).astype(BF16) for hd in heads]
        for hd in heads:
            acc_scr[hd] = corrs[hd] * acc_scr[hd] + _dot(vts[hd], prs[hd])
        return tuple(new_ms)

    def meta_chunk(carry, mask):
        return softmax_pv([_dot_nt(km_ref[PAD:, hsl[hd]], qs[hd]) for hd in heads],
                          [vtm_ref[0, vsl[hd], PAD:] for hd in heads], carry, mask)

    def scores(c, slot_scr):
        r0 = pl.multiple_of(c * TQ, TQ)
        for hd in heads:
            slot_scr[hd] = _dot_nt(k_ref[pl.ds(r0, TQ), hsl[hd]], qs[hd])

    def chunk_vt(c, hd):
        n_sub = TQ // BLOCK
        return jnp.concatenate([vt_ref[n_sub * c + i, vsl[hd], :] for i in range(n_sub)], axis=1)

    def consume(c, slot_scr, carry, mask):
        return softmax_pv([slot_scr[hd] for hd in heads], [chunk_vt(c, hd) for hd in heads],
                          carry, mask)

    def finish():
        dv_out = o_ref.shape[1] // n_heads
        ys = [acc_scr[hd, 0:dv_out, :] * (1.0 / acc_scr[hd, dv_out:dv_out + 1, :])
              for hd in heads]
        for p in range(n_heads // 2):
            y_t = jnp.concatenate([ys[2 * p], ys[2 * p + 1]], axis=0)
            o_ref[:, p * LANES:(p + 1) * LANES] = y_t.T.astype(BF16)

    init = tuple(jnp.full((1, TQ), M_INIT, F32) for _ in heads)

    @pl.when((t == 0) & (b == 0))
    def _():
        acc_scr[...] = jnp.zeros(acc_scr.shape, F32)
        meta_chunk(init, ahead[:N_META, :] <= -(TQ - N_META))
        finish()

    @pl.when(t > 0)
    def _():
        scores(0, s0_scr)
        s_meta = [_dot_nt(km_ref[PAD:, hsl[hd]], qs[hd]) for hd in heads]
        scores(jnp.minimum(1, k_ref.shape[0] // TQ - 1), s1_scr)
        diagonal = ahead <= 0

        s_first = [jnp.where(ahead <= (t - 1) * TQ, s0_scr[hd], NEG_BIG) for hd in heads]
        ms = tuple(jnp.maximum(jnp.max(s_first[hd], axis=0, keepdims=True),
                               jnp.max(s_meta[hd], axis=0, keepdims=True)) for hd in heads)
        for hd in heads:
            acc_scr[hd] = (_dot(chunk_vt(0, hd), jnp.exp2(s_first[hd] - ms[hd]).astype(BF16))
                           + _dot(vtm_ref[0, vsl[hd], PAD:],
                                  jnp.exp2(s_meta[hd] - ms[hd]).astype(BF16)))

        @pl.when(t > 1)
        def _():
            def pair(i, cr):
                c = 2 * i + 1
                scores(c + 1, s0_scr)
                cr = consume(c, s1_scr, cr, None)
                scores(c + 2, s1_scr)
                return consume(c + 1, s0_scr, cr, None)

            n_unmasked = t - 2
            carry = lax.fori_loop(0, n_unmasked // 2, pair, ms)
            c = 2 * (n_unmasked // 2) + 1

            def odd_tail(cr):
                scores(c + 1, s0_scr)
                cr = consume(c, s1_scr, cr, None)
                return consume(c + 1, s0_scr, cr, diagonal)

            lax.cond(n_unmasked % 2 == 1, odd_tail,
                     lambda cr: consume(c, s1_scr, cr, diagonal), carry)

        finish()


def _attention(q, k_meta, k_main, vt, n_heads, bsz, seq, meta_k_block, name):
    n_rows, qw = q.shape
    dv_aug = vt.shape[1] // n_heads
    dv = dv_aug - (DV_AUG - MLA_V)
    tiles_per_seq = seq // TQ
    meta_q_tile = n_rows // TQ - 1
    meta_block = n_rows // BLOCK - 1
    n_fill = _n_fill_blocks(n_rows, bsz, seq, TQ)

    def q_tile(b, t):
        first = _first_step_block(b, meta_q_tile, n_fill, b * tiles_per_seq)
        return (jnp.where(t == 0, first, b * tiles_per_seq + t - 1), 0)

    return pl.pallas_call(
        functools.partial(_attn_kernel, n_heads, n_fill),
        grid=(bsz, tiles_per_seq + 1),
        in_specs=[pl.BlockSpec((TQ, qw), q_tile),
                  pl.BlockSpec((BLOCK, qw), lambda b, t: (meta_k_block, 0)),
                  pl.BlockSpec((seq, qw), lambda b, t: (b, 0)),
                  pl.BlockSpec((1,) + vt.shape[1:], lambda b, t: (meta_block, 0, 0)),
                  pl.BlockSpec((seq // BLOCK,) + vt.shape[1:], lambda b, t: (b, 0, 0))],
        out_specs=pl.BlockSpec((TQ, n_heads * dv), q_tile),
        out_shape=jax.ShapeDtypeStruct((n_rows, n_heads * dv), BF16),
        scratch_shapes=[pltpu.VMEM((n_heads, dv_aug, TQ), F32),
                        pltpu.VMEM((n_heads, TQ, TQ), F32),
                        pltpu.VMEM((n_heads, TQ, TQ), F32)],
        compiler_params=pltpu.CompilerParams(
            dimension_semantics=("arbitrary", "arbitrary"), vmem_limit_bytes=VMEM_LIMIT),
        name=name,
    )(q, k_meta, k_main, vt, vt)


def _select_cols(w, cols, transpose=False):
    place = np.zeros((w.shape[-1], len(cols)), np.float32)
    for j, entry in enumerate(cols):
        if entry is not None:
            place[entry[0], j] = entry[1]
    out = jnp.einsum('lkn,nm->lmk' if transpose else 'lkn,nm->lkm',
                     w.astype(BF16).astype(F32), jnp.asarray(place))
    return out.astype(BF16)


def _span(start, n, sign=1.0):
    return [(start + i, sign) for i in range(n)]


def _rot_half_span(start):
    half = MLA_ROPE // 2
    return _span(start + half, half, -1.0) + _span(start, half)


def _in_proj_weights(w_in):
    sizes = [SSD_D, SSD_CONV_DIM, SSD_HEADS, FOX_D, FOX_D, FOX_D, FOX_HEADS,
             MLA_Q_LORA, MLA_KV_LORA, MLA_ROPE]
    z, xbc, dt, fq, fk, fv, fr, cq, ckv, kr = [int(o) - C_ZX for o in
                                               np.cumsum([0] + sizes[:-1])]
    del z, xbc
    misc = (_span(dt, SSD_HEADS) + _span(fr, FOX_HEADS)
            + [None] * (MISC_KRR - SMALL_F - FOX_HEADS) + _rot_half_span(kr)
            + _span(kr, MLA_ROPE) + [None] * (LANES - MISC_KR - MLA_ROPE))
    rest = w_in[..., C_ZX:]
    w_rest = _select_cols(rest, _span(fq, FOX_D) + _span(fk, FOX_D) + _span(cq, MLA_Q_LORA)
                          + _span(ckv, MLA_KV_LORA) + misc)
    return w_in[..., :C_ZX].astype(BF16), w_rest, _select_cols(rest, _span(fv, FOX_D), True)


def _mla_weights(w_uq, w_ukv):
    qd, kvd = MLA_NOPE + MLA_ROPE, MLA_NOPE + MLA_V
    pad_q = [None] * (HEAD_W - qd)
    wuq = _select_cols(w_uq, sum([_span(h * qd, qd) + pad_q for h in range(MLA_HEADS)], []))
    wuqr = _select_cols(w_uq, sum([[None] * MLA_NOPE + _rot_half_span(h * qd + MLA_NOPE) + pad_q
                                   for h in range(MLA_HEADS)], []))
    wkk = _select_cols(w_ukv, sum([_span(h * kvd, MLA_NOPE) + [None] * (HEAD_W - MLA_NOPE)
                                   for h in range(MLA_HEADS)], []))
    wvt = _select_cols(w_ukv, sum([_span(h * kvd + MLA_NOPE, MLA_V) for h in range(MLA_HEADS)],
                                  []), True)
    return wuq, wuqr, wkk, wvt


def _rows(v, width=None, offset=0):
    v = v.astype(F32)
    if width is not None:
        v = jnp.pad(v, ((0, 0), (offset, width - offset - v.shape[-1])))
    return v[:, None, :]


def _position_tables(seq, tm):
    pos = jnp.concatenate([N_META + jnp.arange(seq, dtype=F32),
                           jnp.arange(tm, dtype=F32) - (tm - N_META)])
    inv_freq = 1.0 / (ROPE_THETA ** (jnp.arange(0, MLA_ROPE, 2, dtype=F32) / MLA_ROPE))
    ang = pos[:, None] * inv_freq[None, :]
    cos, sin = jnp.cos(ang), jnp.sin(ang)
    n = seq + tm
    rope_pad = LANES - MLA_NOPE - MLA_ROPE
    cos128 = jnp.concatenate([jnp.ones((n, MLA_NOPE), F32), cos, cos,
                              jnp.zeros((n, rope_pad), F32)], axis=-1)
    sin128 = jnp.concatenate([jnp.zeros((n, MLA_NOPE), F32), sin, sin,
                              jnp.zeros((n, rope_pad), F32)], axis=-1)
    cosk128 = jnp.concatenate([jnp.zeros((n, MLA_NOPE), F32), cos, cos,
                               jnp.zeros((n, rope_pad), F32)], axis=-1)
    return cos128, sin128, cosk128


def kernel(x, meta, ffn1_w_gate, ffn1_w_up, ffn1_w_down, ln1_g, ln1_b, w_in, conv_w, conv_b, dt_bias, a_log, d_skip, ssd_norm_g, fox_f_b, mla_q_norm_g, mla_w_uq, mla_kv_norm_g, mla_w_ukv, w_out, ln2_g, ln2_b, ffn2_w_gate, ffn2_w_up, ffn2_w_down, ln3_g, ln3_b):
    bsz, seq, _ = x.shape
    assert seq % TQ == 0
    tm = TQ
    n_main = bsz * seq
    n_rows = n_main + tm
    meta_block = n_rows // BLOCK - 1

    h = x.reshape(n_main, D_MODEL)
    meta_tile = jnp.concatenate([jnp.zeros((tm - N_META, D_MODEL), x.dtype),
                                 meta.astype(x.dtype)], axis=0)
    tables = _position_tables(seq, tm)
    bf = lambda w: w.astype(BF16)

    ffn1 = (bf(ffn1_w_gate), bf(ffn1_w_up), bf(ffn1_w_down), _rows(ln1_g), _rows(ln1_b))
    ffn2 = (bf(ffn2_w_gate), bf(ffn2_w_up), bf(ffn2_w_down), _rows(ln3_g), _rows(ln3_b))
    proj_w = (*_in_proj_weights(w_in), *_mla_weights(mla_w_uq, mla_w_ukv),
              _rows(mla_q_norm_g), _rows(mla_kv_norm_g))
    ssd_p = (conv_w.astype(F32), _rows(conv_b), _rows(dt_bias, LANES),
             _rows(-jnp.exp(a_log.astype(F32)), LANES),
             _rows(jnp.repeat(d_skip, SSD_HEAD_DIM, axis=-1)), _rows(ssd_norm_g))
    fb = _rows(fox_f_b, LANES, SMALL_F)
    out_p = (bf(w_out), _rows(ln2_g), _rows(ln2_b))

    for l in range(DEPTH):
        h, zx, small, fq, fk, fvt, mq, mk, mvt = _ffn_ln(
            l, h, ffn1, tm, n_rows, meta_tile=meta_tile if l == 0 else None,
            proj=(proj_w, tables, seq))
        y_ssd = _ssd(l, zx, small, ssd_p, bsz, seq)
        fk_meta, fk_main = _fox_keys(l, fk, small, fb, bsz, seq)
        y_fox = _attention(fq, fk_meta, fk_main, fvt, FOX_HEADS, bsz, seq, 0, "fox_attention")
        y_mla = _attention(mq, mk, mk, mvt, MLA_HEADS, bsz, seq, meta_block, "mla_attention")
        h = _ffn_ln(l, h, ffn2, tm, n_rows if l < DEPTH - 1 else n_main,
                    mix=(y_ssd, y_fox, y_mla, *out_p))

    return h.reshape(bsz, seq, D_MODEL)
```

```python
import functools

import numpy as np
import jax
import jax.numpy as jnp
from jax import lax
from jax.experimental import pallas as pl
from jax.experimental.pallas import tpu as pltpu

F32 = jnp.float32
BF16 = jnp.bfloat16

D_MODEL = 1024
DEPTH = 2
N_META = 16
BLOCK = 128
SSD_HEADS = 8
SSD_HEAD_DIM = 64
SSD_D = SSD_HEADS * SSD_HEAD_DIM
SSD_GROUPS = 2
SSD_STATE = 64
SSD_CONV = 4
SSD_CONV_DIM = SSD_D + 2 * SSD_GROUPS * SSD_STATE
FOX_HEADS = 4
FOX_HEAD_DIM = 64
FOX_D = FOX_HEADS * FOX_HEAD_DIM
MLA_HEADS = 4
MLA_Q_LORA = 256
MLA_KV_LORA = 128
MLA_NOPE = 64
MLA_ROPE = 32
MLA_V = 64
MLA_D = MLA_HEADS * MLA_V
ROPE_THETA = 10000.0
D_MIX = SSD_D + FOX_D + MLA_D
D_FF = 2816
ALPHA = (2 * DEPTH) ** 0.25
EPS = 1e-5

LANES = 128
MXU_W = 256
FF_CHUNK = MXU_W
N_FF_CHUNKS = D_FF // FF_CHUNK
N_OUT_CHUNKS = D_MODEL // FF_CHUNK
TQ = 4 * BLOCK
SSD_STEP_ROWS = 2 * BLOCK
PAD = BLOCK - N_META
NEG_BIG = -1e30
M_INIT = 2 * NEG_BIG
VMEM_LIMIT = 56 * 1024 * 1024

C_XBC = SSD_D
C_ZX = SSD_D + SSD_CONV_DIM
C_FQ = 0
C_CQ = C_FQ + 2 * FOX_D
C_MISC = C_CQ + MLA_Q_LORA
N_IN_ARR = C_MISC + MLA_KV_LORA + LANES
MISC_KRR = 32
MISC_KR = MLA_NOPE
LOG2E = 1.4426950408889634
DV_AUG = MLA_V + 16
SMALL_DT = 0
SMALL_F = 8
HEAD_W = LANES
FOX_BIAS_LANE = FOX_HEAD_DIM


def _sigmoid(x):
    return 1.0 / (1.0 + jnp.exp(-x))


def _softplus(x):
    return jnp.maximum(x, 0.0) + jnp.log(1.0 + jnp.exp(-jnp.abs(x)))


def _layer_norm_rows(y, g, b):
    mu = jnp.mean(y, axis=-1, keepdims=True)
    yc = y - mu
    var = jnp.mean(yc * yc, axis=-1, keepdims=True)
    return yc * lax.rsqrt(var + EPS) * g + b


def _split3(x):
    x1 = x.astype(BF16)
    r1 = x - x1.astype(F32)
    x2 = r1.astype(BF16)
    r2 = r1 - x2.astype(F32)
    return x1, x2, r2.astype(BF16)


def _dot(a, b):
    return jnp.dot(a, b, preferred_element_type=F32)


def _dot_nt(a, b):
    return lax.dot_general(a, b, (((1,), (1,)), ((), ())), preferred_element_type=F32)


def _resident(shape):
    return pl.BlockSpec(shape, lambda *_: (0,) * len(shape), pipeline_mode=pl.Buffered(1))


def _layer(arr, l):
    tail = (0,) * (arr.ndim - 1)
    return pl.BlockSpec((None,) + arr.shape[1:], lambda *_: (l,) + tail,
                        pipeline_mode=pl.Buffered(1))


N_PROJ_IN = 12
N_PROJ_OUT = 8


def _ffn_ln_kernel(prologue, n_main_tiles, with_proj, *refs):
    a_scr, y_scr = refs[-2:]
    refs = refs[:-2]
    if with_proj:
        proj_out, refs = refs[-N_PROJ_OUT:], refs[:-N_PROJ_OUT]
        o_ref = refs[-1]
        proj_in, refs = refs[-1 - N_PROJ_IN:-1], refs[:-1 - N_PROJ_IN]
    else:
        o_ref, refs = refs[-1], refs[:-1]
    wg_ref, wu_ref, wd_ref, g_ref, b_ref = refs[-5:]
    if prologue == "plain":
        x = refs[0][...]
    elif prologue == "meta":
        x_ref, m_ref = refs[:2]
        x = jnp.where(pl.program_id(0) < n_main_tiles, x_ref[...], m_ref[...])
    else:
        h_ref, ys_ref, yf_ref, ym_ref, wo_ref, g2_ref, b2_ref = refs[:7]
        mix = (_dot(ys_ref[...], wo_ref[0:SSD_D, :])
               + _dot(yf_ref[...], wo_ref[SSD_D:SSD_D + FOX_D, :])
               + _dot(ym_ref[...], wo_ref[SSD_D + FOX_D:D_MIX, :]))
        x = _layer_norm_rows(ALPHA * h_ref[...] + mix, g2_ref[...], b2_ref[...])
    xb = x.astype(BF16)

    for c in range(N_FF_CHUNKS):
        cols = slice(c * FF_CHUNK, (c + 1) * FF_CHUNK)
        gate = _dot(xb, wg_ref[:, cols])
        up = _dot(xb, wu_ref[:, cols])
        a_scr[c] = (gate * _sigmoid(gate) * up).astype(BF16)

    tm = x.shape[0]
    outs = []
    for rows in (slice(0, tm // 2), slice(tm // 2, tm)):
        for n in range(N_OUT_CHUNKS):
            cols = slice(n * FF_CHUNK, (n + 1) * FF_CHUNK)
            acc = _dot(a_scr[0, rows], wd_ref[0:FF_CHUNK, cols])
            for c in range(1, N_FF_CHUNKS):
                acc = acc + _dot(a_scr[c, rows], wd_ref[c * FF_CHUNK:(c + 1) * FF_CHUNK, cols])
            y_scr[rows, cols] = ALPHA * x[rows, cols] + 0.5 * acc
        outs.append(_layer_norm_rows(y_scr[rows, :], g_ref[...], b_ref[...]))
        o_ref[rows, :] = outs[-1]
    if with_proj:
        _proj_body(jnp.concatenate(outs, axis=0), *proj_in, *proj_out)


def _ffn_ln(l, h, ffn, tm, n_out_rows, meta_tile=None, mix=None, proj=None):
    def rows(width):
        return pl.BlockSpec((tm, width), lambda i: (i, 0))

    out_specs = rows(D_MODEL)
    out_shape = jax.ShapeDtypeStruct((n_out_rows, D_MODEL), F32)
    p_args, p_specs = [], []
    if proj is not None:
        p_args, p_specs, p_out_specs, p_out_shape = _proj_specs(l, n_out_rows, tm, *proj)
        out_specs, out_shape = [out_specs] + p_out_specs, [out_shape] + p_out_shape

    if mix is not None:
        prologue, n_main = "mix", None
        x_args = [h, *mix]
        x_specs = ([rows(D_MODEL), rows(SSD_D), rows(FOX_D), rows(MLA_D)]
                   + [_layer(p, l) for p in mix[3:]])
    elif meta_tile is not None:
        prologue, n_main = "meta", h.shape[0] // tm
        x_args = [h, meta_tile]
        x_specs = [pl.BlockSpec((tm, D_MODEL), lambda i: (jnp.minimum(i, n_main - 1), 0)),
                   _resident(meta_tile.shape)]
    else:
        prologue, n_main, x_args, x_specs = "plain", None, [h], [rows(D_MODEL)]
    return pl.pallas_call(
        functools.partial(_ffn_ln_kernel, prologue, n_main, proj is not None),
        grid=(n_out_rows // tm,),
        in_specs=x_specs + [_layer(p, l) for p in ffn] + p_specs,
        out_specs=out_specs,
        out_shape=out_shape,
        scratch_shapes=[pltpu.VMEM((N_FF_CHUNKS, tm, FF_CHUNK), BF16),
                        pltpu.VMEM((tm, D_MODEL), F32)],
        compiler_params=pltpu.CompilerParams(
            dimension_semantics=("arbitrary",), vmem_limit_bytes=VMEM_LIMIT),
        name="ffn_ln_proj" if proj is not None else "ffn_ln",
    )(*x_args, *ffn, *p_args)


def _rms_rows(x, g):
    return x * lax.rsqrt(jnp.mean(x * x, axis=-1, keepdims=True) + EPS) * g


def _store_vt_blocks(out_ref, val_t, n_heads):
    dv = val_t.shape[0] // n_heads
    ones = jnp.ones((DV_AUG - dv, BLOCK), out_ref.dtype)
    for r in range(out_ref.shape[0]):
        for hd in range(n_heads):
            out_ref[r, hd * DV_AUG:hd * DV_AUG + dv, :] = (
                val_t[hd * dv:(hd + 1) * dv, r * BLOCK:(r + 1) * BLOCK].astype(out_ref.dtype))
            out_ref[r, hd * DV_AUG + dv:(hd + 1) * DV_AUG, :] = ones


def _proj_body(h, wzx_ref, win_ref, wfvt_ref, wuq_ref, wuqr_ref, wkk_ref, wkvt_ref,
               qg_ref, kvg_ref, cos_ref, sin_ref, cosk_ref,
               zx_ref, small_ref, fq_ref, fk_ref, fvt_ref, mq_ref, mk_ref, mvt_ref):
    hb = h.astype(BF16)
    zx_ref[...] = _dot(hb, wzx_ref[...])
    misc = _dot(hb, win_ref[:, C_MISC:N_IN_ARR])
    small = misc[:, MLA_KV_LORA:]
    small_ref[...] = small

    lane = lax.broadcasted_iota(jnp.int32, (1, LANES), 1)
    lane_lo = lane < FOX_HEAD_DIM
    fox_one = jnp.where((lane >= FOX_BIAS_LANE) & (lane < FOX_BIAS_LANE + 3), 1.0, 0.0)

    fqk = _dot(hb, win_ref[:, C_FQ:C_CQ])
    for half, (ref, scale, extra) in enumerate([(fq_ref, FOX_HEAD_DIM ** -0.5 * LOG2E, fox_one),
                                                (fk_ref, None, None)]):
        for pair in range(FOX_HEADS // 2):
            c0 = half * FOX_D + pair * LANES
            both = fqk[:, c0:c0 + LANES]
            if scale is not None:
                both = both * scale
            for par, val in enumerate([both, pltpu.roll(both, FOX_HEAD_DIM, 1)]):
                val = jnp.where(lane_lo, val, 0.0)
                if extra is not None:
                    val = val + extra
                hd = 2 * pair + par
                ref[:, hd * HEAD_W:(hd + 1) * HEAD_W] = val.astype(BF16)
    _store_vt_blocks(fvt_ref, _dot_nt(wfvt_ref[...], hb), FOX_HEADS)

    cos = cos_ref[...]
    sin = sin_ref[...]
    cqn = _rms_rows(_dot(hb, win_ref[:, C_CQ:C_MISC]), qg_ref[...]).astype(BF16)
    q = _dot(cqn, wuq_ref[...])
    qr = _dot(cqn, wuqr_ref[...])
    kvn = _rms_rows(misc[:, :MLA_KV_LORA], kvg_ref[...]).astype(BF16)
    kn = _dot(kvn, wkk_ref[...])
    _store_vt_blocks(mvt_ref, _dot_nt(wkvt_ref[...], kvn), MLA_HEADS)
    krope = small * cosk_ref[...] + pltpu.roll(small, MISC_KR - MISC_KRR, 1) * sin
    scale = (MLA_NOPE + MLA_ROPE) ** -0.5 * LOG2E
    for hd in range(MLA_HEADS):
        sl = slice(hd * HEAD_W, (hd + 1) * HEAD_W)
        mq_ref[:, sl] = ((q[:, sl] * cos + qr[:, sl] * sin) * scale).astype(BF16)
        mk_ref[:, sl] = (kn[:, sl] + krope).astype(BF16)


def _proj_specs(l, n_rows, tm, weights, tables, seq):
    n_main_tiles = n_rows // tm - 1
    tiles_per_seq = seq // tm

    def rows(width):
        return pl.BlockSpec((tm, width), lambda i: (i, 0))

    tab = pl.BlockSpec((tm, LANES), lambda i: (
        jnp.where(i < n_main_tiles, i % tiles_per_seq, tiles_per_seq), 0))
    vt_rows = FOX_HEADS * DV_AUG
    t_spec = pl.BlockSpec((tm // BLOCK, vt_rows, BLOCK), lambda i: (i, 0, 0))
    t_shape = jax.ShapeDtypeStruct((n_rows // BLOCK, vt_rows, BLOCK), BF16)
    qk_w = FOX_HEADS * HEAD_W
    row_outs = [(C_ZX, F32), (LANES, F32), (qk_w, BF16), (qk_w, BF16)]
    args = [*weights, *tables]
    in_specs = [_layer(w, l) for w in weights] + [tab, tab, tab]
    out_specs = [rows(w) for w, _ in row_outs] + [t_spec, rows(qk_w), rows(qk_w), t_spec]
    out_shape = ([jax.ShapeDtypeStruct((n_rows, w), dt) for w, dt in row_outs]
                 + [t_shape, jax.ShapeDtypeStruct((n_rows, qk_w), BF16),
                    jax.ShapeDtypeStruct((n_rows, qk_w), BF16), t_shape])
    assert len(args) == N_PROJ_IN and len(out_shape) == N_PROJ_OUT
    return args, in_specs, out_specs, out_shape


def _first_step_block(b, meta_block, n_fill, own_first_block):
    return jnp.where(b == 0, meta_block,
                     jnp.where(b <= n_fill, meta_block - b, own_first_block))


def _n_fill_blocks(n_rows, bsz, seq, block_rows):
    n_fill = (n_rows - bsz * seq) // block_rows - 1
    assert bsz > n_fill, "needs one batch row per meta-tile block to zero-fill"
    return n_fill


def _ssd_kernel(n_fill, zx_ref, small_ref, cw_ref, cb_ref, dtb_ref, a_ref, dsk_ref, ng_ref,
                o_ref, conv_scr, s_scr, meta_conv_scr, meta_s_scr):
    b = pl.program_id(0)
    c = pl.program_id(1)

    @pl.when((c == 0) & (b >= 1) & (b <= n_fill))
    def _():
        o_ref[...] = jnp.zeros(o_ref.shape, o_ref.dtype)

    @pl.when((c == 0) & (b == 0))
    def _():
        conv_scr[0:8, :] = jnp.zeros((8, SSD_CONV_DIM), F32)
        s_scr[...] = jnp.zeros(s_scr.shape, F32)

    @pl.when((c == 0) & (b > 0))
    def _():
        conv_scr[0:8, :] = meta_conv_scr[...]
        s_scr[...] = meta_s_scr[...]

    @pl.when((c > 0) | (b == 0))
    def _():
        _ssd_step(c, zx_ref, small_ref, cw_ref, cb_ref, dtb_ref, a_ref, dsk_ref, ng_ref,
                  o_ref, conv_scr, s_scr)

    @pl.when((c == 0) & (b == 0))
    def _():
        meta_conv_scr[...] = conv_scr[0:8, :]
        meta_s_scr[...] = s_scr[...]


def _ssd_step(c, zx_ref, small_ref, cw_ref, cb_ref, dtb_ref, a_ref, dsk_ref, ng_ref,
              o_ref, conv_scr, s_scr):
    R = SSD_STEP_ROWS
    row = lax.broadcasted_iota(jnp.int32, (R, 1), 0)
    valid = jnp.logical_or(c > 0, row >= R - N_META)

    conv_scr[8:8 + R, :] = jnp.where(valid, zx_ref[:, C_XBC:C_ZX], 0.0)
    acc = cb_ref[...]
    for k in range(SSD_CONV):
        off = 8 - (SSD_CONV - 1) + k
        acc = acc + cw_ref[k:k + 1, :] * conv_scr[off:off + R, :]
    conv_scr[0:8, :] = conv_scr[R:R + 8, :]
    xbc = acc * _sigmoid(acc)
    dt = jnp.where(valid, _softplus(small_ref[...] + dtb_ref[...]), 0.0)
    for ch in range(R // BLOCK):
        rows = slice(ch * BLOCK, (ch + 1) * BLOCK)
        _ssd_chunk(xbc[rows], dt[rows], zx_ref.at[rows], a_ref, dsk_ref, ng_ref,
                   o_ref.at[rows], s_scr)


def _ssd_chunk(xbc, dt, zx_ref, a_ref, dsk_ref, ng_ref, o_ref, s_scr):
    Q = BLOCK
    lane = lax.broadcasted_iota(jnp.int32, (1, LANES), 1)
    lane_lo = lane < SSD_HEAD_DIM
    sub = lax.broadcasted_iota(jnp.int32, (LANES, 1), 0)
    bm = xbc[:, SSD_D:SSD_D + LANES]
    cm = xbc[:, SSD_D + LANES:SSD_D + 2 * LANES]

    a = dt * a_ref[...]
    tri = (lax.broadcasted_iota(jnp.int32, (Q, Q), 0)
           >= lax.broadcasted_iota(jnp.int32, (Q, Q), 1))
    tri_b = jnp.where(tri, 1.0, 0.0).astype(BF16)
    a1, a2, a3 = _split3(a)
    a_cum = _dot(tri_b, a1) + _dot(tri_b, a2) + _dot(tri_b, a3)
    a_cum_t = a_cum.T
    bm_t = bm.T

    cm_b = cm.astype(BF16)
    bm_b = bm.astype(BF16)
    cb_g = [_dot_nt(jnp.where(lane_lo, cm, 0.0).astype(BF16), bm_b),
            _dot_nt(jnp.where(lane_lo, 0.0, cm).astype(BF16), bm_b)]
    rows_g = [sub < SSD_STATE, sub >= SSD_STATE]

    pairs_per_group = SSD_HEADS // 2 // SSD_GROUPS
    y_pairs = []
    for p in range(SSD_HEADS // 2):
        g = p // pairs_per_group
        psl = slice(p * LANES, (p + 1) * LANES)
        xs_p = xbc[:, psl]
        dt_pair = jnp.where(lane_lo, dt[:, 2 * p:2 * p + 1], dt[:, 2 * p + 1:2 * p + 2])
        xdt = (xs_p * dt_pair).astype(BF16)
        s_old = s_scr[p]
        yd, upd, e_col, e_last = [], [], [], []
        for par in range(2):
            hd = 2 * p + par
            col = a_cum[:, hd:hd + 1]
            rowv = a_cum_t[hd:hd + 1, :]
            last = a_cum_t[hd:hd + 1, Q - 1:Q]
            seg = jnp.exp(jnp.where(tri, col - rowv, NEG_BIG))
            yd.append(_dot((cb_g[g] * seg).astype(BF16), xdt))
            upd.append(_dot((bm_t * jnp.exp(last - rowv)).astype(BF16), xdt))
            e_col.append(jnp.exp(col))
            e_last.append(jnp.exp(last))
        y_off = _dot(cm_b, s_old.astype(BF16)) * jnp.where(lane_lo, e_col[0], e_col[1])
        s_new = (jnp.where(lane_lo, e_last[0], e_last[1]) * s_old
                 + jnp.where(rows_g[g], jnp.where(lane_lo, upd[0], upd[1]), 0.0))
        s_scr[p] = s_new
        y_p = jnp.where(lane_lo, yd[0], yd[1]) + y_off + dsk_ref[:, psl] * xs_p
        z_p = zx_ref[:, psl]
        y_pairs.append(y_p * (z_p * _sigmoid(z_p)))

    for g in range(SSD_GROUPS):
        ps = range(g * pairs_per_group, (g + 1) * pairs_per_group)
        ss = sum(jnp.sum(y_pairs[p] * y_pairs[p], axis=-1, keepdims=True) for p in ps)
        inv = lax.rsqrt(ss * (1.0 / (pairs_per_group * LANES)) + EPS)
        for p in ps:
            psl = slice(p * LANES, (p + 1) * LANES)
            o_ref[:, psl] = (y_pairs[p] * inv * ng_ref[:, psl]).astype(BF16)


def _ssd(l, zx, small, params, bsz, seq):
    n_rows = zx.shape[0]
    R = SSD_STEP_ROWS
    steps_per_seq = seq // R
    meta_block = n_rows // R - 1
    n_fill = _n_fill_blocks(n_rows, bsz, seq, R)

    def block(b, c):
        first = _first_step_block(b, meta_block, n_fill, b * steps_per_seq)
        return (jnp.where(c == 0, first, b * steps_per_seq + c - 1), 0)

    return pl.pallas_call(
        functools.partial(_ssd_kernel, n_fill),
        grid=(bsz, steps_per_seq + 1),
        in_specs=[pl.BlockSpec((R, C_ZX), block), pl.BlockSpec((R, LANES), block)]
                 + [_layer(p, l) for p in params],
        out_specs=pl.BlockSpec((R, SSD_D), block),
        out_shape=jax.ShapeDtypeStruct((n_rows, SSD_D), BF16),
        scratch_shapes=[pltpu.VMEM((R + 8, SSD_CONV_DIM), F32),
                        pltpu.VMEM((SSD_HEADS // 2, LANES, LANES), F32),
                        pltpu.VMEM((8, SSD_CONV_DIM), F32),
                        pltpu.VMEM((SSD_HEADS // 2, LANES, LANES), F32)],
        compiler_params=pltpu.CompilerParams(
            dimension_semantics=("arbitrary", "arbitrary"), vmem_limit_bytes=VMEM_LIMIT),
        name="ssd_mixer",
    )(zx, small, *params)


def _fox_keys_kernel(km_ref, k_ref, sm_ref, s_ref, fb_ref, om_ref, o_ref):
    T = BLOCK
    width = k_ref.shape[1]
    row = lax.broadcasted_iota(jnp.int32, (T, 1), 0)
    tri = (lax.broadcasted_iota(jnp.int32, (T, T), 0)
           >= lax.broadcasted_iota(jnp.int32, (T, T), 1))
    tri_b = jnp.where(tri, 1.0, 0.0).astype(BF16)
    src = lax.broadcasted_iota(jnp.int32, (LANES, width), 0)
    dst = lax.broadcasted_iota(jnp.int32, (LANES, width), 1)
    dst_head = jnp.right_shift(dst, HEAD_W.bit_length() - 1)
    dst_lane = jnp.bitwise_and(dst, HEAD_W - 1)
    sel = jnp.concatenate(
        [jnp.where((src == SMALL_F + dst_head) & (dst_lane == FOX_BIAS_LANE + i),
                   1.0, 0.0).astype(BF16) for i in range(3)], axis=0)

    def local_cumsum(small_blk, is_meta):
        log_f = -_softplus(-(small_blk + fb_ref[...]))
        if is_meta:
            log_f = jnp.where(row < PAD, 0.0, log_f)
        parts = _dot(tri_b, jnp.concatenate(_split3(log_f), axis=1))
        return parts[:, :LANES] + parts[:, LANES:2 * LANES] + parts[:, 2 * LANES:]

    def keys(c_blk, k_blk):
        pieces = jnp.concatenate(_split3(c_blk * (-LOG2E)), axis=1)
        return (k_blk.astype(F32) + _dot(pieces, sel)).astype(BF16)

    n_blocks = k_ref.shape[0] // T
    rows = [slice(j * T, (j + 1) * T) for j in range(n_blocks)]
    c_meta = local_cumsum(sm_ref[...], True)
    local = [local_cumsum(s_ref[rows[j], :], False) for j in range(n_blocks)]
    om_ref[...] = keys(c_meta, km_ref[...])
    carry = c_meta[T - 1:T, :]
    for j in range(n_blocks):
        c_blk = local[j] + carry
        carry = c_blk[T - 1:T, :]
        o_ref[rows[j], :] = keys(c_blk, k_ref[rows[j], :])


def _fox_keys(l, k, small, fb, bsz, seq):
    n_rows, width = k.shape
    meta_block = n_rows // BLOCK - 1
    main = lambda w: pl.BlockSpec((seq, w), lambda b: (b, 0))
    meta = lambda w: pl.BlockSpec((BLOCK, w), lambda b: (meta_block, 0))
    return pl.pallas_call(
        _fox_keys_kernel,
        grid=(bsz,),
        in_specs=[meta(width), main(width), meta(LANES), main(LANES), _layer(fb, l)],
        out_specs=[pl.BlockSpec((BLOCK, width), lambda b: (0, 0)), main(width)],
        out_shape=[jax.ShapeDtypeStruct((BLOCK, width), BF16),
                   jax.ShapeDtypeStruct((bsz * seq, width), BF16)],
        compiler_params=pltpu.CompilerParams(
            dimension_semantics=("arbitrary",), vmem_limit_bytes=VMEM_LIMIT),
        name="fox_keys",
    )(k, k, small, small, fb)


def _attn_kernel(n_heads, n_fill, q_ref, km_ref, k_ref, vtm_ref, vt_ref, o_ref,
                 acc_scr, s0_scr, s1_scr):
    t = pl.program_id(1)
    b = pl.program_id(0)
    dv = acc_scr.shape[1]

    @pl.when((t == 0) & (b >= 1) & (b <= n_fill))
    def _():
        o_ref[...] = jnp.zeros(o_ref.shape, o_ref.dtype)

    heads = range(n_heads)
    hsl = [slice(hd * HEAD_W, (hd + 1) * HEAD_W) for hd in heads]
    vsl = [slice(hd * dv, (hd + 1) * dv) for hd in heads]
    qs = [q_ref[:, hsl[hd]] for hd in heads]
    ahead = (lax.broadcasted_iota(jnp.int32, (TQ, TQ), 0)
             - lax.broadcasted_iota(jnp.int32, (TQ, TQ), 1))

    def softmax_pv(ss, vts, ms, mask):
        if mask is not None:
            ss = [jnp.where(mask, s, NEG_BIG) for s in ss]
        new_ms = [jnp.maximum(ms[hd], jnp.max(ss[hd], axis=0, keepdims=True)) for hd in heads]
        corrs = [jnp.exp2(ms[hd] - new_ms[hd]) for hd in heads]
        prs = [jnp.exp2(ss[hd] - new_ms[hd]).astype(BF16) for hd in heads]
        for hd in heads:
            acc_scr[hd] = corrs[hd] * acc_scr[hd] + _dot(vts[hd], prs[hd])
        return tuple(new_ms)

    def meta_chunk(carry, mask):
        return softmax_pv([_dot_nt(km_ref[PAD:, hsl[hd]], qs[hd]) for hd in heads],
                          [vtm_ref[0, vsl[hd], PAD:] for hd in heads], carry, mask)

    def scores(c, slot_scr):
        for hd in heads:
            slot_scr[hd] = _dot_nt(k_ref[c * TQ:(c + 1) * TQ, hsl[hd]], qs[hd])

    def chunk_vt(c, hd):
        n_sub = TQ // BLOCK
        return jnp.concatenate([vt_ref[n_sub * c + i, vsl[hd], :] for i in range(n_sub)], axis=1)

    def consume(c, slot_scr, carry, mask):
        return softmax_pv([slot_scr[hd] for hd in heads], [chunk_vt(c, hd) for hd in heads],
                          carry, mask)

    def finish():
        dv_out = o_ref.shape[1] // n_heads
        ys = [acc_scr[hd, 0:dv_out, :] * (1.0 / acc_scr[hd, dv_out:dv_out + 1, :])
              for hd in heads]
        for p in range(n_heads // 2):
            y_t = jnp.concatenate([ys[2 * p], ys[2 * p + 1]], axis=0)
            o_ref[:, p * LANES:(p + 1) * LANES] = y_t.T.astype(BF16)

    init = tuple(jnp.full((1, TQ), M_INIT, F32) for _ in heads)

    @pl.when((t == 0) & (b == 0))
    def _():
        acc_scr[...] = jnp.zeros(acc_scr.shape, F32)
        meta_chunk(init, ahead[:N_META, :] <= -(TQ - N_META))
        finish()

    diagonal = ahead <= 0
    slots = (s0_scr, s1_scr)
    for n_chunks in range(1, k_ref.shape[0] // TQ + 1):
        @pl.when(t == n_chunks)
        def _(n_chunks=n_chunks):
            scores(0, slots[0])
            s_meta = [_dot_nt(km_ref[PAD:, hsl[hd]], qs[hd]) for hd in heads]
            if n_chunks > 1:
                scores(1, slots[1])
            s_first = [slots[0][hd] for hd in heads]
            if n_chunks == 1:
                s_first = [jnp.where(diagonal, s, NEG_BIG) for s in s_first]
            ms = tuple(jnp.maximum(jnp.max(s_first[hd], axis=0, keepdims=True),
                                   jnp.max(s_meta[hd], axis=0, keepdims=True)) for hd in heads)
            for hd in heads:
                acc_scr[hd] = (_dot(chunk_vt(0, hd), jnp.exp2(s_first[hd] - ms[hd]).astype(BF16))
                               + _dot(vtm_ref[0, vsl[hd], PAD:],
                                      jnp.exp2(s_meta[hd] - ms[hd]).astype(BF16)))
            for c in range(1, n_chunks):
                if c + 1 < n_chunks:
                    scores(c + 1, slots[(c + 1) % 2])
                ms = consume(c, slots[c % 2], ms, diagonal if c == n_chunks - 1 else None)
            finish()


def _attention(q, k_meta, k_main, vt, n_heads, bsz, seq, meta_k_block, name):
    n_rows, qw = q.shape
    dv_aug = vt.shape[1] // n_heads
    dv = dv_aug - (DV_AUG - MLA_V)
    tiles_per_seq = seq // TQ
    meta_q_tile = n_rows // TQ - 1
    meta_block = n_rows // BLOCK - 1
    n_fill = _n_fill_blocks(n_rows, bsz, seq, TQ)

    def q_tile(b, t):
        first = _first_step_block(b, meta_q_tile, n_fill, b * tiles_per_seq)
        return (jnp.where(t == 0, first, b * tiles_per_seq + t - 1), 0)

    return pl.pallas_call(
        functools.partial(_attn_kernel, n_heads, n_fill),
        grid=(bsz, tiles_per_seq + 1),
        in_specs=[pl.BlockSpec((TQ, qw), q_tile),
                  pl.BlockSpec((BLOCK, qw), lambda b, t: (meta_k_block, 0)),
                  pl.BlockSpec((seq, qw), lambda b, t: (b, 0)),
                  pl.BlockSpec((1,) + vt.shape[1:], lambda b, t: (meta_block, 0, 0)),
                  pl.BlockSpec((seq // BLOCK,) + vt.shape[1:], lambda b, t: (b, 0, 0))],
        out_specs=pl.BlockSpec((TQ, n_heads * dv), q_tile),
        out_shape=jax.ShapeDtypeStruct((n_rows, n_heads * dv), BF16),
        scratch_shapes=[pltpu.VMEM((n_heads, dv_aug, TQ), F32),
                        pltpu.VMEM((n_heads, TQ, TQ), F32),
                        pltpu.VMEM((n_heads, TQ, TQ), F32)],
        compiler_params=pltpu.CompilerParams(
            dimension_semantics=("arbitrary", "arbitrary"), vmem_limit_bytes=VMEM_LIMIT),
        name=name,
    )(q, k_meta, k_main, vt, vt)


def _select_cols(w, cols, transpose=False):
    place = np.zeros((w.shape[-1], len(cols)), np.float32)
    for j, entry in enumerate(cols):
        if entry is not None:
            place[entry[0], j] = entry[1]
    out = jnp.einsum('lkn,nm->lmk' if transpose else 'lkn,nm->lkm',
                     w.astype(BF16).astype(F32), jnp.asarray(place))
    return out.astype(BF16)


def _span(start, n, sign=1.0):
    return [(start + i, sign) for i in range(n)]


def _rot_half_span(start):
    half = MLA_ROPE // 2
    return _span(start + half, half, -1.0) + _span(start, half)


def _in_proj_weights(w_in):
    sizes = [SSD_D, SSD_CONV_DIM, SSD_HEADS, FOX_D, FOX_D, FOX_D, FOX_HEADS,
             MLA_Q_LORA, MLA_KV_LORA, MLA_ROPE]
    z, xbc, dt, fq, fk, fv, fr, cq, ckv, kr = [int(o) - C_ZX for o in
                                               np.cumsum([0] + sizes[:-1])]
    del z, xbc
    misc = (_span(dt, SSD_HEADS) + _span(fr, FOX_HEADS)
            + [None] * (MISC_KRR - SMALL_F - FOX_HEADS) + _rot_half_span(kr)
            + _span(kr, MLA_ROPE) + [None] * (LANES - MISC_KR - MLA_ROPE))
    rest = w_in[..., C_ZX:]
    w_rest = _select_cols(rest, _span(fq, FOX_D) + _span(fk, FOX_D) + _span(cq, MLA_Q_LORA)
                          + _span(ckv, MLA_KV_LORA) + misc)
    return w_in[..., :C_ZX].astype(BF16), w_rest, _select_cols(rest, _span(fv, FOX_D), True)


def _mla_weights(w_uq, w_ukv):
    qd, kvd = MLA_NOPE + MLA_ROPE, MLA_NOPE + MLA_V
    pad_q = [None] * (HEAD_W - qd)
    wuq = _select_cols(w_uq, sum([_span(h * qd, qd) + pad_q for h in range(MLA_HEADS)], []))
    wuqr = _select_cols(w_uq, sum([[None] * MLA_NOPE + _rot_half_span(h * qd + MLA_NOPE) + pad_q
                                   for h in range(MLA_HEADS)], []))
    wkk = _select_cols(w_ukv, sum([_span(h * kvd, MLA_NOPE) + [None] * (HEAD_W - MLA_NOPE)
                                   for h in range(MLA_HEADS)], []))
    wvt = _select_cols(w_ukv, sum([_span(h * kvd + MLA_NOPE, MLA_V) for h in range(MLA_HEADS)],
                                  []), True)
    return wuq, wuqr, wkk, wvt


def _rows(v, width=None, offset=0):
    v = v.astype(F32)
    if width is not None:
        v = jnp.pad(v, ((0, 0), (offset, width - offset - v.shape[-1])))
    return v[:, None, :]


def _position_tables(seq, tm):
    pos = jnp.concatenate([N_META + jnp.arange(seq, dtype=F32),
                           jnp.arange(tm, dtype=F32) - (tm - N_META)])
    inv_freq = 1.0 / (ROPE_THETA ** (jnp.arange(0, MLA_ROPE, 2, dtype=F32) / MLA_ROPE))
    ang = pos[:, None] * inv_freq[None, :]
    cos, sin = jnp.cos(ang), jnp.sin(ang)
    n = seq + tm
    rope_pad = LANES - MLA_NOPE - MLA_ROPE
    cos128 = jnp.concatenate([jnp.ones((n, MLA_NOPE), F32), cos, cos,
                              jnp.zeros((n, rope_pad), F32)], axis=-1)
    sin128 = jnp.concatenate([jnp.zeros((n, MLA_NOPE), F32), sin, sin,
                              jnp.zeros((n, rope_pad), F32)], axis=-1)
    cosk128 = jnp.concatenate([jnp.zeros((n, MLA_NOPE), F32), cos, cos,
                               jnp.zeros((n, rope_pad), F32)], axis=-1)
    return cos128, sin128, cosk128


def kernel(x, meta, ffn1_w_gate, ffn1_w_up, ffn1_w_down, ln1_g, ln1_b, w_in, conv_w, conv_b, dt_bias, a_log, d_skip, ssd_norm_g, fox_f_b, mla_q_norm_g, mla_w_uq, mla_kv_norm_g, mla_w_ukv, w_out, ln2_g, ln2_b, ffn2_w_gate, ffn2_w_up, ffn2_w_down, ln3_g, ln3_b):
    bsz, seq, _ = x.shape
    assert seq % TQ == 0
    tm = TQ
    n_main = bsz * seq
    n_rows = n_main + tm
    meta_block = n_rows // BLOCK - 1

    h = x.reshape(n_main, D_MODEL)
    meta_tile = jnp.concatenate([jnp.zeros((tm - N_META, D_MODEL), x.dtype),
                                 meta.astype(x.dtype)], axis=0)
    tables = _position_tables(seq, tm)
    bf = lambda w: w.astype(BF16)

    ffn1 = (bf(ffn1_w_gate), bf(ffn1_w_up), bf(ffn1_w_down), _rows(ln1_g), _rows(ln1_b))
    ffn2 = (bf(ffn2_w_gate), bf(ffn2_w_up), bf(ffn2_w_down), _rows(ln3_g), _rows(ln3_b))
    proj_w = (*_in_proj_weights(w_in), *_mla_weights(mla_w_uq, mla_w_ukv),
              _rows(mla_q_norm_g), _rows(mla_kv_norm_g))
    ssd_p = (conv_w.astype(F32), _rows(conv_b), _rows(dt_bias, LANES),
             _rows(-jnp.exp(a_log.astype(F32)), LANES),
             _rows(jnp.repeat(d_skip, SSD_HEAD_DIM, axis=-1)), _rows(ssd_norm_g))
    fb = _rows(fox_f_b, LANES, SMALL_F)
    out_p = (bf(w_out), _rows(ln2_g), _rows(ln2_b))

    for l in range(DEPTH):
        h, zx, small, fq, fk, fvt, mq, mk, mvt = _ffn_ln(
            l, h, ffn1, tm, n_rows, meta_tile=meta_tile if l == 0 else None,
            proj=(proj_w, tables, seq))
        y_ssd = _ssd(l, zx, small, ssd_p, bsz, seq)
        fk_meta, fk_main = _fox_keys(l, fk, small, fb, bsz, seq)
        y_fox = _attention(fq, fk_meta, fk_main, fvt, FOX_HEADS, bsz, seq, 0, "fox_attention")
        y_mla = _attention(mq, mk, mk, mvt, MLA_HEADS, bsz, seq, meta_block, "mla_attention")
        h = _ffn_ln(l, h, ffn2, tm, n_rows if l < DEPTH - 1 else n_main,
                    mix=(y_ssd, y_fox, y_mla, *out_p))

    return h.reshape(bsz, seq, D_MODEL)
```

```python
import functools

import numpy as np
import jax
import jax.numpy as jnp
from jax import lax
from jax.experimental import pallas as pl
from jax.experimental.pallas import tpu as pltpu

F32 = jnp.float32
BF16 = jnp.bfloat16

D_MODEL = 1024
DEPTH = 2
N_META = 16
BLOCK = 128
SSD_HEADS = 8
SSD_HEAD_DIM = 64
SSD_D = SSD_HEADS * SSD_HEAD_DIM
SSD_GROUPS = 2
SSD_STATE = 64
SSD_CONV = 4
SSD_CONV_DIM = SSD_D + 2 * SSD_GROUPS * SSD_STATE
FOX_HEADS = 4
FOX_HEAD_DIM = 64
FOX_D = FOX_HEADS * FOX_HEAD_DIM
MLA_HEADS = 4
MLA_Q_LORA = 256
MLA_KV_LORA = 128
MLA_NOPE = 64
MLA_ROPE = 32
MLA_V = 64
MLA_D = MLA_HEADS * MLA_V
ROPE_THETA = 10000.0
D_MIX = SSD_D + FOX_D + MLA_D
D_FF = 2816
ALPHA = (2 * DEPTH) ** 0.25
EPS = 1e-5

LANES = 128
MXU_W = 256
FF_CHUNK = MXU_W
N_FF_CHUNKS = D_FF // FF_CHUNK
N_OUT_CHUNKS = D_MODEL // FF_CHUNK
TQ = 4 * BLOCK
SSD_STEP_ROWS = 4 * BLOCK
PAD = BLOCK - N_META
NEG_BIG = -1e30
M_INIT = 2 * NEG_BIG
VMEM_LIMIT = 56 * 1024 * 1024

C_XBC = SSD_D
C_ZX = SSD_D + SSD_CONV_DIM
C_FQ = 0
C_CQ = C_FQ + 2 * FOX_D
C_MISC = C_CQ + MLA_Q_LORA
N_IN_ARR = C_MISC + MLA_KV_LORA + LANES
MISC_KRR = 32
MISC_KR = MLA_NOPE
LOG2E = 1.4426950408889634
DV_AUG = MLA_V + 16
SMALL_F = 8
HEAD_W = LANES
FOX_BIAS_LANE = FOX_HEAD_DIM


def _sigmoid(x):
    return 1.0 / (1.0 + jnp.exp(-x))


def _softplus(x):
    return jnp.maximum(x, 0.0) + jnp.log(1.0 + jnp.exp(-jnp.abs(x)))


def _layer_norm_rows(y, g, b):
    mu = jnp.mean(y, axis=-1, keepdims=True)
    yc = y - mu
    var = jnp.mean(yc * yc, axis=-1, keepdims=True)
    return yc * lax.rsqrt(var + EPS) * g + b


def _split3(x):
    x1 = x.astype(BF16)
    r1 = x - x1.astype(F32)
    x2 = r1.astype(BF16)
    r2 = r1 - x2.astype(F32)
    return x1, x2, r2.astype(BF16)


def _dot(a, b):
    return jnp.dot(a, b, preferred_element_type=F32)


def _dot_nt(a, b):
    return lax.dot_general(a, b, (((1,), (1,)), ((), ())), preferred_element_type=F32)


def _resident(shape):
    return pl.BlockSpec(shape, lambda *_: (0,) * len(shape), pipeline_mode=pl.Buffered(1))


def _layer(arr, l):
    tail = (0,) * (arr.ndim - 1)
    return pl.BlockSpec((None,) + arr.shape[1:], lambda *_: (l,) + tail,
                        pipeline_mode=pl.Buffered(1))


N_PROJ_IN = 12
N_PROJ_OUT = 8


def _ffn_ln_kernel(prologue, n_main_tiles, with_proj, *refs):
    a_scr, y_scr = refs[-2:]
    refs = refs[:-2]
    if with_proj:
        proj_out, refs = refs[-N_PROJ_OUT:], refs[:-N_PROJ_OUT]
        o_ref = refs[-1]
        proj_in, refs = refs[-1 - N_PROJ_IN:-1], refs[:-1 - N_PROJ_IN]
    else:
        o_ref, refs = refs[-1], refs[:-1]
    wg_ref, wu_ref, wd_ref, g_ref, b_ref = refs[-5:]
    if prologue == "plain":
        x = refs[0][...]
    elif prologue == "meta":
        x_ref, m_ref = refs[:2]
        x = jnp.where(pl.program_id(0) < n_main_tiles, x_ref[...], m_ref[...])
    else:
        h_ref, ys_ref, yf_ref, ym_ref, wo_ref, g2_ref, b2_ref = refs[:7]
        mix = (_dot(ys_ref[...], wo_ref[0:SSD_D, :])
               + _dot(yf_ref[...], wo_ref[SSD_D:SSD_D + FOX_D, :])
               + _dot(ym_ref[...], wo_ref[SSD_D + FOX_D:D_MIX, :]))
        x = _layer_norm_rows(ALPHA * h_ref[...] + mix, g2_ref[...], b2_ref[...])
    xb = x.astype(BF16)

    for c in range(N_FF_CHUNKS):
        cols = slice(c * FF_CHUNK, (c + 1) * FF_CHUNK)
        gate = _dot(xb, wg_ref[:, cols])
        up = _dot(xb, wu_ref[:, cols])
        a_scr[c] = (gate * _sigmoid(gate) * up).astype(BF16)

    tm = x.shape[0]
    outs = []
    for rows in (slice(0, tm // 2), slice(tm // 2, tm)):
        for n in range(N_OUT_CHUNKS):
            cols = slice(n * FF_CHUNK, (n + 1) * FF_CHUNK)
            acc = _dot(a_scr[0, rows], wd_ref[0:FF_CHUNK, cols])
            for c in range(1, N_FF_CHUNKS):
                acc = acc + _dot(a_scr[c, rows], wd_ref[c * FF_CHUNK:(c + 1) * FF_CHUNK, cols])
            y_scr[rows, cols] = ALPHA * x[rows, cols] + 0.5 * acc
        outs.append(_layer_norm_rows(y_scr[rows, :], g_ref[...], b_ref[...]))
        o_ref[rows, :] = outs[-1]
    if with_proj:
        _proj_body(jnp.concatenate(outs, axis=0), *proj_in, *proj_out)


def _ffn_ln(l, h, ffn, tm, n_out_rows, meta_tile=None, mix=None, proj=None):
    def rows(width):
        return pl.BlockSpec((tm, width), lambda i: (i, 0))

    out_specs = rows(D_MODEL)
    out_shape = jax.ShapeDtypeStruct((n_out_rows, D_MODEL), F32)
    p_args, p_specs = [], []
    if proj is not None:
        p_args, p_specs, p_out_specs, p_out_shape = _proj_specs(l, n_out_rows, tm, *proj)
        out_specs, out_shape = [out_specs] + p_out_specs, [out_shape] + p_out_shape

    if mix is not None:
        prologue, n_main = "mix", None
        x_args = [h, *mix]
        x_specs = ([rows(D_MODEL), rows(SSD_D), rows(FOX_D), rows(MLA_D)]
                   + [_layer(p, l) for p in mix[3:]])
    elif meta_tile is not None:
        prologue, n_main = "meta", h.shape[0] // tm
        x_args = [h, meta_tile]
        x_specs = [pl.BlockSpec((tm, D_MODEL), lambda i: (jnp.minimum(i, n_main - 1), 0)),
                   _resident(meta_tile.shape)]
    else:
        prologue, n_main, x_args, x_specs = "plain", None, [h], [rows(D_MODEL)]
    return pl.pallas_call(
        functools.partial(_ffn_ln_kernel, prologue, n_main, proj is not None),
        grid=(n_out_rows // tm,),
        in_specs=x_specs + [_layer(p, l) for p in ffn] + p_specs,
        out_specs=out_specs,
        out_shape=out_shape,
        scratch_shapes=[pltpu.VMEM((N_FF_CHUNKS, tm, FF_CHUNK), BF16),
                        pltpu.VMEM((tm, D_MODEL), F32)],
        compiler_params=pltpu.CompilerParams(
            dimension_semantics=("arbitrary",), vmem_limit_bytes=VMEM_LIMIT),
        name="ffn_ln_proj" if proj is not None else "ffn_ln",
    )(*x_args, *ffn, *p_args)


def _rms_rows(x, g):
    return x * lax.rsqrt(jnp.mean(x * x, axis=-1, keepdims=True) + EPS) * g


def _store_vt_blocks(out_ref, val_t, n_heads):
    dv = val_t.shape[0] // n_heads
    ones = jnp.ones((DV_AUG - dv, BLOCK), out_ref.dtype)
    for r in range(out_ref.shape[0]):
        for hd in range(n_heads):
            out_ref[r, hd * DV_AUG:hd * DV_AUG + dv, :] = (
                val_t[hd * dv:(hd + 1) * dv, r * BLOCK:(r + 1) * BLOCK].astype(out_ref.dtype))
            out_ref[r, hd * DV_AUG + dv:(hd + 1) * DV_AUG, :] = ones


def _proj_body(h, wzx_ref, win_ref, wfvt_ref, wuq_ref, wuqr_ref, wkk_ref, wkvt_ref,
               qg_ref, kvg_ref, cos_ref, sin_ref, cosk_ref,
               zx_ref, small_ref, fq_ref, fk_ref, fvt_ref, mq_ref, mk_ref, mvt_ref):
    hb = h.astype(BF16)
    zx_ref[...] = _dot(hb, wzx_ref[...])
    misc = _dot(hb, win_ref[:, C_MISC:N_IN_ARR])
    small = misc[:, MLA_KV_LORA:]
    small_ref[...] = small

    lane = lax.broadcasted_iota(jnp.int32, (1, LANES), 1)
    lane_lo = lane < FOX_HEAD_DIM
    fox_one = jnp.where((lane >= FOX_BIAS_LANE) & (lane < FOX_BIAS_LANE + 3), 1.0, 0.0)

    fqk = _dot(hb, win_ref[:, C_FQ:C_CQ])
    for half, (ref, scale, extra) in enumerate([(fq_ref, FOX_HEAD_DIM ** -0.5 * LOG2E, fox_one),
                                                (fk_ref, None, None)]):
        for pair in range(FOX_HEADS // 2):
            c0 = half * FOX_D + pair * LANES
            both = fqk[:, c0:c0 + LANES]
            if scale is not None:
                both = both * scale
            for par, val in enumerate([both, pltpu.roll(both, FOX_HEAD_DIM, 1)]):
                val = jnp.where(lane_lo, val, 0.0)
                if extra is not None:
                    val = val + extra
                hd = 2 * pair + par
                ref[:, hd * HEAD_W:(hd + 1) * HEAD_W] = val.astype(BF16)
    _store_vt_blocks(fvt_ref, _dot_nt(wfvt_ref[...], hb), FOX_HEADS)

    cos = cos_ref[...]
    sin = sin_ref[...]
    cqn = _rms_rows(_dot(hb, win_ref[:, C_CQ:C_MISC]), qg_ref[...]).astype(BF16)
    q = _dot(cqn, wuq_ref[...])
    qr = _dot(cqn, wuqr_ref[...])
    kvn = _rms_rows(misc[:, :MLA_KV_LORA], kvg_ref[...]).astype(BF16)
    kn = _dot(kvn, wkk_ref[...])
    _store_vt_blocks(mvt_ref, _dot_nt(wkvt_ref[...], kvn), MLA_HEADS)
    krope = small * cosk_ref[...] + pltpu.roll(small, MISC_KR - MISC_KRR, 1) * sin
    scale = (MLA_NOPE + MLA_ROPE) ** -0.5 * LOG2E
    for hd in range(MLA_HEADS):
        sl = slice(hd * HEAD_W, (hd + 1) * HEAD_W)
        mq_ref[:, sl] = ((q[:, sl] * cos + qr[:, sl] * sin) * scale).astype(BF16)
        mk_ref[:, sl] = (kn[:, sl] + krope).astype(BF16)


def _proj_specs(l, n_rows, tm, weights, tables, seq):
    n_main_tiles = n_rows // tm - 1
    tiles_per_seq = seq // tm

    def rows(width):
        return pl.BlockSpec((tm, width), lambda i: (i, 0))

    tab = pl.BlockSpec((tm, LANES), lambda i: (
        jnp.where(i < n_main_tiles, i % tiles_per_seq, tiles_per_seq), 0))
    vt_rows = FOX_HEADS * DV_AUG
    t_spec = pl.BlockSpec((tm // BLOCK, vt_rows, BLOCK), lambda i: (i, 0, 0))
    t_shape = jax.ShapeDtypeStruct((n_rows // BLOCK, vt_rows, BLOCK), BF16)
    qk_w = FOX_HEADS * HEAD_W
    row_outs = [(C_ZX, F32), (LANES, F32), (qk_w, BF16), (qk_w, BF16)]
    args = [*weights, *tables]
    in_specs = [_layer(w, l) for w in weights] + [tab, tab, tab]
    out_specs = [rows(w) for w, _ in row_outs] + [t_spec, rows(qk_w), rows(qk_w), t_spec]
    out_shape = ([jax.ShapeDtypeStruct((n_rows, w), dt) for w, dt in row_outs]
                 + [t_shape, jax.ShapeDtypeStruct((n_rows, qk_w), BF16),
                    jax.ShapeDtypeStruct((n_rows, qk_w), BF16), t_shape])
    assert len(args) == N_PROJ_IN and len(out_shape) == N_PROJ_OUT
    return args, in_specs, out_specs, out_shape


def _first_step_block(b, meta_block, n_fill, own_first_block):
    return jnp.where(b == 0, meta_block,
                     jnp.where(b <= n_fill, meta_block - b, own_first_block))


def _n_fill_blocks(n_rows, bsz, seq, block_rows):
    n_fill = (n_rows - bsz * seq) // block_rows - 1
    assert bsz > n_fill, "needs one batch row per meta-tile block to zero-fill"
    return n_fill


def _ssd_kernel(n_fill, zx_ref, small_ref, cw_ref, cb_ref, dtb_ref, a_ref, dsk_ref, ng_ref,
                o_ref, conv_scr, s_scr, meta_conv_scr, meta_s_scr):
    b = pl.program_id(0)
    c = pl.program_id(1)

    @pl.when((c == 0) & (b >= 1) & (b <= n_fill))
    def _():
        o_ref[...] = jnp.zeros(o_ref.shape, o_ref.dtype)

    @pl.when((c == 0) & (b == 0))
    def _():
        conv_scr[0:8, :] = jnp.zeros((8, SSD_CONV_DIM), F32)
        s_scr[...] = jnp.zeros(s_scr.shape, F32)

    @pl.when((c == 0) & (b > 0))
    def _():
        conv_scr[0:8, :] = meta_conv_scr[...]
        s_scr[...] = meta_s_scr[...]

    @pl.when((c > 0) | (b == 0))
    def _():
        _ssd_step(c, zx_ref, small_ref, cw_ref, cb_ref, dtb_ref, a_ref, dsk_ref, ng_ref,
                  o_ref, conv_scr, s_scr)

    @pl.when((c == 0) & (b == 0))
    def _():
        meta_conv_scr[...] = conv_scr[0:8, :]
        meta_s_scr[...] = s_scr[...]


def _ssd_step(c, zx_ref, small_ref, cw_ref, cb_ref, dtb_ref, a_ref, dsk_ref, ng_ref,
              o_ref, conv_scr, s_scr):
    R = SSD_STEP_ROWS
    row = lax.broadcasted_iota(jnp.int32, (R, 1), 0)
    valid = jnp.logical_or(c > 0, row >= R - N_META)

    conv_scr[8:8 + R, :] = jnp.where(valid, zx_ref[:, C_XBC:C_ZX], 0.0)
    acc = cb_ref[...]
    for k in range(SSD_CONV):
        off = 8 - (SSD_CONV - 1) + k
        acc = acc + cw_ref[k:k + 1, :] * conv_scr[off:off + R, :]
    conv_scr[0:8, :] = conv_scr[R:R + 8, :]
    xbc = acc * _sigmoid(acc)
    dt = jnp.where(valid, _softplus(small_ref[...] + dtb_ref[...]), 0.0)
    for ch in range(R // BLOCK):
        rows = slice(ch * BLOCK, (ch + 1) * BLOCK)
        _ssd_chunk(xbc[rows], dt[rows], zx_ref.at[rows], a_ref, dsk_ref, ng_ref,
                   o_ref.at[rows], s_scr)


def _ssd_chunk(xbc, dt, zx_ref, a_ref, dsk_ref, ng_ref, o_ref, s_scr):
    Q = BLOCK
    lane = lax.broadcasted_iota(jnp.int32, (1, LANES), 1)
    lane_lo = lane < SSD_HEAD_DIM
    sub = lax.broadcasted_iota(jnp.int32, (LANES, 1), 0)
    bm = xbc[:, SSD_D:SSD_D + LANES]
    cm = xbc[:, SSD_D + LANES:SSD_D + 2 * LANES]

    a = dt * a_ref[...]
    tri = (lax.broadcasted_iota(jnp.int32, (Q, Q), 0)
           >= lax.broadcasted_iota(jnp.int32, (Q, Q), 1))
    tri_b = jnp.where(tri, 1.0, 0.0).astype(BF16)
    a1, a2, a3 = _split3(a)
    a_cum = _dot(tri_b, a1) + _dot(tri_b, a2) + _dot(tri_b, a3)
    a_cum_t = a_cum.T
    bm_t = bm.T

    cm_b = cm.astype(BF16)
    bm_b = bm.astype(BF16)
    cb_g = [_dot_nt(jnp.where(lane_lo, cm, 0.0).astype(BF16), bm_b),
            _dot_nt(jnp.where(lane_lo, 0.0, cm).astype(BF16), bm_b)]
    rows_g = [sub < SSD_STATE, sub >= SSD_STATE]

    pairs_per_group = SSD_HEADS // 2 // SSD_GROUPS
    y_pairs = []
    for p in range(SSD_HEADS // 2):
        g = p // pairs_per_group
        psl = slice(p * LANES, (p + 1) * LANES)
        xs_p = xbc[:, psl]
        dt_pair = jnp.where(lane_lo, dt[:, 2 * p:2 * p + 1], dt[:, 2 * p + 1:2 * p + 2])
        xdt = (xs_p * dt_pair).astype(BF16)
        s_old = s_scr[p]
        yd, upd, e_col, e_last = [], [], [], []
        for par in range(2):
            hd = 2 * p + par
            col = a_cum[:, hd:hd + 1]
            rowv = a_cum_t[hd:hd + 1, :]
            last = a_cum_t[hd:hd + 1, Q - 1:Q]
            seg = jnp.exp(jnp.where(tri, col - rowv, NEG_BIG))
            yd.append(_dot((cb_g[g] * seg).astype(BF16), xdt))
            upd.append(_dot((bm_t * jnp.exp(last - rowv)).astype(BF16), xdt))
            e_col.append(jnp.exp(col))
            e_last.append(jnp.exp(last))
        y_off = _dot(cm_b, s_old.astype(BF16)) * jnp.where(lane_lo, e_col[0], e_col[1])
        s_new = (jnp.where(lane_lo, e_last[0], e_last[1]) * s_old
                 + jnp.where(rows_g[g], jnp.where(lane_lo, upd[0], upd[1]), 0.0))
        s_scr[p] = s_new
        y_p = jnp.where(lane_lo, yd[0], yd[1]) + y_off + dsk_ref[:, psl] * xs_p
        z_p = zx_ref[:, psl]
        y_pairs.append(y_p * (z_p * _sigmoid(z_p)))

    for g in range(SSD_GROUPS):
        ps = range(g * pairs_per_group, (g + 1) * pairs_per_group)
        ss = sum(jnp.sum(y_pairs[p] * y_pairs[p], axis=-1, keepdims=True) for p in ps)
        inv = lax.rsqrt(ss * (1.0 / (pairs_per_group * LANES)) + EPS)
        for p in ps:
            psl = slice(p * LANES, (p + 1) * LANES)
            o_ref[:, psl] = (y_pairs[p] * inv * ng_ref[:, psl]).astype(BF16)


def _ssd(l, zx, small, params, bsz, seq):
    n_rows = zx.shape[0]
    R = SSD_STEP_ROWS
    steps_per_seq = seq // R
    meta_block = n_rows // R - 1
    n_fill = _n_fill_blocks(n_rows, bsz, seq, R)

    def block(b, c):
        first = _first_step_block(b, meta_block, n_fill, b * steps_per_seq)
        return (jnp.where(c == 0, first, b * steps_per_seq + c - 1), 0)

    return pl.pallas_call(
        functools.partial(_ssd_kernel, n_fill),
        grid=(bsz, steps_per_seq + 1),
        in_specs=[pl.BlockSpec((R, C_ZX), block), pl.BlockSpec((R, LANES), block)]
                 + [_layer(p, l) for p in params],
        out_specs=pl.BlockSpec((R, SSD_D), block),
        out_shape=jax.ShapeDtypeStruct((n_rows, SSD_D), BF16),
        scratch_shapes=[pltpu.VMEM((R + 8, SSD_CONV_DIM), F32),
                        pltpu.VMEM((SSD_HEADS // 2, LANES, LANES), F32),
                        pltpu.VMEM((8, SSD_CONV_DIM), F32),
                        pltpu.VMEM((SSD_HEADS // 2, LANES, LANES), F32)],
        compiler_params=pltpu.CompilerParams(
            dimension_semantics=("arbitrary", "arbitrary"), vmem_limit_bytes=VMEM_LIMIT),
        name="ssd_mixer",
    )(zx, small, *params)


def _fox_keys_kernel(km_ref, k_ref, sm_ref, s_ref, fb_ref, om_ref, o_ref):
    T = BLOCK
    width = k_ref.shape[1]
    row = lax.broadcasted_iota(jnp.int32, (T, 1), 0)
    tri = (lax.broadcasted_iota(jnp.int32, (T, T), 0)
           >= lax.broadcasted_iota(jnp.int32, (T, T), 1))
    tri_b = jnp.where(tri, 1.0, 0.0).astype(BF16)
    src = lax.broadcasted_iota(jnp.int32, (LANES, width), 0)
    dst = lax.broadcasted_iota(jnp.int32, (LANES, width), 1)
    dst_head = jnp.right_shift(dst, HEAD_W.bit_length() - 1)
    dst_lane = jnp.bitwise_and(dst, HEAD_W - 1)
    sel = jnp.concatenate(
        [jnp.where((src == SMALL_F + dst_head) & (dst_lane == FOX_BIAS_LANE + i),
                   1.0, 0.0).astype(BF16) for i in range(3)], axis=0)

    def local_cumsum(small_blk, is_meta):
        log_f = -_softplus(-(small_blk + fb_ref[...]))
        if is_meta:
            log_f = jnp.where(row < PAD, 0.0, log_f)
        parts = _dot(tri_b, jnp.concatenate(_split3(log_f), axis=1))
        return parts[:, :LANES] + parts[:, LANES:2 * LANES] + parts[:, 2 * LANES:]

    def keys(c_blk, k_blk):
        pieces = jnp.concatenate(_split3(c_blk * (-LOG2E)), axis=1)
        return (k_blk.astype(F32) + _dot(pieces, sel)).astype(BF16)

    n_blocks = k_ref.shape[0] // T
    rows = [slice(j * T, (j + 1) * T) for j in range(n_blocks)]
    c_meta = local_cumsum(sm_ref[...], True)
    local = [local_cumsum(s_ref[rows[j], :], False) for j in range(n_blocks)]
    om_ref[...] = keys(c_meta, km_ref[...])
    carry = c_meta[T - 1:T, :]
    for j in range(n_blocks):
        c_blk = local[j] + carry
        carry = c_blk[T - 1:T, :]
        o_ref[rows[j], :] = keys(c_blk, k_ref[rows[j], :])


def _fox_keys(l, k, small, fb, bsz, seq):
    n_rows, width = k.shape
    meta_block = n_rows // BLOCK - 1
    main = lambda w: pl.BlockSpec((seq, w), lambda b: (b, 0))
    meta = lambda w: pl.BlockSpec((BLOCK, w), lambda b: (meta_block, 0))
    return pl.pallas_call(
        _fox_keys_kernel,
        grid=(bsz,),
        in_specs=[meta(width), main(width), meta(LANES), main(LANES), _layer(fb, l)],
        out_specs=[pl.BlockSpec((BLOCK, width), lambda b: (0, 0)), main(width)],
        out_shape=[jax.ShapeDtypeStruct((BLOCK, width), BF16),
                   jax.ShapeDtypeStruct((bsz * seq, width), BF16)],
        compiler_params=pltpu.CompilerParams(
            dimension_semantics=("arbitrary",), vmem_limit_bytes=VMEM_LIMIT),
        name="fox_keys",
    )(k, k, small, small, fb)


def _attn_kernel(n_heads, n_fill, q_ref, km_ref, k_ref, vtm_ref, vt_ref, o_ref,
                 acc_scr, s0_scr, s1_scr):
    t = pl.program_id(1)
    b = pl.program_id(0)
    dv = acc_scr.shape[1]

    @pl.when((t == 0) & (b >= 1) & (b <= n_fill))
    def _():
        o_ref[...] = jnp.zeros(o_ref.shape, o_ref.dtype)

    heads = range(n_heads)
    hsl = [slice(hd * HEAD_W, (hd + 1) * HEAD_W) for hd in heads]
    vsl = [slice(hd * dv, (hd + 1) * dv) for hd in heads]
    qs = [q_ref[:, hsl[hd]] for hd in heads]
    ahead = (lax.broadcasted_iota(jnp.int32, (TQ, TQ), 0)
             - lax.broadcasted_iota(jnp.int32, (TQ, TQ), 1))

    def softmax_pv(ss, vts, ms, mask):
        if mask is not None:
            ss = [jnp.where(mask, s, NEG_BIG) for s in ss]
        new_ms = [jnp.maximum(ms[hd], jnp.max(ss[hd], axis=0, keepdims=True)) for hd in heads]
        corrs = [jnp.exp2(ms[hd] - new_ms[hd]) for hd in heads]
        prs = [jnp.exp2(ss[hd] - new_ms[hd]).astype(BF16) for hd in heads]
        for hd in heads:
            acc_scr[hd] = corrs[hd] * acc_scr[hd] + _dot(vts[hd], prs[hd])
        return tuple(new_ms)

    def meta_chunk(carry, mask):
        return softmax_pv([_dot_nt(km_ref[PAD:, hsl[hd]], qs[hd]) for hd in heads],
                          [vtm_ref[0, vsl[hd], PAD:] for hd in heads], carry, mask)

    def scores(c, slot_scr):
        for hd in heads:
            slot_scr[hd] = _dot_nt(k_ref[c * TQ:(c + 1) * TQ, hsl[hd]], qs[hd])

    def chunk_vt(c, hd):
        n_sub = TQ // BLOCK
        return jnp.concatenate([vt_ref[n_sub * c + i, vsl[hd], :] for i in range(n_sub)], axis=1)

    def consume(c, slot_scr, carry, mask):
        return softmax_pv([slot_scr[hd] for hd in heads], [chunk_vt(c, hd) for hd in heads],
                          carry, mask)

    def finish():
        dv_out = o_ref.shape[1] // n_heads
        ys = [acc_scr[hd, 0:dv_out, :] * (1.0 / acc_scr[hd, dv_out:dv_out + 1, :])
              for hd in heads]
        for p in range(n_heads // 2):
            y_t = jnp.concatenate([ys[2 * p], ys[2 * p + 1]], axis=0)
            o_ref[:, p * LANES:(p + 1) * LANES] = y_t.T.astype(BF16)

    init = tuple(jnp.full((1, TQ), M_INIT, F32) for _ in heads)

    @pl.when((t == 0) & (b == 0))
    def _():
        acc_scr[...] = jnp.zeros(acc_scr.shape, F32)
        meta_chunk(init, ahead[:N_META, :] <= -(TQ - N_META))
        finish()

    diagonal = ahead <= 0
    slots = (s0_scr, s1_scr)
    for n_chunks in range(1, k_ref.shape[0] // TQ + 1):
        @pl.when(t == n_chunks)
        def _(n_chunks=n_chunks):
            scores(0, slots[0])
            s_meta = [_dot_nt(km_ref[PAD:, hsl[hd]], qs[hd]) for hd in heads]
            if n_chunks > 1:
                scores(1, slots[1])
            s_first = [slots[0][hd] for hd in heads]
            if n_chunks == 1:
                s_first = [jnp.where(diagonal, s, NEG_BIG) for s in s_first]
            ms = tuple(jnp.maximum(jnp.max(s_first[hd], axis=0, keepdims=True),
                                   jnp.max(s_meta[hd], axis=0, keepdims=True)) for hd in heads)
            for hd in heads:
                acc_scr[hd] = (_dot(chunk_vt(0, hd), jnp.exp2(s_first[hd] - ms[hd]).astype(BF16))
                               + _dot(vtm_ref[0, vsl[hd], PAD:],
                                      jnp.exp2(s_meta[hd] - ms[hd]).astype(BF16)))
            for c in range(1, n_chunks):
                if c + 1 < n_chunks:
                    scores(c + 1, slots[(c + 1) % 2])
                ms = consume(c, slots[c % 2], ms, diagonal if c == n_chunks - 1 else None)
            finish()


def _attention(q, k_meta, k_main, vt, n_heads, bsz, seq, meta_k_block, name):
    n_rows, qw = q.shape
    dv_aug = vt.shape[1] // n_heads
    dv = dv_aug - (DV_AUG - MLA_V)
    tiles_per_seq = seq // TQ
    meta_q_tile = n_rows // TQ - 1
    meta_block = n_rows // BLOCK - 1
    n_fill = _n_fill_blocks(n_rows, bsz, seq, TQ)

    def q_tile(b, t):
        first = _first_step_block(b, meta_q_tile, n_fill, b * tiles_per_seq)
        return (jnp.where(t == 0, first, b * tiles_per_seq + t - 1), 0)

    return pl.pallas_call(
        functools.partial(_attn_kernel, n_heads, n_fill),
        grid=(bsz, tiles_per_seq + 1),
        in_specs=[pl.BlockSpec((TQ, qw), q_tile),
                  pl.BlockSpec((BLOCK, qw), lambda b, t: (meta_k_block, 0)),
                  pl.BlockSpec((seq, qw), lambda b, t: (b, 0)),
                  pl.BlockSpec((1,) + vt.shape[1:], lambda b, t: (meta_block, 0, 0)),
                  pl.BlockSpec((seq // BLOCK,) + vt.shape[1:], lambda b, t: (b, 0, 0))],
        out_specs=pl.BlockSpec((TQ, n_heads * dv), q_tile),
        out_shape=jax.ShapeDtypeStruct((n_rows, n_heads * dv), BF16),
        scratch_shapes=[pltpu.VMEM((n_heads, dv_aug, TQ), F32),
                        pltpu.VMEM((n_heads, TQ, TQ), F32),
                        pltpu.VMEM((n_heads, TQ, TQ), F32)],
        compiler_params=pltpu.CompilerParams(
            dimension_semantics=("arbitrary", "arbitrary"), vmem_limit_bytes=VMEM_LIMIT),
        name=name,
    )(q, k_meta, k_main, vt, vt)


def _select_cols(w, cols, transpose=False):
    place = np.zeros((w.shape[-1], len(cols)), np.float32)
    for j, entry in enumerate(cols):
        if entry is not None:
            place[entry[0], j] = entry[1]
    out = jnp.einsum('lkn,nm->lmk' if transpose else 'lkn,nm->lkm',
                     w.astype(BF16).astype(F32), jnp.asarray(place))
    return out.astype(BF16)


def _span(start, n, sign=1.0):
    return [(start + i, sign) for i in range(n)]


def _rot_half_span(start):
    half = MLA_ROPE // 2
    return _span(start + half, half, -1.0) + _span(start, half)


def _in_proj_weights(w_in):
    sizes = [SSD_D, SSD_CONV_DIM, SSD_HEADS, FOX_D, FOX_D, FOX_D, FOX_HEADS,
             MLA_Q_LORA, MLA_KV_LORA, MLA_ROPE]
    z, xbc, dt, fq, fk, fv, fr, cq, ckv, kr = [int(o) - C_ZX for o in
                                               np.cumsum([0] + sizes[:-1])]
    del z, xbc
    misc = (_span(dt, SSD_HEADS) + _span(fr, FOX_HEADS)
            + [None] * (MISC_KRR - SMALL_F - FOX_HEADS) + _rot_half_span(kr)
            + _span(kr, MLA_ROPE) + [None] * (LANES - MISC_KR - MLA_ROPE))
    rest = w_in[..., C_ZX:]
    w_rest = _select_cols(rest, _span(fq, FOX_D) + _span(fk, FOX_D) + _span(cq, MLA_Q_LORA)
                          + _span(ckv, MLA_KV_LORA) + misc)
    return w_in[..., :C_ZX].astype(BF16), w_rest, _select_cols(rest, _span(fv, FOX_D), True)


def _mla_weights(w_uq, w_ukv):
    qd, kvd = MLA_NOPE + MLA_ROPE, MLA_NOPE + MLA_V
    pad_q = [None] * (HEAD_W - qd)
    wuq = _select_cols(w_uq, sum([_span(h * qd, qd) + pad_q for h in range(MLA_HEADS)], []))
    wuqr = _select_cols(w_uq, sum([[None] * MLA_NOPE + _rot_half_span(h * qd + MLA_NOPE) + pad_q
                                   for h in range(MLA_HEADS)], []))
    wkk = _select_cols(w_ukv, sum([_span(h * kvd, MLA_NOPE) + [None] * (HEAD_W - MLA_NOPE)
                                   for h in range(MLA_HEADS)], []))
    wvt = _select_cols(w_ukv, sum([_span(h * kvd + MLA_NOPE, MLA_V) for h in range(MLA_HEADS)],
                                  []), True)
    return wuq, wuqr, wkk, wvt


def _rows(v, width=None, offset=0):
    v = v.astype(F32)
    if width is not None:
        v = jnp.pad(v, ((0, 0), (offset, width - offset - v.shape[-1])))
    return v[:, None, :]


def _position_tables(seq, tm):
    pos = jnp.concatenate([N_META + jnp.arange(seq, dtype=F32),
                           jnp.arange(tm, dtype=F32) - (tm - N_META)])
    inv_freq = 1.0 / (ROPE_THETA ** (jnp.arange(0, MLA_ROPE, 2, dtype=F32) / MLA_ROPE))
    ang = pos[:, None] * inv_freq[None, :]
    cos, sin = jnp.cos(ang), jnp.sin(ang)
    n = seq + tm
    rope_pad = LANES - MLA_NOPE - MLA_ROPE
    cos128 = jnp.concatenate([jnp.ones((n, MLA_NOPE), F32), cos, cos,
                              jnp.zeros((n, rope_pad), F32)], axis=-1)
    sin128 = jnp.concatenate([jnp.zeros((n, MLA_NOPE), F32), sin, sin,
                              jnp.zeros((n, rope_pad), F32)], axis=-1)
    cosk128 = jnp.concatenate([jnp.zeros((n, MLA_NOPE), F32), cos, cos,
                               jnp.zeros((n, rope_pad), F32)], axis=-1)
    return cos128, sin128, cosk128


def kernel(x, meta, ffn1_w_gate, ffn1_w_up, ffn1_w_down, ln1_g, ln1_b, w_in, conv_w, conv_b, dt_bias, a_log, d_skip, ssd_norm_g, fox_f_b, mla_q_norm_g, mla_w_uq, mla_kv_norm_g, mla_w_ukv, w_out, ln2_g, ln2_b, ffn2_w_gate, ffn2_w_up, ffn2_w_down, ln3_g, ln3_b):
    bsz, seq, _ = x.shape
    assert seq % TQ == 0
    tm = TQ
    n_main = bsz * seq
    n_rows = n_main + tm
    meta_block = n_rows // BLOCK - 1

    h = x.reshape(n_main, D_MODEL)
    meta_tile = jnp.concatenate([jnp.zeros((tm - N_META, D_MODEL), x.dtype),
                                 meta.astype(x.dtype)], axis=0)
    tables = _position_tables(seq, tm)
    bf = lambda w: w.astype(BF16)

    ffn1 = (bf(ffn1_w_gate), bf(ffn1_w_up), bf(ffn1_w_down), _rows(ln1_g), _rows(ln1_b))
    ffn2 = (bf(ffn2_w_gate), bf(ffn2_w_up), bf(ffn2_w_down), _rows(ln3_g), _rows(ln3_b))
    proj_w = (*_in_proj_weights(w_in), *_mla_weights(mla_w_uq, mla_w_ukv),
              _rows(mla_q_norm_g), _rows(mla_kv_norm_g))
    ssd_p = (conv_w.astype(F32), _rows(conv_b), _rows(dt_bias, LANES),
             _rows(-jnp.exp(a_log.astype(F32)), LANES),
             _rows(jnp.repeat(d_skip, SSD_HEAD_DIM, axis=-1)), _rows(ssd_norm_g))
    fb = _rows(fox_f_b, LANES, SMALL_F)
    out_p = (bf(w_out), _rows(ln2_g), _rows(ln2_b))

    for l in range(DEPTH):
        h, zx, small, fq, fk, fvt, mq, mk, mvt = _ffn_ln(
            l, h, ffn1, tm, n_rows, meta_tile=meta_tile if l == 0 else None,
            proj=(proj_w, tables, seq))
        y_ssd = _ssd(l, zx, small, ssd_p, bsz, seq)
        fk_meta, fk_main = _fox_keys(l, fk, small, fb, bsz, seq)
        y_fox = _attention(fq, fk_meta, fk_main, fvt, FOX_HEADS, bsz, seq, 0, "fox_attention")
        y_mla = _attention(mq, mk, mk, mvt, MLA_HEADS, bsz, seq, meta_block, "mla_attention")
        h = _ffn_ln(l, h, ffn2, tm, n_rows if l < DEPTH - 1 else n_main,
                    mix=(y_ssd, y_fox, y_mla, *out_p))

    return h.reshape(bsz, seq, D_MODEL)
```

```python
import functools

import numpy as np
import jax
import jax.numpy as jnp
from jax import lax
from jax.experimental import pallas as pl
from jax.experimental.pallas import tpu as pltpu

F32 = jnp.float32
BF16 = jnp.bfloat16

D_MODEL = 1024
DEPTH = 2
N_META = 16
BLOCK = 128
SSD_HEADS = 8
SSD_HEAD_DIM = 64
SSD_D = SSD_HEADS * SSD_HEAD_DIM
SSD_GROUPS = 2
SSD_STATE = 64
SSD_CONV = 4
SSD_CONV_DIM = SSD_D + 2 * SSD_GROUPS * SSD_STATE
FOX_HEADS = 4
FOX_HEAD_DIM = 64
FOX_D = FOX_HEADS * FOX_HEAD_DIM
MLA_HEADS = 4
MLA_Q_LORA = 256
MLA_KV_LORA = 128
MLA_NOPE = 64
MLA_ROPE = 32
MLA_V = 64
MLA_D = MLA_HEADS * MLA_V
ROPE_THETA = 10000.0
D_MIX = SSD_D + FOX_D + MLA_D
D_FF = 2816
ALPHA = (2 * DEPTH) ** 0.25
EPS = 1e-5

LANES = 128
MXU_W = 256
FF_CHUNK = MXU_W
N_FF_CHUNKS = D_FF // FF_CHUNK
N_OUT_CHUNKS = D_MODEL // FF_CHUNK
TQ = 4 * BLOCK
SSD_STEP_ROWS = 2 * BLOCK
PAD = BLOCK - N_META
NEG_BIG = -1e30
M_INIT = 2 * NEG_BIG
VMEM_LIMIT = 56 * 1024 * 1024

C_XBC = SSD_D
C_ZX = SSD_D + SSD_CONV_DIM
C_FQ = 0
C_CQ = C_FQ + 2 * FOX_D
C_MISC = C_CQ + MLA_Q_LORA
N_IN_ARR = C_MISC + MLA_KV_LORA + LANES
MISC_KRR = 32
MISC_KR = MLA_NOPE
LOG2E = 1.4426950408889634
DV_AUG = MLA_V + 16
SMALL_DT = 0
SMALL_F = 8
HEAD_W = LANES
FOX_BIAS_LANE = FOX_HEAD_DIM


def _sigmoid(x):
    return 1.0 / (1.0 + jnp.exp(-x))


def _softplus(x):
    return jnp.maximum(x, 0.0) + jnp.log(1.0 + jnp.exp(-jnp.abs(x)))


def _layer_norm_rows(y, g, b):
    mu = jnp.mean(y, axis=-1, keepdims=True)
    yc = y - mu
    var = jnp.mean(yc * yc, axis=-1, keepdims=True)
    return yc * lax.rsqrt(var + EPS) * g + b


def _split3(x):
    x1 = x.astype(BF16)
    r1 = x - x1.astype(F32)
    x2 = r1.astype(BF16)
    r2 = r1 - x2.astype(F32)
    return x1, x2, r2.astype(BF16)


def _dot(a, b):
    return jnp.dot(a, b, preferred_element_type=F32)


def _dot_nt(a, b):
    return lax.dot_general(a, b, (((1,), (1,)), ((), ())), preferred_element_type=F32)


def _resident(shape):
    return pl.BlockSpec(shape, lambda *_: (0,) * len(shape), pipeline_mode=pl.Buffered(1))


def _layer(arr, l):
    tail = (0,) * (arr.ndim - 1)
    return pl.BlockSpec((None,) + arr.shape[1:], lambda *_: (l,) + tail,
                        pipeline_mode=pl.Buffered(1))


N_PROJ_IN = 12
N_PROJ_OUT = 8


def _ffn_ln_kernel(prologue, n_main_tiles, with_proj, *refs):
    a_scr, y_scr = refs[-2:]
    refs = refs[:-2]
    if with_proj:
        proj_out, refs = refs[-N_PROJ_OUT:], refs[:-N_PROJ_OUT]
        o_ref = refs[-1]
        proj_in, refs = refs[-1 - N_PROJ_IN:-1], refs[:-1 - N_PROJ_IN]
    else:
        o_ref, refs = refs[-1], refs[:-1]
    wg_ref, wu_ref, wd_ref, g_ref, b_ref = refs[-5:]
    if prologue == "plain":
        x = refs[0][...]
    elif prologue == "meta":
        x_ref, m_ref = refs[:2]
        x = jnp.where(pl.program_id(0) < n_main_tiles, x_ref[...], m_ref[...])
    else:
        h_ref, ys_ref, yf_ref, ym_ref, wo_ref, g2_ref, b2_ref = refs[:7]
        mix = (_dot(ys_ref[...], wo_ref[0:SSD_D, :])
               + _dot(yf_ref[...], wo_ref[SSD_D:SSD_D + FOX_D, :])
               + _dot(ym_ref[...], wo_ref[SSD_D + FOX_D:D_MIX, :]))
        x = _layer_norm_rows(ALPHA * h_ref[...] + mix, g2_ref[...], b2_ref[...])
    xb = x.astype(BF16)

    for c in range(N_FF_CHUNKS):
        cols = slice(c * FF_CHUNK, (c + 1) * FF_CHUNK)
        gate = _dot(xb, wg_ref[:, cols])
        up = _dot(xb, wu_ref[:, cols])
        a_scr[c] = (gate * _sigmoid(gate) * up).astype(BF16)

    tm = x.shape[0]
    outs = []
    for rows in (slice(0, tm // 2), slice(tm // 2, tm)):
        for n in range(N_OUT_CHUNKS):
            cols = slice(n * FF_CHUNK, (n + 1) * FF_CHUNK)
            acc = _dot(a_scr[0, rows], wd_ref[0:FF_CHUNK, cols])
            for c in range(1, N_FF_CHUNKS):
                acc = acc + _dot(a_scr[c, rows], wd_ref[c * FF_CHUNK:(c + 1) * FF_CHUNK, cols])
            y_scr[rows, cols] = ALPHA * x[rows, cols] + 0.5 * acc
        outs.append(_layer_norm_rows(y_scr[rows, :], g_ref[...], b_ref[...]))
        o_ref[rows, :] = outs[-1]
    if with_proj:
        _proj_body(jnp.concatenate(outs, axis=0), *proj_in, *proj_out)


def _ffn_ln(l, h, ffn, tm, n_out_rows, meta_tile=None, mix=None, proj=None):
    def rows(width):
        return pl.BlockSpec((tm, width), lambda i: (i, 0))

    out_specs = rows(D_MODEL)
    out_shape = jax.ShapeDtypeStruct((n_out_rows, D_MODEL), F32)
    p_args, p_specs = [], []
    if proj is not None:
        p_args, p_specs, p_out_specs, p_out_shape = _proj_specs(l, n_out_rows, tm, *proj)
        out_specs, out_shape = [out_specs] + p_out_specs, [out_shape] + p_out_shape

    if mix is not None:
        prologue, n_main = "mix", None
        x_args = [h, *mix]
        x_specs = ([rows(D_MODEL), rows(SSD_D), rows(FOX_D), rows(MLA_D)]
                   + [_layer(p, l) for p in mix[3:]])
    elif meta_tile is not None:
        prologue, n_main = "meta", h.shape[0] // tm
        x_args = [h, meta_tile]
        x_specs = [pl.BlockSpec((tm, D_MODEL), lambda i: (jnp.minimum(i, n_main - 1), 0)),
                   _resident(meta_tile.shape)]
    else:
        prologue, n_main, x_args, x_specs = "plain", None, [h], [rows(D_MODEL)]
    return pl.pallas_call(
        functools.partial(_ffn_ln_kernel, prologue, n_main, proj is not None),
        grid=(n_out_rows // tm,),
        in_specs=x_specs + [_layer(p, l) for p in ffn] + p_specs,
        out_specs=out_specs,
        out_shape=out_shape,
        scratch_shapes=[pltpu.VMEM((N_FF_CHUNKS, tm, FF_CHUNK), BF16),
                        pltpu.VMEM((tm, D_MODEL), F32)],
        compiler_params=pltpu.CompilerParams(
            dimension_semantics=("arbitrary",), vmem_limit_bytes=VMEM_LIMIT),
        name="ffn_ln_proj" if proj is not None else "ffn_ln",
    )(*x_args, *ffn, *p_args)


def _rms_rows(x, g):
    return x * lax.rsqrt(jnp.mean(x * x, axis=-1, keepdims=True) + EPS) * g


def _store_vt_blocks(out_ref, val_t, n_heads):
    dv = val_t.shape[0] // n_heads
    ones = jnp.ones((DV_AUG - dv, BLOCK), out_ref.dtype)
    for r in range(out_ref.shape[0]):
        for hd in range(n_heads):
            out_ref[r, hd * DV_AUG:hd * DV_AUG + dv, :] = (
                val_t[hd * dv:(hd + 1) * dv, r * BLOCK:(r + 1) * BLOCK].astype(out_ref.dtype))
            out_ref[r, hd * DV_AUG + dv:(hd + 1) * DV_AUG, :] = ones


def _proj_body(h, wzx_ref, win_ref, wfvt_ref, wuq_ref, wuqr_ref, wkk_ref, wkvt_ref,
               qg_ref, kvg_ref, cos_ref, sin_ref, cosk_ref,
               zx_ref, small_ref, fq_ref, fk_ref, fvt_ref, mq_ref, mk_ref, mvt_ref):
    hb = h.astype(BF16)
    zx_ref[...] = _dot(hb, wzx_ref[...])
    misc = _dot(hb, win_ref[:, C_MISC:N_IN_ARR])
    small = misc[:, MLA_KV_LORA:]
    small_ref[...] = small

    lane = lax.broadcasted_iota(jnp.int32, (1, LANES), 1)
    lane_lo = lane < FOX_HEAD_DIM
    fox_one = jnp.where((lane >= FOX_BIAS_LANE) & (lane < FOX_BIAS_LANE + 3), 1.0, 0.0)

    fqk = _dot(hb, win_ref[:, C_FQ:C_CQ])
    for half, (ref, scale, extra) in enumerate([(fq_ref, FOX_HEAD_DIM ** -0.5 * LOG2E, fox_one),
                                                (fk_ref, None, None)]):
        for pair in range(FOX_HEADS // 2):
            c0 = half * FOX_D + pair * LANES
            both = fqk[:, c0:c0 + LANES]
            if scale is not None:
                both = both * scale
            for par, val in enumerate([both, pltpu.roll(both, FOX_HEAD_DIM, 1)]):
                val = jnp.where(lane_lo, val, 0.0)
                if extra is not None:
                    val = val + extra
                hd = 2 * pair + par
                ref[:, hd * HEAD_W:(hd + 1) * HEAD_W] = val.astype(BF16)
    _store_vt_blocks(fvt_ref, _dot_nt(wfvt_ref[...], hb), FOX_HEADS)

    cos = cos_ref[...]
    sin = sin_ref[...]
    cqn = _rms_rows(_dot(hb, win_ref[:, C_CQ:C_MISC]), qg_ref[...]).astype(BF16)
    q = _dot(cqn, wuq_ref[...])
    qr = _dot(cqn, wuqr_ref[...])
    kvn = _rms_rows(misc[:, :MLA_KV_LORA], kvg_ref[...]).astype(BF16)
    kn = _dot(kvn, wkk_ref[...])
    _store_vt_blocks(mvt_ref, _dot_nt(wkvt_ref[...], kvn), MLA_HEADS)
    krope = small * cosk_ref[...] + pltpu.roll(small, MISC_KR - MISC_KRR, 1) * sin
    scale = (MLA_NOPE + MLA_ROPE) ** -0.5 * LOG2E
    for hd in range(MLA_HEADS):
        sl = slice(hd * HEAD_W, (hd + 1) * HEAD_W)
        mq_ref[:, sl] = ((q[:, sl] * cos + qr[:, sl] * sin) * scale).astype(BF16)
        mk_ref[:, sl] = (kn[:, sl] + krope).astype(BF16)


def _proj_specs(l, n_rows, tm, weights, tables, seq):
    n_main_tiles = n_rows // tm - 1
    tiles_per_seq = seq // tm

    def rows(width):
        return pl.BlockSpec((tm, width), lambda i: (i, 0))

    tab = pl.BlockSpec((tm, LANES), lambda i: (
        jnp.where(i < n_main_tiles, i % tiles_per_seq, tiles_per_seq), 0))
    vt_rows = FOX_HEADS * DV_AUG
    t_spec = pl.BlockSpec((tm // BLOCK, vt_rows, BLOCK), lambda i: (i, 0, 0))
    t_shape = jax.ShapeDtypeStruct((n_rows // BLOCK, vt_rows, BLOCK), BF16)
    qk_w = FOX_HEADS * HEAD_W
    row_outs = [(C_ZX, F32), (LANES, F32), (qk_w, BF16), (qk_w, BF16)]
    args = [*weights, *tables]
    in_specs = [_layer(w, l) for w in weights] + [tab, tab, tab]
    out_specs = [rows(w) for w, _ in row_outs] + [t_spec, rows(qk_w), rows(qk_w), t_spec]
    out_shape = ([jax.ShapeDtypeStruct((n_rows, w), dt) for w, dt in row_outs]
                 + [t_shape, jax.ShapeDtypeStruct((n_rows, qk_w), BF16),
                    jax.ShapeDtypeStruct((n_rows, qk_w), BF16), t_shape])
    assert len(args) == N_PROJ_IN and len(out_shape) == N_PROJ_OUT
    return args, in_specs, out_specs, out_shape


def _first_step_block(b, meta_block, n_fill, own_first_block):
    return jnp.where(b == 0, meta_block,
                     jnp.where(b <= n_fill, meta_block - b, own_first_block))


def _n_fill_blocks(n_rows, bsz, seq, block_rows):
    n_fill = (n_rows - bsz * seq) // block_rows - 1
    assert bsz > n_fill, "needs one batch row per meta-tile block to zero-fill"
    return n_fill


def _ssd_kernel(n_fill, zx_ref, small_ref, cw_ref, cb_ref, dtb_ref, a_ref, dsk_ref, ng_ref,
                o_ref, conv_scr, s_scr, meta_conv_scr, meta_s_scr):
    b = pl.program_id(0)
    c = pl.program_id(1)

    @pl.when((c == 0) & (b >= 1) & (b <= n_fill))
    def _():
        o_ref[...] = jnp.zeros(o_ref.shape, o_ref.dtype)

    @pl.when((c == 0) & (b == 0))
    def _():
        conv_scr[0:8, :] = jnp.zeros((8, SSD_CONV_DIM), F32)
        s_scr[...] = jnp.zeros(s_scr.shape, F32)

    @pl.when((c == 0) & (b > 0))
    def _():
        conv_scr[0:8, :] = meta_conv_scr[...]
        s_scr[...] = meta_s_scr[...]

    @pl.when((c > 0) | (b == 0))
    def _():
        _ssd_step(c, zx_ref, small_ref, cw_ref, cb_ref, dtb_ref, a_ref, dsk_ref, ng_ref,
                  o_ref, conv_scr, s_scr)

    @pl.when((c == 0) & (b == 0))
    def _():
        meta_conv_scr[...] = conv_scr[0:8, :]
        meta_s_scr[...] = s_scr[...]


def _ssd_step(c, zx_ref, small_ref, cw_ref, cb_ref, dtb_ref, a_ref, dsk_ref, ng_ref,
              o_ref, conv_scr, s_scr):
    R = SSD_STEP_ROWS
    row = lax.broadcasted_iota(jnp.int32, (R, 1), 0)
    valid = jnp.logical_or(c > 0, row >= R - N_META)

    conv_scr[8:8 + R, :] = jnp.where(valid, zx_ref[:, C_XBC:C_ZX], 0.0)
    acc = cb_ref[...]
    for k in range(SSD_CONV):
        off = 8 - (SSD_CONV - 1) + k
        acc = acc + cw_ref[k:k + 1, :] * conv_scr[off:off + R, :]
    conv_scr[0:8, :] = conv_scr[R:R + 8, :]
    xbc = acc * _sigmoid(acc)
    dt = jnp.where(valid, _softplus(small_ref[...] + dtb_ref[...]), 0.0)
    for ch in range(R // BLOCK):
        rows = slice(ch * BLOCK, (ch + 1) * BLOCK)
        _ssd_chunk(xbc[rows], dt[rows], zx_ref.at[rows], a_ref, dsk_ref, ng_ref,
                   o_ref.at[rows], s_scr)


def _ssd_chunk(xbc, dt, zx_ref, a_ref, dsk_ref, ng_ref, o_ref, s_scr):
    Q = BLOCK
    lane = lax.broadcasted_iota(jnp.int32, (1, LANES), 1)
    lane_lo = lane < SSD_HEAD_DIM
    sub = lax.broadcasted_iota(jnp.int32, (LANES, 1), 0)
    bm = xbc[:, SSD_D:SSD_D + LANES]
    cm = xbc[:, SSD_D + LANES:SSD_D + 2 * LANES]

    a = dt * a_ref[...]
    tri = (lax.broadcasted_iota(jnp.int32, (Q, Q), 0)
           >= lax.broadcasted_iota(jnp.int32, (Q, Q), 1))
    tri_b = jnp.where(tri, 1.0, 0.0).astype(BF16)
    a1, a2, a3 = _split3(a)
    a_cum = _dot(tri_b, a1) + _dot(tri_b, a2) + _dot(tri_b, a3)
    a_cum_t = a_cum.T
    bm_t = bm.T

    cm_b = cm.astype(BF16)
    bm_b = bm.astype(BF16)
    cb_g = [_dot_nt(jnp.where(lane_lo, cm, 0.0).astype(BF16), bm_b),
            _dot_nt(jnp.where(lane_lo, 0.0, cm).astype(BF16), bm_b)]
    rows_g = [sub < SSD_STATE, sub >= SSD_STATE]

    pairs_per_group = SSD_HEADS // 2 // SSD_GROUPS
    y_pairs = []
    for p in range(SSD_HEADS // 2):
        g = p // pairs_per_group
        psl = slice(p * LANES, (p + 1) * LANES)
        xs_p = xbc[:, psl]
        dt_pair = jnp.where(lane_lo, dt[:, 2 * p:2 * p + 1], dt[:, 2 * p + 1:2 * p + 2])
        xdt = (xs_p * dt_pair).astype(BF16)
        s_old = s_scr[p]
        yd, upd, e_col, e_last = [], [], [], []
        for par in range(2):
            hd = 2 * p + par
            col = a_cum[:, hd:hd + 1]
            rowv = a_cum_t[hd:hd + 1, :]
            last = a_cum_t[hd:hd + 1, Q - 1:Q]
            seg = jnp.exp(jnp.where(tri, col - rowv, NEG_BIG))
            yd.append(_dot((cb_g[g] * seg).astype(BF16), xdt))
            upd.append(_dot((bm_t * jnp.exp(last - rowv)).astype(BF16), xdt))
            e_col.append(jnp.exp(col))
            e_last.append(jnp.exp(last))
        y_off = _dot(cm_b, s_old.astype(BF16)) * jnp.where(lane_lo, e_col[0], e_col[1])
        s_new = (jnp.where(lane_lo, e_last[0], e_last[1]) * s_old
                 + jnp.where(rows_g[g], jnp.where(lane_lo, upd[0], upd[1]), 0.0))
        s_scr[p] = s_new
        y_p = jnp.where(lane_lo, yd[0], yd[1]) + y_off + dsk_ref[:, psl] * xs_p
        z_p = zx_ref[:, psl]
        y_pairs.append(y_p * (z_p * _sigmoid(z_p)))

    for g in range(SSD_GROUPS):
        ps = range(g * pairs_per_group, (g + 1) * pairs_per_group)
        ss = sum(jnp.sum(y_pairs[p] * y_pairs[p], axis=-1, keepdims=True) for p in ps)
        inv = lax.rsqrt(ss * (1.0 / (pairs_per_group * LANES)) + EPS)
        for p in ps:
            psl = slice(p * LANES, (p + 1) * LANES)
            o_ref[:, psl] = (y_pairs[p] * inv * ng_ref[:, psl]).astype(BF16)


def _ssd(l, zx, small, params, bsz, seq):
    n_rows = zx.shape[0]
    R = SSD_STEP_ROWS
    steps_per_seq = seq // R
    meta_block = n_rows // R - 1
    n_fill = _n_fill_blocks(n_rows, bsz, seq, R)

    def block(b, c):
        first = _first_step_block(b, meta_block, n_fill, b * steps_per_seq)
        return (jnp.where(c == 0, first, b * steps_per_seq + c - 1), 0)

    return pl.pallas_call(
        functools.partial(_ssd_kernel, n_fill),
        grid=(bsz, steps_per_seq + 1),
        in_specs=[pl.BlockSpec((R, C_ZX), block), pl.BlockSpec((R, LANES), block)]
                 + [_layer(p, l) for p in params],
        out_specs=pl.BlockSpec((R, SSD_D), block),
        out_shape=jax.ShapeDtypeStruct((n_rows, SSD_D), BF16),
        scratch_shapes=[pltpu.VMEM((R + 8, SSD_CONV_DIM), F32),
                        pltpu.VMEM((SSD_HEADS // 2, LANES, LANES), F32),
                        pltpu.VMEM((8, SSD_CONV_DIM), F32),
                        pltpu.VMEM((SSD_HEADS // 2, LANES, LANES), F32)],
        compiler_params=pltpu.CompilerParams(
            dimension_semantics=("arbitrary", "arbitrary"), vmem_limit_bytes=VMEM_LIMIT),
        name="ssd_mixer",
    )(zx, small, *params)


def _fox_keys_kernel(km_ref, k_ref, sm_ref, s_ref, fb_ref, om_ref, o_ref):
    T = BLOCK
    width = k_ref.shape[1]
    row = lax.broadcasted_iota(jnp.int32, (T, 1), 0)
    tri = (lax.broadcasted_iota(jnp.int32, (T, T), 0)
           >= lax.broadcasted_iota(jnp.int32, (T, T), 1))
    tri_b = jnp.where(tri, 1.0, 0.0).astype(BF16)
    src = lax.broadcasted_iota(jnp.int32, (LANES, width), 0)
    dst = lax.broadcasted_iota(jnp.int32, (LANES, width), 1)
    dst_head = jnp.right_shift(dst, HEAD_W.bit_length() - 1)
    dst_lane = jnp.bitwise_and(dst, HEAD_W - 1)
    sel = jnp.concatenate(
        [jnp.where((src == SMALL_F + dst_head) & (dst_lane == FOX_BIAS_LANE + i),
                   1.0, 0.0).astype(BF16) for i in range(3)], axis=0)

    def local_cumsum(small_blk, is_meta):
        log_f = -_softplus(-(small_blk + fb_ref[...]))
        if is_meta:
            log_f = jnp.where(row < PAD, 0.0, log_f)
        parts = _dot(tri_b, jnp.concatenate(_split3(log_f), axis=1))
        return parts[:, :LANES] + parts[:, LANES:2 * LANES] + parts[:, 2 * LANES:]

    def keys(c_blk, k_blk):
        pieces = jnp.concatenate(_split3(c_blk * (-LOG2E)), axis=1)
        return (k_blk.astype(F32) + _dot(pieces, sel)).astype(BF16)

    n_blocks = k_ref.shape[0] // T
    rows = [slice(j * T, (j + 1) * T) for j in range(n_blocks)]
    c_meta = local_cumsum(sm_ref[...], True)
    local = [local_cumsum(s_ref[rows[j], :], False) for j in range(n_blocks)]
    om_ref[...] = keys(c_meta, km_ref[...])
    carry = c_meta[T - 1:T, :]
    for j in range(n_blocks):
        c_blk = local[j] + carry
        carry = c_blk[T - 1:T, :]
        o_ref[rows[j], :] = keys(c_blk, k_ref[rows[j], :])


def _fox_keys(l, k, small, fb, bsz, seq):
    n_rows, width = k.shape
    meta_block = n_rows // BLOCK - 1
    main = lambda w: pl.BlockSpec((seq, w), lambda b: (b, 0))
    meta = lambda w: pl.BlockSpec((BLOCK, w), lambda b: (meta_block, 0))
    return pl.pallas_call(
        _fox_keys_kernel,
        grid=(bsz,),
        in_specs=[meta(width), main(width), meta(LANES), main(LANES), _layer(fb, l)],
        out_specs=[pl.BlockSpec((BLOCK, width), lambda b: (0, 0)), main(width)],
        out_shape=[jax.ShapeDtypeStruct((BLOCK, width), BF16),
                   jax.ShapeDtypeStruct((bsz * seq, width), BF16)],
        compiler_params=pltpu.CompilerParams(
            dimension_semantics=("arbitrary",), vmem_limit_bytes=VMEM_LIMIT),
        name="fox_keys",
    )(k, k, small, small, fb)


def _attn_kernel(n_heads, n_fill, q_ref, km_ref, k_ref, vtm_ref, vt_ref, o_ref,
                 acc_scr, s0_scr, s1_scr):
    t = pl.program_id(1)
    b = pl.program_id(0)
    dv = acc_scr.shape[1]

    @pl.when((t == 0) & (b >= 1) & (b <= n_fill))
    def _():
        o_ref[...] = jnp.zeros(o_ref.shape, o_ref.dtype)

    heads = range(n_heads)
    hsl = [slice(hd * HEAD_W, (hd + 1) * HEAD_W) for hd in heads]
    vsl = [slice(hd * dv, (hd + 1) * dv) for hd in heads]
    qs = [q_ref[:, hsl[hd]] for hd in heads]
    ahead = (lax.broadcasted_iota(jnp.int32, (TQ, TQ), 0)
             - lax.broadcasted_iota(jnp.int32, (TQ, TQ), 1))

    def softmax_pv(ss, vts, ms, mask):
        if mask is not None:
            ss = [jnp.where(mask, s, NEG_BIG) for s in ss]
        new_ms = [jnp.maximum(ms[hd], jnp.max(ss[hd], axis=0, keepdims=True)) for hd in heads]
        corrs = [jnp.exp2(ms[hd] - new_ms[hd]) for hd in heads]
        prs = [jnp.exp2(ss[hd] - new_ms[hd]).astype(BF16) for hd in heads]
        for hd in heads:
            acc_scr[hd] = corrs[hd] * acc_scr[hd] + _dot(vts[hd], prs[hd])
        return tuple(new_ms)

    def meta_chunk(carry, mask):
        return softmax_pv([_dot_nt(km_ref[PAD:, hsl[hd]], qs[hd]) for hd in heads],
                          [vtm_ref[0, vsl[hd], PAD:] for hd in heads], carry, mask)

    def scores(c, slot_scr):
        for hd in heads:
            slot_scr[hd] = _dot_nt(k_ref[c * TQ:(c + 1) * TQ, hsl[hd]], qs[hd])

    def chunk_vt(c, hd):
        n_sub = TQ // BLOCK
        return jnp.concatenate([vt_ref[n_sub * c + i, vsl[hd], :] for i in range(n_sub)], axis=1)

    def consume(c, slot_scr, carry, mask):
        return softmax_pv([slot_scr[hd] for hd in heads], [chunk_vt(c, hd) for hd in heads],
                          carry, mask)

    H = TQ // 2

    def scores_diag(c, slot_scr):
        for hd in heads:
            slot_scr[hd, 0:H, :] = _dot_nt(k_ref[c * TQ:c * TQ + H, hsl[hd]], qs[hd])
            slot_scr[hd, H:TQ, H:TQ] = _dot_nt(k_ref[c * TQ + H:(c + 1) * TQ, hsl[hd]],
                                              qs[hd][H:TQ, :])

    def consume_diag(c, slot_scr, ms):
        vts = [chunk_vt(c, hd) for hd in heads]
        ms = softmax_pv([slot_scr[hd, 0:H, :] for hd in heads], [v[:, 0:H] for v in vts], ms,
                        ahead[0:H, :] <= 0)
        tri = ahead[0:H, 0:H] <= 0
        for hd in heads:
            s = jnp.where(tri, slot_scr[hd, H:TQ, H:TQ], NEG_BIG)
            m_old = ms[hd][:, H:TQ]
            m_new = jnp.maximum(m_old, jnp.max(s, axis=0, keepdims=True))
            pr = jnp.exp2(s - m_new).astype(BF16)
            acc_scr[hd, :, H:TQ] = (jnp.exp2(m_old - m_new) * acc_scr[hd, :, H:TQ]
                                    + _dot(vts[hd][:, H:TQ], pr))

    def finish():
        dv_out = o_ref.shape[1] // n_heads
        ys = [acc_scr[hd, 0:dv_out, :] * (1.0 / acc_scr[hd, dv_out:dv_out + 1, :])
              for hd in heads]
        for p in range(n_heads // 2):
            y_t = jnp.concatenate([ys[2 * p], ys[2 * p + 1]], axis=0)
            o_ref[:, p * LANES:(p + 1) * LANES] = y_t.T.astype(BF16)

    init = tuple(jnp.full((1, TQ), M_INIT, F32) for _ in heads)

    @pl.when((t == 0) & (b == 0))
    def _():
        acc_scr[...] = jnp.zeros(acc_scr.shape, F32)
        meta_chunk(init, ahead[:N_META, :] <= -(TQ - N_META))
        finish()

    diagonal = ahead <= 0
    slots = (s0_scr, s1_scr)
    for n_chunks in range(1, k_ref.shape[0] // TQ + 1):
        @pl.when(t == n_chunks)
        def _(n_chunks=n_chunks):
            scores(0, slots[0])
            s_meta = [_dot_nt(km_ref[PAD:, hsl[hd]], qs[hd]) for hd in heads]
            last = n_chunks - 1
            if n_chunks > 1:
                (scores_diag if last == 1 else scores)(1, slots[1])
            s_first = [slots[0][hd] for hd in heads]
            if n_chunks == 1:
                s_first = [jnp.where(diagonal, s, NEG_BIG) for s in s_first]
            ms = tuple(jnp.maximum(jnp.max(s_first[hd], axis=0, keepdims=True),
                                   jnp.max(s_meta[hd], axis=0, keepdims=True)) for hd in heads)
            for hd in heads:
                acc_scr[hd] = (_dot(chunk_vt(0, hd), jnp.exp2(s_first[hd] - ms[hd]).astype(BF16))
                               + _dot(vtm_ref[0, vsl[hd], PAD:],
                                      jnp.exp2(s_meta[hd] - ms[hd]).astype(BF16)))
            for c in range(1, n_chunks):
                if c < last:
                    (scores_diag if c + 1 == last else scores)(c + 1, slots[(c + 1) % 2])
                    ms = consume(c, slots[c % 2], ms, None)
                else:
                    consume_diag(c, slots[c % 2], ms)
            finish()


def _attention(q, k_meta, k_main, vt, n_heads, bsz, seq, meta_k_block, name):
    n_rows, qw = q.shape
    dv_aug = vt.shape[1] // n_heads
    dv = dv_aug - (DV_AUG - MLA_V)
    tiles_per_seq = seq // TQ
    meta_q_tile = n_rows // TQ - 1
    meta_block = n_rows // BLOCK - 1
    n_fill = _n_fill_blocks(n_rows, bsz, seq, TQ)

    def q_tile(b, t):
        first = _first_step_block(b, meta_q_tile, n_fill, b * tiles_per_seq)
        return (jnp.where(t == 0, first, b * tiles_per_seq + t - 1), 0)

    return pl.pallas_call(
        functools.partial(_attn_kernel, n_heads, n_fill),
        grid=(bsz, tiles_per_seq + 1),
        in_specs=[pl.BlockSpec((TQ, qw), q_tile),
                  pl.BlockSpec((BLOCK, qw), lambda b, t: (meta_k_block, 0)),
                  pl.BlockSpec((seq, qw), lambda b, t: (b, 0)),
                  pl.BlockSpec((1,) + vt.shape[1:], lambda b, t: (meta_block, 0, 0)),
                  pl.BlockSpec((seq // BLOCK,) + vt.shape[1:], lambda b, t: (b, 0, 0))],
        out_specs=pl.BlockSpec((TQ, n_heads * dv), q_tile),
        out_shape=jax.ShapeDtypeStruct((n_rows, n_heads * dv), BF16),
        scratch_shapes=[pltpu.VMEM((n_heads, dv_aug, TQ), F32),
                        pltpu.VMEM((n_heads, TQ, TQ), F32),
                        pltpu.VMEM((n_heads, TQ, TQ), F32)],
        compiler_params=pltpu.CompilerParams(
            dimension_semantics=("arbitrary", "arbitrary"), vmem_limit_bytes=VMEM_LIMIT),
        name=name,
    )(q, k_meta, k_main, vt, vt)


def _select_cols(w, cols, transpose=False):
    place = np.zeros((w.shape[-1], len(cols)), np.float32)
    for j, entry in enumerate(cols):
        if entry is not None:
            place[entry[0], j] = entry[1]
    out = jnp.einsum('lkn,nm->lmk' if transpose else 'lkn,nm->lkm',
                     w.astype(BF16).astype(F32), jnp.asarray(place))
    return out.astype(BF16)


def _span(start, n, sign=1.0):
    return [(start + i, sign) for i in range(n)]


def _rot_half_span(start):
    half = MLA_ROPE // 2
    return _span(start + half, half, -1.0) + _span(start, half)


def _in_proj_weights(w_in):
    sizes = [SSD_D, SSD_CONV_DIM, SSD_HEADS, FOX_D, FOX_D, FOX_D, FOX_HEADS,
             MLA_Q_LORA, MLA_KV_LORA, MLA_ROPE]
    z, xbc, dt, fq, fk, fv, fr, cq, ckv, kr = [int(o) - C_ZX for o in
                                               np.cumsum([0] + sizes[:-1])]
    del z, xbc
    misc = (_span(dt, SSD_HEADS) + _span(fr, FOX_HEADS)
            + [None] * (MISC_KRR - SMALL_F - FOX_HEADS) + _rot_half_span(kr)
            + _span(kr, MLA_ROPE) + [None] * (LANES - MISC_KR - MLA_ROPE))
    rest = w_in[..., C_ZX:]
    w_rest = _select_cols(rest, _span(fq, FOX_D) + _span(fk, FOX_D) + _span(cq, MLA_Q_LORA)
                          + _span(ckv, MLA_KV_LORA) + misc)
    return w_in[..., :C_ZX].astype(BF16), w_rest, _select_cols(rest, _span(fv, FOX_D), True)


def _mla_weights(w_uq, w_ukv):
    qd, kvd = MLA_NOPE + MLA_ROPE, MLA_NOPE + MLA_V
    pad_q = [None] * (HEAD_W - qd)
    wuq = _select_cols(w_uq, sum([_span(h * qd, qd) + pad_q for h in range(MLA_HEADS)], []))
    wuqr = _select_cols(w_uq, sum([[None] * MLA_NOPE + _rot_half_span(h * qd + MLA_NOPE) + pad_q
                                   for h in range(MLA_HEADS)], []))
    wkk = _select_cols(w_ukv, sum([_span(h * kvd, MLA_NOPE) + [None] * (HEAD_W - MLA_NOPE)
                                   for h in range(MLA_HEADS)], []))
    wvt = _select_cols(w_ukv, sum([_span(h * kvd + MLA_NOPE, MLA_V) for h in range(MLA_HEADS)],
                                  []), True)
    return wuq, wuqr, wkk, wvt


def _rows(v, width=None, offset=0):
    v = v.astype(F32)
    if width is not None:
        v = jnp.pad(v, ((0, 0), (offset, width - offset - v.shape[-1])))
    return v[:, None, :]


def _position_tables(seq, tm):
    pos = jnp.concatenate([N_META + jnp.arange(seq, dtype=F32),
                           jnp.arange(tm, dtype=F32) - (tm - N_META)])
    inv_freq = 1.0 / (ROPE_THETA ** (jnp.arange(0, MLA_ROPE, 2, dtype=F32) / MLA_ROPE))
    ang = pos[:, None] * inv_freq[None, :]
    cos, sin = jnp.cos(ang), jnp.sin(ang)
    n = seq + tm
    rope_pad = LANES - MLA_NOPE - MLA_ROPE
    cos128 = jnp.concatenate([jnp.ones((n, MLA_NOPE), F32), cos, cos,
                              jnp.zeros((n, rope_pad), F32)], axis=-1)
    sin128 = jnp.concatenate([jnp.zeros((n, MLA_NOPE), F32), sin, sin,
                              jnp.zeros((n, rope_pad), F32)], axis=-1)
    cosk128 = jnp.concatenate([jnp.zeros((n, MLA_NOPE), F32), cos, cos,
                               jnp.zeros((n, rope_pad), F32)], axis=-1)
    return cos128, sin128, cosk128


def kernel(x, meta, ffn1_w_gate, ffn1_w_up, ffn1_w_down, ln1_g, ln1_b, w_in, conv_w, conv_b, dt_bias, a_log, d_skip, ssd_norm_g, fox_f_b, mla_q_norm_g, mla_w_uq, mla_kv_norm_g, mla_w_ukv, w_out, ln2_g, ln2_b, ffn2_w_gate, ffn2_w_up, ffn2_w_down, ln3_g, ln3_b):
    bsz, seq, _ = x.shape
    assert seq % TQ == 0
    tm = TQ
    n_main = bsz * seq
    n_rows = n_main + tm
    meta_block = n_rows // BLOCK - 1

    h = x.reshape(n_main, D_MODEL)
    meta_tile = jnp.concatenate([jnp.zeros((tm - N_META, D_MODEL), x.dtype),
                                 meta.astype(x.dtype)], axis=0)
    tables = _position_tables(seq, tm)
    bf = lambda w: w.astype(BF16)

    ffn1 = (bf(ffn1_w_gate), bf(ffn1_w_up), bf(ffn1_w_down), _rows(ln1_g), _rows(ln1_b))
    ffn2 = (bf(ffn2_w_gate), bf(ffn2_w_up), bf(ffn2_w_down), _rows(ln3_g), _rows(ln3_b))
    proj_w = (*_in_proj_weights(w_in), *_mla_weights(mla_w_uq, mla_w_ukv),
              _rows(mla_q_norm_g), _rows(mla_kv_norm_g))
    ssd_p = (conv_w.astype(F32), _rows(conv_b), _rows(dt_bias, LANES),
             _rows(-jnp.exp(a_log.astype(F32)), LANES),
             _rows(jnp.repeat(d_skip, SSD_HEAD_DIM, axis=-1)), _rows(ssd_norm_g))
    fb = _rows(fox_f_b, LANES, SMALL_F)
    out_p = (bf(w_out), _rows(ln2_g), _rows(ln2_b))

    for l in range(DEPTH):
        h, zx, small, fq, fk, fvt, mq, mk, mvt = _ffn_ln(
            l, h, ffn1, tm, n_rows, meta_tile=meta_tile if l == 0 else None,
            proj=(proj_w, tables, seq))
        y_ssd = _ssd(l, zx, small, ssd_p, bsz, seq)
        fk_meta, fk_main = _fox_keys(l, fk, small, fb, bsz, seq)
        y_fox = _attention(fq, fk_meta, fk_main, fvt, FOX_HEADS, bsz, seq, 0, "fox_attention")
        y_mla = _attention(mq, mk, mk, mvt, MLA_HEADS, bsz, seq, meta_block, "mla_attention")
        h = _ffn_ln(l, h, ffn2, tm, n_rows if l < DEPTH - 1 else n_main,
                    mix=(y_ssd, y_fox, y_mla, *out_p))

    return h.reshape(bsz, seq, D_MODEL)
```
